```python
import math
import jax, jax.numpy as jnp
from jax import lax
import numpy as np

D_MODEL = 1024
BATCH = 4
SEQ = 4096
DEPTH = 4

CTX_LEN = 256
GRID_W = 64
N_HEADS = 8
HEAD_DIM = 64
ATT_W = N_HEADS * HEAD_DIM
WIN_ROWS = 8
WIN_COLS = 16
HY_W = 512
HY_ORDER = 2
HY_SHORT = 3
HY_EMB = 33
HY_BANDS = (HY_EMB - 1) // 2
HY_FILT = 64
HY_DECAY_TARGET = 1e-2
HY_FAST_PCT = 0.3
HY_SLOW_PCT = 1.5
HY_MAX_DECAY = math.log(HY_DECAY_TARGET) / HY_FAST_PCT
HY_MIN_DECAY = math.log(HY_DECAY_TARGET) / HY_SLOW_PCT
HY_DECAY_SHIFT = 0.05
PROJ_W = 3 * ATT_W + 3 * HY_W + 2 * D_MODEL
D_FF = 2816
N_EXPERTS = 8
TOP_K = 2
N_DENSE = (DEPTH + 1) // 2
N_MOE = DEPTH // 2
N_MOD = 6
EPS = 1e-6

kernel_name = "hybrid_na_hyena_moe_dit_block"


def rmsnorm(x, g):
    xf = x.astype(jnp.float32)
    y = xf * lax.rsqrt(jnp.mean(xf * xf, axis=-1, keepdims=True) + EPS)
    return (y * g.astype(jnp.float32)).astype(x.dtype)


def modulate(h, shift, scale):
    return h * (1.0 + scale) + shift


def split_proj(p):
    idx = [ATT_W, 2 * ATT_W, 3 * ATT_W, 3 * ATT_W + 3 * HY_W, 3 * ATT_W + 3 * HY_W + D_MODEL]
    return jnp.split(p, idx, axis=-1)


def heads(t):
    return t.reshape(*t.shape[:-1], N_HEADS, HEAD_DIM)


def neighbourhood_attention(q, k, v, k_ctx, v_ctx, rpb):
    B, L, H, Dh = q.shape
    rows = L // GRID_W
    kr = min(WIN_ROWS, rows)
    qg = q.reshape(B, rows, GRID_W, H, Dh) * (Dh ** -0.5)
    kg = k.reshape(B, rows, GRID_W, H, Dh)
    vg = v.reshape(B, rows, GRID_W, H, Dh)
    r = jnp.arange(rows)
    row_idx = jnp.clip(r - kr // 2, 0, rows - kr)[:, None] + jnp.arange(kr)[None, :]
    k_blk = jnp.take(kg, row_idx, axis=1)
    v_blk = jnp.take(vg, row_idx, axis=1)
    col = jnp.arange(GRID_W)
    c0 = jnp.clip(col - WIN_COLS // 2, 0, GRID_W - WIN_COLS)
    valid = (col[None, :] >= c0[:, None]) & (col[None, :] < c0[:, None] + WIN_COLS)
    dr = row_idx - r[:, None] + (WIN_ROWS - 1)
    dc = jnp.clip(col[None, :] - col[:, None] + (WIN_COLS - 1), 0, 2 * WIN_COLS - 2)
    bias = rpb[:, dr[:, None, :, None], dc[None, :, None, :]]
    s_loc = jnp.einsum('brqhd,brikhd->bhrqik', qg, k_blk).astype(jnp.float32)
    s_loc = jnp.where(valid[:, None, :], s_loc + bias.astype(jnp.float32), -jnp.inf)
    s_loc = s_loc.reshape(B, H, rows, GRID_W, kr * GRID_W)
    s_ctx = jnp.einsum('brqhd,bchd->bhrqc', qg, k_ctx).astype(jnp.float32)
    p = jax.nn.softmax(jnp.concatenate([s_loc, s_ctx], axis=-1), axis=-1).astype(q.dtype)
    p_loc = p[..., :kr * GRID_W].reshape(B, H, rows, GRID_W, kr, GRID_W)
    p_ctx = p[..., kr * GRID_W:]
    o = (jnp.einsum('bhrqik,brikhd->brqhd', p_loc, v_blk)
         + jnp.einsum('bhrqc,bchd->brqhd', p_ctx, v_ctx))
    return o.reshape(B, L, H * Dh)


def context_attention(q, k, v):
    B, C, H, Dh = q.shape
    s = jnp.einsum('bqhd,bkhd->bhqk', q * (Dh ** -0.5), k).astype(jnp.float32)
    p = jax.nn.softmax(s, axis=-1).astype(q.dtype)
    return jnp.einsum('bhqk,bkhd->bqhd', p, v).reshape(B, C, H * Dh)


def short_conv(u, w, b):
    L = u.shape[1]
    pad = HY_SHORT // 2
    up = jnp.pad(u, ((0, 0), (pad, HY_SHORT - 1 - pad), (0, 0)))
    return sum(up[:, j:j + L] * w[j] for j in range(HY_SHORT)) + b


def implicit_filters(L, w1, b1, w2, b2, w3, freq):
    f32 = jnp.float32
    pos = jnp.arange(L, dtype=f32)
    t = pos / max(L - 1, 1)
    bands = jnp.linspace(1e-4, HY_BANDS - 1, HY_BANDS, dtype=f32)
    ang = (2.0 * math.pi / L) * pos[:, None] * bands[None, :]
    z = jnp.concatenate([t[:, None], jnp.cos(ang), -jnp.sin(ang)], axis=-1)
    a = jnp.sin(freq[0].astype(f32) * (z @ w1.astype(f32) + b1.astype(f32)))
    a = jnp.sin(freq[1].astype(f32) * (a @ w2.astype(f32) + b2.astype(f32)))
    h = a @ w3.astype(f32)
    dist = jnp.abs(pos - L // 2) * (2.0 / L)
    rates = jnp.abs(jnp.linspace(HY_MIN_DECAY, HY_MAX_DECAY, HY_ORDER * HY_W, dtype=f32))
    h = h * (jnp.exp(-dist[:, None] * rates[None, :]) + HY_DECAY_SHIFT)
    return h.reshape(L, HY_ORDER, HY_W).transpose(1, 0, 2)


def fft_long_conv(u, h, bias):
    L = u.shape[1]
    uf = u.astype(jnp.float32)
    spec = jnp.fft.rfft(uf, n=2 * L, axis=1) * jnp.fft.rfft(h, n=2 * L, axis=0)[None]
    y = jnp.fft.irfft(spec, n=2 * L, axis=1)[:, L // 2:L // 2 + L]
    return (y + uf * bias.astype(jnp.float32)).astype(u.dtype)


def hyena(u, conv_w, conv_b, w1, b1, w2, b2, w3, freq, bias):
    L = u.shape[1]
    u = short_conv(u, conv_w, conv_b)
    v, x1, x2 = jnp.split(u, 3, axis=-1)
    h = implicit_filters(L, w1, b1, w2, b2, w3, freq)
    z = x1 * fft_long_conv(v, h[0], bias[0])
    return x2 * fft_long_conv(z, h[1], bias[1])


def merge(o_att, o_hy, ga, gb, w_pa, w_pb, w_o):
    return (jax.nn.sigmoid(ga) * (o_att @ w_pa) + jax.nn.sigmoid(gb) * (o_hy @ w_pb)) @ w_o


def swiglu(h, w1, w3, w2):
    return (jax.nn.silu(h @ w1) * (h @ w3)) @ w2


def channel_mixer(h, l, ffn_w1, ffn_w3, ffn_w2, moe_router, moe_w1, moe_w3, moe_w2):
    i = l // 2
    if l % 2 == 0:
        return swiglu(h, ffn_w1[i], ffn_w3[i], ffn_w2[i])
    logits = (h @ moe_router[i]).astype(jnp.float32)
    top_val, top_idx = lax.top_k(logits, TOP_K)
    gate = jax.nn.softmax(top_val, axis=-1)
    weight = jnp.einsum('...k,...ke->...e', gate,
                        jax.nn.one_hot(top_idx, N_EXPERTS, dtype=jnp.float32)).astype(h.dtype)
    out = jnp.zeros_like(h)
    for e in range(N_EXPERTS):
        out = out + weight[..., e:e + 1] * swiglu(h, moe_w1[i, e], moe_w3[i, e], moe_w2[i, e])
    return out


def setup_inputs(seed: int = 0) -> dict:
    key = jax.random.key(seed)
    ks = jax.random.split(key, 32)
    f32 = jnp.float32

    def nrm(k, shape, s):
        return s * jax.random.normal(k, shape, f32)

    D = D_MODEL
    return {
        'x': nrm(ks[0], (BATCH, SEQ, D), 1.0),
        'c': nrm(ks[1], (BATCH, D), 1.0),
        'ctx': nrm(ks[2], (BATCH, CTX_LEN, D), 1.0),
        'c_ctx': nrm(ks[3], (D,), 1.0),
        'w_mod': nrm(ks[4], (DEPTH, D, N_MOD * D), 0.5 * D ** -0.5),
        'b_mod': nrm(ks[5], (DEPTH, N_MOD * D), 0.02),
        'norm1_g': 1.0 + nrm(ks[6], (DEPTH, D), 0.02),
        'norm2_g': 1.0 + nrm(ks[7], (DEPTH, D), 0.02),
        'w_in': nrm(ks[8], (DEPTH, D, PROJ_W), D ** -0.5),
        'rpb': nrm(ks[9], (DEPTH, N_HEADS, 2 * WIN_ROWS - 1, 2 * WIN_COLS - 1), 0.1),
        'hy_conv_w': nrm(ks[10], (DEPTH, HY_SHORT, 3 * HY_W), HY_SHORT ** -0.5),
        'hy_conv_b': nrm(ks[11], (DEPTH, 3 * HY_W), 0.02),
        'hy_w1': nrm(ks[12], (DEPTH, HY_EMB, HY_FILT), HY_EMB ** -0.5),
        'hy_b1': nrm(ks[13], (DEPTH, HY_FILT), 0.1),
        'hy_w2': nrm(ks[14], (DEPTH, HY_FILT, HY_FILT), HY_FILT ** -0.5),
        'hy_b2': nrm(ks[15], (DEPTH, HY_FILT), 0.1),
        'hy_w3': nrm(ks[16], (DEPTH, HY_FILT, HY_ORDER * HY_W), 0.05 * HY_FILT ** -0.5),
        'hy_freq': 1.0 + nrm(ks[17], (DEPTH, 2, HY_FILT), 0.02),
        'hy_bias': nrm(ks[18], (DEPTH, HY_ORDER, HY_W), 0.5),
        'w_pa': nrm(ks[19], (DEPTH, ATT_W, D), ATT_W ** -0.5),
        'w_pb': nrm(ks[20], (DEPTH, HY_W, D), HY_W ** -0.5),
        'w_o': nrm(ks[21], (DEPTH, D, D), D ** -0.5),
        'ffn_w1': nrm(ks[22], (N_DENSE, D, D_FF), D ** -0.5),
        'ffn_w3': nrm(ks[23], (N_DENSE, D, D_FF), D ** -0.5),
        'ffn_w2': nrm(ks[24], (N_DENSE, D_FF, D), D_FF ** -0.5),
        'moe_router': nrm(ks[25], (N_MOE, D, N_EXPERTS), D ** -0.5),
        'moe_w1': nrm(ks[26], (N_MOE, N_EXPERTS, D, D_FF), D ** -0.5),
        'moe_w3': nrm(ks[27], (N_MOE, N_EXPERTS, D, D_FF), D ** -0.5),
        'moe_w2': nrm(ks[28], (N_MOE, N_EXPERTS, D_FF, D), D_FF ** -0.5),
        'final_g': 1.0 + nrm(ks[29], (D,), 0.02),
    }


def reference(x, c, ctx, c_ctx, w_mod, b_mod, norm1_g, norm2_g, w_in, rpb, hy_conv_w, hy_conv_b,
              hy_w1, hy_b1, hy_w2, hy_b2, hy_w3, hy_freq, hy_bias, w_pa, w_pb, w_o,
              ffn_w1, ffn_w3, ffn_w2, moe_router, moe_w1, moe_w3, moe_w2, final_g):
    s_lat = jax.nn.silu(c)[:, None, :]
    s_ctx = jax.nn.silu(c_ctx)[None, None, :]
    xl, xc = x, ctx
    for l in range(DEPTH):
        last = l == DEPTH - 1
        mod_l = jnp.split(s_lat @ w_mod[l] + b_mod[l], N_MOD, axis=-1)
        mod_c = jnp.split(s_ctx @ w_mod[l] + b_mod[l], N_MOD, axis=-1)
        hy_l = (hy_conv_w[l], hy_conv_b[l], hy_w1[l], hy_b1[l], hy_w2[l], hy_b2[l], hy_w3[l],
                hy_freq[l], hy_bias[l])

        hc = modulate(rmsnorm(xc, norm1_g[l]), mod_c[0], mod_c[1])
        if last:
            k_c, v_c = jnp.split(hc @ w_in[l, :, ATT_W:3 * ATT_W], 2, axis=-1)
        else:
            q_c, k_c, v_c, u_c, ga_c, gb_c = split_proj(hc @ w_in[l])
            mix_c = merge(context_attention(heads(q_c), heads(k_c), heads(v_c)), hyena(u_c, *hy_l),
                          ga_c, gb_c, w_pa[l], w_pb[l], w_o[l])

        hl = modulate(rmsnorm(xl, norm1_g[l]), mod_l[0], mod_l[1])
        q_l, k_l, v_l, u_l, ga_l, gb_l = split_proj(hl @ w_in[l])
        o_att = neighbourhood_attention(heads(q_l), heads(k_l), heads(v_l), heads(k_c), heads(v_c), rpb[l])
        o_hy = hyena(u_l, *hy_l)
        xl = xl + mod_l[2] * merge(o_att, o_hy, ga_l, gb_l, w_pa[l], w_pb[l], w_o[l])

        hl = modulate(rmsnorm(xl, norm2_g[l]), mod_l[3], mod_l[4])
        xl = xl + mod_l[5] * channel_mixer(hl, l, ffn_w1, ffn_w3, ffn_w2, moe_router, moe_w1, moe_w3, moe_w2)
        if not last:
            xc = xc + mod_c[2] * mix_c
            hc = modulate(rmsnorm(xc, norm2_g[l]), mod_c[3], mod_c[4])
            xc = xc + mod_c[5] * channel_mixer(hc, l, ffn_w1, ffn_w3, ffn_w2, moe_router, moe_w1, moe_w3, moe_w2)
    return rmsnorm(xl, final_g)
```

```python
import functools
import math

import numpy as np
import jax
import jax.numpy as jnp
from jax import lax
from jax.experimental import pallas as pl
from jax.experimental.pallas import tpu as pltpu

F32 = jnp.float32
BF16 = jnp.bfloat16
HIGHEST = lax.Precision.HIGHEST

D = 1024
NB = 4
SEQ = 4096
DEPTH = 4
CTX = 256
GRID_W = 64
GRID_H = SEQ // GRID_W
N_HEADS = 8
HEAD_DIM = 64
ATT_W = N_HEADS * HEAD_DIM
WIN_ROWS = 8
WIN_COLS = 16
HY_W = 512
HY_ORDER = 2
HY_SHORT = 3
HY_EMB = 33
HY_BANDS = (HY_EMB - 1) // 2
HY_FILT = 64
HY_DECAY_TARGET = 1e-2
HY_MAX_DECAY = math.log(HY_DECAY_TARGET) / 0.3
HY_MIN_DECAY = math.log(HY_DECAY_TARGET) / 1.5
HY_DECAY_SHIFT = 0.05
PROJ_W = 3 * ATT_W + 3 * HY_W + 2 * D
D_FF = 2816
N_EXPERTS = 8
N_MOD = 6
EPS = 1e-6

LAT_ROWS = NB * SEQ
CTX_ROWS = NB * CTX
ROWS = LAT_ROWS + CTX_ROWS
MOD_ROWS = 8
CTX_MOD_ROW = NB

LANE = 128
TM = 1024
VMEM_LIMIT = 56 * 1024 * 1024

COL_Q, COL_K, COL_V = 0, ATT_W // LANE, 2 * ATT_W // LANE
COL_HV = 3 * ATT_W // LANE
COL_HX1 = COL_HV + HY_W // LANE
COL_HX2 = COL_HX1 + HY_W // LANE
COL_GA = (3 * ATT_W + 3 * HY_W) // D
COL_GB = COL_GA + 1


def _mod_row(i):
    return jnp.where(i < LAT_ROWS // TM, (i * TM) // SEQ, CTX_MOD_ROW)


def _params(*sem):
    return pltpu.CompilerParams(dimension_semantics=sem, vmem_limit_bytes=VMEM_LIMIT)


def _mod_kernel(s_ref, w_ref, b_ref, o_ref):
    s = s_ref[...]
    s = s * jax.nn.sigmoid(s)
    o_ref[0] = jnp.dot(s, w_ref[0], precision=HIGHEST, preferred_element_type=F32) + b_ref[0]


def _modulation(c, c_ctx, w_mod, b_mod):
    s = jnp.zeros((MOD_ROWS, D), F32).at[:NB].set(c).at[CTX_MOD_ROW].set(c_ctx)
    tn = 1536
    return pl.pallas_call(
        _mod_kernel,
        grid=(DEPTH, N_MOD * D // tn),
        in_specs=[pl.BlockSpec((MOD_ROWS, D), lambda l, j: (0, 0)),
                  pl.BlockSpec((1, D, tn), lambda l, j: (l, 0, j)),
                  pl.BlockSpec((1, 1, tn), lambda l, j: (l, 0, j))],
        out_specs=pl.BlockSpec((1, MOD_ROWS, tn), lambda l, j: (l, 0, j)),
        out_shape=jax.ShapeDtypeStruct((DEPTH, MOD_ROWS, N_MOD * D), F32),
        compiler_params=_params("parallel", "parallel"),
        name="modulation",
    )(s, w_mod, b_mod.reshape(DEPTH, 1, N_MOD * D))


def _norm_mod(x, g, shift, scale):
    y = x * lax.rsqrt(jnp.mean(x * x, axis=-1, keepdims=True) + EPS) * g
    return y * (1.0 + scale) + shift


def _inproj_kernel(x_ref, mod_ref, g_ref, w_ref, o_ref, h_scr):
    @pl.when(pl.program_id(1) == 0)
    def _():
        m = mod_ref[0]
        h = _norm_mod(x_ref[...], g_ref[...], m[:, 0:D], m[:, D:2 * D])
        h_scr[...] = h.astype(BF16)

    o_ref[...] = jnp.dot(h_scr[...], w_ref[0].astype(BF16),
                         preferred_element_type=F32).astype(BF16)


def _in_proj(x, mod_l, g, w, layer):
    tn = 512
    return pl.pallas_call(
        _inproj_kernel,
        grid=(ROWS // TM, PROJ_W // tn),
        in_specs=[pl.BlockSpec((TM, D), lambda i, j: (i, 0)),
                  pl.BlockSpec((1, 1, N_MOD * D), lambda i, j: (_mod_row(i), 0, 0)),
                  pl.BlockSpec((1, D), lambda i, j: (0, 0)),
                  pl.BlockSpec((1, D, tn), lambda i, j: (layer, 0, j))],
        out_specs=pl.BlockSpec((TM, tn), lambda i, j: (i, j)),
        out_shape=jax.ShapeDtypeStruct((ROWS, PROJ_W), BF16),
        scratch_shapes=[pltpu.VMEM((TM, D), BF16)],
        compiler_params=_params("parallel", "arbitrary"),
        name="in_proj",
    )(x, mod_l, g, w)


NEG = -1e30


def _nt_dot(a, b):
    return lax.dot_general(a, b, (((1,), (1,)), ((), ())), preferred_element_type=F32)


def _attend_pair(qp, keys, values, biases):
    lane = lax.broadcasted_iota(jnp.int32, qp.shape, 1)
    qs = qp * jnp.asarray(HEAD_DIM ** -0.5, BF16)
    outs = []
    for a in range(2):
        sel = (lane < HEAD_DIM) if a == 0 else (lane >= HEAD_DIM)
        qa = jnp.where(sel, qs, jnp.zeros_like(qs))
        ss = []
        for k, b in zip(keys, biases):
            s = _nt_dot(qa, k)
            if b is not None:
                s = s + b[a]
            ss.append(s)
        m = functools.reduce(jnp.maximum, [jnp.max(s, axis=-1, keepdims=True) for s in ss])
        ps = [jnp.exp(s - m) for s in ss]
        den = functools.reduce(jnp.add, [jnp.sum(p, axis=-1, keepdims=True) for p in ps])
        o = functools.reduce(jnp.add, [jnp.dot(p.astype(BF16), v, preferred_element_type=F32)
                                       for p, v in zip(ps, values)])
        outs.append(o / den)
    return jnp.where(lane < HEAD_DIM, outs[0], outs[1])


def _natt_kernel(q_ref, k_ref, v_ref, kc_ref, vc_ref, bt_ref, o_ref):
    r = pl.program_id(1)
    r0 = jnp.clip(r - WIN_ROWS // 2, 0, GRID_H - WIN_ROWS)
    start = pl.multiple_of(r0 * GRID_W, GRID_W)
    nloc = WIN_ROWS * GRID_W
    for hp in range(N_HEADS // 2):
        cs = slice(hp * LANE, (hp + 1) * LANE)
        o = _attend_pair(
            q_ref[:, cs],
            [k_ref[pl.ds(start, nloc), cs], kc_ref[:, cs]],
            [v_ref[pl.ds(start, nloc), cs], vc_ref[:, cs]],
            [(bt_ref[0, 2 * hp], bt_ref[0, 2 * hp + 1]), None])
        o_ref[:, cs] = o.astype(BF16)


def _bias_tables(rpb_l):
    col = np.arange(GRID_W)
    c0 = np.clip(col - WIN_COLS // 2, 0, GRID_W - WIN_COLS)
    valid = (col[None, :] >= c0[:, None]) & (col[None, :] < c0[:, None] + WIN_COLS)
    dc = np.clip(col[None, :] - col[:, None] + (WIN_COLS - 1), 0, 2 * WIN_COLS - 2)
    t = rpb_l[:, :, dc]
    t = jnp.where(valid[None, None], t, NEG)
    tabs = []
    for d0 in range(WIN_ROWS):
        w = t[:, d0:d0 + WIN_ROWS]
        tabs.append(w.transpose(0, 2, 1, 3).reshape(N_HEADS, GRID_W, WIN_ROWS * GRID_W))
    return jnp.stack(tabs)


def _bias_index(r):
    r0 = jnp.clip(r - WIN_ROWS // 2, 0, GRID_H - WIN_ROWS)
    return r0 - r + (WIN_ROWS - 1)


def _neighbourhood_attention(p, rpb_l):
    bt = _bias_tables(rpb_l)
    lat_blk = SEQ // GRID_W
    ctx_blk0 = LAT_ROWS // CTX
    return pl.pallas_call(
        _natt_kernel,
        grid=(NB, GRID_H),
        in_specs=[pl.BlockSpec((GRID_W, ATT_W), lambda b, r: (b * lat_blk + r, 0)),
                  pl.BlockSpec((SEQ, ATT_W), lambda b, r: (b, 1)),
                  pl.BlockSpec((SEQ, ATT_W), lambda b, r: (b, 2)),
                  pl.BlockSpec((CTX, ATT_W), lambda b, r: (ctx_blk0 + b, 1)),
                  pl.BlockSpec((CTX, ATT_W), lambda b, r: (ctx_blk0 + b, 2)),
                  pl.BlockSpec((1, N_HEADS, GRID_W, WIN_ROWS * GRID_W),
                               lambda b, r: (_bias_index(r), 0, 0, 0))],
        out_specs=pl.BlockSpec((GRID_W, ATT_W), lambda b, r: (b * lat_blk + r, 0)),
        out_shape=jax.ShapeDtypeStruct((LAT_ROWS, ATT_W), BF16),
        compiler_params=_params("parallel", "arbitrary"),
        name="neighbourhood_attention",
    )(p, p, p, p, p, bt)


def _catt_kernel(q_ref, k_ref, v_ref, o_ref):
    for hp in range(N_HEADS // 2):
        cs = slice(hp * LANE, (hp + 1) * LANE)
        o = _attend_pair(q_ref[:, cs], [k_ref[:, cs]], [v_ref[:, cs]], [None])
        o_ref[:, cs] = o.astype(BF16)


def _context_attention(p):
    ctx_blk0 = LAT_ROWS // CTX
    return pl.pallas_call(
        _catt_kernel,
        grid=(NB,),
        in_specs=[pl.BlockSpec((CTX, ATT_W), lambda b: (ctx_blk0 + b, 0)),
                  pl.BlockSpec((CTX, ATT_W), lambda b: (ctx_blk0 + b, 1)),
                  pl.BlockSpec((CTX, ATT_W), lambda b: (ctx_blk0 + b, 2))],
        out_specs=pl.BlockSpec((CTX, ATT_W), lambda b: (b, 0)),
        out_shape=jax.ShapeDtypeStruct((CTX_ROWS, ATT_W), BF16),
        compiler_params=_params("parallel"),
        name="context_attention",
    )(p, p, p)


def _filter_features(L):
    pos = np.arange(L, dtype=np.float64)
    t = pos / max(L - 1, 1)
    bands = np.linspace(1e-4, HY_BANDS - 1, HY_BANDS)
    ang = (2.0 * math.pi / L) * pos[:, None] * bands[None, :]
    z = np.concatenate([t[:, None], np.cos(ang), -np.sin(ang)], axis=-1)
    zp = np.zeros((L, LANE), np.float32)
    zp[:, :HY_EMB] = z
    return zp


def _filter_kernel(z_ref, w1_ref, b1_ref, w2_ref, b2_ref, w3_ref, fr_ref, rate_ref, o_ref, *, L, tl):
    dot = functools.partial(jnp.dot, precision=HIGHEST, preferred_element_type=F32)
    fr = fr_ref[0]
    a = jnp.sin(fr[0:1] * (dot(z_ref[...], w1_ref[0]) + b1_ref[0]))
    a = jnp.sin(fr[1:2] * (dot(a, w2_ref[0]) + b2_ref[0]))
    h = dot(a, w3_ref[0])
    pos = (pl.program_id(1) * tl + lax.broadcasted_iota(jnp.int32, (tl, 1), 0)).astype(F32)
    dist = jnp.abs(pos - float(L // 2)) * (2.0 / L)
    o_ref[0] = h * (jnp.exp(-dist * rate_ref[...]) + HY_DECAY_SHIFT)


def _implicit_filters(L, w1, b1, w2, b2, w3, freq):
    tl = min(L, 512)
    z = jnp.asarray(_filter_features(L))
    w1p = jnp.zeros((DEPTH, LANE, HY_FILT), F32).at[:, :HY_EMB].set(w1)
    rates = np.abs(np.linspace(HY_MIN_DECAY, HY_MAX_DECAY, HY_ORDER * HY_W)).astype(np.float32)
    cw = HY_ORDER * HY_W
    return pl.pallas_call(
        functools.partial(_filter_kernel, L=L, tl=tl),
        grid=(DEPTH, L // tl),
        in_specs=[pl.BlockSpec((tl, LANE), lambda l, i: (i, 0)),
                  pl.BlockSpec((1, LANE, HY_FILT), lambda l, i: (l, 0, 0)),
                  pl.BlockSpec((1, 1, HY_FILT), lambda l, i: (l, 0, 0)),
                  pl.BlockSpec((1, HY_FILT, HY_FILT), lambda l, i: (l, 0, 0)),
                  pl.BlockSpec((1, 1, HY_FILT), lambda l, i: (l, 0, 0)),
                  pl.BlockSpec((1, HY_FILT, cw), lambda l, i: (l, 0, 0)),
                  pl.BlockSpec((1, 2, HY_FILT), lambda l, i: (l, 0, 0)),
                  pl.BlockSpec((1, cw), lambda l, i: (0, 0))],
        out_specs=pl.BlockSpec((1, tl, cw), lambda l, i: (l, i, 0)),
        out_shape=jax.ShapeDtypeStruct((DEPTH, L, cw), F32),
        compiler_params=_params("parallel", "parallel"),
        name=f"implicit_filters_{L}",
    )(z, w1p, b1.reshape(DEPTH, 1, HY_FILT), w2, b2.reshape(DEPTH, 1, HY_FILT), w3, freq,
      jnp.asarray(rates).reshape(1, cw))


FFT_N = 2 * SEQ
FA = 64
FB = 128
FA_IN = SEQ // FB
PITCH = 136
HALO = 8


def _fft_tables():
    a = np.arange(FA_IN)
    r = np.arange(FA)
    b = np.arange(FB)
    q = np.arange(FB)
    ph = a[None, None, :] * r[None, :, None] / FA + b[:, None, None] * r[None, :, None] / FFT_N
    g1 = np.exp(-2j * np.pi * ph)
    g1 = np.concatenate([g1.real, g1.imag], axis=1)
    f = np.exp(-2j * np.pi * q[:, None] * b[None, :] / FB)
    f2 = np.concatenate([f.real, f.imag], axis=0)
    f2c = np.concatenate([f.real, -f.imag], axis=0)
    ao = np.arange(FA_IN) + (SEQ // 2) // FB
    ph2 = ao[None, :, None] * r[None, None, :] / FA + b[:, None, None] * r[None, None, :] / FFT_N
    g2 = np.exp(2j * np.pi * ph2) / FFT_N
    g2 = np.concatenate([g2.real, g2.imag], axis=1)
    return tuple(jnp.asarray(t, dtype=BF16) for t in (g1, f2, f2c, g2))


def _cplx(out, m, n):
    re = out[0:m, 0:n] - out[m:2 * m, n:2 * n]
    im = out[0:m, n:2 * n] + out[m:2 * m, 0:n]
    return re, im


def _short_conv_slabs(src_ref, rows, t_scr, cw_ref, cb_ref, apply_conv, emit):
    nslab = rows // FB
    if not apply_conv:
        for a in range(nslab):
            emit(a, src_ref[a * FB:(a + 1) * FB, :].astype(F32))
        return
    zero = jnp.zeros((HALO, LANE), F32)
    t_scr[0:HALO, :] = zero
    t_scr[HALO + rows:2 * HALO + rows, :] = zero
    for a in range(nslab):
        t_scr[HALO + a * FB:HALO + (a + 1) * FB, :] = src_ref[a * FB:(a + 1) * FB, :].astype(F32)
    w0, w1, w2, bias = cw_ref[0:1, :], cw_ref[1:2, :], cw_ref[2:3, :], cb_ref[...]
    for a in range(nslab):
        o = HALO + a * FB
        y = (w0 * t_scr[o - 1:o - 1 + FB, :] + w1 * t_scr[o:o + FB, :]
             + w2 * t_scr[o + 1:o + 1 + FB, :] + bias)
        emit(a, y)


def _fwd_stage1(b, s_refs, g1_ref, y_r, y_i):
    cols = [s[pl.ds(b, FA_IN, stride=PITCH), :].astype(BF16) for s in s_refs]
    rhs = cols[0] if len(cols) == 1 else jnp.concatenate(cols, axis=1)
    out = jnp.dot(g1_ref[b], rhs, preferred_element_type=F32)
    if len(cols) == 1:
        re, im = out[0:FA], out[FA:2 * FA]
    else:
        re, im = _cplx(out, FA, LANE)
    y_r[pl.ds(b, FA, stride=PITCH), :] = re
    y_i[pl.ds(b, FA, stride=PITCH), :] = im


def _fwd_stage2(r, y_r, y_i, f_ref):
    o = pl.multiple_of(r * PITCH, 8)
    rhs = jnp.concatenate([y_r[pl.ds(o, FB), :].astype(BF16), y_i[pl.ds(o, FB), :].astype(BF16)], axis=1)
    out = jnp.dot(f_ref[...], rhs, preferred_element_type=F32)
    return _cplx(out, FB, LANE)


def _spectrum_kernel(h_ref, g1_ref, f_ref, hr_ref, hi_ref, s_scr, y_r, y_i):
    for a in range(FA_IN):
        s_scr[a * PITCH:a * PITCH + FB, :] = h_ref[0, a * FB:(a + 1) * FB, :]

    def s1(b, c):
        _fwd_stage1(b, [s_scr], g1_ref, y_r, y_i)
        return c
    lax.fori_loop(0, FB, s1, 0)

    def s2(r, c):
        xr, xi = _fwd_stage2(r, y_r, y_i, f_ref)
        o = pl.multiple_of(r * FB, FB)
        hr_ref[0, pl.ds(o, FB), :] = xr
        hi_ref[0, pl.ds(o, FB), :] = xi
        return c
    lax.fori_loop(0, FA, s2, 0)


def _filter_spectra(h, tabs):
    g1, f2, _, _ = tabs
    cw = HY_ORDER * HY_W
    spec = pl.BlockSpec((1, FFT_N, LANE), lambda l, c: (l, 0, c))
    return pl.pallas_call(
        _spectrum_kernel,
        grid=(DEPTH, cw // LANE),
        in_specs=[pl.BlockSpec((1, SEQ, LANE), lambda l, c: (l, 0, c)),
                  pl.BlockSpec(g1.shape, lambda l, c: (0, 0, 0)),
                  pl.BlockSpec(f2.shape, lambda l, c: (0, 0))],
        out_specs=[spec, spec],
        out_shape=[jax.ShapeDtypeStruct((DEPTH, FFT_N, cw), F32)] * 2,
        scratch_shapes=[pltpu.VMEM((FA_IN * PITCH, LANE), F32),
                        pltpu.VMEM((FA * PITCH, LANE), F32),
                        pltpu.VMEM((FA * PITCH, LANE), F32)],
        compiler_params=_params("parallel", "parallel"),
        name="filter_spectra",
    )(h, g1, f2)


def _hyconv_kernel(u_ref, x_ref, cwu_ref, cbu_ref, cwx_ref, cbx_ref,
                   hr_ref, hi_ref, bias_ref, g1_ref, f_ref, fc_ref, g2_ref,
                   o_ref, t_scr, s_r, s_i, y_r, y_i, *, conv_input):
    halves = [(s_r, slice(0, SEQ)), (s_i, slice(SEQ, 2 * SEQ))]

    def fill(s):
        def emit(a, slab):
            s[a * PITCH:a * PITCH + FB, :] = slab
        return emit
    for s, rows in halves:
        _short_conv_slabs(u_ref.at[rows, :], SEQ, t_scr, cwu_ref, cbu_ref, conv_input, fill(s))

    def s1(b, c):
        _fwd_stage1(b, [s_r, s_i], g1_ref, y_r, y_i)
        return c
    lax.fori_loop(0, FB, s1, 0)

    def s2(r, c):
        xr, xi = _fwd_stage2(r, y_r, y_i, f_ref)
        oh = pl.multiple_of(r * FB, FB)
        hr = hr_ref[0, pl.ds(oh, FB), :]
        hi = hi_ref[0, pl.ds(oh, FB), :]
        vr = xr * hr - xi * hi
        vi = xr * hi + xi * hr
        rhs = jnp.concatenate([vr.astype(BF16), vi.astype(BF16)], axis=1)
        wr, wi = _cplx(jnp.dot(fc_ref[...], rhs, preferred_element_type=F32), FB, LANE)
        o = pl.multiple_of(r * PITCH, 8)
        y_r[pl.ds(o, FB), :] = wr
        y_i[pl.ds(o, FB), :] = wi
        return c
    lax.fori_loop(0, FA, s2, 0)

    bias = bias_ref[0]

    def s3(b, c):
        rhs = jnp.concatenate([y_r[pl.ds(b, FA, stride=PITCH), :].astype(BF16),
                               y_i[pl.ds(b, FA, stride=PITCH), :].astype(BF16)], axis=1)
        re, im = _cplx(jnp.dot(g2_ref[b], rhs, preferred_element_type=F32), FA_IN, LANE)
        idx = pl.ds(b, FA_IN, stride=PITCH)
        s_r[idx, :] = re + s_r[idx, :] * bias
        s_i[idx, :] = im + s_i[idx, :] * bias
        return c
    lax.fori_loop(0, FB, s3, 0)

    for s, rows in halves:
        def emit(a, slab, s=s, out=o_ref.at[rows, :]):
            out[a * FB:(a + 1) * FB, :] = (slab * s[a * PITCH:a * PITCH + FB, :]).astype(out.dtype)
        _short_conv_slabs(x_ref.at[rows, :], SEQ, t_scr, cwx_ref, cbx_ref, True, emit)


def _hyena_conv(u, u_col, x, x_col, conv_w, conv_b, cu_col, cx_col, spec_re, spec_im, layer, order,
                hy_bias_l, tabs, conv_input):
    g1, f2, f2c, g2 = tabs
    nblk = HY_W // LANE
    tok = lambda col: pl.BlockSpec((2 * SEQ, LANE), lambda c, p: (p, col + c))
    chan = lambda rows, col: pl.BlockSpec((rows, LANE), lambda c, p: (0, col + c))
    hspec = pl.BlockSpec((1, FFT_N, LANE), lambda c, p: (layer, 0, order * nblk + c))
    const = lambda shape: pl.BlockSpec(shape, lambda c, p: (0,) * len(shape))
    return pl.pallas_call(
        functools.partial(_hyconv_kernel, conv_input=conv_input),
        grid=(nblk, NB // 2),
        in_specs=[tok(u_col), tok(x_col),
                  chan(HY_SHORT, cu_col), chan(1, cu_col), chan(HY_SHORT, cx_col), chan(1, cx_col),
                  hspec, hspec,
                  pl.BlockSpec((1, 1, LANE), lambda c, p: (order, 0, c)),
                  const(g1.shape), const(f2.shape), const(f2c.shape), const(g2.shape)],
        out_specs=pl.BlockSpec((2 * SEQ, LANE), lambda c, p: (p, c)),
        out_shape=jax.ShapeDtypeStruct((LAT_ROWS, HY_W), BF16),
        scratch_shapes=[pltpu.VMEM((SEQ + 2 * HALO, LANE), F32),
                        pltpu.VMEM((FA_IN * PITCH, LANE), F32),
                        pltpu.VMEM((FA_IN * PITCH, LANE), F32),
                        pltpu.VMEM((FA * PITCH, LANE), F32),
                        pltpu.VMEM((FA * PITCH, LANE), F32)],
        compiler_params=_params("arbitrary", "arbitrary"),
        name=f"hyena_conv{order}",
    )(u, x, conv_w, conv_b, conv_w, conv_b, spec_re, spec_im,
      hy_bias_l.reshape(HY_ORDER, 1, HY_W), g1, f2, f2c, g2)


CN = 2 * CTX


def _ctx_tables():
    k = np.arange(CN)
    n = np.arange(CTX)
    f = np.exp(-2j * np.pi * k[:, None] * n[None, :] / CN)
    fwd = np.concatenate([f.real, f.imag], axis=0)
    t = np.arange(CTX) + CTX // 2
    g = np.exp(2j * np.pi * t[:, None] * k[None, :] / CN) / CN
    inv = np.concatenate([g.real, g.imag], axis=0)
    return jnp.asarray(fwd, dtype=BF16), jnp.asarray(inv, dtype=BF16)


def _ctx_hyena_kernel(v_ref, x1_ref, x2_ref, cwv_ref, cbv_ref, cw1_ref, cb1_ref, cw2_ref, cb2_ref,
                      h0_ref, h1_ref, bias_ref, fwd_ref, inv_ref, o_ref, t_scr, s_scr, g_scr):
    def conv_into(src_ref, b, cw_ref, cb_ref, dst, col):
        view = src_ref.at[b * CTX:(b + 1) * CTX, :]

        def emit(a, slab):
            dst[a * FB:(a + 1) * FB, col * LANE:(col + 1) * LANE] = slab
        _short_conv_slabs(view, CTX, t_scr, cw_ref, cb_ref, True, emit)

    def spectrum(h_ref):
        out = jnp.dot(fwd_ref[...], h_ref[0].astype(BF16), preferred_element_type=F32)
        return out[0:CN], out[CN:2 * CN]

    def long_conv(z2, hr, hi, bias):
        x = jnp.dot(fwd_ref[...], z2.astype(BF16), preferred_element_type=F32)
        xr, xi = _cplx(x, CN, LANE)
        vr = xr * hr - xi * hi
        vi = xr * hi + xi * hr
        rhs = jnp.concatenate([vr.astype(BF16), vi.astype(BF16)], axis=1)
        y = jnp.dot(inv_ref[...], rhs, preferred_element_type=F32)
        yr, yi = _cplx(y, CTX, LANE)
        return jnp.concatenate([yr, yi], axis=1) + z2 * jnp.concatenate([bias, bias], axis=1)

    h0r, h0i = spectrum(h0_ref)
    h1r, h1i = spectrum(h1_ref)
    for p in range(NB // 2):
        for half in range(2):
            conv_into(v_ref, 2 * p + half, cwv_ref, cbv_ref, s_scr, half)
            conv_into(x1_ref, 2 * p + half, cw1_ref, cb1_ref, g_scr, half)
        z = g_scr[...] * long_conv(s_scr[...], h0r, h0i, bias_ref[0])
        for half in range(2):
            conv_into(x2_ref, 2 * p + half, cw2_ref, cb2_ref, g_scr, half)
        y = g_scr[...] * long_conv(z, h1r, h1i, bias_ref[1])
        for half in range(2):
            b = 2 * p + half
            o_ref[b * CTX:(b + 1) * CTX, :] = y[:, half * LANE:(half + 1) * LANE].astype(BF16)


def _ctx_hyena(p, conv_w, conv_b, filt_ctx, layer, hy_bias_l, ctabs):
    fwd, inv = ctabs
    nblk = HY_W // LANE
    ctx_blk = LAT_ROWS // CTX_ROWS
    tok = lambda col: pl.BlockSpec((CTX_ROWS, LANE), lambda c: (ctx_blk, col + c))
    chan = lambda rows, col: pl.BlockSpec((rows, LANE), lambda c: (0, col + c))
    hspec = lambda order: pl.BlockSpec((1, CTX, LANE), lambda c: (layer, 0, order * nblk + c))
    const = lambda shape: pl.BlockSpec(shape, lambda c: (0,) * len(shape))
    return pl.pallas_call(
        _ctx_hyena_kernel,
        grid=(nblk,),
        in_specs=[tok(COL_HV), tok(COL_HX1), tok(COL_HX2),
                  chan(HY_SHORT, 0), chan(1, 0), chan(HY_SHORT, nblk), chan(1, nblk),
                  chan(HY_SHORT, 2 * nblk), chan(1, 2 * nblk),
                  hspec(0), hspec(1),
                  pl.BlockSpec((HY_ORDER, 1, LANE), lambda c: (0, 0, c)),
                  const(fwd.shape), const(inv.shape)],
        out_specs=pl.BlockSpec((CTX_ROWS, LANE), lambda c: (0, c)),
        out_shape=jax.ShapeDtypeStruct((CTX_ROWS, HY_W), BF16),
        scratch_shapes=[pltpu.VMEM((CTX + 2 * HALO, LANE), F32),
                        pltpu.VMEM((CTX, 2 * LANE), F32),
                        pltpu.VMEM((CTX, 2 * LANE), F32)],
        compiler_params=_params("parallel"),
        name="context_hyena",
    )(p, p, p, conv_w, conv_b, conv_w, conv_b, conv_w, conv_b, filt_ctx, filt_ctx,
      hy_bias_l.reshape(HY_ORDER, 1, HY_W), fwd, inv)


def _route(logits):
    lane = lax.broadcasted_iota(jnp.int32, logits.shape, 1)
    ninf = jnp.asarray(-jnp.inf, F32)
    lg = jnp.where(lane < N_EXPERTS, logits, ninf)
    m1 = jnp.max(lg, axis=-1, keepdims=True)
    i1 = jnp.min(jnp.where(lg == m1, lane, LANE), axis=-1, keepdims=True)
    lg2 = jnp.where(lane == i1, ninf, lg)
    m2 = jnp.max(lg2, axis=-1, keepdims=True)
    i2 = jnp.min(jnp.where(lg2 == m2, lane, LANE), axis=-1, keepdims=True)
    e = jnp.exp(m2 - m1)
    den = 1.0 + e
    return jnp.where(lane == i1, 1.0 / den, 0.0) + jnp.where(lane == i2, e / den, 0.0)


def _merge_kernel(oa_ref, oh_ref, ga_ref, gb_ref, x_ref, mod_ref, g2_ref, wpa_ref, wpb_ref, wo_ref,
                  *rest, routed):
    if routed:
        wr_ref, xo_ref, h_ref, rw_ref, wpa_s, wpb_s, wo_s = rest
    else:
        xo_ref, h_ref, wpa_s, wpb_s, wo_s = rest

    @pl.when(pl.program_id(0) == 0)
    def _():
        wpa_s[...] = wpa_ref[0].astype(BF16)
        wpb_s[...] = wpb_ref[0].astype(BF16)
        wo_s[...] = wo_ref[0].astype(BF16)

    a = jnp.dot(oa_ref[...], wpa_s[...], preferred_element_type=F32)
    b = jnp.dot(oh_ref[...], wpb_s[...], preferred_element_type=F32)
    mix = (jax.nn.sigmoid(ga_ref[...].astype(F32)) * a + jax.nn.sigmoid(gb_ref[...].astype(F32)) * b)
    y = jnp.dot(mix.astype(BF16), wo_s[...], preferred_element_type=F32)
    m = mod_ref[0]
    x = x_ref[...] + m[:, 2 * D:3 * D] * y
    xo_ref[...] = x
    h = _norm_mod(x, g2_ref[...], m[:, 3 * D:4 * D], m[:, 4 * D:5 * D])
    h_ref[...] = h.astype(BF16)
    if routed:
        logits = jnp.dot(h, wr_ref[...], precision=HIGHEST, preferred_element_type=F32)
        rw_ref[...] = _route(logits)


def _merge(o_att, o_hy, p, x, mod_l, g2, w_pa, w_pb, w_o, layer, router):
    routed = router is not None
    tile = lambda w: pl.BlockSpec((TM, w), lambda i: (i, 0))
    full = lambda a: pl.BlockSpec(a.shape, lambda i: (0,) * a.ndim)
    layered = lambda a: pl.BlockSpec((1,) + a.shape[1:], lambda i: (layer, 0, 0))
    in_specs = [tile(ATT_W), tile(HY_W),
                pl.BlockSpec((TM, D), lambda i: (i, COL_GA)),
                pl.BlockSpec((TM, D), lambda i: (i, COL_GB)),
                tile(D),
                pl.BlockSpec((1, 1, N_MOD * D), lambda i: (_mod_row(i), 0, 0)),
                pl.BlockSpec((1, D), lambda i: (0, 0)),
                layered(w_pa), layered(w_pb), layered(w_o)]
    args = [o_att, o_hy, p, p, x, mod_l, g2, w_pa, w_pb, w_o]
    out_specs = [tile(D), tile(D)]
    out_shape = [jax.ShapeDtypeStruct((ROWS, D), F32), jax.ShapeDtypeStruct((ROWS, D), BF16)]
    if routed:
        wr = jnp.zeros((D, LANE), F32).at[:, :N_EXPERTS].set(router)
        in_specs.append(full(wr))
        args.append(wr)
        out_specs.append(tile(LANE))
        out_shape.append(jax.ShapeDtypeStruct((ROWS, LANE), F32))
    return pl.pallas_call(
        functools.partial(_merge_kernel, routed=routed),
        grid=(ROWS // TM,),
        in_specs=in_specs, out_specs=out_specs, out_shape=out_shape,
        scratch_shapes=[pltpu.VMEM(w_pa.shape[1:], BF16), pltpu.VMEM(w_pb.shape[1:], BF16),
                        pltpu.VMEM(w_o.shape[1:], BF16)],
        compiler_params=_params("arbitrary"),
        name="merge",
    )(*args)


TF = 256


def _ffn_kernel(h_ref, x_ref, mod_ref, w1_ref, w3_ref, w2_ref, *rest, routed, final):
    rest = list(rest)
    rw_ref = rest.pop(0) if routed else None
    fg_ref = rest.pop(0) if final else None
    o_ref, acc = rest
    e = pl.program_id(1)
    f = pl.program_id(2)
    first = (e == 0) & (f == 0)
    last = (e == pl.num_programs(1) - 1) & (f == pl.num_programs(2) - 1)

    @pl.when(first)
    def _():
        acc[...] = jnp.zeros_like(acc)

    h = h_ref[...]
    a = jnp.dot(h, w1_ref[0].astype(BF16), preferred_element_type=F32)
    b = jnp.dot(h, w3_ref[0].astype(BF16), preferred_element_type=F32)
    t = a * jax.nn.sigmoid(a) * b
    if routed:
        lane = lax.broadcasted_iota(jnp.int32, rw_ref.shape, 1)
        t = t * jnp.sum(jnp.where(lane == e, rw_ref[...], 0.0), axis=-1, keepdims=True)
    acc[...] += jnp.dot(t.astype(BF16), w2_ref[0].astype(BF16), preferred_element_type=F32)

    @pl.when(last)
    def _():
        x = x_ref[...] + mod_ref[0][:, 5 * D:6 * D] * acc[...]
        if final:
            x = x * lax.rsqrt(jnp.mean(x * x, axis=-1, keepdims=True) + EPS) * fg_ref[...]
        o_ref[...] = x


def _channel_mixer(h, x, mod_l, w1, w3, w2, e0, ne, rw=None, final_g=None):
    routed = rw is not None
    final = final_g is not None
    rows = LAT_ROWS if final else ROWS
    tile = lambda w: pl.BlockSpec((TM, w), lambda i, e, f: (i, 0))
    in_specs = [tile(D), tile(D),
                pl.BlockSpec((1, 1, N_MOD * D), lambda i, e, f: (_mod_row(i), 0, 0)),
                pl.BlockSpec((1, D, TF), lambda i, e, f: (e0 + e, 0, f)),
                pl.BlockSpec((1, D, TF), lambda i, e, f: (e0 + e, 0, f)),
                pl.BlockSpec((1, TF, D), lambda i, e, f: (e0 + e, f, 0))]
    args = [h, x, mod_l, w1, w3, w2]
    if routed:
        in_specs.append(tile(LANE))
        args.append(rw)
    if final:
        in_specs.append(pl.BlockSpec((1, D), lambda i, e, f: (0, 0)))
        args.append(final_g.reshape(1, D))
    return pl.pallas_call(
        functools.partial(_ffn_kernel, routed=routed, final=final),
        grid=(rows // TM, ne, D_FF // TF),
        in_specs=in_specs,
        out_specs=tile(D),
        out_shape=jax.ShapeDtypeStruct((rows, D), F32),
        scratch_shapes=[pltpu.VMEM((TM, D), F32)],
        compiler_params=_params("parallel", "arbitrary", "arbitrary"),
        name="moe" if routed else "ffn",
    )(*args)


def kernel(x, c, ctx, c_ctx, w_mod, b_mod, norm1_g, norm2_g, w_in, rpb, hy_conv_w, hy_conv_b, hy_w1, hy_b1, hy_w2, hy_b2, hy_w3, hy_freq, hy_bias, w_pa, w_pb, w_o, ffn_w1, ffn_w3, ffn_w2, moe_router, moe_w1, moe_w3, moe_w2, final_g):
    xs = jnp.concatenate([x.reshape(LAT_ROWS, D), ctx.reshape(CTX_ROWS, D)], axis=0)
    mod = _modulation(c, c_ctx, w_mod, b_mod)
    tabs = _fft_tables()
    ctabs = _ctx_tables()
    filt_lat = _implicit_filters(SEQ, hy_w1, hy_b1, hy_w2, hy_b2, hy_w3, hy_freq)
    filt_ctx = _implicit_filters(CTX, hy_w1, hy_b1, hy_w2, hy_b2, hy_w3, hy_freq)
    spec_re, spec_im = _filter_spectra(filt_lat, tabs)
    nblk = HY_W // LANE

    moe_w = [w.reshape((-1,) + w.shape[2:]) for w in (moe_w1, moe_w3, moe_w2)]

    for l in range(DEPTH):
        last = l == DEPTH - 1
        mod_l = mod[l].reshape(MOD_ROWS, 1, N_MOD * D)
        cw, cb = hy_conv_w[l], hy_conv_b[l].reshape(1, 3 * HY_W)
        p = _in_proj(xs, mod_l, norm1_g[l].reshape(1, D), w_in, l)

        o_att = jnp.concatenate([_neighbourhood_attention(p, rpb[l]), _context_attention(p)], axis=0)
        z = _hyena_conv(p, COL_HV, p, COL_HX1, cw, cb, 0, nblk, spec_re, spec_im, l, 0,
                        hy_bias[l], tabs, True)
        o_hy = _hyena_conv(z, 0, p, COL_HX2, cw, cb, 0, 2 * nblk, spec_re, spec_im, l, 1,
                           hy_bias[l], tabs, False)
        o_hy = jnp.concatenate([o_hy, _ctx_hyena(p, cw, cb, filt_ctx, l, hy_bias[l], ctabs)], axis=0)

        i = l // 2
        router = moe_router[i] if l % 2 else None
        res = _merge(o_att, o_hy, p, xs, mod_l, norm2_g[l].reshape(1, D), w_pa, w_pb, w_o, l, router)
        fg = final_g if last else None
        if l % 2 == 0:
            xs = _channel_mixer(res[1], res[0], mod_l, ffn_w1, ffn_w3, ffn_w2, i, 1, final_g=fg)
        else:
            xs = _channel_mixer(res[1], res[0], mod_l, *moe_w, i * N_EXPERTS, N_EXPERTS,
                                rw=res[2], final_g=fg)
    return xs.reshape(NB, SEQ, D)
```

```python
import functools
import math

import numpy as np
import jax
import jax.numpy as jnp
from jax import lax
from jax.experimental import pallas as pl
from jax.experimental.pallas import tpu as pltpu

F32 = jnp.float32
BF16 = jnp.bfloat16
HIGHEST = lax.Precision.HIGHEST

D = 1024
NB = 4
SEQ = 4096
DEPTH = 4
CTX = 256
GRID_W = 64
GRID_H = SEQ // GRID_W
N_HEADS = 8
HEAD_DIM = 64
ATT_W = N_HEADS * HEAD_DIM
WIN_ROWS = 8
WIN_COLS = 16
HY_W = 512
HY_ORDER = 2
HY_SHORT = 3
HY_EMB = 33
HY_BANDS = (HY_EMB - 1) // 2
HY_FILT = 64
HY_DECAY_TARGET = 1e-2
HY_MAX_DECAY = math.log(HY_DECAY_TARGET) / 0.3
HY_MIN_DECAY = math.log(HY_DECAY_TARGET) / 1.5
HY_DECAY_SHIFT = 0.05
PROJ_W = 3 * ATT_W + 3 * HY_W + 2 * D
D_FF = 2816
N_EXPERTS = 8
N_MOD = 6
EPS = 1e-6

LAT_ROWS = NB * SEQ
CTX_ROWS = NB * CTX
ROWS = LAT_ROWS + CTX_ROWS
MOD_ROWS = 8
CTX_MOD_ROW = NB

LANE = 128
TM = 1024
VMEM_LIMIT = 56 * 1024 * 1024

COL_Q, COL_K, COL_V = 0, ATT_W // LANE, 2 * ATT_W // LANE
COL_HV = 3 * ATT_W // LANE
COL_HX1 = COL_HV + HY_W // LANE
COL_HX2 = COL_HX1 + HY_W // LANE
COL_GA = (3 * ATT_W + 3 * HY_W) // D
COL_GB = COL_GA + 1


def _mod_row(i):
    return jnp.where(i < LAT_ROWS // TM, (i * TM) // SEQ, CTX_MOD_ROW)


def _params(*sem):
    return pltpu.CompilerParams(dimension_semantics=sem, vmem_limit_bytes=VMEM_LIMIT)


def _mod_kernel(s_ref, w_ref, b_ref, o_ref):
    s = s_ref[...]
    s = s * jax.nn.sigmoid(s)
    o_ref[0] = jnp.dot(s, w_ref[0], precision=HIGHEST, preferred_element_type=F32) + b_ref[0]


def _modulation(c, c_ctx, w_mod, b_mod):
    s = jnp.zeros((MOD_ROWS, D), F32).at[:NB].set(c).at[CTX_MOD_ROW].set(c_ctx)
    tn = 1536
    return pl.pallas_call(
        _mod_kernel,
        grid=(DEPTH, N_MOD * D // tn),
        in_specs=[pl.BlockSpec((MOD_ROWS, D), lambda l, j: (0, 0)),
                  pl.BlockSpec((1, D, tn), lambda l, j: (l, 0, j)),
                  pl.BlockSpec((1, 1, tn), lambda l, j: (l, 0, j))],
        out_specs=pl.BlockSpec((1, MOD_ROWS, tn), lambda l, j: (l, 0, j)),
        out_shape=jax.ShapeDtypeStruct((DEPTH, MOD_ROWS, N_MOD * D), F32),
        compiler_params=_params("parallel", "parallel"),
        name="modulation",
    )(s, w_mod, b_mod.reshape(DEPTH, 1, N_MOD * D))


def _norm_mod(x, g, shift, scale):
    y = x * lax.rsqrt(jnp.mean(x * x, axis=-1, keepdims=True) + EPS) * g
    return y * (1.0 + scale) + shift


def _inproj_kernel(x_ref, mod_ref, g_ref, w_ref, o_ref, h_scr):
    @pl.when(pl.program_id(1) == 0)
    def _():
        m = mod_ref[0]
        h = _norm_mod(x_ref[...], g_ref[...], m[:, 0:D], m[:, D:2 * D])
        h_scr[...] = h.astype(BF16)

    o_ref[...] = jnp.dot(h_scr[...], w_ref[0].astype(BF16),
                         preferred_element_type=F32).astype(BF16)


def _in_proj(x, mod_l, g, w, layer):
    tn = 512
    return pl.pallas_call(
        _inproj_kernel,
        grid=(ROWS // TM, PROJ_W // tn),
        in_specs=[pl.BlockSpec((TM, D), lambda i, j: (i, 0)),
                  pl.BlockSpec((1, 1, N_MOD * D), lambda i, j: (_mod_row(i), 0, 0)),
                  pl.BlockSpec((1, D), lambda i, j: (0, 0)),
                  pl.BlockSpec((1, D, tn), lambda i, j: (layer, 0, j))],
        out_specs=pl.BlockSpec((TM, tn), lambda i, j: (i, j)),
        out_shape=jax.ShapeDtypeStruct((ROWS, PROJ_W), BF16),
        scratch_shapes=[pltpu.VMEM((TM, D), BF16)],
        compiler_params=_params("parallel", "arbitrary"),
        name="in_proj",
    )(x, mod_l, g, w)


NEG = -1e30


def _nt_dot(a, b):
    return lax.dot_general(a, b, (((1,), (1,)), ((), ())), preferred_element_type=F32)


def _attend_pair(qp, keys, values, biases):
    lane = lax.broadcasted_iota(jnp.int32, qp.shape, 1)
    qs = qp * jnp.asarray(HEAD_DIM ** -0.5, BF16)
    outs = []
    for a in range(2):
        sel = (lane < HEAD_DIM) if a == 0 else (lane >= HEAD_DIM)
        qa = jnp.where(sel, qs, jnp.zeros_like(qs))
        ss = []
        for k, b in zip(keys, biases):
            s = _nt_dot(qa, k)
            if b is not None:
                s = s + b[a]
            ss.append(s)
        m = functools.reduce(jnp.maximum, [jnp.max(s, axis=-1, keepdims=True) for s in ss])
        ps = [jnp.exp(s - m) for s in ss]
        den = functools.reduce(jnp.add, [jnp.sum(p, axis=-1, keepdims=True) for p in ps])
        o = functools.reduce(jnp.add, [jnp.dot(p.astype(BF16), v, preferred_element_type=F32)
                                       for p, v in zip(ps, values)])
        outs.append(o / den)
    return jnp.where(lane < HEAD_DIM, outs[0], outs[1])


NLOC = WIN_ROWS * GRID_W
NKEY = NLOC + CTX
NA_STEPS = GRID_H // 2 + 1


def _window_start(r):
    r0 = jnp.clip(r - WIN_ROWS // 2, 0, GRID_H - WIN_ROWS)
    return pl.multiple_of(r0 * GRID_W, GRID_W)


def _natt_kernel(q_ref, k_ref, v_ref, kc_ref, vc_ref, bt0_ref, bt1_ref, o_ref,
                 s0, s1, p0, p1, l0, l1):
    j = pl.program_id(1)
    lane = lax.broadcasted_iota(jnp.int32, (GRID_W, LANE), 1)
    lo = lane < HEAD_DIM

    @pl.when(j == 0)
    def _():
        s1[...] = jnp.zeros_like(s1)
        p0[...] = jnp.zeros_like(p0)
        p1[...] = jnp.zeros_like(p1)
        l0[...] = jnp.ones_like(l0)
        l1[...] = jnp.ones_like(l1)

    def scores(half, r, s, bt_ref):
        start = _window_start(r)
        rows = slice(half * GRID_W, (half + 1) * GRID_W)
        for hp in range(N_HEADS // 2):
            cs = slice(hp * LANE, (hp + 1) * LANE)
            qs = q_ref[rows, cs] * jnp.asarray(HEAD_DIM ** -0.5, BF16)
            kw = k_ref[pl.ds(start, NLOC), cs]
            kc = kc_ref[:, cs]
            for a in range(2):
                qa = jnp.where(lo if a == 0 else ~lo, qs, jnp.zeros_like(qs))
                u = 2 * hp + a
                s[u, :, 0:NLOC] = _nt_dot(qa, kw) + bt_ref[0, u]
                s[u, :, NLOC:NKEY] = _nt_dot(qa, kc)

    def softmax(s, p, l):
        for u in range(N_HEADS):
            x = s[u]
            e = jnp.exp(x - jnp.max(x, axis=-1, keepdims=True))
            l[u] = jnp.broadcast_to(jnp.sum(e, axis=-1, keepdims=True), (GRID_W, LANE))
            p[u] = e.astype(BF16)

    def output(half, r, p, l):
        start = _window_start(r)
        rows = slice(half * GRID_W, (half + 1) * GRID_W)
        for hp in range(N_HEADS // 2):
            cs = slice(hp * LANE, (hp + 1) * LANE)
            vw = v_ref[pl.ds(start, NLOC), cs]
            vc = vc_ref[:, cs]
            o = []
            for u in (2 * hp, 2 * hp + 1):
                acc = (jnp.dot(p[u, :, 0:NLOC], vw, preferred_element_type=F32)
                       + jnp.dot(p[u, :, NLOC:NKEY], vc, preferred_element_type=F32))
                o.append(acc / l[u])
            o_ref[rows, cs] = jnp.where(lo, o[0], o[1]).astype(BF16)

    ra = jnp.minimum(2 * j, GRID_H - 2)
    rc = jnp.maximum(2 * j - 2, 0)
    softmax(s1, p1, l1)
    scores(0, ra, s0, bt0_ref)
    output(0, rc, p0, l0)
    softmax(s0, p0, l0)
    scores(1, ra + 1, s1, bt1_ref)
    output(1, rc + 1, p1, l1)


def _bias_tables(rpb_l):
    col = np.arange(GRID_W)
    c0 = np.clip(col - WIN_COLS // 2, 0, GRID_W - WIN_COLS)
    valid = (col[None, :] >= c0[:, None]) & (col[None, :] < c0[:, None] + WIN_COLS)
    dc = np.clip(col[None, :] - col[:, None] + (WIN_COLS - 1), 0, 2 * WIN_COLS - 2)
    t = rpb_l[:, :, dc]
    t = jnp.where(valid[None, None], t, NEG)
    tabs = []
    for d0 in range(WIN_ROWS):
        w = t[:, d0:d0 + WIN_ROWS]
        tabs.append(w.transpose(0, 2, 1, 3).reshape(N_HEADS, GRID_W, WIN_ROWS * GRID_W))
    return jnp.stack(tabs)


def _bias_index(r):
    r0 = jnp.clip(r - WIN_ROWS // 2, 0, GRID_H - WIN_ROWS)
    return r0 - r + (WIN_ROWS - 1)


def _neighbourhood_attention(p, rpb_l):
    bt = _bias_tables(rpb_l)
    blk = GRID_H // 2
    ctx_blk0 = LAT_ROWS // CTX
    score_row = lambda j, half: jnp.minimum(2 * j, GRID_H - 2) + half
    bias_spec = lambda half: pl.BlockSpec((1, N_HEADS, GRID_W, NLOC),
                                          lambda b, j: (_bias_index(score_row(j, half)), 0, 0, 0))
    return pl.pallas_call(
        _natt_kernel,
        grid=(NB, NA_STEPS),
        in_specs=[pl.BlockSpec((2 * GRID_W, ATT_W), lambda b, j: (b * blk + jnp.minimum(j, blk - 1), 0)),
                  pl.BlockSpec((SEQ, ATT_W), lambda b, j: (b, 1)),
                  pl.BlockSpec((SEQ, ATT_W), lambda b, j: (b, 2)),
                  pl.BlockSpec((CTX, ATT_W), lambda b, j: (ctx_blk0 + b, 1)),
                  pl.BlockSpec((CTX, ATT_W), lambda b, j: (ctx_blk0 + b, 2)),
                  bias_spec(0), bias_spec(1)],
        out_specs=pl.BlockSpec((2 * GRID_W, ATT_W), lambda b, j: (b * blk + jnp.maximum(j - 1, 0), 0)),
        out_shape=jax.ShapeDtypeStruct((LAT_ROWS, ATT_W), BF16),
        scratch_shapes=[pltpu.VMEM((N_HEADS, GRID_W, NKEY), F32), pltpu.VMEM((N_HEADS, GRID_W, NKEY), F32),
                        pltpu.VMEM((N_HEADS, GRID_W, NKEY), BF16), pltpu.VMEM((N_HEADS, GRID_W, NKEY), BF16),
                        pltpu.VMEM((N_HEADS, GRID_W, LANE), F32), pltpu.VMEM((N_HEADS, GRID_W, LANE), F32)],
        compiler_params=_params("parallel", "arbitrary"),
        name="neighbourhood_attention",
    )(p, p, p, p, p, bt, bt)


def _catt_kernel(q_ref, k_ref, v_ref, o_ref):
    for hp in range(N_HEADS // 2):
        cs = slice(hp * LANE, (hp + 1) * LANE)
        o = _attend_pair(q_ref[:, cs], [k_ref[:, cs]], [v_ref[:, cs]], [None])
        o_ref[:, cs] = o.astype(BF16)


def _context_attention(p):
    ctx_blk0 = LAT_ROWS // CTX
    return pl.pallas_call(
        _catt_kernel,
        grid=(NB,),
        in_specs=[pl.BlockSpec((CTX, ATT_W), lambda b: (ctx_blk0 + b, 0)),
                  pl.BlockSpec((CTX, ATT_W), lambda b: (ctx_blk0 + b, 1)),
                  pl.BlockSpec((CTX, ATT_W), lambda b: (ctx_blk0 + b, 2))],
        out_specs=pl.BlockSpec((CTX, ATT_W), lambda b: (b, 0)),
        out_shape=jax.ShapeDtypeStruct((CTX_ROWS, ATT_W), BF16),
        compiler_params=_params("parallel"),
        name="context_attention",
    )(p, p, p)


def _filter_features(L):
    pos = np.arange(L, dtype=np.float64)
    t = pos / max(L - 1, 1)
    bands = np.linspace(1e-4, HY_BANDS - 1, HY_BANDS)
    ang = (2.0 * math.pi / L) * pos[:, None] * bands[None, :]
    z = np.concatenate([t[:, None], np.cos(ang), -np.sin(ang)], axis=-1)
    zp = np.zeros((L, LANE), np.float32)
    zp[:, :HY_EMB] = z
    return zp


def _filter_kernel(z_ref, w1_ref, b1_ref, w2_ref, b2_ref, w3_ref, fr_ref, rate_ref, o_ref, *, L, tl):
    dot = functools.partial(jnp.dot, precision=HIGHEST, preferred_element_type=F32)
    fr = fr_ref[0]
    a = jnp.sin(fr[0:1] * (dot(z_ref[...], w1_ref[0]) + b1_ref[0]))
    a = jnp.sin(fr[1:2] * (dot(a, w2_ref[0]) + b2_ref[0]))
    h = dot(a, w3_ref[0])
    pos = (pl.program_id(1) * tl + lax.broadcasted_iota(jnp.int32, (tl, 1), 0)).astype(F32)
    dist = jnp.abs(pos - float(L // 2)) * (2.0 / L)
    o_ref[0] = h * (jnp.exp(-dist * rate_ref[...]) + HY_DECAY_SHIFT)


def _implicit_filters(L, w1, b1, w2, b2, w3, freq):
    tl = min(L, 512)
    z = jnp.asarray(_filter_features(L))
    w1p = jnp.zeros((DEPTH, LANE, HY_FILT), F32).at[:, :HY_EMB].set(w1)
    rates = np.abs(np.linspace(HY_MIN_DECAY, HY_MAX_DECAY, HY_ORDER * HY_W)).astype(np.float32)
    cw = HY_ORDER * HY_W
    return pl.pallas_call(
        functools.partial(_filter_kernel, L=L, tl=tl),
        grid=(DEPTH, L // tl),
        in_specs=[pl.BlockSpec((tl, LANE), lambda l, i: (i, 0)),
                  pl.BlockSpec((1, LANE, HY_FILT), lambda l, i: (l, 0, 0)),
                  pl.BlockSpec((1, 1, HY_FILT), lambda l, i: (l, 0, 0)),
                  pl.BlockSpec((1, HY_FILT, HY_FILT), lambda l, i: (l, 0, 0)),
                  pl.BlockSpec((1, 1, HY_FILT), lambda l, i: (l, 0, 0)),
                  pl.BlockSpec((1, HY_FILT, cw), lambda l, i: (l, 0, 0)),
                  pl.BlockSpec((1, 2, HY_FILT), lambda l, i: (l, 0, 0)),
                  pl.BlockSpec((1, cw), lambda l, i: (0, 0))],
        out_specs=pl.BlockSpec((1, tl, cw), lambda l, i: (l, i, 0)),
        out_shape=jax.ShapeDtypeStruct((DEPTH, L, cw), F32),
        compiler_params=_params("parallel", "parallel"),
        name=f"implicit_filters_{L}",
    )(z, w1p, b1.reshape(DEPTH, 1, HY_FILT), w2, b2.reshape(DEPTH, 1, HY_FILT), w3, freq,
      jnp.asarray(rates).reshape(1, cw))


FFT_N = 2 * SEQ
FA = 64
FB = 128
FA_IN = SEQ // FB
PITCH = 136
HALO = 8
UNROLL_ROWS = 16
UNROLL_SLABS = 8


def _fft_tables():
    a = np.arange(FA_IN)
    r = np.arange(FA)
    b = np.arange(FB)
    q = np.arange(FB)
    ph = a[None, None, :] * r[None, :, None] / FA + b[:, None, None] * r[None, :, None] / FFT_N
    g1 = np.exp(-2j * np.pi * ph)
    g1 = np.concatenate([g1.real, g1.imag], axis=1)
    f = np.exp(-2j * np.pi * q[:, None] * b[None, :] / FB)
    f2 = np.concatenate([f.real, f.imag], axis=0)
    f2c = np.concatenate([f.real, -f.imag], axis=0)
    ao = np.arange(FA_IN) + (SEQ // 2) // FB
    ph2 = ao[None, :, None] * r[None, None, :] / FA + b[:, None, None] * r[None, None, :] / FFT_N
    g2 = np.exp(2j * np.pi * ph2) / FFT_N
    g2 = np.concatenate([g2.real, g2.imag], axis=1)
    return tuple(jnp.asarray(t, dtype=BF16) for t in (g1, f2, f2c, g2))


def _cplx(out, m, n):
    re = out[0:m, 0:n] - out[m:2 * m, n:2 * n]
    im = out[0:m, n:2 * n] + out[m:2 * m, 0:n]
    return re, im


def _short_conv_slabs(src_ref, rows, t_scr, cw_ref, cb_ref, apply_conv, emit):
    nslab = rows // FB
    if not apply_conv:
        for a in range(nslab):
            emit(a, src_ref[a * FB:(a + 1) * FB, :].astype(F32))
        return
    zero = jnp.zeros((HALO, LANE), F32)
    t_scr[0:HALO, :] = zero
    t_scr[HALO + rows:2 * HALO + rows, :] = zero
    for a in range(nslab):
        t_scr[HALO + a * FB:HALO + (a + 1) * FB, :] = src_ref[a * FB:(a + 1) * FB, :].astype(F32)
    w0, w1, w2, bias = cw_ref[0:1, :], cw_ref[1:2, :], cw_ref[2:3, :], cb_ref[...]
    for a in range(nslab):
        o = HALO + a * FB
        y = (w0 * t_scr[o - 1:o - 1 + FB, :] + w1 * t_scr[o:o + FB, :]
             + w2 * t_scr[o + 1:o + 1 + FB, :] + bias)
        emit(a, y)


def _fwd_stage1(b, s_refs, g1_ref, y_r, y_i):
    cols = [s[pl.ds(b, FA_IN, stride=PITCH), :].astype(BF16) for s in s_refs]
    rhs = cols[0] if len(cols) == 1 else jnp.concatenate(cols, axis=1)
    out = jnp.dot(g1_ref[b], rhs, preferred_element_type=F32)
    if len(cols) == 1:
        re, im = out[0:FA], out[FA:2 * FA]
    else:
        re, im = _cplx(out, FA, LANE)
    y_r[pl.ds(b, FA, stride=PITCH), :] = re
    y_i[pl.ds(b, FA, stride=PITCH), :] = im


def _fwd_stage2(r, y_r, y_i, f_ref):
    o = pl.multiple_of(r * PITCH, 8)
    rhs = jnp.concatenate([y_r[pl.ds(o, FB), :].astype(BF16), y_i[pl.ds(o, FB), :].astype(BF16)], axis=1)
    out = jnp.dot(f_ref[...], rhs, preferred_element_type=F32)
    return _cplx(out, FB, LANE)


def _spectrum_kernel(h_ref, g1_ref, f_ref, hr_ref, hi_ref, s_scr, y_r, y_i):
    for a in range(FA_IN):
        s_scr[a * PITCH:a * PITCH + FB, :] = h_ref[0, a * FB:(a + 1) * FB, :]

    def s1(b, c):
        _fwd_stage1(b, [s_scr], g1_ref, y_r, y_i)
        return c
    lax.fori_loop(0, FB, s1, 0, unroll=UNROLL_ROWS)

    def s2(r, c):
        xr, xi = _fwd_stage2(r, y_r, y_i, f_ref)
        o = pl.multiple_of(r * FB, FB)
        hr_ref[0, pl.ds(o, FB), :] = xr
        hi_ref[0, pl.ds(o, FB), :] = xi
        return c
    lax.fori_loop(0, FA, s2, 0, unroll=UNROLL_SLABS)


def _filter_spectra(h, tabs):
    g1, f2, _, _ = tabs
    cw = HY_ORDER * HY_W
    spec = pl.BlockSpec((1, FFT_N, LANE), lambda l, c: (l, 0, c))
    return pl.pallas_call(
        _spectrum_kernel,
        grid=(DEPTH, cw // LANE),
        in_specs=[pl.BlockSpec((1, SEQ, LANE), lambda l, c: (l, 0, c)),
                  pl.BlockSpec(g1.shape, lambda l, c: (0, 0, 0)),
                  pl.BlockSpec(f2.shape, lambda l, c: (0, 0))],
        out_specs=[spec, spec],
        out_shape=[jax.ShapeDtypeStruct((DEPTH, FFT_N, cw), F32)] * 2,
        scratch_shapes=[pltpu.VMEM((FA_IN * PITCH, LANE), F32),
                        pltpu.VMEM((FA * PITCH, LANE), F32),
                        pltpu.VMEM((FA * PITCH, LANE), F32)],
        compiler_params=_params("parallel", "parallel"),
        name="filter_spectra",
    )(h, g1, f2)


def _hyconv_kernel(u_ref, x_ref, cwu_ref, cbu_ref, cwx_ref, cbx_ref,
                   hr_ref, hi_ref, bias_ref, g1_ref, f_ref, fc_ref, g2_ref,
                   o_ref, t_scr, s_r, s_i, y_r, y_i, *, conv_input):
    halves = [(s_r, slice(0, SEQ)), (s_i, slice(SEQ, 2 * SEQ))]

    def fill(s):
        def emit(a, slab):
            s[a * PITCH:a * PITCH + FB, :] = slab
        return emit
    for s, rows in halves:
        _short_conv_slabs(u_ref.at[rows, :], SEQ, t_scr, cwu_ref, cbu_ref, conv_input, fill(s))

    def s1(b, c):
        _fwd_stage1(b, [s_r, s_i], g1_ref, y_r, y_i)
        return c
    lax.fori_loop(0, FB, s1, 0, unroll=UNROLL_ROWS)

    def s2(r, c):
        xr, xi = _fwd_stage2(r, y_r, y_i, f_ref)
        oh = pl.multiple_of(r * FB, FB)
        hr = hr_ref[0, pl.ds(oh, FB), :]
        hi = hi_ref[0, pl.ds(oh, FB), :]
        vr = xr * hr - xi * hi
        vi = xr * hi + xi * hr
        rhs = jnp.concatenate([vr.astype(BF16), vi.astype(BF16)], axis=1)
        wr, wi = _cplx(jnp.dot(fc_ref[...], rhs, preferred_element_type=F32), FB, LANE)
        o = pl.multiple_of(r * PITCH, 8)
        y_r[pl.ds(o, FB), :] = wr
        y_i[pl.ds(o, FB), :] = wi
        return c
    lax.fori_loop(0, FA, s2, 0, unroll=UNROLL_SLABS)

    bias = bias_ref[0]

    def s3(b, c):
        rhs = jnp.concatenate([y_r[pl.ds(b, FA, stride=PITCH), :].astype(BF16),
                               y_i[pl.ds(b, FA, stride=PITCH), :].astype(BF16)], axis=1)
        re, im = _cplx(jnp.dot(g2_ref[b], rhs, preferred_element_type=F32), FA_IN, LANE)
        idx = pl.ds(b, FA_IN, stride=PITCH)
        s_r[idx, :] = re + s_r[idx, :] * bias
        s_i[idx, :] = im + s_i[idx, :] * bias
        return c
    lax.fori_loop(0, FB, s3, 0, unroll=UNROLL_ROWS)

    for s, rows in halves:
        def emit(a, slab, s=s, out=o_ref.at[rows, :]):
            out[a * FB:(a + 1) * FB, :] = (slab * s[a * PITCH:a * PITCH + FB, :]).astype(out.dtype)
        _short_conv_slabs(x_ref.at[rows, :], SEQ, t_scr, cwx_ref, cbx_ref, True, emit)


def _hyena_conv(u, u_col, x, x_col, conv_w, conv_b, cu_col, cx_col, spec_re, spec_im, layer, order,
                hy_bias_l, tabs, conv_input):
    g1, f2, f2c, g2 = tabs
    nblk = HY_W // LANE
    tok = lambda col: pl.BlockSpec((2 * SEQ, LANE), lambda c, p: (p, col + c))
    chan = lambda rows, col: pl.BlockSpec((rows, LANE), lambda c, p: (0, col + c))
    hspec = pl.BlockSpec((1, FFT_N, LANE), lambda c, p: (layer, 0, order * nblk + c))
    const = lambda shape: pl.BlockSpec(shape, lambda c, p: (0,) * len(shape))
    return pl.pallas_call(
        functools.partial(_hyconv_kernel, conv_input=conv_input),
        grid=(nblk, NB // 2),
        in_specs=[tok(u_col), tok(x_col),
                  chan(HY_SHORT, cu_col), chan(1, cu_col), chan(HY_SHORT, cx_col), chan(1, cx_col),
                  hspec, hspec,
                  pl.BlockSpec((1, 1, LANE), lambda c, p: (order, 0, c)),
                  const(g1.shape), const(f2.shape), const(f2c.shape), const(g2.shape)],
        out_specs=pl.BlockSpec((2 * SEQ, LANE), lambda c, p: (p, c)),
        out_shape=jax.ShapeDtypeStruct((LAT_ROWS, HY_W), BF16),
        scratch_shapes=[pltpu.VMEM((SEQ + 2 * HALO, LANE), F32),
                        pltpu.VMEM((FA_IN * PITCH, LANE), F32),
                        pltpu.VMEM((FA_IN * PITCH, LANE), F32),
                        pltpu.VMEM((FA * PITCH, LANE), F32),
                        pltpu.VMEM((FA * PITCH, LANE), F32)],
        compiler_params=_params("arbitrary", "arbitrary"),
        name=f"hyena_conv{order}",
    )(u, x, conv_w, conv_b, conv_w, conv_b, spec_re, spec_im,
      hy_bias_l.reshape(HY_ORDER, 1, HY_W), g1, f2, f2c, g2)


CN = 2 * CTX


def _ctx_tables():
    k = np.arange(CN)
    n = np.arange(CTX)
    f = np.exp(-2j * np.pi * k[:, None] * n[None, :] / CN)
    fwd = np.concatenate([f.real, f.imag], axis=0)
    t = np.arange(CTX) + CTX // 2
    g = np.exp(2j * np.pi * t[:, None] * k[None, :] / CN) / CN
    inv = np.concatenate([g.real, g.imag], axis=0)
    return jnp.asarray(fwd, dtype=BF16), jnp.asarray(inv, dtype=BF16)


def _ctx_hyena_kernel(v_ref, x1_ref, x2_ref, cwv_ref, cbv_ref, cw1_ref, cb1_ref, cw2_ref, cb2_ref,
                      h0_ref, h1_ref, bias_ref, fwd_ref, inv_ref, o_ref, t_scr, s_scr, g_scr):
    def conv_into(src_ref, b, cw_ref, cb_ref, dst, col):
        view = src_ref.at[b * CTX:(b + 1) * CTX, :]

        def emit(a, slab):
            dst[a * FB:(a + 1) * FB, col * LANE:(col + 1) * LANE] = slab
        _short_conv_slabs(view, CTX, t_scr, cw_ref, cb_ref, True, emit)

    def spectrum(h_ref):
        out = jnp.dot(fwd_ref[...], h_ref[0].astype(BF16), preferred_element_type=F32)
        return out[0:CN], out[CN:2 * CN]

    def long_conv(z2, hr, hi, bias):
        x = jnp.dot(fwd_ref[...], z2.astype(BF16), preferred_element_type=F32)
        xr, xi = _cplx(x, CN, LANE)
        vr = xr * hr - xi * hi
        vi = xr * hi + xi * hr
        rhs = jnp.concatenate([vr.astype(BF16), vi.astype(BF16)], axis=1)
        y = jnp.dot(inv_ref[...], rhs, preferred_element_type=F32)
        yr, yi = _cplx(y, CTX, LANE)
        return jnp.concatenate([yr, yi], axis=1) + z2 * jnp.concatenate([bias, bias], axis=1)

    h0r, h0i = spectrum(h0_ref)
    h1r, h1i = spectrum(h1_ref)
    for p in range(NB // 2):
        for half in range(2):
            conv_into(v_ref, 2 * p + half, cwv_ref, cbv_ref, s_scr, half)
            conv_into(x1_ref, 2 * p + half, cw1_ref, cb1_ref, g_scr, half)
        z = g_scr[...] * long_conv(s_scr[...], h0r, h0i, bias_ref[0])
        for half in range(2):
            conv_into(x2_ref, 2 * p + half, cw2_ref, cb2_ref, g_scr, half)
        y = g_scr[...] * long_conv(z, h1r, h1i, bias_ref[1])
        for half in range(2):
            b = 2 * p + half
            o_ref[b * CTX:(b + 1) * CTX, :] = y[:, half * LANE:(half + 1) * LANE].astype(BF16)


def _ctx_hyena(p, conv_w, conv_b, filt_ctx, layer, hy_bias_l, ctabs):
    fwd, inv = ctabs
    nblk = HY_W // LANE
    ctx_blk = LAT_ROWS // CTX_ROWS
    tok = lambda col: pl.BlockSpec((CTX_ROWS, LANE), lambda c: (ctx_blk, col + c))
    chan = lambda rows, col: pl.BlockSpec((rows, LANE), lambda c: (0, col + c))
    hspec = lambda order: pl.BlockSpec((1, CTX, LANE), lambda c: (layer, 0, order * nblk + c))
    const = lambda shape: pl.BlockSpec(shape, lambda c: (0,) * len(shape))
    return pl.pallas_call(
        _ctx_hyena_kernel,
        grid=(nblk,),
        in_specs=[tok(COL_HV), tok(COL_HX1), tok(COL_HX2),
                  chan(HY_SHORT, 0), chan(1, 0), chan(HY_SHORT, nblk), chan(1, nblk),
                  chan(HY_SHORT, 2 * nblk), chan(1, 2 * nblk),
                  hspec(0), hspec(1),
                  pl.BlockSpec((HY_ORDER, 1, LANE), lambda c: (0, 0, c)),
                  const(fwd.shape), const(inv.shape)],
        out_specs=pl.BlockSpec((CTX_ROWS, LANE), lambda c: (0, c)),
        out_shape=jax.ShapeDtypeStruct((CTX_ROWS, HY_W), BF16),
        scratch_shapes=[pltpu.VMEM((CTX + 2 * HALO, LANE), F32),
                        pltpu.VMEM((CTX, 2 * LANE), F32),
                        pltpu.VMEM((CTX, 2 * LANE), F32)],
        compiler_params=_params("parallel"),
        name="context_hyena",
    )(p, p, p, conv_w, conv_b, conv_w, conv_b, conv_w, conv_b, filt_ctx, filt_ctx,
      hy_bias_l.reshape(HY_ORDER, 1, HY_W), fwd, inv)


def _route(logits):
    lane = lax.broadcasted_iota(jnp.int32, logits.shape, 1)
    ninf = jnp.asarray(-jnp.inf, F32)
    lg = jnp.where(lane < N_EXPERTS, logits, ninf)
    m1 = jnp.max(lg, axis=-1, keepdims=True)
    i1 = jnp.min(jnp.where(lg == m1, lane, LANE), axis=-1, keepdims=True)
    lg2 = jnp.where(lane == i1, ninf, lg)
    m2 = jnp.max(lg2, axis=-1, keepdims=True)
    i2 = jnp.min(jnp.where(lg2 == m2, lane, LANE), axis=-1, keepdims=True)
    e = jnp.exp(m2 - m1)
    den = 1.0 + e
    return jnp.where(lane == i1, 1.0 / den, 0.0) + jnp.where(lane == i2, e / den, 0.0)


def _merge_kernel(oa_ref, oh_ref, ga_ref, gb_ref, x_ref, mod_ref, g2_ref, wpa_ref, wpb_ref, wo_ref,
                  *rest, routed):
    if routed:
        wr_ref, xo_ref, h_ref, rw_ref, wpa_s, wpb_s, wo_s = rest
    else:
        xo_ref, h_ref, wpa_s, wpb_s, wo_s = rest

    @pl.when(pl.program_id(0) == 0)
    def _():
        wpa_s[...] = wpa_ref[0].astype(BF16)
        wpb_s[...] = wpb_ref[0].astype(BF16)
        wo_s[...] = wo_ref[0].astype(BF16)

    a = jnp.dot(oa_ref[...], wpa_s[...], preferred_element_type=F32)
    b = jnp.dot(oh_ref[...], wpb_s[...], preferred_element_type=F32)
    mix = (jax.nn.sigmoid(ga_ref[...].astype(F32)) * a + jax.nn.sigmoid(gb_ref[...].astype(F32)) * b)
    y = jnp.dot(mix.astype(BF16), wo_s[...], preferred_element_type=F32)
    m = mod_ref[0]
    x = x_ref[...] + m[:, 2 * D:3 * D] * y
    xo_ref[...] = x
    h = _norm_mod(x, g2_ref[...], m[:, 3 * D:4 * D], m[:, 4 * D:5 * D])
    h_ref[...] = h.astype(BF16)
    if routed:
        logits = jnp.dot(h, wr_ref[...], precision=HIGHEST, preferred_element_type=F32)
        rw_ref[...] = _route(logits)


def _merge(o_att, o_hy, p, x, mod_l, g2, w_pa, w_pb, w_o, layer, router):
    routed = router is not None
    tile = lambda w: pl.BlockSpec((TM, w), lambda i: (i, 0))
    full = lambda a: pl.BlockSpec(a.shape, lambda i: (0,) * a.ndim)
    layered = lambda a: pl.BlockSpec((1,) + a.shape[1:], lambda i: (layer, 0, 0))
    in_specs = [tile(ATT_W), tile(HY_W),
                pl.BlockSpec((TM, D), lambda i: (i, COL_GA)),
                pl.BlockSpec((TM, D), lambda i: (i, COL_GB)),
                tile(D),
                pl.BlockSpec((1, 1, N_MOD * D), lambda i: (_mod_row(i), 0, 0)),
                pl.BlockSpec((1, D), lambda i: (0, 0)),
                layered(w_pa), layered(w_pb), layered(w_o)]
    args = [o_att, o_hy, p, p, x, mod_l, g2, w_pa, w_pb, w_o]
    out_specs = [tile(D), tile(D)]
    out_shape = [jax.ShapeDtypeStruct((ROWS, D), F32), jax.ShapeDtypeStruct((ROWS, D), BF16)]
    if routed:
        wr = jnp.zeros((D, LANE), F32).at[:, :N_EXPERTS].set(router)
        in_specs.append(full(wr))
        args.append(wr)
        out_specs.append(tile(LANE))
        out_shape.append(jax.ShapeDtypeStruct((ROWS, LANE), F32))
    return pl.pallas_call(
        functools.partial(_merge_kernel, routed=routed),
        grid=(ROWS // TM,),
        in_specs=in_specs, out_specs=out_specs, out_shape=out_shape,
        scratch_shapes=[pltpu.VMEM(w_pa.shape[1:], BF16), pltpu.VMEM(w_pb.shape[1:], BF16),
                        pltpu.VMEM(w_o.shape[1:], BF16)],
        compiler_params=_params("arbitrary"),
        name="merge",
    )(*args)


TF = 256


def _ffn_kernel(h_ref, x_ref, mod_ref, w1_ref, w3_ref, w2_ref, *rest, routed, final):
    rest = list(rest)
    rw_ref = rest.pop(0) if routed else None
    fg_ref = rest.pop(0) if final else None
    o_ref, acc = rest
    e = pl.program_id(1)
    f = pl.program_id(2)
    first = (e == 0) & (f == 0)
    last = (e == pl.num_programs(1) - 1) & (f == pl.num_programs(2) - 1)

    @pl.when(first)
    def _():
        acc[...] = jnp.zeros_like(acc)

    h = h_ref[...]
    a = jnp.dot(h, w1_ref[0].astype(BF16), preferred_element_type=F32)
    b = jnp.dot(h, w3_ref[0].astype(BF16), preferred_element_type=F32)
    t = a * jax.nn.sigmoid(a) * b
    if routed:
        lane = lax.broadcasted_iota(jnp.int32, rw_ref.shape, 1)
        t = t * jnp.sum(jnp.where(lane == e, rw_ref[...], 0.0), axis=-1, keepdims=True)
    acc[...] += jnp.dot(t.astype(BF16), w2_ref[0].astype(BF16), preferred_element_type=F32)

    @pl.when(last)
    def _():
        x = x_ref[...] + mod_ref[0][:, 5 * D:6 * D] * acc[...]
        if final:
            x = x * lax.rsqrt(jnp.mean(x * x, axis=-1, keepdims=True) + EPS) * fg_ref[...]
        o_ref[...] = x


def _channel_mixer(h, x, mod_l, w1, w3, w2, e0, ne, rw=None, final_g=None):
    routed = rw is not None
    final = final_g is not None
    rows = LAT_ROWS if final else ROWS
    tile = lambda w: pl.BlockSpec((TM, w), lambda i, e, f: (i, 0))
    in_specs = [tile(D), tile(D),
                pl.BlockSpec((1, 1, N_MOD * D), lambda i, e, f: (_mod_row(i), 0, 0)),
                pl.BlockSpec((1, D, TF), lambda i, e, f: (e0 + e, 0, f)),
                pl.BlockSpec((1, D, TF), lambda i, e, f: (e0 + e, 0, f)),
                pl.BlockSpec((1, TF, D), lambda i, e, f: (e0 + e, f, 0))]
    args = [h, x, mod_l, w1, w3, w2]
    if routed:
        in_specs.append(tile(LANE))
        args.append(rw)
    if final:
        in_specs.append(pl.BlockSpec((1, D), lambda i, e, f: (0, 0)))
        args.append(final_g.reshape(1, D))
    return pl.pallas_call(
        functools.partial(_ffn_kernel, routed=routed, final=final),
        grid=(rows // TM, ne, D_FF // TF),
        in_specs=in_specs,
        out_specs=tile(D),
        out_shape=jax.ShapeDtypeStruct((rows, D), F32),
        scratch_shapes=[pltpu.VMEM((TM, D), F32)],
        compiler_params=_params("parallel", "arbitrary", "arbitrary"),
        name="moe" if routed else "ffn",
    )(*args)


def kernel(x, c, ctx, c_ctx, w_mod, b_mod, norm1_g, norm2_g, w_in, rpb, hy_conv_w, hy_conv_b, hy_w1, hy_b1, hy_w2, hy_b2, hy_w3, hy_freq, hy_bias, w_pa, w_pb, w_o, ffn_w1, ffn_w3, ffn_w2, moe_router, moe_w1, moe_w3, moe_w2, final_g):
    xs = jnp.concatenate([x.reshape(LAT_ROWS, D), ctx.reshape(CTX_ROWS, D)], axis=0)
    mod = _modulation(c, c_ctx, w_mod, b_mod)
    tabs = _fft_tables()
    ctabs = _ctx_tables()
    filt_lat = _implicit_filters(SEQ, hy_w1, hy_b1, hy_w2, hy_b2, hy_w3, hy_freq)
    filt_ctx = _implicit_filters(CTX, hy_w1, hy_b1, hy_w2, hy_b2, hy_w3, hy_freq)
    spec_re, spec_im = _filter_spectra(filt_lat, tabs)
    nblk = HY_W // LANE

    moe_w = [w.reshape((-1,) + w.shape[2:]) for w in (moe_w1, moe_w3, moe_w2)]

    for l in range(DEPTH):
        last = l == DEPTH - 1
        mod_l = mod[l].reshape(MOD_ROWS, 1, N_MOD * D)
        cw, cb = hy_conv_w[l], hy_conv_b[l].reshape(1, 3 * HY_W)
        p = _in_proj(xs, mod_l, norm1_g[l].reshape(1, D), w_in, l)

        o_att = jnp.concatenate([_neighbourhood_attention(p, rpb[l]), _context_attention(p)], axis=0)
        z = _hyena_conv(p, COL_HV, p, COL_HX1, cw, cb, 0, nblk, spec_re, spec_im, l, 0,
                        hy_bias[l], tabs, True)
        o_hy = _hyena_conv(z, 0, p, COL_HX2, cw, cb, 0, 2 * nblk, spec_re, spec_im, l, 1,
                           hy_bias[l], tabs, False)
        o_hy = jnp.concatenate([o_hy, _ctx_hyena(p, cw, cb, filt_ctx, l, hy_bias[l], ctabs)], axis=0)

        i = l // 2
        router = moe_router[i] if l % 2 else None
        res = _merge(o_att, o_hy, p, xs, mod_l, norm2_g[l].reshape(1, D), w_pa, w_pb, w_o, l, router)
        fg = final_g if last else None
        if l % 2 == 0:
            xs = _channel_mixer(res[1], res[0], mod_l, ffn_w1, ffn_w3, ffn_w2, i, 1, final_g=fg)
        else:
            xs = _channel_mixer(res[1], res[0], mod_l, *moe_w, i * N_EXPERTS, N_EXPERTS,
                                rw=res[2], final_g=fg)
    return xs.reshape(NB, SEQ, D)
```

```python
import functools
import math

import numpy as np
import jax
import jax.numpy as jnp
from jax import lax
from jax.experimental import pallas as pl
from jax.experimental.pallas import tpu as pltpu

F32 = jnp.float32
BF16 = jnp.bfloat16
HIGHEST = lax.Precision.HIGHEST

D = 1024
NB = 4
SEQ = 4096
DEPTH = 4
CTX = 256
GRID_W = 64
GRID_H = SEQ // GRID_W
N_HEADS = 8
HEAD_DIM = 64
ATT_W = N_HEADS * HEAD_DIM
WIN_ROWS = 8
WIN_COLS = 16
HY_W = 512
HY_ORDER = 2
HY_SHORT = 3
HY_EMB = 33
HY_BANDS = (HY_EMB - 1) // 2
HY_FILT = 64
HY_DECAY_TARGET = 1e-2
HY_MAX_DECAY = math.log(HY_DECAY_TARGET) / 0.3
HY_MIN_DECAY = math.log(HY_DECAY_TARGET) / 1.5
HY_DECAY_SHIFT = 0.05
PROJ_W = 3 * ATT_W + 3 * HY_W + 2 * D
D_FF = 2816
N_EXPERTS = 8
N_MOD = 6
EPS = 1e-6

LAT_ROWS = NB * SEQ
CTX_ROWS = NB * CTX
ROWS = LAT_ROWS + CTX_ROWS
MOD_ROWS = 8
CTX_MOD_ROW = NB

LANE = 128
TM = 1024
VMEM_LIMIT = 56 * 1024 * 1024

COL_Q, COL_K, COL_V = 0, ATT_W // LANE, 2 * ATT_W // LANE
COL_HV = 3 * ATT_W // LANE
COL_HX1 = COL_HV + HY_W // LANE
COL_HX2 = COL_HX1 + HY_W // LANE
COL_GA = (3 * ATT_W + 3 * HY_W) // D
COL_GB = COL_GA + 1


def _mod_row(i):
    return jnp.where(i < LAT_ROWS // TM, (i * TM) // SEQ, CTX_MOD_ROW)


def _params(*sem):
    return pltpu.CompilerParams(dimension_semantics=sem, vmem_limit_bytes=VMEM_LIMIT)


def _mod_kernel(s_ref, w_ref, b_ref, o_ref):
    s = s_ref[...]
    s = s * jax.nn.sigmoid(s)
    o_ref[0] = jnp.dot(s, w_ref[0], precision=HIGHEST, preferred_element_type=F32) + b_ref[0]


def _modulation(c, c_ctx, w_mod, b_mod):
    s = jnp.zeros((MOD_ROWS, D), F32).at[:NB].set(c).at[CTX_MOD_ROW].set(c_ctx)
    tn = 1536
    return pl.pallas_call(
        _mod_kernel,
        grid=(DEPTH, N_MOD * D // tn),
        in_specs=[pl.BlockSpec((MOD_ROWS, D), lambda l, j: (0, 0)),
                  pl.BlockSpec((1, D, tn), lambda l, j: (l, 0, j)),
                  pl.BlockSpec((1, 1, tn), lambda l, j: (l, 0, j))],
        out_specs=pl.BlockSpec((1, MOD_ROWS, tn), lambda l, j: (l, 0, j)),
        out_shape=jax.ShapeDtypeStruct((DEPTH, MOD_ROWS, N_MOD * D), F32),
        compiler_params=_params("parallel", "parallel"),
        name="modulation",
    )(s, w_mod, b_mod.reshape(DEPTH, 1, N_MOD * D))


def _norm_mod(x, g, shift, scale):
    y = x * lax.rsqrt(jnp.mean(x * x, axis=-1, keepdims=True) + EPS) * g
    return y * (1.0 + scale) + shift


def _inproj_kernel(x_ref, mod_ref, g_ref, w_ref, o_ref, h_scr):
    @pl.when(pl.program_id(1) == 0)
    def _():
        m = mod_ref[0]
        h = _norm_mod(x_ref[...], g_ref[...], m[:, 0:D], m[:, D:2 * D])
        h_scr[...] = h.astype(BF16)

    o_ref[...] = jnp.dot(h_scr[...], w_ref[0].astype(BF16),
                         preferred_element_type=F32).astype(BF16)


def _in_proj(x, mod_l, g, w, layer):
    tn = 512
    return pl.pallas_call(
        _inproj_kernel,
        grid=(ROWS // TM, PROJ_W // tn),
        in_specs=[pl.BlockSpec((TM, D), lambda i, j: (i, 0)),
                  pl.BlockSpec((1, 1, N_MOD * D), lambda i, j: (_mod_row(i), 0, 0)),
                  pl.BlockSpec((1, D), lambda i, j: (0, 0)),
                  pl.BlockSpec((1, D, tn), lambda i, j: (layer, 0, j))],
        out_specs=pl.BlockSpec((TM, tn), lambda i, j: (i, j)),
        out_shape=jax.ShapeDtypeStruct((ROWS, PROJ_W), BF16),
        scratch_shapes=[pltpu.VMEM((TM, D), BF16)],
        compiler_params=_params("parallel", "arbitrary"),
        name="in_proj",
    )(x, mod_l, g, w)


NEG = -1e30


def _nt_dot(a, b):
    return lax.dot_general(a, b, (((1,), (1,)), ((), ())), preferred_element_type=F32)


def _attend_pair(qp, keys, values, biases):
    lane = lax.broadcasted_iota(jnp.int32, qp.shape, 1)
    qs = qp * jnp.asarray(HEAD_DIM ** -0.5, BF16)
    outs = []
    for a in range(2):
        sel = (lane < HEAD_DIM) if a == 0 else (lane >= HEAD_DIM)
        qa = jnp.where(sel, qs, jnp.zeros_like(qs))
        ss = []
        for k, b in zip(keys, biases):
            s = _nt_dot(qa, k)
            if b is not None:
                s = s + b[a]
            ss.append(s)
        m = functools.reduce(jnp.maximum, [jnp.max(s, axis=-1, keepdims=True) for s in ss])
        ps = [jnp.exp(s - m) for s in ss]
        den = functools.reduce(jnp.add, [jnp.sum(p, axis=-1, keepdims=True) for p in ps])
        o = functools.reduce(jnp.add, [jnp.dot(p.astype(BF16), v, preferred_element_type=F32)
                                       for p, v in zip(ps, values)])
        outs.append(o / den)
    return jnp.where(lane < HEAD_DIM, outs[0], outs[1])


NLOC = WIN_ROWS * GRID_W
NKEY = NLOC + CTX
NA_PAIRS = GRID_H // 2
NA_STEPS = NA_PAIRS + 2
NA_UNITS = 2 * N_HEADS


def _window_start(r):
    r0 = jnp.clip(r - WIN_ROWS // 2, 0, GRID_H - WIN_ROWS)
    return pl.multiple_of(r0 * GRID_W, GRID_W)


def _natt_kernel(q_ref, k_ref, v_ref, kc_ref, vc_ref, bt0_ref, bt1_ref, o_ref, s_scr, p_scr, l_scr):
    j = pl.program_id(1)
    cur = j % 2
    prev = 1 - cur
    lane = lax.broadcasted_iota(jnp.int32, (GRID_W, LANE), 1)
    lo = lane < HEAD_DIM
    hi = lane >= HEAD_DIM

    @pl.when(j == 0)
    def _():
        s_scr[...] = jnp.zeros_like(s_scr)
        p_scr[...] = jnp.zeros_like(p_scr)
        l_scr[...] = jnp.ones_like(l_scr)

    ro = jnp.maximum(2 * j - 4, 0)
    for half in range(2):
        start = _window_start(ro + half)
        rows = slice(half * GRID_W, (half + 1) * GRID_W)
        for hp in range(N_HEADS // 2):
            cs = slice(hp * LANE, (hp + 1) * LANE)
            vw = v_ref[pl.ds(start, NLOC), cs]
            vc = vc_ref[:, cs]
            o = []
            for u in (half * N_HEADS + 2 * hp, half * N_HEADS + 2 * hp + 1):
                acc = (jnp.dot(p_scr[cur, u, :, 0:NLOC], vw, preferred_element_type=F32)
                       + jnp.dot(p_scr[cur, u, :, NLOC:NKEY], vc, preferred_element_type=F32))
                o.append(acc / l_scr[cur, u])
            o_ref[rows, cs] = jnp.where(lo, o[0], o[1]).astype(BF16)

    for u in range(NA_UNITS):
        x = s_scr[prev, u]
        e = jnp.exp(x - jnp.max(x, axis=-1, keepdims=True))
        l_scr[prev, u] = jnp.broadcast_to(jnp.sum(e, axis=-1, keepdims=True), (GRID_W, LANE))
        p_scr[prev, u] = e.astype(BF16)

    rs = jnp.minimum(2 * j, GRID_H - 2)
    for half, bt_ref in enumerate((bt0_ref, bt1_ref)):
        start = _window_start(rs + half)
        rows = slice(half * GRID_W, (half + 1) * GRID_W)
        for hp in range(N_HEADS // 2):
            cs = slice(hp * LANE, (hp + 1) * LANE)
            qs = q_ref[rows, cs] * jnp.asarray(HEAD_DIM ** -0.5, BF16)
            kw = k_ref[pl.ds(start, NLOC), cs]
            kc = kc_ref[:, cs]
            for a in range(2):
                qa = jnp.where(lo if a == 0 else hi, qs, jnp.zeros_like(qs))
                u = half * N_HEADS + 2 * hp + a
                s_scr[cur, u, :, 0:NLOC] = _nt_dot(qa, kw) + bt_ref[0, 2 * hp + a]
                s_scr[cur, u, :, NLOC:NKEY] = _nt_dot(qa, kc)


def _bias_tables(rpb_l):
    col = np.arange(GRID_W)
    c0 = np.clip(col - WIN_COLS // 2, 0, GRID_W - WIN_COLS)
    valid = (col[None, :] >= c0[:, None]) & (col[None, :] < c0[:, None] + WIN_COLS)
    dc = np.clip(col[None, :] - col[:, None] + (WIN_COLS - 1), 0, 2 * WIN_COLS - 2)
    t = rpb_l[:, :, dc]
    t = jnp.where(valid[None, None], t, NEG)
    tabs = []
    for d0 in range(WIN_ROWS):
        w = t[:, d0:d0 + WIN_ROWS]
        tabs.append(w.transpose(0, 2, 1, 3).reshape(N_HEADS, GRID_W, WIN_ROWS * GRID_W))
    return jnp.stack(tabs)


def _bias_index(r):
    r0 = jnp.clip(r - WIN_ROWS // 2, 0, GRID_H - WIN_ROWS)
    return r0 - r + (WIN_ROWS - 1)


def _neighbourhood_attention(p, rpb_l):
    bt = _bias_tables(rpb_l)
    blk = GRID_H // 2
    ctx_blk0 = LAT_ROWS // CTX
    score_row = lambda j, half: jnp.minimum(2 * j, GRID_H - 2) + half
    bias_spec = lambda half: pl.BlockSpec((1, N_HEADS, GRID_W, NLOC),
                                          lambda b, j: (_bias_index(score_row(j, half)), 0, 0, 0))
    return pl.pallas_call(
        _natt_kernel,
        grid=(NB, NA_STEPS),
        in_specs=[pl.BlockSpec((2 * GRID_W, ATT_W), lambda b, j: (b * blk + jnp.minimum(j, blk - 1), 0)),
                  pl.BlockSpec((SEQ, ATT_W), lambda b, j: (b, 1)),
                  pl.BlockSpec((SEQ, ATT_W), lambda b, j: (b, 2)),
                  pl.BlockSpec((CTX, ATT_W), lambda b, j: (ctx_blk0 + b, 1)),
                  pl.BlockSpec((CTX, ATT_W), lambda b, j: (ctx_blk0 + b, 2)),
                  bias_spec(0), bias_spec(1)],
        out_specs=pl.BlockSpec((2 * GRID_W, ATT_W), lambda b, j: (b * blk + jnp.maximum(j - 2, 0), 0)),
        out_shape=jax.ShapeDtypeStruct((LAT_ROWS, ATT_W), BF16),
        scratch_shapes=[pltpu.VMEM((2, NA_UNITS, GRID_W, NKEY), F32),
                        pltpu.VMEM((2, NA_UNITS, GRID_W, NKEY), BF16),
                        pltpu.VMEM((2, NA_UNITS, GRID_W, LANE), F32)],
        compiler_params=_params("parallel", "arbitrary"),
        name="neighbourhood_attention",
    )(p, p, p, p, p, bt, bt)


def _catt_kernel(q_ref, k_ref, v_ref, o_ref):
    for hp in range(N_HEADS // 2):
        cs = slice(hp * LANE, (hp + 1) * LANE)
        o = _attend_pair(q_ref[:, cs], [k_ref[:, cs]], [v_ref[:, cs]], [None])
        o_ref[:, cs] = o.astype(BF16)


def _context_attention(p):
    ctx_blk0 = LAT_ROWS // CTX
    return pl.pallas_call(
        _catt_kernel,
        grid=(NB,),
        in_specs=[pl.BlockSpec((CTX, ATT_W), lambda b: (ctx_blk0 + b, 0)),
                  pl.BlockSpec((CTX, ATT_W), lambda b: (ctx_blk0 + b, 1)),
                  pl.BlockSpec((CTX, ATT_W), lambda b: (ctx_blk0 + b, 2))],
        out_specs=pl.BlockSpec((CTX, ATT_W), lambda b: (b, 0)),
        out_shape=jax.ShapeDtypeStruct((CTX_ROWS, ATT_W), BF16),
        compiler_params=_params("parallel"),
        name="context_attention",
    )(p, p, p)


def _filter_features(L):
    pos = np.arange(L, dtype=np.float64)
    t = pos / max(L - 1, 1)
    bands = np.linspace(1e-4, HY_BANDS - 1, HY_BANDS)
    ang = (2.0 * math.pi / L) * pos[:, None] * bands[None, :]
    z = np.concatenate([t[:, None], np.cos(ang), -np.sin(ang)], axis=-1)
    zp = np.zeros((L, LANE), np.float32)
    zp[:, :HY_EMB] = z
    return zp


def _filter_kernel(z_ref, w1_ref, b1_ref, w2_ref, b2_ref, w3_ref, fr_ref, rate_ref, o_ref, *, L, tl):
    dot = functools.partial(jnp.dot, precision=HIGHEST, preferred_element_type=F32)
    fr = fr_ref[0]
    a = jnp.sin(fr[0:1] * (dot(z_ref[...], w1_ref[0]) + b1_ref[0]))
    a = jnp.sin(fr[1:2] * (dot(a, w2_ref[0]) + b2_ref[0]))
    h = dot(a, w3_ref[0])
    pos = (pl.program_id(1) * tl + lax.broadcasted_iota(jnp.int32, (tl, 1), 0)).astype(F32)
    dist = jnp.abs(pos - float(L // 2)) * (2.0 / L)
    o_ref[0] = h * (jnp.exp(-dist * rate_ref[...]) + HY_DECAY_SHIFT)


def _implicit_filters(L, w1, b1, w2, b2, w3, freq):
    tl = min(L, 512)
    z = jnp.asarray(_filter_features(L))
    w1p = jnp.zeros((DEPTH, LANE, HY_FILT), F32).at[:, :HY_EMB].set(w1)
    rates = np.abs(np.linspace(HY_MIN_DECAY, HY_MAX_DECAY, HY_ORDER * HY_W)).astype(np.float32)
    cw = HY_ORDER * HY_W
    return pl.pallas_call(
        functools.partial(_filter_kernel, L=L, tl=tl),
        grid=(DEPTH, L // tl),
        in_specs=[pl.BlockSpec((tl, LANE), lambda l, i: (i, 0)),
                  pl.BlockSpec((1, LANE, HY_FILT), lambda l, i: (l, 0, 0)),
                  pl.BlockSpec((1, 1, HY_FILT), lambda l, i: (l, 0, 0)),
                  pl.BlockSpec((1, HY_FILT, HY_FILT), lambda l, i: (l, 0, 0)),
                  pl.BlockSpec((1, 1, HY_FILT), lambda l, i: (l, 0, 0)),
                  pl.BlockSpec((1, HY_FILT, cw), lambda l, i: (l, 0, 0)),
                  pl.BlockSpec((1, 2, HY_FILT), lambda l, i: (l, 0, 0)),
                  pl.BlockSpec((1, cw), lambda l, i: (0, 0))],
        out_specs=pl.BlockSpec((1, tl, cw), lambda l, i: (l, i, 0)),
        out_shape=jax.ShapeDtypeStruct((DEPTH, L, cw), F32),
        compiler_params=_params("parallel", "parallel"),
        name=f"implicit_filters_{L}",
    )(z, w1p, b1.reshape(DEPTH, 1, HY_FILT), w2, b2.reshape(DEPTH, 1, HY_FILT), w3, freq,
      jnp.asarray(rates).reshape(1, cw))


FFT_N = 2 * SEQ
FA = 64
FB = 128
FA_IN = SEQ // FB
PITCH = 136
HALO = 8
UNROLL_ROWS = 16
UNROLL_SLABS = 8


def _fft_tables():
    a = np.arange(FA_IN)
    r = np.arange(FA)
    b = np.arange(FB)
    q = np.arange(FB)
    ph = a[None, None, :] * r[None, :, None] / FA + b[:, None, None] * r[None, :, None] / FFT_N
    g1 = np.exp(-2j * np.pi * ph)
    g1 = np.concatenate([g1.real, g1.imag], axis=1)
    f = np.exp(-2j * np.pi * q[:, None] * b[None, :] / FB)
    f2 = np.concatenate([f.real, f.imag], axis=0)
    f2c = np.concatenate([f.real, -f.imag], axis=0)
    ao = np.arange(FA_IN) + (SEQ // 2) // FB
    ph2 = ao[None, :, None] * r[None, None, :] / FA + b[:, None, None] * r[None, None, :] / FFT_N
    g2 = np.exp(2j * np.pi * ph2) / FFT_N
    g2 = np.concatenate([g2.real, g2.imag], axis=1)
    return tuple(jnp.asarray(t, dtype=BF16) for t in (g1, f2, f2c, g2))


def _cplx(out, m, n):
    re = out[0:m, 0:n] - out[m:2 * m, n:2 * n]
    im = out[0:m, n:2 * n] + out[m:2 * m, 0:n]
    return re, im


def _short_conv_slabs(src_ref, rows, t_scr, cw_ref, cb_ref, apply_conv, emit):
    nslab = rows // FB
    if not apply_conv:
        for a in range(nslab):
            emit(a, src_ref[a * FB:(a + 1) * FB, :].astype(F32))
        return
    zero = jnp.zeros((HALO, LANE), F32)
    t_scr[0:HALO, :] = zero
    t_scr[HALO + rows:2 * HALO + rows, :] = zero
    for a in range(nslab):
        t_scr[HALO + a * FB:HALO + (a + 1) * FB, :] = src_ref[a * FB:(a + 1) * FB, :].astype(F32)
    w0, w1, w2, bias = cw_ref[0:1, :], cw_ref[1:2, :], cw_ref[2:3, :], cb_ref[...]
    for a in range(nslab):
        o = HALO + a * FB
        y = (w0 * t_scr[o - 1:o - 1 + FB, :] + w1 * t_scr[o:o + FB, :]
             + w2 * t_scr[o + 1:o + 1 + FB, :] + bias)
        emit(a, y)


def _fwd_stage1(b, s_refs, g1_ref, y_r, y_i):
    cols = [s[pl.ds(b, FA_IN, stride=PITCH), :].astype(BF16) for s in s_refs]
    rhs = cols[0] if len(cols) == 1 else jnp.concatenate(cols, axis=1)
    out = jnp.dot(g1_ref[b], rhs, preferred_element_type=F32)
    if len(cols) == 1:
        re, im = out[0:FA], out[FA:2 * FA]
    else:
        re, im = _cplx(out, FA, LANE)
    y_r[pl.ds(b, FA, stride=PITCH), :] = re
    y_i[pl.ds(b, FA, stride=PITCH), :] = im


def _fwd_stage2(r, y_r, y_i, f_ref):
    o = pl.multiple_of(r * PITCH, 8)
    rhs = jnp.concatenate([y_r[pl.ds(o, FB), :].astype(BF16), y_i[pl.ds(o, FB), :].astype(BF16)], axis=1)
    out = jnp.dot(f_ref[...], rhs, preferred_element_type=F32)
    return _cplx(out, FB, LANE)


def _spectrum_kernel(h_ref, g1_ref, f_ref, hr_ref, hi_ref, s_scr, y_r, y_i):
    for a in range(FA_IN):
        s_scr[a * PITCH:a * PITCH + FB, :] = h_ref[0, a * FB:(a + 1) * FB, :]

    def s1(b, c):
        _fwd_stage1(b, [s_scr], g1_ref, y_r, y_i)
        return c
    lax.fori_loop(0, FB, s1, 0, unroll=UNROLL_ROWS)

    def s2(r, c):
        xr, xi = _fwd_stage2(r, y_r, y_i, f_ref)
        o = pl.multiple_of(r * FB, FB)
        hr_ref[0, pl.ds(o, FB), :] = xr
        hi_ref[0, pl.ds(o, FB), :] = xi
        return c
    lax.fori_loop(0, FA, s2, 0, unroll=UNROLL_SLABS)


def _filter_spectra(h, tabs):
    g1, f2, _, _ = tabs
    cw = HY_ORDER * HY_W
    spec = pl.BlockSpec((1, FFT_N, LANE), lambda l, c: (l, 0, c))
    return pl.pallas_call(
        _spectrum_kernel,
        grid=(DEPTH, cw // LANE),
        in_specs=[pl.BlockSpec((1, SEQ, LANE), lambda l, c: (l, 0, c)),
                  pl.BlockSpec(g1.shape, lambda l, c: (0, 0, 0)),
                  pl.BlockSpec(f2.shape, lambda l, c: (0, 0))],
        out_specs=[spec, spec],
        out_shape=[jax.ShapeDtypeStruct((DEPTH, FFT_N, cw), F32)] * 2,
        scratch_shapes=[pltpu.VMEM((FA_IN * PITCH, LANE), F32),
                        pltpu.VMEM((FA * PITCH, LANE), F32),
                        pltpu.VMEM((FA * PITCH, LANE), F32)],
        compiler_params=_params("parallel", "parallel"),
        name="filter_spectra",
    )(h, g1, f2)


def _hyconv_kernel(u_ref, x_ref, cwu_ref, cbu_ref, cwx_ref, cbx_ref,
                   hr_ref, hi_ref, bias_ref, g1_ref, f_ref, fc_ref, g2_ref,
                   o_ref, t_scr, s_r, s_i, y_r, y_i, *, conv_input):
    halves = [(s_r, slice(0, SEQ)), (s_i, slice(SEQ, 2 * SEQ))]

    def fill(s):
        def emit(a, slab):
            s[a * PITCH:a * PITCH + FB, :] = slab
        return emit
    for s, rows in halves:
        _short_conv_slabs(u_ref.at[rows, :], SEQ, t_scr, cwu_ref, cbu_ref, conv_input, fill(s))

    def s1(b, c):
        _fwd_stage1(b, [s_r, s_i], g1_ref, y_r, y_i)
        return c
    lax.fori_loop(0, FB, s1, 0, unroll=UNROLL_ROWS)

    def s2(r, c):
        xr, xi = _fwd_stage2(r, y_r, y_i, f_ref)
        oh = pl.multiple_of(r * FB, FB)
        hr = hr_ref[0, pl.ds(oh, FB), :]
        hi = hi_ref[0, pl.ds(oh, FB), :]
        vr = xr * hr - xi * hi
        vi = xr * hi + xi * hr
        rhs = jnp.concatenate([vr.astype(BF16), vi.astype(BF16)], axis=1)
        wr, wi = _cplx(jnp.dot(fc_ref[...], rhs, preferred_element_type=F32), FB, LANE)
        o = pl.multiple_of(r * PITCH, 8)
        y_r[pl.ds(o, FB), :] = wr
        y_i[pl.ds(o, FB), :] = wi
        return c
    lax.fori_loop(0, FA, s2, 0, unroll=UNROLL_SLABS)

    bias = bias_ref[0]

    def s3(b, c):
        rhs = jnp.concatenate([y_r[pl.ds(b, FA, stride=PITCH), :].astype(BF16),
                               y_i[pl.ds(b, FA, stride=PITCH), :].astype(BF16)], axis=1)
        re, im = _cplx(jnp.dot(g2_ref[b], rhs, preferred_element_type=F32), FA_IN, LANE)
        idx = pl.ds(b, FA_IN, stride=PITCH)
        s_r[idx, :] = re + s_r[idx, :] * bias
        s_i[idx, :] = im + s_i[idx, :] * bias
        return c
    lax.fori_loop(0, FB, s3, 0, unroll=UNROLL_ROWS)

    for s, rows in halves:
        def emit(a, slab, s=s, out=o_ref.at[rows, :]):
            out[a * FB:(a + 1) * FB, :] = (slab * s[a * PITCH:a * PITCH + FB, :]).astype(out.dtype)
        _short_conv_slabs(x_ref.at[rows, :], SEQ, t_scr, cwx_ref, cbx_ref, True, emit)


def _hyena_conv(u, u_col, x, x_col, conv_w, conv_b, cu_col, cx_col, spec_re, spec_im, layer, order,
                hy_bias_l, tabs, conv_input):
    g1, f2, f2c, g2 = tabs
    nblk = HY_W // LANE
    tok = lambda col: pl.BlockSpec((2 * SEQ, LANE), lambda c, p: (p, col + c))
    chan = lambda rows, col: pl.BlockSpec((rows, LANE), lambda c, p: (0, col + c))
    hspec = pl.BlockSpec((1, FFT_N, LANE), lambda c, p: (layer, 0, order * nblk + c))
    const = lambda shape: pl.BlockSpec(shape, lambda c, p: (0,) * len(shape))
    return pl.pallas_call(
        functools.partial(_hyconv_kernel, conv_input=conv_input),
        grid=(nblk, NB // 2),
        in_specs=[tok(u_col), tok(x_col),
                  chan(HY_SHORT, cu_col), chan(1, cu_col), chan(HY_SHORT, cx_col), chan(1, cx_col),
                  hspec, hspec,
                  pl.BlockSpec((1, 1, LANE), lambda c, p: (order, 0, c)),
                  const(g1.shape), const(f2.shape), const(f2c.shape), const(g2.shape)],
        out_specs=pl.BlockSpec((2 * SEQ, LANE), lambda c, p: (p, c)),
        out_shape=jax.ShapeDtypeStruct((LAT_ROWS, HY_W), BF16),
        scratch_shapes=[pltpu.VMEM((SEQ + 2 * HALO, LANE), F32),
                        pltpu.VMEM((FA_IN * PITCH, LANE), F32),
                        pltpu.VMEM((FA_IN * PITCH, LANE), F32),
                        pltpu.VMEM((FA * PITCH, LANE), F32),
                        pltpu.VMEM((FA * PITCH, LANE), F32)],
        compiler_params=_params("arbitrary", "arbitrary"),
        name=f"hyena_conv{order}",
    )(u, x, conv_w, conv_b, conv_w, conv_b, spec_re, spec_im,
      hy_bias_l.reshape(HY_ORDER, 1, HY_W), g1, f2, f2c, g2)


CN = 2 * CTX


def _ctx_tables():
    k = np.arange(CN)
    n = np.arange(CTX)
    f = np.exp(-2j * np.pi * k[:, None] * n[None, :] / CN)
    fwd = np.concatenate([f.real, f.imag], axis=0)
    t = np.arange(CTX) + CTX // 2
    g = np.exp(2j * np.pi * t[:, None] * k[None, :] / CN) / CN
    inv = np.concatenate([g.real, g.imag], axis=0)
    return jnp.asarray(fwd, dtype=BF16), jnp.asarray(inv, dtype=BF16)


def _ctx_hyena_kernel(v_ref, x1_ref, x2_ref, cwv_ref, cbv_ref, cw1_ref, cb1_ref, cw2_ref, cb2_ref,
                      h0_ref, h1_ref, bias_ref, fwd_ref, inv_ref, o_ref, t_scr, s_scr, g_scr):
    def conv_into(src_ref, b, cw_ref, cb_ref, dst, col):
        view = src_ref.at[b * CTX:(b + 1) * CTX, :]

        def emit(a, slab):
            dst[a * FB:(a + 1) * FB, col * LANE:(col + 1) * LANE] = slab
        _short_conv_slabs(view, CTX, t_scr, cw_ref, cb_ref, True, emit)

    def spectrum(h_ref):
        out = jnp.dot(fwd_ref[...], h_ref[0].astype(BF16), preferred_element_type=F32)
        return out[0:CN], out[CN:2 * CN]

    def long_conv(z2, hr, hi, bias):
        x = jnp.dot(fwd_ref[...], z2.astype(BF16), preferred_element_type=F32)
        xr, xi = _cplx(x, CN, LANE)
        vr = xr * hr - xi * hi
        vi = xr * hi + xi * hr
        rhs = jnp.concatenate([vr.astype(BF16), vi.astype(BF16)], axis=1)
        y = jnp.dot(inv_ref[...], rhs, preferred_element_type=F32)
        yr, yi = _cplx(y, CTX, LANE)
        return jnp.concatenate([yr, yi], axis=1) + z2 * jnp.concatenate([bias, bias], axis=1)

    h0r, h0i = spectrum(h0_ref)
    h1r, h1i = spectrum(h1_ref)
    for p in range(NB // 2):
        for half in range(2):
            conv_into(v_ref, 2 * p + half, cwv_ref, cbv_ref, s_scr, half)
            conv_into(x1_ref, 2 * p + half, cw1_ref, cb1_ref, g_scr, half)
        z = g_scr[...] * long_conv(s_scr[...], h0r, h0i, bias_ref[0])
        for half in range(2):
            conv_into(x2_ref, 2 * p + half, cw2_ref, cb2_ref, g_scr, half)
        y = g_scr[...] * long_conv(z, h1r, h1i, bias_ref[1])
        for half in range(2):
            b = 2 * p + half
            o_ref[b * CTX:(b + 1) * CTX, :] = y[:, half * LANE:(half + 1) * LANE].astype(BF16)


def _ctx_hyena(p, conv_w, conv_b, filt_ctx, layer, hy_bias_l, ctabs):
    fwd, inv = ctabs
    nblk = HY_W // LANE
    ctx_blk = LAT_ROWS // CTX_ROWS
    tok = lambda col: pl.BlockSpec((CTX_ROWS, LANE), lambda c: (ctx_blk, col + c))
    chan = lambda rows, col: pl.BlockSpec((rows, LANE), lambda c: (0, col + c))
    hspec = lambda order: pl.BlockSpec((1, CTX, LANE), lambda c: (layer, 0, order * nblk + c))
    const = lambda shape: pl.BlockSpec(shape, lambda c: (0,) * len(shape))
    return pl.pallas_call(
        _ctx_hyena_kernel,
        grid=(nblk,),
        in_specs=[tok(COL_HV), tok(COL_HX1), tok(COL_HX2),
                  chan(HY_SHORT, 0), chan(1, 0), chan(HY_SHORT, nblk), chan(1, nblk),
                  chan(HY_SHORT, 2 * nblk), chan(1, 2 * nblk),
                  hspec(0), hspec(1),
                  pl.BlockSpec((HY_ORDER, 1, LANE), lambda c: (0, 0, c)),
                  const(fwd.shape), const(inv.shape)],
        out_specs=pl.BlockSpec((CTX_ROWS, LANE), lambda c: (0, c)),
        out_shape=jax.ShapeDtypeStruct((CTX_ROWS, HY_W), BF16),
        scratch_shapes=[pltpu.VMEM((CTX + 2 * HALO, LANE), F32),
                        pltpu.VMEM((CTX, 2 * LANE), F32),
                        pltpu.VMEM((CTX, 2 * LANE), F32)],
        compiler_params=_params("parallel"),
        name="context_hyena",
    )(p, p, p, conv_w, conv_b, conv_w, conv_b, conv_w, conv_b, filt_ctx, filt_ctx,
      hy_bias_l.reshape(HY_ORDER, 1, HY_W), fwd, inv)


def _merge_kernel(oa_ref, oh_ref, ga_ref, gb_ref, x_ref, mod_ref, g2_ref, wpa_ref, wpb_ref, wo_ref,
                  *rest, routed):
    if routed:
        wr_ref, xo_ref, h_ref, rw_ref, wpa_s, wpb_s, wo_s = rest
    else:
        xo_ref, h_ref, wpa_s, wpb_s, wo_s = rest

    @pl.when(pl.program_id(0) == 0)
    def _():
        wpa_s[...] = wpa_ref[0].astype(BF16)
        wpb_s[...] = wpb_ref[0].astype(BF16)
        wo_s[...] = wo_ref[0].astype(BF16)

    a = jnp.dot(oa_ref[...], wpa_s[...], preferred_element_type=F32)
    b = jnp.dot(oh_ref[...], wpb_s[...], preferred_element_type=F32)
    mix = (jax.nn.sigmoid(ga_ref[...].astype(F32)) * a + jax.nn.sigmoid(gb_ref[...].astype(F32)) * b)
    y = jnp.dot(mix.astype(BF16), wo_s[...], preferred_element_type=F32)
    m = mod_ref[0]
    x = x_ref[...] + m[:, 2 * D:3 * D] * y
    xo_ref[...] = x
    h = _norm_mod(x, g2_ref[...], m[:, 3 * D:4 * D], m[:, 4 * D:5 * D])
    h_ref[...] = h.astype(BF16)
    if routed:
        rw_ref[...] = jnp.dot(h, wr_ref[...], precision=HIGHEST, preferred_element_type=F32)


def _merge(o_att, o_hy, p, x, mod_l, g2, w_pa, w_pb, w_o, layer, router):
    routed = router is not None
    tile = lambda w: pl.BlockSpec((TM, w), lambda i: (i, 0))
    full = lambda a: pl.BlockSpec(a.shape, lambda i: (0,) * a.ndim)
    layered = lambda a: pl.BlockSpec((1,) + a.shape[1:], lambda i: (layer, 0, 0))
    in_specs = [tile(ATT_W), tile(HY_W),
                pl.BlockSpec((TM, D), lambda i: (i, COL_GA)),
                pl.BlockSpec((TM, D), lambda i: (i, COL_GB)),
                tile(D),
                pl.BlockSpec((1, 1, N_MOD * D), lambda i: (_mod_row(i), 0, 0)),
                pl.BlockSpec((1, D), lambda i: (0, 0)),
                layered(w_pa), layered(w_pb), layered(w_o)]
    args = [o_att, o_hy, p, p, x, mod_l, g2, w_pa, w_pb, w_o]
    out_specs = [tile(D), tile(D)]
    out_shape = [jax.ShapeDtypeStruct((ROWS, D), F32), jax.ShapeDtypeStruct((ROWS, D), BF16)]
    if routed:
        wr = jnp.zeros((D, LANE), F32).at[:, :N_EXPERTS].set(router)
        in_specs.append(full(wr))
        args.append(wr)
        out_specs.append(tile(LANE))
        out_shape.append(jax.ShapeDtypeStruct((ROWS, LANE), F32))
    return pl.pallas_call(
        functools.partial(_merge_kernel, routed=routed),
        grid=(ROWS // TM,),
        in_specs=in_specs, out_specs=out_specs, out_shape=out_shape,
        scratch_shapes=[pltpu.VMEM(w_pa.shape[1:], BF16), pltpu.VMEM(w_pb.shape[1:], BF16),
                        pltpu.VMEM(w_o.shape[1:], BF16)],
        compiler_params=_params("arbitrary"),
        name="merge",
    )(*args)


TF = 256


NF = D_FF // TF


def _swiglu_step(h, w1, w3, w2, acc):
    a = jnp.dot(h, w1.astype(BF16), preferred_element_type=F32)
    b = jnp.dot(h, w3.astype(BF16), preferred_element_type=F32)
    t = a * jax.nn.sigmoid(a) * b
    acc[...] += jnp.dot(t.astype(BF16), w2.astype(BF16), preferred_element_type=F32)


def _residual_out(x, gate, y, fg_ref):
    x = x + gate * y
    if fg_ref is not None:
        x = x * lax.rsqrt(jnp.mean(x * x, axis=-1, keepdims=True) + EPS) * fg_ref[...]
    return x


def _ffn_kernel(h_ref, x_ref, mod_ref, w1_ref, w3_ref, w2_ref, *rest, final):
    rest = list(rest)
    fg_ref = rest.pop(0) if final else None
    o_ref, acc = rest
    f = pl.program_id(1)

    @pl.when(f == 0)
    def _():
        acc[...] = jnp.zeros_like(acc)

    _swiglu_step(h_ref[...], w1_ref[0], w3_ref[0], w2_ref[0], acc)

    @pl.when(f == NF - 1)
    def _():
        o_ref[...] = _residual_out(x_ref[...], mod_ref[0][:, 5 * D:6 * D], acc[...], fg_ref)


def _dense_mixer(h, x, mod_l, w1, w3, w2, e, final_g=None):
    final = final_g is not None
    rows = LAT_ROWS if final else ROWS
    tile = lambda w: pl.BlockSpec((TM, w), lambda i, f: (i, 0))
    in_specs = [tile(D), tile(D),
                pl.BlockSpec((1, 1, N_MOD * D), lambda i, f: (_mod_row(i), 0, 0)),
                pl.BlockSpec((1, D, TF), lambda i, f: (e, 0, f)),
                pl.BlockSpec((1, D, TF), lambda i, f: (e, 0, f)),
                pl.BlockSpec((1, TF, D), lambda i, f: (e, f, 0))]
    args = [h, x, mod_l, w1, w3, w2]
    if final:
        in_specs.append(pl.BlockSpec((1, D), lambda i, f: (0, 0)))
        args.append(final_g.reshape(1, D))
    return pl.pallas_call(
        functools.partial(_ffn_kernel, final=final),
        grid=(rows // TM, NF),
        in_specs=in_specs,
        out_specs=tile(D),
        out_shape=jax.ShapeDtypeStruct((rows, D), F32),
        scratch_shapes=[pltpu.VMEM((TM, D), F32)],
        compiler_params=_params("parallel", "arbitrary"),
        name="ffn",
    )(*args)


NT = ROWS // TM
MOE_CH = 128
MOE_CMAX = TM // MOE_CH
MOE_NCH = 2 * TM // MOE_CH + N_EXPERTS
MOE_SLOTS = MOE_NCH * MOE_CH
MOE_GC = 8
MOE_GROUPS = NT * MOE_NCH // MOE_GC + N_EXPERTS
MOE_SB = 512
PLAN_ROWS = 8


def _plan_kernel(lg_ref, col_ref, row_ref, nch_ref):
    lane = lax.broadcasted_iota(jnp.int32, (TM, LANE), 1)
    ninf = jnp.asarray(-jnp.inf, F32)
    lg = jnp.where(lane < N_EXPERTS, lg_ref[...], ninf)
    m1 = jnp.max(lg, axis=-1, keepdims=True)
    i1 = jnp.min(jnp.where(lg == m1, lane, LANE), axis=-1, keepdims=True)
    lg2 = jnp.where(lane == i1, ninf, lg)
    m2 = jnp.max(lg2, axis=-1, keepdims=True)
    i2 = jnp.min(jnp.where(lg2 == m2, lane, LANE), axis=-1, keepdims=True)
    e = jnp.exp(m2 - m1)
    den = 1.0 + e
    sel1 = lane == i1
    sel2 = lane == i2
    onehot = jnp.where(sel1, 1.0, jnp.where(sel2, 1.0, 0.0))
    tri = jnp.where(lax.broadcasted_iota(jnp.int32, (TM, TM), 1) < lax.broadcasted_iota(jnp.int32, (TM, TM), 0),
                    1.0, 0.0).astype(BF16)
    rank = jnp.dot(tri, onehot.astype(BF16), preferred_element_type=F32)
    cnt = jnp.sum(onehot, axis=0, keepdims=True)
    nch = jnp.floor((cnt + (MOE_CH - 1)) * (1.0 / MOE_CH))
    upper = jnp.where(lax.broadcasted_iota(jnp.int32, (LANE, LANE), 0) < lax.broadcasted_iota(jnp.int32, (LANE, LANE), 1),
                      1.0, 0.0).astype(BF16)
    cbase = jnp.dot(jnp.broadcast_to(nch, (8, LANE)).astype(BF16), upper,
                    preferred_element_type=F32)[0:1]
    slot = MOE_CH * cbase + rank
    slot0 = jnp.sum(jnp.where(sel1, slot, 0.0), axis=-1, keepdims=True)
    slot1 = jnp.sum(jnp.where(sel2, slot, 0.0), axis=-1, keepdims=True)
    col = jnp.where(lane == 0, slot0, jnp.where(lane == 1, slot1,
                    jnp.where(lane == 2, 1.0 / den, jnp.where(lane == 3, e / den, 0.0))))
    col_ref[...] = col
    row_ref[...] = jnp.transpose(col)[0:PLAN_ROWS]
    nch_ref[0] = nch


def _moe_plan(logits):
    return pl.pallas_call(
        _plan_kernel,
        grid=(NT,),
        in_specs=[pl.BlockSpec((TM, LANE), lambda i: (i, 0))],
        out_specs=[pl.BlockSpec((TM, LANE), lambda i: (i, 0)),
                   pl.BlockSpec((PLAN_ROWS, TM), lambda i: (0, i)),
                   pl.BlockSpec((1, 1, LANE), lambda i: (i, 0, 0))],
        out_shape=[jax.ShapeDtypeStruct((ROWS, LANE), F32),
                   jax.ShapeDtypeStruct((PLAN_ROWS, ROWS), F32),
                   jax.ShapeDtypeStruct((NT, 1, LANE), F32)],
        compiler_params=_params("parallel"),
        name="moe_plan",
    )(logits)


def _moe_tables(nch):
    n = nch[:, 0, :N_EXPERTS].astype(jnp.int32)
    cbase = jnp.cumsum(n, axis=1) - n
    used = jnp.sum(n, axis=1)
    c = jnp.arange(MOE_CMAX)
    valid = (c[None, None, :] < n.T[:, :, None]).reshape(N_EXPERTS, -1)
    cid = (jnp.arange(NT)[None, :, None] * MOE_NCH + cbase.T[:, :, None] + c[None, None, :]).reshape(N_EXPERTS, -1)
    ne = jnp.sum(n, axis=0)
    ge = (ne + MOE_GC - 1) // MOE_GC
    gend = jnp.cumsum(ge)
    nused = gend[-1]
    size = MOE_GROUPS * MOE_GC
    pos = (gend - ge)[:, None] * MOE_GC + jnp.cumsum(valid, axis=1) - 1
    dest = jnp.where(valid, pos, size).reshape(-1)
    tab = jnp.full((size,), -1, jnp.int32).at[dest].set(cid.reshape(-1).astype(jnp.int32), mode='drop')
    at = jnp.arange(size)
    tab = tab[jnp.maximum(lax.cummax(jnp.where(tab >= 0, at, -1)), 0)]
    gexp = jnp.minimum(jnp.sum(jnp.arange(MOE_GROUPS)[:, None] >= gend[None, :], axis=1), N_EXPERTS - 1)
    pool = jnp.zeros((NT * MOE_NCH,), jnp.int32).at[jnp.where(at < nused * MOE_GC, tab, NT * MOE_NCH)].set(
        at.astype(jnp.int32), mode='drop')
    return tab, gexp.astype(jnp.int32), nused.reshape(1).astype(jnp.int32), used.astype(jnp.int32), pool


def _dispatch_kernel(used_ref, h_ref, row_ref, xs_ref):
    used_rows = used_ref[pl.program_id(0)] * MOE_CH
    s0 = row_ref[0:1, :]
    s1 = row_ref[1:2, :]
    for blk in range(MOE_SLOTS // MOE_SB):
        rows = slice(blk * MOE_SB, (blk + 1) * MOE_SB)

        @pl.when(blk * MOE_SB < used_rows)
        def _():
            r = (lax.broadcasted_iota(jnp.int32, (MOE_SB, TM), 0) + blk * MOE_SB).astype(F32)
            s = jnp.where(r == s0, 1.0, jnp.where(r == s1, 1.0, 0.0)).astype(BF16)
            xs_ref[rows, :] = jnp.dot(s, h_ref[...], preferred_element_type=F32).astype(BF16)

        @pl.when(blk * MOE_SB >= used_rows)
        def _():
            xs_ref[rows, :] = jnp.zeros((MOE_SB, D), BF16)


def _moe_dispatch(h, plan_row, used):
    return pl.pallas_call(
        _dispatch_kernel,
        grid_spec=pltpu.PrefetchScalarGridSpec(
            num_scalar_prefetch=1, grid=(NT,),
            in_specs=[pl.BlockSpec((TM, D), lambda i, u: (i, 0)),
                      pl.BlockSpec((PLAN_ROWS, TM), lambda i, u: (0, i))],
            out_specs=pl.BlockSpec((MOE_SLOTS, D), lambda i, u: (i, 0))),
        out_shape=jax.ShapeDtypeStruct((NT * MOE_SLOTS, D), BF16),
        compiler_params=_params("parallel"),
        name="moe_dispatch",
    )(used, h, plan_row)


def _experts_kernel(tab_ref, gexp_ref, nused_ref, *refs):
    x_refs = refs[:MOE_GC]
    w1_ref, w3_ref, w2_ref, y_ref, x_scr, acc = refs[MOE_GC:]
    f = pl.program_id(1)

    @pl.when(pl.program_id(0) < nused_ref[0])
    def _():
        @pl.when(f == 0)
        def _():
            for j, r in enumerate(x_refs):
                x_scr[j * MOE_CH:(j + 1) * MOE_CH, :] = r[...]
            acc[...] = jnp.zeros_like(acc)

        _swiglu_step(x_scr[...], w1_ref[0], w3_ref[0], w2_ref[0], acc)

        @pl.when(f == NF - 1)
        def _():
            y_ref[...] = acc[...].astype(BF16)


def _moe_experts(xs, tab, gexp, nused, w1, w3, w2, e0):
    grp = lambda g, nu: jnp.minimum(g, nu[0] - 1)
    ftile = lambda g, f, nu: jnp.where(g < nu[0], f, NF - 1)
    chunk = lambda j: pl.BlockSpec((MOE_CH, D), lambda g, f, tab, ge, nu: (tab[grp(g, nu) * MOE_GC + j], 0))
    rows = MOE_GC * MOE_CH
    return pl.pallas_call(
        _experts_kernel,
        grid_spec=pltpu.PrefetchScalarGridSpec(
            num_scalar_prefetch=3, grid=(MOE_GROUPS, NF),
            in_specs=[chunk(j) for j in range(MOE_GC)] + [
                pl.BlockSpec((1, D, TF), lambda g, f, tab, ge, nu: (e0 + ge[grp(g, nu)], 0, ftile(g, f, nu))),
                pl.BlockSpec((1, D, TF), lambda g, f, tab, ge, nu: (e0 + ge[grp(g, nu)], 0, ftile(g, f, nu))),
                pl.BlockSpec((1, TF, D), lambda g, f, tab, ge, nu: (e0 + ge[grp(g, nu)], ftile(g, f, nu), 0))],
            out_specs=pl.BlockSpec((rows, D), lambda g, f, tab, ge, nu: (grp(g, nu), 0)),
            scratch_shapes=[pltpu.VMEM((rows, D), BF16), pltpu.VMEM((rows, D), F32)]),
        out_shape=jax.ShapeDtypeStruct((MOE_GROUPS * rows, D), BF16),
        compiler_params=_params("arbitrary", "arbitrary"),
        name="moe_experts",
    )(tab, gexp, nused, *([xs] * MOE_GC), w1, w3, w2)


def _combine_kernel(used_ref, pool_ref, *refs, final):
    refs = list(refs)
    y_refs = [refs.pop(0) for _ in range(MOE_NCH)]
    col_ref, x_ref, mod_ref = refs[:3]
    fg_ref = refs[3] if final else None
    o_ref, y_scr, acc = refs[-3:]
    used = used_ref[pl.program_id(0)]
    col = col_ref[...]
    s0, s1, g0, g1 = col[:, 0:1], col[:, 1:2], col[:, 2:3], col[:, 3:4]
    acc[...] = jnp.zeros_like(acc)
    per_blk = MOE_SB // MOE_CH
    for blk in range(MOE_SLOTS // MOE_SB):
        @pl.when(blk * per_blk < used)
        def _():
            for q in range(blk * per_blk, (blk + 1) * per_blk):
                y_scr[q * MOE_CH:(q + 1) * MOE_CH, :] = y_refs[q][...]
            c = (lax.broadcasted_iota(jnp.int32, (TM, MOE_SB), 1) + blk * MOE_SB).astype(F32)
            w = jnp.where(c == s0, g0, jnp.where(c == s1, g1, 0.0)).astype(BF16)
            acc[...] += jnp.dot(w, y_scr[blk * MOE_SB:(blk + 1) * MOE_SB, :], preferred_element_type=F32)
    o_ref[...] = _residual_out(x_ref[...], mod_ref[0][:, 5 * D:6 * D], acc[...], fg_ref)


def _moe_combine(y, pool, used, plan_col, x, mod_l, final_g=None):
    final = final_g is not None
    rows = LAT_ROWS if final else ROWS
    tile = lambda w: pl.BlockSpec((TM, w), lambda i, u, pc: (i, 0))
    chunk = lambda q: pl.BlockSpec((MOE_CH, D), lambda i, u, pc: (pc[i * MOE_NCH + q], 0))
    in_specs = [chunk(q) for q in range(MOE_NCH)] + [
        tile(LANE), tile(D), pl.BlockSpec((1, 1, N_MOD * D), lambda i, u, pc: (_mod_row(i), 0, 0))]
    args = [y] * MOE_NCH + [plan_col, x, mod_l]
    if final:
        in_specs.append(pl.BlockSpec((1, D), lambda i, u, pc: (0, 0)))
        args.append(final_g.reshape(1, D))
    return pl.pallas_call(
        functools.partial(_combine_kernel, final=final),
        grid_spec=pltpu.PrefetchScalarGridSpec(
            num_scalar_prefetch=2, grid=(rows // TM,),
            in_specs=in_specs, out_specs=tile(D),
            scratch_shapes=[pltpu.VMEM((MOE_SLOTS, D), BF16), pltpu.VMEM((TM, D), F32)]),
        out_shape=jax.ShapeDtypeStruct((rows, D), F32),
        compiler_params=_params("parallel"),
        name="moe_combine",
    )(used, pool, *args)


def _moe_mixer(h, logits, x, mod_l, w1, w3, w2, e0, final_g=None):
    plan_col, plan_row, nch = _moe_plan(logits)
    tab, gexp, nused, used, pool = _moe_tables(nch)
    xs = _moe_dispatch(h, plan_row, used)
    y = _moe_experts(xs, tab, gexp, nused, w1, w3, w2, e0)
    return _moe_combine(y, pool, used, plan_col, x, mod_l, final_g)


def kernel(x, c, ctx, c_ctx, w_mod, b_mod, norm1_g, norm2_g, w_in, rpb, hy_conv_w, hy_conv_b, hy_w1, hy_b1, hy_w2, hy_b2, hy_w3, hy_freq, hy_bias, w_pa, w_pb, w_o, ffn_w1, ffn_w3, ffn_w2, moe_router, moe_w1, moe_w3, moe_w2, final_g):
    xs = jnp.concatenate([x.reshape(LAT_ROWS, D), ctx.reshape(CTX_ROWS, D)], axis=0)
    mod = _modulation(c, c_ctx, w_mod, b_mod)
    tabs = _fft_tables()
    ctabs = _ctx_tables()
    filt_lat = _implicit_filters(SEQ, hy_w1, hy_b1, hy_w2, hy_b2, hy_w3, hy_freq)
    filt_ctx = _implicit_filters(CTX, hy_w1, hy_b1, hy_w2, hy_b2, hy_w3, hy_freq)
    spec_re, spec_im = _filter_spectra(filt_lat, tabs)
    nblk = HY_W // LANE

    moe_w = [w.reshape((-1,) + w.shape[2:]).astype(BF16) for w in (moe_w1, moe_w3, moe_w2)]

    for l in range(DEPTH):
        last = l == DEPTH - 1
        mod_l = mod[l].reshape(MOD_ROWS, 1, N_MOD * D)
        cw, cb = hy_conv_w[l], hy_conv_b[l].reshape(1, 3 * HY_W)
        p = _in_proj(xs, mod_l, norm1_g[l].reshape(1, D), w_in, l)

        o_att = jnp.concatenate([_neighbourhood_attention(p, rpb[l]), _context_attention(p)], axis=0)
        z = _hyena_conv(p, COL_HV, p, COL_HX1, cw, cb, 0, nblk, spec_re, spec_im, l, 0,
                        hy_bias[l], tabs, True)
        o_hy = _hyena_conv(z, 0, p, COL_HX2, cw, cb, 0, 2 * nblk, spec_re, spec_im, l, 1,
                           hy_bias[l], tabs, False)
        o_hy = jnp.concatenate([o_hy, _ctx_hyena(p, cw, cb, filt_ctx, l, hy_bias[l], ctabs)], axis=0)

        i = l // 2
        router = moe_router[i] if l % 2 else None
        res = _merge(o_att, o_hy, p, xs, mod_l, norm2_g[l].reshape(1, D), w_pa, w_pb, w_o, l, router)
        fg = final_g if last else None
        if l % 2 == 0:
            xs = _dense_mixer(res[1], res[0], mod_l, ffn_w1, ffn_w3, ffn_w2, i, final_g=fg)
        else:
            xs = _moe_mixer(res[1], res[2], res[0], mod_l, *moe_w, i * N_EXPERTS, final_g=fg)
    return xs.reshape(NB, SEQ, D)
```

```python
import functools
import math

import numpy as np
import jax
import jax.numpy as jnp
from jax import lax
from jax.experimental import pallas as pl
from jax.experimental.pallas import tpu as pltpu

F32 = jnp.float32
BF16 = jnp.bfloat16
HIGHEST = lax.Precision.HIGHEST

D = 1024
NB = 4
SEQ = 4096
DEPTH = 4
CTX = 256
GRID_W = 64
GRID_H = SEQ // GRID_W
N_HEADS = 8
HEAD_DIM = 64
ATT_W = N_HEADS * HEAD_DIM
WIN_ROWS = 8
WIN_COLS = 16
HY_W = 512
HY_ORDER = 2
HY_SHORT = 3
HY_EMB = 33
HY_BANDS = (HY_EMB - 1) // 2
HY_FILT = 64
HY_DECAY_TARGET = 1e-2
HY_MAX_DECAY = math.log(HY_DECAY_TARGET) / 0.3
HY_MIN_DECAY = math.log(HY_DECAY_TARGET) / 1.5
HY_DECAY_SHIFT = 0.05
PROJ_W = 3 * ATT_W + 3 * HY_W + 2 * D
D_FF = 2816
N_EXPERTS = 8
N_MOD = 6
EPS = 1e-6

LAT_ROWS = NB * SEQ
CTX_ROWS = NB * CTX
ROWS = LAT_ROWS + CTX_ROWS
MOD_ROWS = 8
CTX_MOD_ROW = NB

LANE = 128
TM = 1024
VMEM_LIMIT = 56 * 1024 * 1024

COL_Q, COL_K, COL_V = 0, ATT_W // LANE, 2 * ATT_W // LANE
COL_HV = 3 * ATT_W // LANE
COL_HX1 = COL_HV + HY_W // LANE
COL_HX2 = COL_HX1 + HY_W // LANE
COL_GA = (3 * ATT_W + 3 * HY_W) // D
COL_GB = COL_GA + 1


def _mod_row(i):
    return jnp.where(i < LAT_ROWS // TM, (i * TM) // SEQ, CTX_MOD_ROW)


def _params(*sem):
    return pltpu.CompilerParams(dimension_semantics=sem, vmem_limit_bytes=VMEM_LIMIT)


def _mod_kernel(s_ref, w_ref, b_ref, o_ref):
    s = s_ref[...]
    s = s * jax.nn.sigmoid(s)
    o_ref[0] = jnp.dot(s, w_ref[0], precision=HIGHEST, preferred_element_type=F32) + b_ref[0]


def _modulation(c, c_ctx, w_mod, b_mod):
    s = jnp.zeros((MOD_ROWS, D), F32).at[:NB].set(c).at[CTX_MOD_ROW].set(c_ctx)
    tn = 1536
    return pl.pallas_call(
        _mod_kernel,
        grid=(DEPTH, N_MOD * D // tn),
        in_specs=[pl.BlockSpec((MOD_ROWS, D), lambda l, j: (0, 0)),
                  pl.BlockSpec((1, D, tn), lambda l, j: (l, 0, j)),
                  pl.BlockSpec((1, 1, tn), lambda l, j: (l, 0, j))],
        out_specs=pl.BlockSpec((1, MOD_ROWS, tn), lambda l, j: (l, 0, j)),
        out_shape=jax.ShapeDtypeStruct((DEPTH, MOD_ROWS, N_MOD * D), F32),
        compiler_params=_params("parallel", "parallel"),
        name="modulation",
    )(s, w_mod, b_mod.reshape(DEPTH, 1, N_MOD * D))


def _norm_mod(x, g, shift, scale):
    y = x * lax.rsqrt(jnp.mean(x * x, axis=-1, keepdims=True) + EPS) * g
    return y * (1.0 + scale) + shift


def _inproj_kernel(x_ref, mod_ref, g_ref, w_ref, o_ref, h_scr):
    @pl.when(pl.program_id(1) == 0)
    def _():
        m = mod_ref[0]
        h = _norm_mod(x_ref[...], g_ref[...], m[:, 0:D], m[:, D:2 * D])
        h_scr[...] = h.astype(BF16)

    o_ref[...] = jnp.dot(h_scr[...], w_ref[0].astype(BF16),
                         preferred_element_type=F32).astype(BF16)


def _in_proj(x, mod_l, g, w, layer):
    tn = 1024
    return pl.pallas_call(
        _inproj_kernel,
        grid=(ROWS // TM, PROJ_W // tn),
        in_specs=[pl.BlockSpec((TM, D), lambda i, j: (i, 0)),
                  pl.BlockSpec((1, 1, N_MOD * D), lambda i, j: (_mod_row(i), 0, 0)),
                  pl.BlockSpec((1, D), lambda i, j: (0, 0)),
                  pl.BlockSpec((1, D, tn), lambda i, j: (layer, 0, j))],
        out_specs=pl.BlockSpec((TM, tn), lambda i, j: (i, j)),
        out_shape=jax.ShapeDtypeStruct((ROWS, PROJ_W), BF16),
        scratch_shapes=[pltpu.VMEM((TM, D), BF16)],
        compiler_params=_params("parallel", "arbitrary"),
        name="in_proj",
    )(x, mod_l, g, w)


NEG = -1e30


def _nt_dot(a, b):
    return lax.dot_general(a, b, (((1,), (1,)), ((), ())), preferred_element_type=F32)


def _attend_pair(qp, keys, values, biases):
    lane = lax.broadcasted_iota(jnp.int32, qp.shape, 1)
    qs = qp * jnp.asarray(HEAD_DIM ** -0.5, BF16)
    outs = []
    for a in range(2):
        sel = (lane < HEAD_DIM) if a == 0 else (lane >= HEAD_DIM)
        qa = jnp.where(sel, qs, jnp.zeros_like(qs))
        ss = []
        for k, b in zip(keys, biases):
            s = _nt_dot(qa, k)
            if b is not None:
                s = s + b[a]
            ss.append(s)
        m = functools.reduce(jnp.maximum, [jnp.max(s, axis=-1, keepdims=True) for s in ss])
        ps = [jnp.exp(s - m) for s in ss]
        den = functools.reduce(jnp.add, [jnp.sum(p, axis=-1, keepdims=True) for p in ps])
        o = functools.reduce(jnp.add, [jnp.dot(p.astype(BF16), v, preferred_element_type=F32)
                                       for p, v in zip(ps, values)])
        outs.append(o / den)
    return jnp.where(lane < HEAD_DIM, outs[0], outs[1])


NLOC = WIN_ROWS * GRID_W
NKEY = NLOC + CTX
NA_PAIRS = GRID_H // 2
NA_STEPS = NA_PAIRS + 2
NA_UNITS = 2 * N_HEADS


def _window_start(r):
    r0 = jnp.clip(r - WIN_ROWS // 2, 0, GRID_H - WIN_ROWS)
    return pl.multiple_of(r0 * GRID_W, GRID_W)


def _natt_kernel(q_ref, k_ref, v_ref, kc_ref, vc_ref, bt0_ref, bt1_ref, o_ref, s0, s1, p0, p1, l0, l1):
    j = pl.program_id(1)
    lane = lax.broadcasted_iota(jnp.int32, (GRID_W, LANE), 1)
    lo = lane < HEAD_DIM
    hi = lane >= HEAD_DIM

    @pl.when(j == 0)
    def _():
        s1[...] = jnp.zeros_like(s1)
        p0[...] = jnp.zeros_like(p0)
        p1[...] = jnp.zeros_like(p1)
        l0[...] = jnp.ones_like(l0)
        l1[...] = jnp.ones_like(l1)

    def step(s_cur, s_prev, p_cur, p_prev, l_cur, l_prev):
        ro = jnp.maximum(2 * j - 4, 0)
        for half in range(2):
            start = _window_start(ro + half)
            rows = slice(half * GRID_W, (half + 1) * GRID_W)
            for hp in range(N_HEADS // 2):
                cs = slice(hp * LANE, (hp + 1) * LANE)
                vw = v_ref[pl.ds(start, NLOC), cs]
                vc = vc_ref[:, cs]
                o = []
                for u in (half * N_HEADS + 2 * hp, half * N_HEADS + 2 * hp + 1):
                    acc = (jnp.dot(p_cur[u, :, 0:NLOC], vw, preferred_element_type=F32)
                           + jnp.dot(p_cur[u, :, NLOC:NKEY], vc, preferred_element_type=F32))
                    o.append(acc / l_cur[u])
                o_ref[rows, cs] = jnp.where(lo, o[0], o[1]).astype(BF16)

        for u in range(NA_UNITS):
            x = s_prev[u]
            e = jnp.exp(x - jnp.max(x, axis=-1, keepdims=True))
            l_prev[u] = jnp.broadcast_to(jnp.sum(e, axis=-1, keepdims=True), (GRID_W, LANE))
            p_prev[u] = e.astype(BF16)

        rs = jnp.minimum(2 * j, GRID_H - 2)
        for half, bt_ref in enumerate((bt0_ref, bt1_ref)):
            start = _window_start(rs + half)
            rows = slice(half * GRID_W, (half + 1) * GRID_W)
            for hp in range(N_HEADS // 2):
                cs = slice(hp * LANE, (hp + 1) * LANE)
                qs = q_ref[rows, cs] * jnp.asarray(HEAD_DIM ** -0.5, BF16)
                kw = k_ref[pl.ds(start, NLOC), cs]
                kc = kc_ref[:, cs]
                for a in range(2):
                    qa = jnp.where(lo if a == 0 else hi, qs, jnp.zeros_like(qs))
                    u = half * N_HEADS + 2 * hp + a
                    s_cur[u, :, 0:NLOC] = _nt_dot(qa, kw) + bt_ref[0, 2 * hp + a]
                    s_cur[u, :, NLOC:NKEY] = _nt_dot(qa, kc)

    @pl.when(j % 2 == 0)
    def _():
        step(s0, s1, p0, p1, l0, l1)

    @pl.when(j % 2 == 1)
    def _():
        step(s1, s0, p1, p0, l1, l0)


def _bias_tables(rpb_l):
    col = np.arange(GRID_W)
    c0 = np.clip(col - WIN_COLS // 2, 0, GRID_W - WIN_COLS)
    valid = (col[None, :] >= c0[:, None]) & (col[None, :] < c0[:, None] + WIN_COLS)
    dc = np.clip(col[None, :] - col[:, None] + (WIN_COLS - 1), 0, 2 * WIN_COLS - 2)
    t = rpb_l[:, :, dc]
    t = jnp.where(valid[None, None], t, NEG)
    tabs = []
    for d0 in range(WIN_ROWS):
        w = t[:, d0:d0 + WIN_ROWS]
        tabs.append(w.transpose(0, 2, 1, 3).reshape(N_HEADS, GRID_W, WIN_ROWS * GRID_W))
    return jnp.stack(tabs)


def _bias_index(r):
    r0 = jnp.clip(r - WIN_ROWS // 2, 0, GRID_H - WIN_ROWS)
    return r0 - r + (WIN_ROWS - 1)


def _neighbourhood_attention(p, rpb_l):
    bt = _bias_tables(rpb_l)
    blk = GRID_H // 2
    ctx_blk0 = LAT_ROWS // CTX
    score_row = lambda j, half: jnp.minimum(2 * j, GRID_H - 2) + half
    bias_spec = lambda half: pl.BlockSpec((1, N_HEADS, GRID_W, NLOC),
                                          lambda b, j: (_bias_index(score_row(j, half)), 0, 0, 0))
    return pl.pallas_call(
        _natt_kernel,
        grid=(NB, NA_STEPS),
        in_specs=[pl.BlockSpec((2 * GRID_W, ATT_W), lambda b, j: (b * blk + jnp.minimum(j, blk - 1), 0)),
                  pl.BlockSpec((SEQ, ATT_W), lambda b, j: (b, 1)),
                  pl.BlockSpec((SEQ, ATT_W), lambda b, j: (b, 2)),
                  pl.BlockSpec((CTX, ATT_W), lambda b, j: (ctx_blk0 + b, 1)),
                  pl.BlockSpec((CTX, ATT_W), lambda b, j: (ctx_blk0 + b, 2)),
                  bias_spec(0), bias_spec(1)],
        out_specs=pl.BlockSpec((2 * GRID_W, ATT_W), lambda b, j: (b * blk + jnp.maximum(j - 2, 0), 0)),
        out_shape=jax.ShapeDtypeStruct((LAT_ROWS, ATT_W), BF16),
        scratch_shapes=[pltpu.VMEM((NA_UNITS, GRID_W, NKEY), F32)] * 2
                       + [pltpu.VMEM((NA_UNITS, GRID_W, NKEY), BF16)] * 2
                       + [pltpu.VMEM((NA_UNITS, GRID_W, LANE), F32)] * 2,
        compiler_params=_params("parallel", "arbitrary"),
        name="neighbourhood_attention",
    )(p, p, p, p, p, bt, bt)


def _catt_kernel(q_ref, k_ref, v_ref, o_ref):
    for hp in range(N_HEADS // 2):
        cs = slice(hp * LANE, (hp + 1) * LANE)
        o = _attend_pair(q_ref[:, cs], [k_ref[:, cs]], [v_ref[:, cs]], [None])
        o_ref[:, cs] = o.astype(BF16)


def _context_attention(p):
    ctx_blk0 = LAT_ROWS // CTX
    return pl.pallas_call(
        _catt_kernel,
        grid=(NB,),
        in_specs=[pl.BlockSpec((CTX, ATT_W), lambda b: (ctx_blk0 + b, 0)),
                  pl.BlockSpec((CTX, ATT_W), lambda b: (ctx_blk0 + b, 1)),
                  pl.BlockSpec((CTX, ATT_W), lambda b: (ctx_blk0 + b, 2))],
        out_specs=pl.BlockSpec((CTX, ATT_W), lambda b: (b, 0)),
        out_shape=jax.ShapeDtypeStruct((CTX_ROWS, ATT_W), BF16),
        compiler_params=_params("parallel"),
        name="context_attention",
    )(p, p, p)


def _filter_features(L):
    pos = np.arange(L, dtype=np.float64)
    t = pos / max(L - 1, 1)
    bands = np.linspace(1e-4, HY_BANDS - 1, HY_BANDS)
    ang = (2.0 * math.pi / L) * pos[:, None] * bands[None, :]
    z = np.concatenate([t[:, None], np.cos(ang), -np.sin(ang)], axis=-1)
    zp = np.zeros((L, LANE), np.float32)
    zp[:, :HY_EMB] = z
    return zp


def _filter_kernel(z_ref, w1_ref, b1_ref, w2_ref, b2_ref, w3_ref, fr_ref, rate_ref, o_ref, *, L, tl):
    dot = functools.partial(jnp.dot, precision=HIGHEST, preferred_element_type=F32)
    fr = fr_ref[0]
    a = jnp.sin(fr[0:1] * (dot(z_ref[...], w1_ref[0]) + b1_ref[0]))
    a = jnp.sin(fr[1:2] * (dot(a, w2_ref[0]) + b2_ref[0]))
    h = dot(a, w3_ref[0])
    pos = (pl.program_id(1) * tl + lax.broadcasted_iota(jnp.int32, (tl, 1), 0)).astype(F32)
    dist = jnp.abs(pos - float(L // 2)) * (2.0 / L)
    o_ref[0] = h * (jnp.exp(-dist * rate_ref[...]) + HY_DECAY_SHIFT)


def _implicit_filters(L, w1, b1, w2, b2, w3, freq):
    tl = min(L, 512)
    z = jnp.asarray(_filter_features(L))
    w1p = jnp.zeros((DEPTH, LANE, HY_FILT), F32).at[:, :HY_EMB].set(w1)
    rates = np.abs(np.linspace(HY_MIN_DECAY, HY_MAX_DECAY, HY_ORDER * HY_W)).astype(np.float32)
    cw = HY_ORDER * HY_W
    return pl.pallas_call(
        functools.partial(_filter_kernel, L=L, tl=tl),
        grid=(DEPTH, L // tl),
        in_specs=[pl.BlockSpec((tl, LANE), lambda l, i: (i, 0)),
                  pl.BlockSpec((1, LANE, HY_FILT), lambda l, i: (l, 0, 0)),
                  pl.BlockSpec((1, 1, HY_FILT), lambda l, i: (l, 0, 0)),
                  pl.BlockSpec((1, HY_FILT, HY_FILT), lambda l, i: (l, 0, 0)),
                  pl.BlockSpec((1, 1, HY_FILT), lambda l, i: (l, 0, 0)),
                  pl.BlockSpec((1, HY_FILT, cw), lambda l, i: (l, 0, 0)),
                  pl.BlockSpec((1, 2, HY_FILT), lambda l, i: (l, 0, 0)),
                  pl.BlockSpec((1, cw), lambda l, i: (0, 0))],
        out_specs=pl.BlockSpec((1, tl, cw), lambda l, i: (l, i, 0)),
        out_shape=jax.ShapeDtypeStruct((DEPTH, L, cw), F32),
        compiler_params=_params("parallel", "parallel"),
        name=f"implicit_filters_{L}",
    )(z, w1p, b1.reshape(DEPTH, 1, HY_FILT), w2, b2.reshape(DEPTH, 1, HY_FILT), w3, freq,
      jnp.asarray(rates).reshape(1, cw))


FFT_N = 2 * SEQ
FA = 64
FB = 128
FA_IN = SEQ // FB
PITCH = 136
HALO = 8
UNROLL_ROWS = 16
UNROLL_SLABS = 8


def _fft_tables():
    a = np.arange(FA_IN)
    r = np.arange(FA)
    b = np.arange(FB)
    q = np.arange(FB)
    ph = a[None, None, :] * r[None, :, None] / FA + b[:, None, None] * r[None, :, None] / FFT_N
    g1 = np.exp(-2j * np.pi * ph)
    g1 = np.concatenate([g1.real, g1.imag], axis=1)
    f = np.exp(-2j * np.pi * q[:, None] * b[None, :] / FB)
    f2 = np.concatenate([f.real, f.imag], axis=0)
    f2c = np.concatenate([f.real, -f.imag], axis=0)
    ao = np.arange(FA_IN) + (SEQ // 2) // FB
    ph2 = ao[None, :, None] * r[None, None, :] / FA + b[:, None, None] * r[None, None, :] / FFT_N
    g2 = np.exp(2j * np.pi * ph2) / FFT_N
    g2 = np.concatenate([g2.real, g2.imag], axis=1)
    return tuple(jnp.asarray(t, dtype=BF16) for t in (g1, f2, f2c, g2))


def _cplx(out, m, n):
    re = out[0:m, 0:n] - out[m:2 * m, n:2 * n]
    im = out[0:m, n:2 * n] + out[m:2 * m, 0:n]
    return re, im


def _short_conv_slabs(src_ref, rows, t_scr, cw_ref, cb_ref, apply_conv, emit):
    nslab = rows // FB
    if not apply_conv:
        for a in range(nslab):
            emit(a, src_ref[a * FB:(a + 1) * FB, :].astype(F32))
        return
    zero = jnp.zeros((HALO, LANE), F32)
    t_scr[0:HALO, :] = zero
    t_scr[HALO + rows:2 * HALO + rows, :] = zero
    for a in range(nslab):
        t_scr[HALO + a * FB:HALO + (a + 1) * FB, :] = src_ref[a * FB:(a + 1) * FB, :].astype(F32)
    w0, w1, w2, bias = cw_ref[0:1, :], cw_ref[1:2, :], cw_ref[2:3, :], cb_ref[...]
    for a in range(nslab):
        o = HALO + a * FB
        y = (w0 * t_scr[o - 1:o - 1 + FB, :] + w1 * t_scr[o:o + FB, :]
             + w2 * t_scr[o + 1:o + 1 + FB, :] + bias)
        emit(a, y)


def _fwd_stage1(b, s_refs, g1_ref, y_r, y_i):
    cols = [s[pl.ds(b, FA_IN, stride=PITCH), :].astype(BF16) for s in s_refs]
    rhs = cols[0] if len(cols) == 1 else jnp.concatenate(cols, axis=1)
    out = jnp.dot(g1_ref[b], rhs, preferred_element_type=F32)
    if len(cols) == 1:
        re, im = out[0:FA], out[FA:2 * FA]
    else:
        re, im = _cplx(out, FA, LANE)
    y_r[pl.ds(b, FA, stride=PITCH), :] = re
    y_i[pl.ds(b, FA, stride=PITCH), :] = im


def _fwd_stage2(r, y_r, y_i, f_ref):
    o = pl.multiple_of(r * PITCH, 8)
    rhs = jnp.concatenate([y_r[pl.ds(o, FB), :].astype(BF16), y_i[pl.ds(o, FB), :].astype(BF16)], axis=1)
    out = jnp.dot(f_ref[...], rhs, preferred_element_type=F32)
    return _cplx(out, FB, LANE)


def _spectrum_kernel(h_ref, g1_ref, f_ref, hr_ref, hi_ref, s_scr, y_r, y_i):
    for a in range(FA_IN):
        s_scr[a * PITCH:a * PITCH + FB, :] = h_ref[0, a * FB:(a + 1) * FB, :]

    def s1(b, c):
        _fwd_stage1(b, [s_scr], g1_ref, y_r, y_i)
        return c
    lax.fori_loop(0, FB, s1, 0, unroll=UNROLL_ROWS)

    def s2(r, c):
        xr, xi = _fwd_stage2(r, y_r, y_i, f_ref)
        o = pl.multiple_of(r * FB, FB)
        hr_ref[0, pl.ds(o, FB), :] = xr
        hi_ref[0, pl.ds(o, FB), :] = xi
        return c
    lax.fori_loop(0, FA, s2, 0, unroll=UNROLL_SLABS)


def _filter_spectra(h, tabs):
    g1, f2, _, _ = tabs
    cw = HY_ORDER * HY_W
    spec = pl.BlockSpec((1, FFT_N, LANE), lambda l, c: (l, 0, c))
    return pl.pallas_call(
        _spectrum_kernel,
        grid=(DEPTH, cw // LANE),
        in_specs=[pl.BlockSpec((1, SEQ, LANE), lambda l, c: (l, 0, c)),
                  pl.BlockSpec(g1.shape, lambda l, c: (0, 0, 0)),
                  pl.BlockSpec(f2.shape, lambda l, c: (0, 0))],
        out_specs=[spec, spec],
        out_shape=[jax.ShapeDtypeStruct((DEPTH, FFT_N, cw), F32)] * 2,
        scratch_shapes=[pltpu.VMEM((FA_IN * PITCH, LANE), F32),
                        pltpu.VMEM((FA * PITCH, LANE), F32),
                        pltpu.VMEM((FA * PITCH, LANE), F32)],
        compiler_params=_params("parallel", "parallel"),
        name="filter_spectra",
    )(h, g1, f2)


def _hyconv_kernel(u_ref, x_ref, cwu_ref, cbu_ref, cwx_ref, cbx_ref,
                   hr_ref, hi_ref, bias_ref, g1_ref, f_ref, fc_ref, g2_ref,
                   o_ref, t_scr, s_r, s_i, y_r, y_i, *, conv_input):
    halves = [(s_r, slice(0, SEQ)), (s_i, slice(SEQ, 2 * SEQ))]

    def fill(s):
        def emit(a, slab):
            s[a * PITCH:a * PITCH + FB, :] = slab
        return emit
    for s, rows in halves:
        _short_conv_slabs(u_ref.at[rows, :], SEQ, t_scr, cwu_ref, cbu_ref, conv_input, fill(s))

    def s1(b, c):
        _fwd_stage1(b, [s_r, s_i], g1_ref, y_r, y_i)
        return c
    lax.fori_loop(0, FB, s1, 0, unroll=UNROLL_ROWS)

    def s2(r, c):
        xr, xi = _fwd_stage2(r, y_r, y_i, f_ref)
        oh = pl.multiple_of(r * FB, FB)
        hr = hr_ref[0, pl.ds(oh, FB), :]
        hi = hi_ref[0, pl.ds(oh, FB), :]
        vr = xr * hr - xi * hi
        vi = xr * hi + xi * hr
        rhs = jnp.concatenate([vr.astype(BF16), vi.astype(BF16)], axis=1)
        wr, wi = _cplx(jnp.dot(fc_ref[...], rhs, preferred_element_type=F32), FB, LANE)
        o = pl.multiple_of(r * PITCH, 8)
        y_r[pl.ds(o, FB), :] = wr
        y_i[pl.ds(o, FB), :] = wi
        return c
    lax.fori_loop(0, FA, s2, 0, unroll=UNROLL_SLABS)

    bias = bias_ref[0]

    def s3(b, c):
        rhs = jnp.concatenate([y_r[pl.ds(b, FA, stride=PITCH), :].astype(BF16),
                               y_i[pl.ds(b, FA, stride=PITCH), :].astype(BF16)], axis=1)
        re, im = _cplx(jnp.dot(g2_ref[b], rhs, preferred_element_type=F32), FA_IN, LANE)
        idx = pl.ds(b, FA_IN, stride=PITCH)
        s_r[idx, :] = re + s_r[idx, :] * bias
        s_i[idx, :] = im + s_i[idx, :] * bias
        return c
    lax.fori_loop(0, FB, s3, 0, unroll=UNROLL_ROWS)

    for s, rows in halves:
        def emit(a, slab, s=s, out=o_ref.at[rows, :]):
            out[a * FB:(a + 1) * FB, :] = (slab * s[a * PITCH:a * PITCH + FB, :]).astype(out.dtype)
        _short_conv_slabs(x_ref.at[rows, :], SEQ, t_scr, cwx_ref, cbx_ref, True, emit)


def _hyena_conv(u, u_col, x, x_col, conv_w, conv_b, cu_col, cx_col, spec_re, spec_im, layer, order,
                hy_bias_l, tabs, conv_input):
    g1, f2, f2c, g2 = tabs
    nblk = HY_W // LANE
    tok = lambda col: pl.BlockSpec((2 * SEQ, LANE), lambda c, p: (p, col + c))
    chan = lambda rows, col: pl.BlockSpec((rows, LANE), lambda c, p: (0, col + c))
    hspec = pl.BlockSpec((1, FFT_N, LANE), lambda c, p: (layer, 0, order * nblk + c))
    const = lambda shape: pl.BlockSpec(shape, lambda c, p: (0,) * len(shape))
    return pl.pallas_call(
        functools.partial(_hyconv_kernel, conv_input=conv_input),
        grid=(nblk, NB // 2),
        in_specs=[tok(u_col), tok(x_col),
                  chan(HY_SHORT, cu_col), chan(1, cu_col), chan(HY_SHORT, cx_col), chan(1, cx_col),
                  hspec, hspec,
                  pl.BlockSpec((1, 1, LANE), lambda c, p: (order, 0, c)),
                  const(g1.shape), const(f2.shape), const(f2c.shape), const(g2.shape)],
        out_specs=pl.BlockSpec((2 * SEQ, LANE), lambda c, p: (p, c)),
        out_shape=jax.ShapeDtypeStruct((LAT_ROWS, HY_W), BF16),
        scratch_shapes=[pltpu.VMEM((SEQ + 2 * HALO, LANE), F32),
                        pltpu.VMEM((FA_IN * PITCH, LANE), F32),
                        pltpu.VMEM((FA_IN * PITCH, LANE), F32),
                        pltpu.VMEM((FA * PITCH, LANE), F32),
                        pltpu.VMEM((FA * PITCH, LANE), F32)],
        compiler_params=_params("arbitrary", "arbitrary"),
        name=f"hyena_conv{order}",
    )(u, x, conv_w, conv_b, conv_w, conv_b, spec_re, spec_im,
      hy_bias_l.reshape(HY_ORDER, 1, HY_W), g1, f2, f2c, g2)


CN = 2 * CTX


def _ctx_tables():
    k = np.arange(CN)
    n = np.arange(CTX)
    f = np.exp(-2j * np.pi * k[:, None] * n[None, :] / CN)
    fwd = np.concatenate([f.real, f.imag], axis=0)
    t = np.arange(CTX) + CTX // 2
    g = np.exp(2j * np.pi * t[:, None] * k[None, :] / CN) / CN
    inv = np.concatenate([g.real, g.imag], axis=0)
    return jnp.asarray(fwd, dtype=BF16), jnp.asarray(inv, dtype=BF16)


def _ctx_hyena_kernel(v_ref, x1_ref, x2_ref, cwv_ref, cbv_ref, cw1_ref, cb1_ref, cw2_ref, cb2_ref,
                      h0_ref, h1_ref, bias_ref, fwd_ref, inv_ref, o_ref, t_scr, s_scr, g_scr):
    def conv_into(src_ref, b, cw_ref, cb_ref, dst, col):
        view = src_ref.at[b * CTX:(b + 1) * CTX, :]

        def emit(a, slab):
            dst[a * FB:(a + 1) * FB, col * LANE:(col + 1) * LANE] = slab
        _short_conv_slabs(view, CTX, t_scr, cw_ref, cb_ref, True, emit)

    def spectrum(h_ref):
        out = jnp.dot(fwd_ref[...], h_ref[0].astype(BF16), preferred_element_type=F32)
        return out[0:CN], out[CN:2 * CN]

    def long_conv(z2, hr, hi, bias):
        x = jnp.dot(fwd_ref[...], z2.astype(BF16), preferred_element_type=F32)
        xr, xi = _cplx(x, CN, LANE)
        vr = xr * hr - xi * hi
        vi = xr * hi + xi * hr
        rhs = jnp.concatenate([vr.astype(BF16), vi.astype(BF16)], axis=1)
        y = jnp.dot(inv_ref[...], rhs, preferred_element_type=F32)
        yr, yi = _cplx(y, CTX, LANE)
        return jnp.concatenate([yr, yi], axis=1) + z2 * jnp.concatenate([bias, bias], axis=1)

    h0r, h0i = spectrum(h0_ref)
    h1r, h1i = spectrum(h1_ref)
    for p in range(NB // 2):
        for half in range(2):
            conv_into(v_ref, 2 * p + half, cwv_ref, cbv_ref, s_scr, half)
            conv_into(x1_ref, 2 * p + half, cw1_ref, cb1_ref, g_scr, half)
        z = g_scr[...] * long_conv(s_scr[...], h0r, h0i, bias_ref[0])
        for half in range(2):
            conv_into(x2_ref, 2 * p + half, cw2_ref, cb2_ref, g_scr, half)
        y = g_scr[...] * long_conv(z, h1r, h1i, bias_ref[1])
        for half in range(2):
            b = 2 * p + half
            o_ref[b * CTX:(b + 1) * CTX, :] = y[:, half * LANE:(half + 1) * LANE].astype(BF16)


def _ctx_hyena(p, conv_w, conv_b, filt_ctx, layer, hy_bias_l, ctabs):
    fwd, inv = ctabs
    nblk = HY_W // LANE
    ctx_blk = LAT_ROWS // CTX_ROWS
    tok = lambda col: pl.BlockSpec((CTX_ROWS, LANE), lambda c: (ctx_blk, col + c))
    chan = lambda rows, col: pl.BlockSpec((rows, LANE), lambda c: (0, col + c))
    hspec = lambda order: pl.BlockSpec((1, CTX, LANE), lambda c: (layer, 0, order * nblk + c))
    const = lambda shape: pl.BlockSpec(shape, lambda c: (0,) * len(shape))
    return pl.pallas_call(
        _ctx_hyena_kernel,
        grid=(nblk,),
        in_specs=[tok(COL_HV), tok(COL_HX1), tok(COL_HX2),
                  chan(HY_SHORT, 0), chan(1, 0), chan(HY_SHORT, nblk), chan(1, nblk),
                  chan(HY_SHORT, 2 * nblk), chan(1, 2 * nblk),
                  hspec(0), hspec(1),
                  pl.BlockSpec((HY_ORDER, 1, LANE), lambda c: (0, 0, c)),
                  const(fwd.shape), const(inv.shape)],
        out_specs=pl.BlockSpec((CTX_ROWS, LANE), lambda c: (0, c)),
        out_shape=jax.ShapeDtypeStruct((CTX_ROWS, HY_W), BF16),
        scratch_shapes=[pltpu.VMEM((CTX + 2 * HALO, LANE), F32),
                        pltpu.VMEM((CTX, 2 * LANE), F32),
                        pltpu.VMEM((CTX, 2 * LANE), F32)],
        compiler_params=_params("parallel"),
        name="context_hyena",
    )(p, p, p, conv_w, conv_b, conv_w, conv_b, conv_w, conv_b, filt_ctx, filt_ctx,
      hy_bias_l.reshape(HY_ORDER, 1, HY_W), fwd, inv)


def _merge_kernel(oa_ref, oh_ref, ga_ref, gb_ref, x_ref, mod_ref, g2_ref, wpa_ref, wpb_ref, wo_ref,
                  *rest, routed):
    if routed:
        wr_ref, xo_ref, h_ref, rw_ref, wpa_s, wpb_s, wo_s = rest
    else:
        xo_ref, h_ref, wpa_s, wpb_s, wo_s = rest

    @pl.when(pl.program_id(0) == 0)
    def _():
        wpa_s[...] = wpa_ref[0].astype(BF16)
        wpb_s[...] = wpb_ref[0].astype(BF16)
        wo_s[...] = wo_ref[0].astype(BF16)

    a = jnp.dot(oa_ref[...], wpa_s[...], preferred_element_type=F32)
    b = jnp.dot(oh_ref[...], wpb_s[...], preferred_element_type=F32)
    mix = (jax.nn.sigmoid(ga_ref[...].astype(F32)) * a + jax.nn.sigmoid(gb_ref[...].astype(F32)) * b)
    y = jnp.dot(mix.astype(BF16), wo_s[...], preferred_element_type=F32)
    m = mod_ref[0]
    x = x_ref[...] + m[:, 2 * D:3 * D] * y
    xo_ref[...] = x
    h = _norm_mod(x, g2_ref[...], m[:, 3 * D:4 * D], m[:, 4 * D:5 * D])
    h_ref[...] = h.astype(BF16)
    if routed:
        wr = wr_ref[...]
        hh, wh = h.astype(BF16), wr.astype(BF16)
        hl, wl = (h - hh.astype(F32)).astype(BF16), (wr - wh.astype(F32)).astype(BF16)
        dot = functools.partial(jnp.dot, preferred_element_type=F32)
        rw_ref[...] = dot(hh, wh) + (dot(hh, wl) + dot(hl, wh))


def _merge(o_att, o_hy, p, x, mod_l, g2, w_pa, w_pb, w_o, layer, router):
    routed = router is not None
    tile = lambda w: pl.BlockSpec((TM, w), lambda i: (i, 0))
    full = lambda a: pl.BlockSpec(a.shape, lambda i: (0,) * a.ndim)
    layered = lambda a: pl.BlockSpec((1,) + a.shape[1:], lambda i: (layer, 0, 0))
    in_specs = [tile(ATT_W), tile(HY_W),
                pl.BlockSpec((TM, D), lambda i: (i, COL_GA)),
                pl.BlockSpec((TM, D), lambda i: (i, COL_GB)),
                tile(D),
                pl.BlockSpec((1, 1, N_MOD * D), lambda i: (_mod_row(i), 0, 0)),
                pl.BlockSpec((1, D), lambda i: (0, 0)),
                layered(w_pa), layered(w_pb), layered(w_o)]
    args = [o_att, o_hy, p, p, x, mod_l, g2, w_pa, w_pb, w_o]
    out_specs = [tile(D), tile(D)]
    out_shape = [jax.ShapeDtypeStruct((ROWS, D), F32), jax.ShapeDtypeStruct((ROWS, D), BF16)]
    if routed:
        wr = jnp.zeros((D, LANE), F32).at[:, :N_EXPERTS].set(router)
        in_specs.append(full(wr))
        args.append(wr)
        out_specs.append(tile(LANE))
        out_shape.append(jax.ShapeDtypeStruct((ROWS, LANE), F32))
    return pl.pallas_call(
        functools.partial(_merge_kernel, routed=routed),
        grid=(ROWS // TM,),
        in_specs=in_specs, out_specs=out_specs, out_shape=out_shape,
        scratch_shapes=[pltpu.VMEM(w_pa.shape[1:], BF16), pltpu.VMEM(w_pb.shape[1:], BF16),
                        pltpu.VMEM(w_o.shape[1:], BF16)],
        compiler_params=_params("arbitrary"),
        name="merge",
    )(*args)


TF = 256


NF = D_FF // TF


SWIGLU_SUB = 512


def _swiglu_step(h, w1_ref, w3_ref, w2_ref, acc):
    tf = w1_ref.shape[2]
    for lo in range(0, tf, SWIGLU_SUB):
        hi = min(lo + SWIGLU_SUB, tf)
        a = jnp.dot(h, w1_ref[0, :, lo:hi].astype(BF16), preferred_element_type=F32)
        b = jnp.dot(h, w3_ref[0, :, lo:hi].astype(BF16), preferred_element_type=F32)
        t = a * jax.nn.sigmoid(a) * b
        acc[...] += jnp.dot(t.astype(BF16), w2_ref[0, lo:hi, :].astype(BF16), preferred_element_type=F32)


def _residual_out(x, gate, y, fg_ref):
    x = x + gate * y
    if fg_ref is not None:
        x = x * lax.rsqrt(jnp.mean(x * x, axis=-1, keepdims=True) + EPS) * fg_ref[...]
    return x


def _ffn_kernel(h_ref, x_ref, mod_ref, w1_ref, w3_ref, w2_ref, *rest, final):
    rest = list(rest)
    fg_ref = rest.pop(0) if final else None
    o_ref, acc = rest
    f = pl.program_id(1)

    @pl.when(f == 0)
    def _():
        acc[...] = jnp.zeros_like(acc)

    _swiglu_step(h_ref[...], w1_ref, w3_ref, w2_ref, acc)

    @pl.when(f == NF - 1)
    def _():
        o_ref[...] = _residual_out(x_ref[...], mod_ref[0][:, 5 * D:6 * D], acc[...], fg_ref)


def _dense_mixer(h, x, mod_l, w1, w3, w2, e, final_g=None):
    final = final_g is not None
    rows = LAT_ROWS if final else ROWS
    tile = lambda w: pl.BlockSpec((TM, w), lambda i, f: (i, 0))
    in_specs = [tile(D), tile(D),
                pl.BlockSpec((1, 1, N_MOD * D), lambda i, f: (_mod_row(i), 0, 0)),
                pl.BlockSpec((1, D, TF), lambda i, f: (e, 0, f)),
                pl.BlockSpec((1, D, TF), lambda i, f: (e, 0, f)),
                pl.BlockSpec((1, TF, D), lambda i, f: (e, f, 0))]
    args = [h, x, mod_l, w1, w3, w2]
    if final:
        in_specs.append(pl.BlockSpec((1, D), lambda i, f: (0, 0)))
        args.append(final_g.reshape(1, D))
    return pl.pallas_call(
        functools.partial(_ffn_kernel, final=final),
        grid=(rows // TM, NF),
        in_specs=in_specs,
        out_specs=tile(D),
        out_shape=jax.ShapeDtypeStruct((rows, D), F32),
        scratch_shapes=[pltpu.VMEM((TM, D), F32)],
        compiler_params=_params("parallel", "arbitrary"),
        name="ffn",
    )(*args)


NT = ROWS // TM
MOE_CH = 128
MOE_CMAX = TM // MOE_CH
MOE_NCH = 2 * TM // MOE_CH + N_EXPERTS
MOE_SLOTS = MOE_NCH * MOE_CH
MOE_GC = 8
MOE_GROUPS = NT * MOE_NCH // MOE_GC + N_EXPERTS
MOE_SB = 512
MOE_TF = D_FF // 2
MOE_NF = D_FF // MOE_TF
PLAN_ROWS = 8


def _plan_kernel(lg_ref, col_ref, row_ref, nch_ref):
    lane = lax.broadcasted_iota(jnp.int32, (TM, LANE), 1)
    ninf = jnp.asarray(-jnp.inf, F32)
    lg = jnp.where(lane < N_EXPERTS, lg_ref[...], ninf)
    m1 = jnp.max(lg, axis=-1, keepdims=True)
    i1 = jnp.min(jnp.where(lg == m1, lane, LANE), axis=-1, keepdims=True)
    lg2 = jnp.where(lane == i1, ninf, lg)
    m2 = jnp.max(lg2, axis=-1, keepdims=True)
    i2 = jnp.min(jnp.where(lg2 == m2, lane, LANE), axis=-1, keepdims=True)
    e = jnp.exp(m2 - m1)
    den = 1.0 + e
    sel1 = lane == i1
    sel2 = lane == i2
    onehot = jnp.where(sel1, 1.0, jnp.where(sel2, 1.0, 0.0))
    tri = jnp.where(lax.broadcasted_iota(jnp.int32, (TM, TM), 1) < lax.broadcasted_iota(jnp.int32, (TM, TM), 0),
                    1.0, 0.0).astype(BF16)
    rank = jnp.dot(tri, onehot.astype(BF16), preferred_element_type=F32)
    cnt = jnp.sum(onehot, axis=0, keepdims=True)
    nch = jnp.floor((cnt + (MOE_CH - 1)) * (1.0 / MOE_CH))
    upper = jnp.where(lax.broadcasted_iota(jnp.int32, (LANE, LANE), 0) < lax.broadcasted_iota(jnp.int32, (LANE, LANE), 1),
                      1.0, 0.0).astype(BF16)
    cbase = jnp.dot(jnp.broadcast_to(nch, (8, LANE)).astype(BF16), upper,
                    preferred_element_type=F32)[0:1]
    slot = MOE_CH * cbase + rank
    slot0 = jnp.sum(jnp.where(sel1, slot, 0.0), axis=-1, keepdims=True)
    slot1 = jnp.sum(jnp.where(sel2, slot, 0.0), axis=-1, keepdims=True)
    col = jnp.where(lane == 0, slot0, jnp.where(lane == 1, slot1,
                    jnp.where(lane == 2, 1.0 / den, jnp.where(lane == 3, e / den, 0.0))))
    col_ref[...] = col
    row_ref[...] = jnp.transpose(col)[0:PLAN_ROWS]
    nch_ref[0] = nch


def _moe_plan(logits):
    return pl.pallas_call(
        _plan_kernel,
        grid=(NT,),
        in_specs=[pl.BlockSpec((TM, LANE), lambda i: (i, 0))],
        out_specs=[pl.BlockSpec((TM, LANE), lambda i: (i, 0)),
                   pl.BlockSpec((PLAN_ROWS, TM), lambda i: (0, i)),
                   pl.BlockSpec((1, 1, LANE), lambda i: (i, 0, 0))],
        out_shape=[jax.ShapeDtypeStruct((ROWS, LANE), F32),
                   jax.ShapeDtypeStruct((PLAN_ROWS, ROWS), F32),
                   jax.ShapeDtypeStruct((NT, 1, LANE), F32)],
        compiler_params=_params("parallel"),
        name="moe_plan",
    )(logits)


def _moe_tables(nch):
    n = nch[:, 0, :N_EXPERTS].astype(jnp.int32)
    cbase = jnp.cumsum(n, axis=1) - n
    used = jnp.sum(n, axis=1)
    c = jnp.arange(MOE_CMAX)
    valid = (c[None, None, :] < n.T[:, :, None]).reshape(N_EXPERTS, -1)
    cid = (jnp.arange(NT)[None, :, None] * MOE_NCH + cbase.T[:, :, None] + c[None, None, :]).reshape(N_EXPERTS, -1)
    ne = jnp.sum(n, axis=0)
    ge = (ne + MOE_GC - 1) // MOE_GC
    gend = jnp.cumsum(ge)
    nused = gend[-1]
    size = MOE_GROUPS * MOE_GC
    pos = (gend - ge)[:, None] * MOE_GC + jnp.cumsum(valid, axis=1) - 1
    dest = jnp.where(valid, pos, size).reshape(-1)
    tab = jnp.full((size,), -1, jnp.int32).at[dest].set(cid.reshape(-1).astype(jnp.int32), mode='drop')
    at = jnp.arange(size)
    tab = tab[jnp.maximum(lax.cummax(jnp.where(tab >= 0, at, -1)), 0)]
    gexp = jnp.minimum(jnp.sum(jnp.arange(MOE_GROUPS)[:, None] >= gend[None, :], axis=1), N_EXPERTS - 1)
    pool = jnp.zeros((NT * MOE_NCH,), jnp.int32).at[jnp.where(at < nused * MOE_GC, tab, NT * MOE_NCH)].set(
        at.astype(jnp.int32), mode='drop')
    return tab, gexp.astype(jnp.int32), nused.reshape(1).astype(jnp.int32), used.astype(jnp.int32), pool


def _dispatch_kernel(used_ref, h_ref, row_ref, xs_ref):
    used_rows = used_ref[pl.program_id(0)] * MOE_CH
    s0 = row_ref[0:1, :]
    s1 = row_ref[1:2, :]
    for blk in range(MOE_SLOTS // MOE_SB):
        rows = slice(blk * MOE_SB, (blk + 1) * MOE_SB)

        @pl.when(blk * MOE_SB < used_rows)
        def _():
            r = (lax.broadcasted_iota(jnp.int32, (MOE_SB, TM), 0) + blk * MOE_SB).astype(F32)
            s = jnp.where(r == s0, 1.0, jnp.where(r == s1, 1.0, 0.0)).astype(BF16)
            xs_ref[rows, :] = jnp.dot(s, h_ref[...], preferred_element_type=F32).astype(BF16)

        @pl.when(blk * MOE_SB >= used_rows)
        def _():
            xs_ref[rows, :] = jnp.zeros((MOE_SB, D), BF16)


def _moe_dispatch(h, plan_row, used):
    return pl.pallas_call(
        _dispatch_kernel,
        grid_spec=pltpu.PrefetchScalarGridSpec(
            num_scalar_prefetch=1, grid=(NT,),
            in_specs=[pl.BlockSpec((TM, D), lambda i, u: (i, 0)),
                      pl.BlockSpec((PLAN_ROWS, TM), lambda i, u: (0, i))],
            out_specs=pl.BlockSpec((MOE_SLOTS, D), lambda i, u: (i, 0))),
        out_shape=jax.ShapeDtypeStruct((NT * MOE_SLOTS, D), BF16),
        compiler_params=_params("parallel"),
        name="moe_dispatch",
    )(used, h, plan_row)


def _experts_kernel(tab_ref, gexp_ref, nused_ref, *refs):
    x_refs = refs[:MOE_GC]
    w1_ref, w3_ref, w2_ref, y_ref, x_scr, acc = refs[MOE_GC:]
    f = pl.program_id(1)

    @pl.when(pl.program_id(0) < nused_ref[0])
    def _():
        @pl.when(f == 0)
        def _():
            for j, r in enumerate(x_refs):
                x_scr[j * MOE_CH:(j + 1) * MOE_CH, :] = r[...]
            acc[...] = jnp.zeros_like(acc)

        _swiglu_step(x_scr[...], w1_ref, w3_ref, w2_ref, acc)

        @pl.when(f == MOE_NF - 1)
        def _():
            y_ref[...] = acc[...].astype(BF16)


def _moe_experts(xs, tab, gexp, nused, w1, w3, w2, e0):
    grp = lambda g, nu: jnp.minimum(g, nu[0] - 1)
    ftile = lambda g, f, nu: jnp.where(g < nu[0], f, MOE_NF - 1)
    chunk = lambda j: pl.BlockSpec((MOE_CH, D), lambda g, f, tab, ge, nu: (tab[grp(g, nu) * MOE_GC + j], 0))
    rows = MOE_GC * MOE_CH
    return pl.pallas_call(
        _experts_kernel,
        grid_spec=pltpu.PrefetchScalarGridSpec(
            num_scalar_prefetch=3, grid=(MOE_GROUPS, MOE_NF),
            in_specs=[chunk(j) for j in range(MOE_GC)] + [
                pl.BlockSpec((1, D, MOE_TF), lambda g, f, tab, ge, nu: (e0 + ge[grp(g, nu)], 0, ftile(g, f, nu))),
                pl.BlockSpec((1, D, MOE_TF), lambda g, f, tab, ge, nu: (e0 + ge[grp(g, nu)], 0, ftile(g, f, nu))),
                pl.BlockSpec((1, MOE_TF, D), lambda g, f, tab, ge, nu: (e0 + ge[grp(g, nu)], ftile(g, f, nu), 0))],
            out_specs=pl.BlockSpec((rows, D), lambda g, f, tab, ge, nu: (grp(g, nu), 0)),
            scratch_shapes=[pltpu.VMEM((rows, D), BF16), pltpu.VMEM((rows, D), F32)]),
        out_shape=jax.ShapeDtypeStruct((MOE_GROUPS * rows, D), BF16),
        compiler_params=_params("arbitrary", "arbitrary"),
        name="moe_experts",
    )(tab, gexp, nused, *([xs] * MOE_GC), w1, w3, w2)


def _combine_kernel(used_ref, pool_ref, *refs, final):
    refs = list(refs)
    y_refs = [refs.pop(0) for _ in range(MOE_NCH)]
    col_ref, x_ref, mod_ref = refs[:3]
    fg_ref = refs[3] if final else None
    o_ref, y_scr, acc = refs[-3:]
    used = used_ref[pl.program_id(0)]
    col = col_ref[...]
    s0, s1, g0, g1 = col[:, 0:1], col[:, 1:2], col[:, 2:3], col[:, 3:4]
    acc[...] = jnp.zeros_like(acc)
    per_blk = MOE_SB // MOE_CH
    for blk in range(MOE_SLOTS // MOE_SB):
        @pl.when(blk * per_blk < used)
        def _():
            for q in range(blk * per_blk, (blk + 1) * per_blk):
                y_scr[q * MOE_CH:(q + 1) * MOE_CH, :] = y_refs[q][...]
            c = (lax.broadcasted_iota(jnp.int32, (TM, MOE_SB), 1) + blk * MOE_SB).astype(F32)
            w = jnp.where(c == s0, g0, jnp.where(c == s1, g1, 0.0)).astype(BF16)
            acc[...] += jnp.dot(w, y_scr[blk * MOE_SB:(blk + 1) * MOE_SB, :], preferred_element_type=F32)
    o_ref[...] = _residual_out(x_ref[...], mod_ref[0][:, 5 * D:6 * D], acc[...], fg_ref)


def _moe_combine(y, pool, used, plan_col, x, mod_l, final_g=None):
    final = final_g is not None
    rows = LAT_ROWS if final else ROWS
    tile = lambda w: pl.BlockSpec((TM, w), lambda i, u, pc: (i, 0))
    chunk = lambda q: pl.BlockSpec((MOE_CH, D), lambda i, u, pc: (pc[i * MOE_NCH + q], 0))
    in_specs = [chunk(q) for q in range(MOE_NCH)] + [
        tile(LANE), tile(D), pl.BlockSpec((1, 1, N_MOD * D), lambda i, u, pc: (_mod_row(i), 0, 0))]
    args = [y] * MOE_NCH + [plan_col, x, mod_l]
    if final:
        in_specs.append(pl.BlockSpec((1, D), lambda i, u, pc: (0, 0)))
        args.append(final_g.reshape(1, D))
    return pl.pallas_call(
        functools.partial(_combine_kernel, final=final),
        grid_spec=pltpu.PrefetchScalarGridSpec(
            num_scalar_prefetch=2, grid=(rows // TM,),
            in_specs=in_specs, out_specs=tile(D),
            scratch_shapes=[pltpu.VMEM((MOE_SLOTS, D), BF16), pltpu.VMEM((TM, D), F32)]),
        out_shape=jax.ShapeDtypeStruct((rows, D), F32),
        compiler_params=_params("parallel"),
        name="moe_combine",
    )(used, pool, *args)


def _moe_mixer(h, logits, x, mod_l, w1, w3, w2, e0, final_g=None):
    plan_col, plan_row, nch = _moe_plan(logits)
    tab, gexp, nused, used, pool = _moe_tables(nch)
    xs = _moe_dispatch(h, plan_row, used)
    y = _moe_experts(xs, tab, gexp, nused, w1, w3, w2, e0)
    return _moe_combine(y, pool, used, plan_col, x, mod_l, final_g)


def kernel(x, c, ctx, c_ctx, w_mod, b_mod, norm1_g, norm2_g, w_in, rpb, hy_conv_w, hy_conv_b, hy_w1, hy_b1, hy_w2, hy_b2, hy_w3, hy_freq, hy_bias, w_pa, w_pb, w_o, ffn_w1, ffn_w3, ffn_w2, moe_router, moe_w1, moe_w3, moe_w2, final_g):
    xs = jnp.concatenate([x.reshape(LAT_ROWS, D), ctx.reshape(CTX_ROWS, D)], axis=0)
    mod = _modulation(c, c_ctx, w_mod, b_mod)
    tabs = _fft_tables()
    ctabs = _ctx_tables()
    filt_lat = _implicit_filters(SEQ, hy_w1, hy_b1, hy_w2, hy_b2, hy_w3, hy_freq)
    filt_ctx = _implicit_filters(CTX, hy_w1, hy_b1, hy_w2, hy_b2, hy_w3, hy_freq)
    spec_re, spec_im = _filter_spectra(filt_lat, tabs)
    nblk = HY_W // LANE

    moe_w = [w.reshape((-1,) + w.shape[2:]).astype(BF16) for w in (moe_w1, moe_w3, moe_w2)]

    for l in range(DEPTH):
        last = l == DEPTH - 1
        mod_l = mod[l].reshape(MOD_ROWS, 1, N_MOD * D)
        cw, cb = hy_conv_w[l], hy_conv_b[l].reshape(1, 3 * HY_W)
        p = _in_proj(xs, mod_l, norm1_g[l].reshape(1, D), w_in, l)

        o_att = jnp.concatenate([_neighbourhood_attention(p, rpb[l]), _context_attention(p)], axis=0)
        z = _hyena_conv(p, COL_HV, p, COL_HX1, cw, cb, 0, nblk, spec_re, spec_im, l, 0,
                        hy_bias[l], tabs, True)
        o_hy = _hyena_conv(z, 0, p, COL_HX2, cw, cb, 0, 2 * nblk, spec_re, spec_im, l, 1,
                           hy_bias[l], tabs, False)
        o_hy = jnp.concatenate([o_hy, _ctx_hyena(p, cw, cb, filt_ctx, l, hy_bias[l], ctabs)], axis=0)

        i = l // 2
        router = moe_router[i] if l % 2 else None
        res = _merge(o_att, o_hy, p, xs, mod_l, norm2_g[l].reshape(1, D), w_pa, w_pb, w_o, l, router)
        fg = final_g if last else None
        if l % 2 == 0:
            xs = _dense_mixer(res[1], res[0], mod_l, ffn_w1, ffn_w3, ffn_w2, i, final_g=fg)
        else:
            xs = _moe_mixer(res[1], res[2], res[0], mod_l, *moe_w, i * N_EXPERTS, final_g=fg)
    return xs.reshape(NB, SEQ, D)
```

```python
import functools
import math

import numpy as np
import jax
import jax.numpy as jnp
from jax import lax
from jax.experimental import pallas as pl
from jax.experimental.pallas import tpu as pltpu

F32 = jnp.float32
BF16 = jnp.bfloat16
HIGHEST = lax.Precision.HIGHEST

D = 1024
NB = 4
SEQ = 4096
DEPTH = 4
CTX = 256
GRID_W = 64
GRID_H = SEQ // GRID_W
N_HEADS = 8
HEAD_DIM = 64
ATT_W = N_HEADS * HEAD_DIM
WIN_ROWS = 8
WIN_COLS = 16
HY_W = 512
HY_ORDER = 2
HY_SHORT = 3
HY_EMB = 33
HY_BANDS = (HY_EMB - 1) // 2
HY_FILT = 64
HY_DECAY_TARGET = 1e-2
HY_MAX_DECAY = math.log(HY_DECAY_TARGET) / 0.3
HY_MIN_DECAY = math.log(HY_DECAY_TARGET) / 1.5
HY_DECAY_SHIFT = 0.05
PROJ_W = 3 * ATT_W + 3 * HY_W + 2 * D
D_FF = 2816
N_EXPERTS = 8
N_MOD = 6
EPS = 1e-6

LAT_ROWS = NB * SEQ
CTX_ROWS = NB * CTX
ROWS = LAT_ROWS + CTX_ROWS
MOD_ROWS = 8
CTX_MOD_ROW = NB

LANE = 128
TM = 1024
VMEM_LIMIT = 56 * 1024 * 1024

COL_Q, COL_K, COL_V = 0, ATT_W // LANE, 2 * ATT_W // LANE
COL_HV = 3 * ATT_W // LANE
COL_HX1 = COL_HV + HY_W // LANE
COL_HX2 = COL_HX1 + HY_W // LANE
COL_GA = (3 * ATT_W + 3 * HY_W) // D
COL_GB = COL_GA + 1


def _mod_row(i):
    return jnp.where(i < LAT_ROWS // TM, (i * TM) // SEQ, CTX_MOD_ROW)


def _params(*sem):
    return pltpu.CompilerParams(dimension_semantics=sem, vmem_limit_bytes=VMEM_LIMIT)


def _mod_kernel(s_ref, w_ref, b_ref, o_ref):
    s = s_ref[...]
    s = s * jax.nn.sigmoid(s)
    o_ref[0] = jnp.dot(s, w_ref[0], precision=HIGHEST, preferred_element_type=F32) + b_ref[0]


def _modulation(c, c_ctx, w_mod, b_mod):
    s = jnp.zeros((MOD_ROWS, D), F32).at[:NB].set(c).at[CTX_MOD_ROW].set(c_ctx)
    tn = 1536
    return pl.pallas_call(
        _mod_kernel,
        grid=(DEPTH, N_MOD * D // tn),
        in_specs=[pl.BlockSpec((MOD_ROWS, D), lambda l, j: (0, 0)),
                  pl.BlockSpec((1, D, tn), lambda l, j: (l, 0, j)),
                  pl.BlockSpec((1, 1, tn), lambda l, j: (l, 0, j))],
        out_specs=pl.BlockSpec((1, MOD_ROWS, tn), lambda l, j: (l, 0, j)),
        out_shape=jax.ShapeDtypeStruct((DEPTH, MOD_ROWS, N_MOD * D), F32),
        compiler_params=_params("parallel", "parallel"),
        name="modulation",
    )(s, w_mod, b_mod.reshape(DEPTH, 1, N_MOD * D))


def _norm_mod(x, g, shift, scale):
    y = x * lax.rsqrt(jnp.mean(x * x, axis=-1, keepdims=True) + EPS) * g
    return y * (1.0 + scale) + shift


def _inproj_kernel(x_ref, mod_ref, g_ref, w_ref, o_ref, h_scr):
    @pl.when(pl.program_id(1) == 0)
    def _():
        m = mod_ref[0]
        h = _norm_mod(x_ref[...], g_ref[...], m[:, 0:D], m[:, D:2 * D])
        h_scr[...] = h.astype(BF16)

    o_ref[...] = jnp.dot(h_scr[...], w_ref[0].astype(BF16),
                         preferred_element_type=F32).astype(BF16)


def _in_proj(x, mod_l, g, w, layer):
    tn = 1024
    return pl.pallas_call(
        _inproj_kernel,
        grid=(ROWS // TM, PROJ_W // tn),
        in_specs=[pl.BlockSpec((TM, D), lambda i, j: (i, 0)),
                  pl.BlockSpec((1, 1, N_MOD * D), lambda i, j: (_mod_row(i), 0, 0)),
                  pl.BlockSpec((1, D), lambda i, j: (0, 0)),
                  pl.BlockSpec((1, D, tn), lambda i, j: (layer, 0, j))],
        out_specs=pl.BlockSpec((TM, tn), lambda i, j: (i, j)),
        out_shape=jax.ShapeDtypeStruct((ROWS, PROJ_W), BF16),
        scratch_shapes=[pltpu.VMEM((TM, D), BF16)],
        compiler_params=_params("parallel", "arbitrary"),
        name="in_proj",
    )(x, mod_l, g, w)


NEG = -1e30


def _nt_dot(a, b):
    return lax.dot_general(a, b, (((1,), (1,)), ((), ())), preferred_element_type=F32)


def _attend_pair(qp, keys, values, biases):
    lane = lax.broadcasted_iota(jnp.int32, qp.shape, 1)
    qs = qp * jnp.asarray(HEAD_DIM ** -0.5, BF16)
    outs = []
    for a in range(2):
        sel = (lane < HEAD_DIM) if a == 0 else (lane >= HEAD_DIM)
        qa = jnp.where(sel, qs, jnp.zeros_like(qs))
        ss = []
        for k, b in zip(keys, biases):
            s = _nt_dot(qa, k)
            if b is not None:
                s = s + b[a]
            ss.append(s)
        m = functools.reduce(jnp.maximum, [jnp.max(s, axis=-1, keepdims=True) for s in ss])
        ps = [jnp.exp(s - m) for s in ss]
        den = functools.reduce(jnp.add, [jnp.sum(p, axis=-1, keepdims=True) for p in ps])
        o = functools.reduce(jnp.add, [jnp.dot(p.astype(BF16), v, preferred_element_type=F32)
                                       for p, v in zip(ps, values)])
        outs.append(o / den)
    return jnp.where(lane < HEAD_DIM, outs[0], outs[1])


NLOC = WIN_ROWS * GRID_W
NKEY = NLOC + CTX
NA_PAIRS = GRID_H // 2
NA_STEPS = NA_PAIRS + 2
NA_UNITS = 2 * N_HEADS


def _window_start(r):
    r0 = jnp.clip(r - WIN_ROWS // 2, 0, GRID_H - WIN_ROWS)
    return pl.multiple_of(r0 * GRID_W, GRID_W)


def _natt_kernel(q_ref, k_ref, v_ref, kc_ref, vc_ref, bt0_ref, bt1_ref, o_ref, s0, s1, p0, p1, l0, l1):
    j = pl.program_id(1)
    lane = lax.broadcasted_iota(jnp.int32, (GRID_W, LANE), 1)
    lo = lane < HEAD_DIM
    hi = lane >= HEAD_DIM

    @pl.when(j == 0)
    def _():
        s1[...] = jnp.zeros_like(s1)
        p0[...] = jnp.zeros_like(p0)
        p1[...] = jnp.zeros_like(p1)
        l0[...] = jnp.ones_like(l0)
        l1[...] = jnp.ones_like(l1)

    def step(s_cur, s_prev, p_cur, p_prev, l_cur, l_prev):
        ro = jnp.maximum(2 * j - 4, 0)
        for half in range(2):
            start = _window_start(ro + half)
            rows = slice(half * GRID_W, (half + 1) * GRID_W)
            for hp in range(N_HEADS // 2):
                cs = slice(hp * LANE, (hp + 1) * LANE)
                vw = v_ref[pl.ds(start, NLOC), cs]
                vc = vc_ref[:, cs]
                o = []
                for u in (half * N_HEADS + 2 * hp, half * N_HEADS + 2 * hp + 1):
                    acc = (jnp.dot(p_cur[u, :, 0:NLOC], vw, preferred_element_type=F32)
                           + jnp.dot(p_cur[u, :, NLOC:NKEY], vc, preferred_element_type=F32))
                    o.append(acc / l_cur[u])
                o_ref[rows, cs] = jnp.where(lo, o[0], o[1]).astype(BF16)

        for u in range(NA_UNITS):
            x = s_prev[u]
            e = jnp.exp(x - jnp.max(x, axis=-1, keepdims=True))
            l_prev[u] = jnp.broadcast_to(jnp.sum(e, axis=-1, keepdims=True), (GRID_W, LANE))
            p_prev[u] = e.astype(BF16)

        rs = jnp.minimum(2 * j, GRID_H - 2)
        for half, bt_ref in enumerate((bt0_ref, bt1_ref)):
            start = _window_start(rs + half)
            rows = slice(half * GRID_W, (half + 1) * GRID_W)
            for hp in range(N_HEADS // 2):
                cs = slice(hp * LANE, (hp + 1) * LANE)
                qs = q_ref[rows, cs] * jnp.asarray(HEAD_DIM ** -0.5, BF16)
                kw = k_ref[pl.ds(start, NLOC), cs]
                kc = kc_ref[:, cs]
                for a in range(2):
                    qa = jnp.where(lo if a == 0 else hi, qs, jnp.zeros_like(qs))
                    u = half * N_HEADS + 2 * hp + a
                    s_cur[u, :, 0:NLOC] = _nt_dot(qa, kw) + bt_ref[0, 2 * hp + a].astype(F32)
                    s_cur[u, :, NLOC:NKEY] = _nt_dot(qa, kc)

    @pl.when(j % 2 == 0)
    def _():
        step(s0, s1, p0, p1, l0, l1)

    @pl.when(j % 2 == 1)
    def _():
        step(s1, s0, p1, p0, l1, l0)


def _bias_tables(rpb_l):
    col = np.arange(GRID_W)
    c0 = np.clip(col - WIN_COLS // 2, 0, GRID_W - WIN_COLS)
    valid = (col[None, :] >= c0[:, None]) & (col[None, :] < c0[:, None] + WIN_COLS)
    dc = np.clip(col[None, :] - col[:, None] + (WIN_COLS - 1), 0, 2 * WIN_COLS - 2)
    t = rpb_l[:, :, dc]
    t = jnp.where(valid[None, None], t, NEG)
    tabs = []
    for d0 in range(WIN_ROWS):
        w = t[:, d0:d0 + WIN_ROWS]
        tabs.append(w.transpose(0, 2, 1, 3).reshape(N_HEADS, GRID_W, WIN_ROWS * GRID_W))
    return jnp.stack(tabs).astype(BF16)


def _bias_index(r):
    r0 = jnp.clip(r - WIN_ROWS // 2, 0, GRID_H - WIN_ROWS)
    return r0 - r + (WIN_ROWS - 1)


def _neighbourhood_attention(p, rpb_l):
    bt = _bias_tables(rpb_l)
    blk = GRID_H // 2
    ctx_blk0 = LAT_ROWS // CTX
    score_row = lambda j, half: jnp.minimum(2 * j, GRID_H - 2) + half
    bias_spec = lambda half: pl.BlockSpec((1, N_HEADS, GRID_W, NLOC),
                                          lambda b, j: (_bias_index(score_row(j, half)), 0, 0, 0))
    return pl.pallas_call(
        _natt_kernel,
        grid=(NB, NA_STEPS),
        in_specs=[pl.BlockSpec((2 * GRID_W, ATT_W), lambda b, j: (b * blk + jnp.minimum(j, blk - 1), 0)),
                  pl.BlockSpec((SEQ, ATT_W), lambda b, j: (b, 1)),
                  pl.BlockSpec((SEQ, ATT_W), lambda b, j: (b, 2)),
                  pl.BlockSpec((CTX, ATT_W), lambda b, j: (ctx_blk0 + b, 1)),
                  pl.BlockSpec((CTX, ATT_W), lambda b, j: (ctx_blk0 + b, 2)),
                  bias_spec(0), bias_spec(1)],
        out_specs=pl.BlockSpec((2 * GRID_W, ATT_W), lambda b, j: (b * blk + jnp.maximum(j - 2, 0), 0)),
        out_shape=jax.ShapeDtypeStruct((LAT_ROWS, ATT_W), BF16),
        scratch_shapes=[pltpu.VMEM((NA_UNITS, GRID_W, NKEY), F32)] * 2
                       + [pltpu.VMEM((NA_UNITS, GRID_W, NKEY), BF16)] * 2
                       + [pltpu.VMEM((NA_UNITS, GRID_W, LANE), F32)] * 2,
        compiler_params=_params("parallel", "arbitrary"),
        name="neighbourhood_attention",
    )(p, p, p, p, p, bt, bt)


def _catt_kernel(q_ref, k_ref, v_ref, o_ref):
    for hp in range(N_HEADS // 2):
        cs = slice(hp * LANE, (hp + 1) * LANE)
        o = _attend_pair(q_ref[:, cs], [k_ref[:, cs]], [v_ref[:, cs]], [None])
        o_ref[:, cs] = o.astype(BF16)


def _context_attention(p):
    ctx_blk0 = LAT_ROWS // CTX
    return pl.pallas_call(
        _catt_kernel,
        grid=(NB,),
        in_specs=[pl.BlockSpec((CTX, ATT_W), lambda b: (ctx_blk0 + b, 0)),
                  pl.BlockSpec((CTX, ATT_W), lambda b: (ctx_blk0 + b, 1)),
                  pl.BlockSpec((CTX, ATT_W), lambda b: (ctx_blk0 + b, 2))],
        out_specs=pl.BlockSpec((CTX, ATT_W), lambda b: (b, 0)),
        out_shape=jax.ShapeDtypeStruct((CTX_ROWS, ATT_W), BF16),
        compiler_params=_params("parallel"),
        name="context_attention",
    )(p, p, p)


def _filter_features(L):
    pos = np.arange(L, dtype=np.float64)
    t = pos / max(L - 1, 1)
    bands = np.linspace(1e-4, HY_BANDS - 1, HY_BANDS)
    ang = (2.0 * math.pi / L) * pos[:, None] * bands[None, :]
    z = np.concatenate([t[:, None], np.cos(ang), -np.sin(ang)], axis=-1)
    zp = np.zeros((L, LANE), np.float32)
    zp[:, :HY_EMB] = z
    return zp


def _filter_kernel(z_ref, w1_ref, b1_ref, w2_ref, b2_ref, w3_ref, fr_ref, rate_ref, o_ref, *, L, tl):
    dot = functools.partial(jnp.dot, precision=HIGHEST, preferred_element_type=F32)
    fr = fr_ref[0]
    a = jnp.sin(fr[0:1] * (dot(z_ref[...], w1_ref[0]) + b1_ref[0]))
    a = jnp.sin(fr[1:2] * (dot(a, w2_ref[0]) + b2_ref[0]))
    h = dot(a, w3_ref[0])
    pos = (pl.program_id(1) * tl + lax.broadcasted_iota(jnp.int32, (tl, 1), 0)).astype(F32)
    dist = jnp.abs(pos - float(L // 2)) * (2.0 / L)
    o_ref[0] = h * (jnp.exp(-dist * rate_ref[...]) + HY_DECAY_SHIFT)


def _implicit_filters(L, w1, b1, w2, b2, w3, freq):
    tl = min(L, 512)
    z = jnp.asarray(_filter_features(L))
    w1p = jnp.zeros((DEPTH, LANE, HY_FILT), F32).at[:, :HY_EMB].set(w1)
    rates = np.abs(np.linspace(HY_MIN_DECAY, HY_MAX_DECAY, HY_ORDER * HY_W)).astype(np.float32)
    cw = HY_ORDER * HY_W
    return pl.pallas_call(
        functools.partial(_filter_kernel, L=L, tl=tl),
        grid=(DEPTH, L // tl),
        in_specs=[pl.BlockSpec((tl, LANE), lambda l, i: (i, 0)),
                  pl.BlockSpec((1, LANE, HY_FILT), lambda l, i: (l, 0, 0)),
                  pl.BlockSpec((1, 1, HY_FILT), lambda l, i: (l, 0, 0)),
                  pl.BlockSpec((1, HY_FILT, HY_FILT), lambda l, i: (l, 0, 0)),
                  pl.BlockSpec((1, 1, HY_FILT), lambda l, i: (l, 0, 0)),
                  pl.BlockSpec((1, HY_FILT, cw), lambda l, i: (l, 0, 0)),
                  pl.BlockSpec((1, 2, HY_FILT), lambda l, i: (l, 0, 0)),
                  pl.BlockSpec((1, cw), lambda l, i: (0, 0))],
        out_specs=pl.BlockSpec((1, tl, cw), lambda l, i: (l, i, 0)),
        out_shape=jax.ShapeDtypeStruct((DEPTH, L, cw), F32),
        compiler_params=_params("parallel", "parallel"),
        name=f"implicit_filters_{L}",
    )(z, w1p, b1.reshape(DEPTH, 1, HY_FILT), w2, b2.reshape(DEPTH, 1, HY_FILT), w3, freq,
      jnp.asarray(rates).reshape(1, cw))


FFT_N = 2 * SEQ
FA = 64
FB = 128
FA_IN = SEQ // FB
PITCH = 136
HALO = 8
UNROLL_ROWS = 16
UNROLL_SLABS = 8


def _fft_tables():
    a = np.arange(FA_IN)
    r = np.arange(FA)
    b = np.arange(FB)
    q = np.arange(FB)
    ph = a[None, None, :] * r[None, :, None] / FA + b[:, None, None] * r[None, :, None] / FFT_N
    g1 = np.exp(-2j * np.pi * ph)
    g1 = np.concatenate([g1.real, g1.imag], axis=1)
    f = np.exp(-2j * np.pi * q[:, None] * b[None, :] / FB)
    f2 = np.concatenate([f.real, f.imag], axis=0)
    f2c = np.concatenate([f.real, -f.imag], axis=0)
    ao = np.arange(FA_IN) + (SEQ // 2) // FB
    ph2 = ao[None, :, None] * r[None, None, :] / FA + b[:, None, None] * r[None, None, :] / FFT_N
    g2 = np.exp(2j * np.pi * ph2) / FFT_N
    g2 = np.concatenate([g2.real, g2.imag], axis=1)
    return tuple(jnp.asarray(t, dtype=BF16) for t in (g1, f2, f2c, g2))


def _cplx(out, m, n):
    re = out[0:m, 0:n] - out[m:2 * m, n:2 * n]
    im = out[0:m, n:2 * n] + out[m:2 * m, 0:n]
    return re, im


def _short_conv_slabs(src_ref, rows, t_scr, cw_ref, cb_ref, apply_conv, emit):
    nslab = rows // FB
    if not apply_conv:
        for a in range(nslab):
            emit(a, src_ref[a * FB:(a + 1) * FB, :].astype(F32))
        return
    zero = jnp.zeros((HALO, LANE), F32)
    t_scr[0:HALO, :] = zero
    t_scr[HALO + rows:2 * HALO + rows, :] = zero
    for a in range(nslab):
        t_scr[HALO + a * FB:HALO + (a + 1) * FB, :] = src_ref[a * FB:(a + 1) * FB, :].astype(F32)
    w0, w1, w2, bias = cw_ref[0:1, :], cw_ref[1:2, :], cw_ref[2:3, :], cb_ref[...]
    for a in range(nslab):
        o = HALO + a * FB
        y = (w0 * t_scr[o - 1:o - 1 + FB, :] + w1 * t_scr[o:o + FB, :]
             + w2 * t_scr[o + 1:o + 1 + FB, :] + bias)
        emit(a, y)


def _fwd_stage1(b, s_refs, g1_ref, y_r, y_i):
    cols = [s[pl.ds(b, FA_IN, stride=PITCH), :].astype(BF16) for s in s_refs]
    rhs = cols[0] if len(cols) == 1 else jnp.concatenate(cols, axis=1)
    out = jnp.dot(g1_ref[b], rhs, preferred_element_type=F32)
    if len(cols) == 1:
        re, im = out[0:FA], out[FA:2 * FA]
    else:
        re, im = _cplx(out, FA, LANE)
    y_r[pl.ds(b, FA, stride=PITCH), :] = re
    y_i[pl.ds(b, FA, stride=PITCH), :] = im


def _fwd_stage2(r, y_r, y_i, f_ref):
    o = pl.multiple_of(r * PITCH, 8)
    rhs = jnp.concatenate([y_r[pl.ds(o, FB), :].astype(BF16), y_i[pl.ds(o, FB), :].astype(BF16)], axis=1)
    out = jnp.dot(f_ref[...], rhs, preferred_element_type=F32)
    return _cplx(out, FB, LANE)


def _spectrum_kernel(h_ref, g1_ref, f_ref, hr_ref, hi_ref, s_scr, y_r, y_i):
    for a in range(FA_IN):
        s_scr[a * PITCH:a * PITCH + FB, :] = h_ref[0, a * FB:(a + 1) * FB, :]

    def s1(b, c):
        _fwd_stage1(b, [s_scr], g1_ref, y_r, y_i)
        return c
    lax.fori_loop(0, FB, s1, 0, unroll=UNROLL_ROWS)

    def s2(r, c):
        xr, xi = _fwd_stage2(r, y_r, y_i, f_ref)
        o = pl.multiple_of(r * FB, FB)
        hr_ref[0, pl.ds(o, FB), :] = xr
        hi_ref[0, pl.ds(o, FB), :] = xi
        return c
    lax.fori_loop(0, FA, s2, 0, unroll=UNROLL_SLABS)


def _filter_spectra(h, tabs):
    g1, f2, _, _ = tabs
    cw = HY_ORDER * HY_W
    spec = pl.BlockSpec((1, FFT_N, LANE), lambda l, c: (l, 0, c))
    return pl.pallas_call(
        _spectrum_kernel,
        grid=(DEPTH, cw // LANE),
        in_specs=[pl.BlockSpec((1, SEQ, LANE), lambda l, c: (l, 0, c)),
                  pl.BlockSpec(g1.shape, lambda l, c: (0, 0, 0)),
                  pl.BlockSpec(f2.shape, lambda l, c: (0, 0))],
        out_specs=[spec, spec],
        out_shape=[jax.ShapeDtypeStruct((DEPTH, FFT_N, cw), F32)] * 2,
        scratch_shapes=[pltpu.VMEM((FA_IN * PITCH, LANE), F32),
                        pltpu.VMEM((FA * PITCH, LANE), F32),
                        pltpu.VMEM((FA * PITCH, LANE), F32)],
        compiler_params=_params("parallel", "parallel"),
        name="filter_spectra",
    )(h, g1, f2)


def _hyconv_kernel(u_ref, x_ref, cwu_ref, cbu_ref, cwx_ref, cbx_ref,
                   hr_ref, hi_ref, bias_ref, g1_ref, f_ref, fc_ref, g2_ref,
                   o_ref, t_scr, s_r, s_i, y_r, y_i, *, conv_input):
    halves = [(s_r, slice(0, SEQ)), (s_i, slice(SEQ, 2 * SEQ))]

    def fill(s):
        def emit(a, slab):
            s[a * PITCH:a * PITCH + FB, :] = slab
        return emit
    for s, rows in halves:
        _short_conv_slabs(u_ref.at[rows, :], SEQ, t_scr, cwu_ref, cbu_ref, conv_input, fill(s))

    def s1(b, c):
        _fwd_stage1(b, [s_r, s_i], g1_ref, y_r, y_i)
        return c
    lax.fori_loop(0, FB, s1, 0, unroll=UNROLL_ROWS)

    def s2(r, c):
        xr, xi = _fwd_stage2(r, y_r, y_i, f_ref)
        oh = pl.multiple_of(r * FB, FB)
        hr = hr_ref[0, pl.ds(oh, FB), :]
        hi = hi_ref[0, pl.ds(oh, FB), :]
        vr = xr * hr - xi * hi
        vi = xr * hi + xi * hr
        rhs = jnp.concatenate([vr.astype(BF16), vi.astype(BF16)], axis=1)
        wr, wi = _cplx(jnp.dot(fc_ref[...], rhs, preferred_element_type=F32), FB, LANE)
        o = pl.multiple_of(r * PITCH, 8)
        y_r[pl.ds(o, FB), :] = wr
        y_i[pl.ds(o, FB), :] = wi
        return c
    lax.fori_loop(0, FA, s2, 0, unroll=UNROLL_SLABS)

    bias = bias_ref[0]

    def s3(b, c):
        rhs = jnp.concatenate([y_r[pl.ds(b, FA, stride=PITCH), :].astype(BF16),
                               y_i[pl.ds(b, FA, stride=PITCH), :].astype(BF16)], axis=1)
        re, im = _cplx(jnp.dot(g2_ref[b], rhs, preferred_element_type=F32), FA_IN, LANE)
        idx = pl.ds(b, FA_IN, stride=PITCH)
        s_r[idx, :] = re + s_r[idx, :] * bias
        s_i[idx, :] = im + s_i[idx, :] * bias
        return c
    lax.fori_loop(0, FB, s3, 0, unroll=UNROLL_ROWS)

    for s, rows in halves:
        def emit(a, slab, s=s, out=o_ref.at[rows, :]):
            out[a * FB:(a + 1) * FB, :] = (slab * s[a * PITCH:a * PITCH + FB, :]).astype(out.dtype)
        _short_conv_slabs(x_ref.at[rows, :], SEQ, t_scr, cwx_ref, cbx_ref, True, emit)


def _hyena_conv(u, u_col, x, x_col, conv_w, conv_b, cu_col, cx_col, spec_re, spec_im, layer, order,
                hy_bias_l, tabs, conv_input):
    g1, f2, f2c, g2 = tabs
    nblk = HY_W // LANE
    tok = lambda col: pl.BlockSpec((2 * SEQ, LANE), lambda c, p: (p, col + c))
    chan = lambda rows, col: pl.BlockSpec((rows, LANE), lambda c, p: (0, col + c))
    hspec = pl.BlockSpec((1, FFT_N, LANE), lambda c, p: (layer, 0, order * nblk + c))
    const = lambda shape: pl.BlockSpec(shape, lambda c, p: (0,) * len(shape))
    return pl.pallas_call(
        functools.partial(_hyconv_kernel, conv_input=conv_input),
        grid=(nblk, NB // 2),
        in_specs=[tok(u_col), tok(x_col),
                  chan(HY_SHORT, cu_col), chan(1, cu_col), chan(HY_SHORT, cx_col), chan(1, cx_col),
                  hspec, hspec,
                  pl.BlockSpec((1, 1, LANE), lambda c, p: (order, 0, c)),
                  const(g1.shape), const(f2.shape), const(f2c.shape), const(g2.shape)],
        out_specs=pl.BlockSpec((2 * SEQ, LANE), lambda c, p: (p, c)),
        out_shape=jax.ShapeDtypeStruct((LAT_ROWS, HY_W), BF16),
        scratch_shapes=[pltpu.VMEM((SEQ + 2 * HALO, LANE), F32),
                        pltpu.VMEM((FA_IN * PITCH, LANE), F32),
                        pltpu.VMEM((FA_IN * PITCH, LANE), F32),
                        pltpu.VMEM((FA * PITCH, LANE), F32),
                        pltpu.VMEM((FA * PITCH, LANE), F32)],
        compiler_params=_params("arbitrary", "arbitrary"),
        name=f"hyena_conv{order}",
    )(u, x, conv_w, conv_b, conv_w, conv_b, spec_re, spec_im,
      hy_bias_l.reshape(HY_ORDER, 1, HY_W), g1, f2, f2c, g2)


CN = 2 * CTX


def _ctx_tables():
    k = np.arange(CN)
    n = np.arange(CTX)
    f = np.exp(-2j * np.pi * k[:, None] * n[None, :] / CN)
    fwd = np.concatenate([f.real, f.imag], axis=0)
    t = np.arange(CTX) + CTX // 2
    g = np.exp(2j * np.pi * t[:, None] * k[None, :] / CN) / CN
    inv = np.concatenate([g.real, g.imag], axis=0)
    return jnp.asarray(fwd, dtype=BF16), jnp.asarray(inv, dtype=BF16)


def _ctx_hyena_kernel(v_ref, x1_ref, x2_ref, cwv_ref, cbv_ref, cw1_ref, cb1_ref, cw2_ref, cb2_ref,
                      h0_ref, h1_ref, bias_ref, fwd_ref, inv_ref, o_ref, t_scr, s_scr, g_scr):
    def conv_into(src_ref, b, cw_ref, cb_ref, dst, col):
        view = src_ref.at[b * CTX:(b + 1) * CTX, :]

        def emit(a, slab):
            dst[a * FB:(a + 1) * FB, col * LANE:(col + 1) * LANE] = slab
        _short_conv_slabs(view, CTX, t_scr, cw_ref, cb_ref, True, emit)

    def spectrum(h_ref):
        out = jnp.dot(fwd_ref[...], h_ref[0].astype(BF16), preferred_element_type=F32)
        return out[0:CN], out[CN:2 * CN]

    def long_conv(z2, hr, hi, bias):
        x = jnp.dot(fwd_ref[...], z2.astype(BF16), preferred_element_type=F32)
        xr, xi = _cplx(x, CN, LANE)
        vr = xr * hr - xi * hi
        vi = xr * hi + xi * hr
        rhs = jnp.concatenate([vr.astype(BF16), vi.astype(BF16)], axis=1)
        y = jnp.dot(inv_ref[...], rhs, preferred_element_type=F32)
        yr, yi = _cplx(y, CTX, LANE)
        return jnp.concatenate([yr, yi], axis=1) + z2 * jnp.concatenate([bias, bias], axis=1)

    h0r, h0i = spectrum(h0_ref)
    h1r, h1i = spectrum(h1_ref)
    for p in range(NB // 2):
        for half in range(2):
            conv_into(v_ref, 2 * p + half, cwv_ref, cbv_ref, s_scr, half)
            conv_into(x1_ref, 2 * p + half, cw1_ref, cb1_ref, g_scr, half)
        z = g_scr[...] * long_conv(s_scr[...], h0r, h0i, bias_ref[0])
        for half in range(2):
            conv_into(x2_ref, 2 * p + half, cw2_ref, cb2_ref, g_scr, half)
        y = g_scr[...] * long_conv(z, h1r, h1i, bias_ref[1])
        for half in range(2):
            b = 2 * p + half
            o_ref[b * CTX:(b + 1) * CTX, :] = y[:, half * LANE:(half + 1) * LANE].astype(BF16)


def _ctx_hyena(p, conv_w, conv_b, filt_ctx, layer, hy_bias_l, ctabs):
    fwd, inv = ctabs
    nblk = HY_W // LANE
    ctx_blk = LAT_ROWS // CTX_ROWS
    tok = lambda col: pl.BlockSpec((CTX_ROWS, LANE), lambda c: (ctx_blk, col + c))
    chan = lambda rows, col: pl.BlockSpec((rows, LANE), lambda c: (0, col + c))
    hspec = lambda order: pl.BlockSpec((1, CTX, LANE), lambda c: (layer, 0, order * nblk + c))
    const = lambda shape: pl.BlockSpec(shape, lambda c: (0,) * len(shape))
    return pl.pallas_call(
        _ctx_hyena_kernel,
        grid=(nblk,),
        in_specs=[tok(COL_HV), tok(COL_HX1), tok(COL_HX2),
                  chan(HY_SHORT, 0), chan(1, 0), chan(HY_SHORT, nblk), chan(1, nblk),
                  chan(HY_SHORT, 2 * nblk), chan(1, 2 * nblk),
                  hspec(0), hspec(1),
                  pl.BlockSpec((HY_ORDER, 1, LANE), lambda c: (0, 0, c)),
                  const(fwd.shape), const(inv.shape)],
        out_specs=pl.BlockSpec((CTX_ROWS, LANE), lambda c: (0, c)),
        out_shape=jax.ShapeDtypeStruct((CTX_ROWS, HY_W), BF16),
        scratch_shapes=[pltpu.VMEM((CTX + 2 * HALO, LANE), F32),
                        pltpu.VMEM((CTX, 2 * LANE), F32),
                        pltpu.VMEM((CTX, 2 * LANE), F32)],
        compiler_params=_params("parallel"),
        name="context_hyena",
    )(p, p, p, conv_w, conv_b, conv_w, conv_b, conv_w, conv_b, filt_ctx, filt_ctx,
      hy_bias_l.reshape(HY_ORDER, 1, HY_W), fwd, inv)


def _merge_kernel(oa_ref, oh_ref, ga_ref, gb_ref, x_ref, mod_ref, g2_ref, wpa_ref, wpb_ref, wo_ref,
                  *rest, routed):
    if routed:
        wr_ref, xo_ref, h_ref, rw_ref, wpa_s, wpb_s, wo_s = rest
    else:
        xo_ref, h_ref, wpa_s, wpb_s, wo_s = rest

    @pl.when(pl.program_id(0) == 0)
    def _():
        wpa_s[...] = wpa_ref[0].astype(BF16)
        wpb_s[...] = wpb_ref[0].astype(BF16)
        wo_s[...] = wo_ref[0].astype(BF16)

    a = jnp.dot(oa_ref[...], wpa_s[...], preferred_element_type=F32)
    b = jnp.dot(oh_ref[...], wpb_s[...], preferred_element_type=F32)
    mix = (jax.nn.sigmoid(ga_ref[...].astype(F32)) * a + jax.nn.sigmoid(gb_ref[...].astype(F32)) * b)
    y = jnp.dot(mix.astype(BF16), wo_s[...], preferred_element_type=F32)
    m = mod_ref[0]
    x = x_ref[...] + m[:, 2 * D:3 * D] * y
    xo_ref[...] = x
    h = _norm_mod(x, g2_ref[...], m[:, 3 * D:4 * D], m[:, 4 * D:5 * D])
    h_ref[...] = h.astype(BF16)
    if routed:
        wr = wr_ref[...]
        hh, wh = h.astype(BF16), wr.astype(BF16)
        hl, wl = (h - hh.astype(F32)).astype(BF16), (wr - wh.astype(F32)).astype(BF16)
        dot = functools.partial(jnp.dot, preferred_element_type=F32)
        rw_ref[...] = dot(hh, wh) + (dot(hh, wl) + dot(hl, wh))


def _merge(o_att, o_hy, p, x, mod_l, g2, w_pa, w_pb, w_o, layer, router):
    routed = router is not None
    tile = lambda w: pl.BlockSpec((TM, w), lambda i: (i, 0))
    full = lambda a: pl.BlockSpec(a.shape, lambda i: (0,) * a.ndim)
    layered = lambda a: pl.BlockSpec((1,) + a.shape[1:], lambda i: (layer, 0, 0))
    in_specs = [tile(ATT_W), tile(HY_W),
                pl.BlockSpec((TM, D), lambda i: (i, COL_GA)),
                pl.BlockSpec((TM, D), lambda i: (i, COL_GB)),
                tile(D),
                pl.BlockSpec((1, 1, N_MOD * D), lambda i: (_mod_row(i), 0, 0)),
                pl.BlockSpec((1, D), lambda i: (0, 0)),
                layered(w_pa), layered(w_pb), layered(w_o)]
    args = [o_att, o_hy, p, p, x, mod_l, g2, w_pa, w_pb, w_o]
    out_specs = [tile(D), tile(D)]
    out_shape = [jax.ShapeDtypeStruct((ROWS, D), F32), jax.ShapeDtypeStruct((ROWS, D), BF16)]
    if routed:
        wr = jnp.zeros((D, LANE), F32).at[:, :N_EXPERTS].set(router)
        in_specs.append(full(wr))
        args.append(wr)
        out_specs.append(tile(LANE))
        out_shape.append(jax.ShapeDtypeStruct((ROWS, LANE), F32))
    return pl.pallas_call(
        functools.partial(_merge_kernel, routed=routed),
        grid=(ROWS // TM,),
        in_specs=in_specs, out_specs=out_specs, out_shape=out_shape,
        scratch_shapes=[pltpu.VMEM(w_pa.shape[1:], BF16), pltpu.VMEM(w_pb.shape[1:], BF16),
                        pltpu.VMEM(w_o.shape[1:], BF16)],
        compiler_params=_params("arbitrary"),
        name="merge",
    )(*args)


TF = D_FF // 2


NF = D_FF // TF


SWIGLU_SUB = 512


def _swiglu_step(h, w1_ref, w3_ref, w2_ref, acc):
    tf = w1_ref.shape[2]
    for lo in range(0, tf, SWIGLU_SUB):
        hi = min(lo + SWIGLU_SUB, tf)
        a = jnp.dot(h, w1_ref[0, :, lo:hi].astype(BF16), preferred_element_type=F32)
        b = jnp.dot(h, w3_ref[0, :, lo:hi].astype(BF16), preferred_element_type=F32)
        t = a * jax.nn.sigmoid(a) * b
        acc[...] += jnp.dot(t.astype(BF16), w2_ref[0, lo:hi, :].astype(BF16), preferred_element_type=F32)


def _residual_out(x, gate, y, fg_ref):
    x = x + gate * y
    if fg_ref is not None:
        x = x * lax.rsqrt(jnp.mean(x * x, axis=-1, keepdims=True) + EPS) * fg_ref[...]
    return x


def _ffn_kernel(h_ref, x_ref, mod_ref, w1_ref, w3_ref, w2_ref, *rest, final):
    rest = list(rest)
    fg_ref = rest.pop(0) if final else None
    o_ref, acc = rest
    f = pl.program_id(1)

    @pl.when(f == 0)
    def _():
        acc[...] = jnp.zeros_like(acc)

    _swiglu_step(h_ref[...], w1_ref, w3_ref, w2_ref, acc)

    @pl.when(f == NF - 1)
    def _():
        o_ref[...] = _residual_out(x_ref[...], mod_ref[0][:, 5 * D:6 * D], acc[...], fg_ref)


def _dense_mixer(h, x, mod_l, w1, w3, w2, e, final_g=None):
    final = final_g is not None
    rows = LAT_ROWS if final else ROWS
    tile = lambda w: pl.BlockSpec((TM, w), lambda i, f: (i, 0))
    in_specs = [tile(D), tile(D),
                pl.BlockSpec((1, 1, N_MOD * D), lambda i, f: (_mod_row(i), 0, 0)),
                pl.BlockSpec((1, D, TF), lambda i, f: (e, 0, f)),
                pl.BlockSpec((1, D, TF), lambda i, f: (e, 0, f)),
                pl.BlockSpec((1, TF, D), lambda i, f: (e, f, 0))]
    args = [h, x, mod_l, w1, w3, w2]
    if final:
        in_specs.append(pl.BlockSpec((1, D), lambda i, f: (0, 0)))
        args.append(final_g.reshape(1, D))
    return pl.pallas_call(
        functools.partial(_ffn_kernel, final=final),
        grid=(rows // TM, NF),
        in_specs=in_specs,
        out_specs=tile(D),
        out_shape=jax.ShapeDtypeStruct((rows, D), F32),
        scratch_shapes=[pltpu.VMEM((TM, D), F32)],
        compiler_params=_params("parallel", "arbitrary"),
        name="ffn",
    )(*args)


NT = ROWS // TM
MOE_CH = 64
MOE_CMAX = TM // MOE_CH
MOE_NCH = 2 * TM // MOE_CH + N_EXPERTS
MOE_SLOTS = MOE_NCH * MOE_CH
MOE_GC = 16
MOE_GROUPS = NT * MOE_NCH // MOE_GC + N_EXPERTS
MOE_SB = 512
MOE_TF, MOE_NF = TF, NF
PLAN_ROWS = 8


def _plan_kernel(lg_ref, col_ref, row_ref, nch_ref):
    lane = lax.broadcasted_iota(jnp.int32, (TM, LANE), 1)
    ninf = jnp.asarray(-jnp.inf, F32)
    lg = jnp.where(lane < N_EXPERTS, lg_ref[...], ninf)
    m1 = jnp.max(lg, axis=-1, keepdims=True)
    i1 = jnp.min(jnp.where(lg == m1, lane, LANE), axis=-1, keepdims=True)
    lg2 = jnp.where(lane == i1, ninf, lg)
    m2 = jnp.max(lg2, axis=-1, keepdims=True)
    i2 = jnp.min(jnp.where(lg2 == m2, lane, LANE), axis=-1, keepdims=True)
    e = jnp.exp(m2 - m1)
    den = 1.0 + e
    sel1 = lane == i1
    sel2 = lane == i2
    onehot = jnp.where(sel1, 1.0, jnp.where(sel2, 1.0, 0.0))
    tri = jnp.where(lax.broadcasted_iota(jnp.int32, (TM, TM), 1) < lax.broadcasted_iota(jnp.int32, (TM, TM), 0),
                    1.0, 0.0).astype(BF16)
    rank = jnp.dot(tri, onehot.astype(BF16), preferred_element_type=F32)
    cnt = jnp.sum(onehot, axis=0, keepdims=True)
    nch = jnp.floor((cnt + (MOE_CH - 1)) * (1.0 / MOE_CH))
    upper = jnp.where(lax.broadcasted_iota(jnp.int32, (LANE, LANE), 0) < lax.broadcasted_iota(jnp.int32, (LANE, LANE), 1),
                      1.0, 0.0).astype(BF16)
    cbase = jnp.dot(jnp.broadcast_to(nch, (8, LANE)).astype(BF16), upper,
                    preferred_element_type=F32)[0:1]
    slot = MOE_CH * cbase + rank
    slot0 = jnp.sum(jnp.where(sel1, slot, 0.0), axis=-1, keepdims=True)
    slot1 = jnp.sum(jnp.where(sel2, slot, 0.0), axis=-1, keepdims=True)
    col = jnp.where(lane == 0, slot0, jnp.where(lane == 1, slot1,
                    jnp.where(lane == 2, 1.0 / den, jnp.where(lane == 3, e / den, 0.0))))
    col_ref[...] = col
    row_ref[...] = jnp.transpose(col)[0:PLAN_ROWS]
    nch_ref[0] = nch


def _moe_plan(logits):
    return pl.pallas_call(
        _plan_kernel,
        grid=(NT,),
        in_specs=[pl.BlockSpec((TM, LANE), lambda i: (i, 0))],
        out_specs=[pl.BlockSpec((TM, LANE), lambda i: (i, 0)),
                   pl.BlockSpec((PLAN_ROWS, TM), lambda i: (0, i)),
                   pl.BlockSpec((1, 1, LANE), lambda i: (i, 0, 0))],
        out_shape=[jax.ShapeDtypeStruct((ROWS, LANE), F32),
                   jax.ShapeDtypeStruct((PLAN_ROWS, ROWS), F32),
                   jax.ShapeDtypeStruct((NT, 1, LANE), F32)],
        compiler_params=_params("parallel"),
        name="moe_plan",
    )(logits)


def _moe_tables(nch):
    n = nch[:, 0, :N_EXPERTS].astype(jnp.int32)
    cbase = jnp.cumsum(n, axis=1) - n
    used = jnp.sum(n, axis=1)
    c = jnp.arange(MOE_CMAX)
    valid = (c[None, None, :] < n.T[:, :, None]).reshape(N_EXPERTS, -1)
    cid = (jnp.arange(NT)[None, :, None] * MOE_NCH + cbase.T[:, :, None] + c[None, None, :]).reshape(N_EXPERTS, -1)
    ne = jnp.sum(n, axis=0)
    ge = (ne + MOE_GC - 1) // MOE_GC
    gend = jnp.cumsum(ge)
    nused = gend[-1]
    size = MOE_GROUPS * MOE_GC
    pos = (gend - ge)[:, None] * MOE_GC + jnp.cumsum(valid, axis=1) - 1
    dest = jnp.where(valid, pos, size).reshape(-1)
    tab = jnp.full((size,), -1, jnp.int32).at[dest].set(cid.reshape(-1).astype(jnp.int32), mode='drop')
    at = jnp.arange(size)
    tab = tab[jnp.maximum(lax.cummax(jnp.where(tab >= 0, at, -1)), 0)]
    gexp = jnp.minimum(jnp.sum(jnp.arange(MOE_GROUPS)[:, None] >= gend[None, :], axis=1), N_EXPERTS - 1)
    pool = jnp.zeros((NT * MOE_NCH,), jnp.int32).at[jnp.where(at < nused * MOE_GC, tab, NT * MOE_NCH)].set(
        at.astype(jnp.int32), mode='drop')
    return tab, gexp.astype(jnp.int32), nused.reshape(1).astype(jnp.int32), used.astype(jnp.int32), pool


def _dispatch_kernel(used_ref, h_ref, row_ref, xs_ref):
    used_rows = used_ref[pl.program_id(0)] * MOE_CH
    s0 = row_ref[0:1, :]
    s1 = row_ref[1:2, :]
    for blk in range(MOE_SLOTS // MOE_SB):
        rows = slice(blk * MOE_SB, (blk + 1) * MOE_SB)

        @pl.when(blk * MOE_SB < used_rows)
        def _():
            r = (lax.broadcasted_iota(jnp.int32, (MOE_SB, TM), 0) + blk * MOE_SB).astype(F32)
            s = jnp.where(r == s0, 1.0, jnp.where(r == s1, 1.0, 0.0)).astype(BF16)
            xs_ref[rows, :] = jnp.dot(s, h_ref[...], preferred_element_type=F32).astype(BF16)

        @pl.when(blk * MOE_SB >= used_rows)
        def _():
            xs_ref[rows, :] = jnp.zeros((MOE_SB, D), BF16)


def _moe_dispatch(h, plan_row, used):
    return pl.pallas_call(
        _dispatch_kernel,
        grid_spec=pltpu.PrefetchScalarGridSpec(
            num_scalar_prefetch=1, grid=(NT,),
            in_specs=[pl.BlockSpec((TM, D), lambda i, u: (i, 0)),
                      pl.BlockSpec((PLAN_ROWS, TM), lambda i, u: (0, i))],
            out_specs=pl.BlockSpec((MOE_SLOTS, D), lambda i, u: (i, 0))),
        out_shape=jax.ShapeDtypeStruct((NT * MOE_SLOTS, D), BF16),
        compiler_params=_params("parallel"),
        name="moe_dispatch",
    )(used, h, plan_row)


def _experts_kernel(tab_ref, gexp_ref, nused_ref, *refs):
    x_refs = refs[:MOE_GC]
    w1_ref, w3_ref, w2_ref, y_ref, x_scr, acc = refs[MOE_GC:]
    f = pl.program_id(1)

    @pl.when(pl.program_id(0) < nused_ref[0])
    def _():
        @pl.when(f == 0)
        def _():
            for j, r in enumerate(x_refs):
                x_scr[j * MOE_CH:(j + 1) * MOE_CH, :] = r[...]
            acc[...] = jnp.zeros_like(acc)

        _swiglu_step(x_scr[...], w1_ref, w3_ref, w2_ref, acc)

        @pl.when(f == MOE_NF - 1)
        def _():
            y_ref[...] = acc[...].astype(BF16)


def _moe_experts(xs, tab, gexp, nused, w1, w3, w2, e0):
    grp = lambda g, nu: jnp.minimum(g, nu[0] - 1)
    ftile = lambda g, f, nu: jnp.where(g < nu[0], f, MOE_NF - 1)
    chunk = lambda j: pl.BlockSpec((MOE_CH, D), lambda g, f, tab, ge, nu: (tab[grp(g, nu) * MOE_GC + j], 0))
    rows = MOE_GC * MOE_CH
    return pl.pallas_call(
        _experts_kernel,
        grid_spec=pltpu.PrefetchScalarGridSpec(
            num_scalar_prefetch=3, grid=(MOE_GROUPS, MOE_NF),
            in_specs=[chunk(j) for j in range(MOE_GC)] + [
                pl.BlockSpec((1, D, MOE_TF), lambda g, f, tab, ge, nu: (e0 + ge[grp(g, nu)], 0, ftile(g, f, nu))),
                pl.BlockSpec((1, D, MOE_TF), lambda g, f, tab, ge, nu: (e0 + ge[grp(g, nu)], 0, ftile(g, f, nu))),
                pl.BlockSpec((1, MOE_TF, D), lambda g, f, tab, ge, nu: (e0 + ge[grp(g, nu)], ftile(g, f, nu), 0))],
            out_specs=pl.BlockSpec((rows, D), lambda g, f, tab, ge, nu: (grp(g, nu), 0)),
            scratch_shapes=[pltpu.VMEM((rows, D), BF16), pltpu.VMEM((rows, D), F32)]),
        out_shape=jax.ShapeDtypeStruct((MOE_GROUPS * rows, D), BF16),
        compiler_params=_params("arbitrary", "arbitrary"),
        name="moe_experts",
    )(tab, gexp, nused, *([xs] * MOE_GC), w1, w3, w2)


def _combine_kernel(used_ref, pool_ref, *refs, final):
    refs = list(refs)
    y_refs = [refs.pop(0) for _ in range(MOE_NCH)]
    col_ref, x_ref, mod_ref = refs[:3]
    fg_ref = refs[3] if final else None
    o_ref, y_scr, acc = refs[-3:]
    used = used_ref[pl.program_id(0)]
    col = col_ref[...]
    s0, s1, g0, g1 = col[:, 0:1], col[:, 1:2], col[:, 2:3], col[:, 3:4]
    acc[...] = jnp.zeros_like(acc)
    per_blk = MOE_SB // MOE_CH
    for blk in range(MOE_SLOTS // MOE_SB):
        @pl.when(blk * per_blk < used)
        def _():
            for q in range(blk * per_blk, (blk + 1) * per_blk):
                y_scr[q * MOE_CH:(q + 1) * MOE_CH, :] = y_refs[q][...]
            c = (lax.broadcasted_iota(jnp.int32, (TM, MOE_SB), 1) + blk * MOE_SB).astype(F32)
            w = jnp.where(c == s0, g0, jnp.where(c == s1, g1, 0.0)).astype(BF16)
            acc[...] += jnp.dot(w, y_scr[blk * MOE_SB:(blk + 1) * MOE_SB, :], preferred_element_type=F32)
    o_ref[...] = _residual_out(x_ref[...], mod_ref[0][:, 5 * D:6 * D], acc[...], fg_ref)


def _moe_combine(y, pool, used, plan_col, x, mod_l, final_g=None):
    final = final_g is not None
    rows = LAT_ROWS if final else ROWS
    tile = lambda w: pl.BlockSpec((TM, w), lambda i, u, pc: (i, 0))
    chunk = lambda q: pl.BlockSpec((MOE_CH, D), lambda i, u, pc: (pc[i * MOE_NCH + q], 0))
    in_specs = [chunk(q) for q in range(MOE_NCH)] + [
        tile(LANE), tile(D), pl.BlockSpec((1, 1, N_MOD * D), lambda i, u, pc: (_mod_row(i), 0, 0))]
    args = [y] * MOE_NCH + [plan_col, x, mod_l]
    if final:
        in_specs.append(pl.BlockSpec((1, D), lambda i, u, pc: (0, 0)))
        args.append(final_g.reshape(1, D))
    return pl.pallas_call(
        functools.partial(_combine_kernel, final=final),
        grid_spec=pltpu.PrefetchScalarGridSpec(
            num_scalar_prefetch=2, grid=(rows // TM,),
            in_specs=in_specs, out_specs=tile(D),
            scratch_shapes=[pltpu.VMEM((MOE_SLOTS, D), BF16), pltpu.VMEM((TM, D), F32)]),
        out_shape=jax.ShapeDtypeStruct((rows, D), F32),
        compiler_params=_params("parallel"),
        name="moe_combine",
    )(used, pool, *args)


def _moe_mixer(h, logits, x, mod_l, w1, w3, w2, e0, final_g=None):
    plan_col, plan_row, nch = _moe_plan(logits)
    tab, gexp, nused, used, pool = _moe_tables(nch)
    xs = _moe_dispatch(h, plan_row, used)
    y = _moe_experts(xs, tab, gexp, nused, w1, w3, w2, e0)
    return _moe_combine(y, pool, used, plan_col, x, mod_l, final_g)


def kernel(x, c, ctx, c_ctx, w_mod, b_mod, norm1_g, norm2_g, w_in, rpb, hy_conv_w, hy_conv_b, hy_w1, hy_b1, hy_w2, hy_b2, hy_w3, hy_freq, hy_bias, w_pa, w_pb, w_o, ffn_w1, ffn_w3, ffn_w2, moe_router, moe_w1, moe_w3, moe_w2, final_g):
    xs = jnp.concatenate([x.reshape(LAT_ROWS, D), ctx.reshape(CTX_ROWS, D)], axis=0)
    mod = _modulation(c, c_ctx, w_mod, b_mod)
    tabs = _fft_tables()
    ctabs = _ctx_tables()
    filt_lat = _implicit_filters(SEQ, hy_w1, hy_b1, hy_w2, hy_b2, hy_w3, hy_freq)
    filt_ctx = _implicit_filters(CTX, hy_w1, hy_b1, hy_w2, hy_b2, hy_w3, hy_freq)
    spec_re, spec_im = _filter_spectra(filt_lat, tabs)
    nblk = HY_W // LANE

    moe_w = [w.reshape((-1,) + w.shape[2:]).astype(BF16) for w in (moe_w1, moe_w3, moe_w2)]
    ffn_w = [w.astype(BF16) for w in (ffn_w1, ffn_w3, ffn_w2)]

    for l in range(DEPTH):
        last = l == DEPTH - 1
        mod_l = mod[l].reshape(MOD_ROWS, 1, N_MOD * D)
        cw, cb = hy_conv_w[l], hy_conv_b[l].reshape(1, 3 * HY_W)
        p = _in_proj(xs, mod_l, norm1_g[l].reshape(1, D), w_in, l)

        o_att = jnp.concatenate([_neighbourhood_attention(p, rpb[l]), _context_attention(p)], axis=0)
        z = _hyena_conv(p, COL_HV, p, COL_HX1, cw, cb, 0, nblk, spec_re, spec_im, l, 0,
                        hy_bias[l], tabs, True)
        o_hy = _hyena_conv(z, 0, p, COL_HX2, cw, cb, 0, 2 * nblk, spec_re, spec_im, l, 1,
                           hy_bias[l], tabs, False)
        o_hy = jnp.concatenate([o_hy, _ctx_hyena(p, cw, cb, filt_ctx, l, hy_bias[l], ctabs)], axis=0)

        i = l // 2
        router = moe_router[i] if l % 2 else None
        res = _merge(o_att, o_hy, p, xs, mod_l, norm2_g[l].reshape(1, D), w_pa, w_pb, w_o, l, router)
        fg = final_g if last else None
        if l % 2 == 0:
            xs = _dense_mixer(res[1], res[0], mod_l, *ffn_w, i, final_g=fg)
        else:
            xs = _moe_mixer(res[1], res[2], res[0], mod_l, *moe_w, i * N_EXPERTS, final_g=fg)
    return xs.reshape(NB, SEQ, D)
```

```python
import functools
import math

import numpy as np
import jax
import jax.numpy as jnp
from jax import lax
from jax.experimental import pallas as pl
from jax.experimental.pallas import tpu as pltpu

F32 = jnp.float32
BF16 = jnp.bfloat16
HIGHEST = lax.Precision.HIGHEST

D = 1024
NB = 4
SEQ = 4096
DEPTH = 4
CTX = 256
GRID_W = 64
GRID_H = SEQ // GRID_W
N_HEADS = 8
HEAD_DIM = 64
ATT_W = N_HEADS * HEAD_DIM
WIN_ROWS = 8
WIN_COLS = 16
HY_W = 512
HY_ORDER = 2
HY_SHORT = 3
HY_EMB = 33
HY_BANDS = (HY_EMB - 1) // 2
HY_FILT = 64
HY_DECAY_TARGET = 1e-2
HY_MAX_DECAY = math.log(HY_DECAY_TARGET) / 0.3
HY_MIN_DECAY = math.log(HY_DECAY_TARGET) / 1.5
HY_DECAY_SHIFT = 0.05
PROJ_W = 3 * ATT_W + 3 * HY_W + 2 * D
D_FF = 2816
N_EXPERTS = 8
N_MOD = 6
EPS = 1e-6

LAT_ROWS = NB * SEQ
CTX_ROWS = NB * CTX
ROWS = LAT_ROWS + CTX_ROWS
MOD_ROWS = 8
CTX_MOD_ROW = NB

LANE = 128
TM = 1024
VMEM_LIMIT = 56 * 1024 * 1024

COL_Q, COL_K, COL_V = 0, ATT_W // LANE, 2 * ATT_W // LANE
COL_HV = 3 * ATT_W // LANE
COL_HX1 = COL_HV + HY_W // LANE
COL_HX2 = COL_HX1 + HY_W // LANE
COL_GA = (3 * ATT_W + 3 * HY_W) // D
COL_GB = COL_GA + 1


def _mod_row(i):
    return jnp.where(i < LAT_ROWS // TM, (i * TM) // SEQ, CTX_MOD_ROW)


def _params(*sem):
    return pltpu.CompilerParams(dimension_semantics=sem, vmem_limit_bytes=VMEM_LIMIT)


def _mod_kernel(s_ref, w_ref, b_ref, o_ref):
    s = s_ref[...]
    s = s * jax.nn.sigmoid(s)
    o_ref[0] = jnp.dot(s, w_ref[0], precision=HIGHEST, preferred_element_type=F32) + b_ref[0]


def _modulation(c, c_ctx, w_mod, b_mod):
    s = jnp.zeros((MOD_ROWS, D), F32).at[:NB].set(c).at[CTX_MOD_ROW].set(c_ctx)
    tn = 1536
    return pl.pallas_call(
        _mod_kernel,
        grid=(DEPTH, N_MOD * D // tn),
        in_specs=[pl.BlockSpec((MOD_ROWS, D), lambda l, j: (0, 0)),
                  pl.BlockSpec((1, D, tn), lambda l, j: (l, 0, j)),
                  pl.BlockSpec((1, 1, tn), lambda l, j: (l, 0, j))],
        out_specs=pl.BlockSpec((1, MOD_ROWS, tn), lambda l, j: (l, 0, j)),
        out_shape=jax.ShapeDtypeStruct((DEPTH, MOD_ROWS, N_MOD * D), F32),
        compiler_params=_params("parallel", "parallel"),
        name="modulation",
    )(s, w_mod, b_mod.reshape(DEPTH, 1, N_MOD * D))


def _norm_mod(x, g, shift, scale):
    y = x * lax.rsqrt(jnp.mean(x * x, axis=-1, keepdims=True) + EPS) * g
    return y * (1.0 + scale) + shift


def _inproj_kernel(x_ref, mod_ref, g_ref, w_ref, o_ref, h_scr):
    @pl.when(pl.program_id(1) == 0)
    def _():
        m = mod_ref[0]
        h = _norm_mod(x_ref[...], g_ref[...], m[:, 0:D], m[:, D:2 * D])
        h_scr[...] = h.astype(BF16)

    o_ref[...] = jnp.dot(h_scr[...], w_ref[0].astype(BF16),
                         preferred_element_type=F32).astype(BF16)


def _in_proj(x, mod_l, g, w, layer):
    tn = 1024
    return pl.pallas_call(
        _inproj_kernel,
        grid=(ROWS // TM, PROJ_W // tn),
        in_specs=[pl.BlockSpec((TM, D), lambda i, j: (i, 0)),
                  pl.BlockSpec((1, 1, N_MOD * D), lambda i, j: (_mod_row(i), 0, 0)),
                  pl.BlockSpec((1, D), lambda i, j: (0, 0)),
                  pl.BlockSpec((1, D, tn), lambda i, j: (layer, 0, j))],
        out_specs=pl.BlockSpec((TM, tn), lambda i, j: (i, j)),
        out_shape=jax.ShapeDtypeStruct((ROWS, PROJ_W), BF16),
        scratch_shapes=[pltpu.VMEM((TM, D), BF16)],
        compiler_params=_params("parallel", "arbitrary"),
        name="in_proj",
    )(x, mod_l, g, w)


NEG = -1e30


def _nt_dot(a, b):
    return lax.dot_general(a, b, (((1,), (1,)), ((), ())), preferred_element_type=F32)


def _attend_pair(qp, keys, values, biases):
    lane = lax.broadcasted_iota(jnp.int32, qp.shape, 1)
    qs = qp * jnp.asarray(HEAD_DIM ** -0.5, BF16)
    outs = []
    for a in range(2):
        sel = (lane < HEAD_DIM) if a == 0 else (lane >= HEAD_DIM)
        qa = jnp.where(sel, qs, jnp.zeros_like(qs))
        ss = []
        for k, b in zip(keys, biases):
            s = _nt_dot(qa, k)
            if b is not None:
                s = s + b[a]
            ss.append(s)
        m = functools.reduce(jnp.maximum, [jnp.max(s, axis=-1, keepdims=True) for s in ss])
        ps = [jnp.exp(s - m) for s in ss]
        den = functools.reduce(jnp.add, [jnp.sum(p, axis=-1, keepdims=True) for p in ps])
        o = functools.reduce(jnp.add, [jnp.dot(p.astype(BF16), v, preferred_element_type=F32)
                                       for p, v in zip(ps, values)])
        outs.append(o / den)
    return jnp.where(lane < HEAD_DIM, outs[0], outs[1])


NLOC = WIN_ROWS * GRID_W
NKEY = NLOC + CTX
NA_PAIRS = GRID_H // 2
NA_STEPS = NA_PAIRS + 2
NA_UNITS = 2 * N_HEADS


def _window_start(r):
    r0 = jnp.clip(r - WIN_ROWS // 2, 0, GRID_H - WIN_ROWS)
    return pl.multiple_of(r0 * GRID_W, GRID_W)


def _natt_kernel(q_ref, k_ref, v_ref, kc_ref, vc_ref, bt0_ref, bt1_ref, o_ref, s0, s1, p0, p1, l0, l1):
    j = pl.program_id(1)
    lane = lax.broadcasted_iota(jnp.int32, (GRID_W, LANE), 1)
    lo = lane < HEAD_DIM
    hi = lane >= HEAD_DIM

    @pl.when(j == 0)
    def _():
        s1[...] = jnp.zeros_like(s1)
        p0[...] = jnp.zeros_like(p0)
        p1[...] = jnp.zeros_like(p1)
        l0[...] = jnp.ones_like(l0)
        l1[...] = jnp.ones_like(l1)

    def step(s_cur, s_prev, p_cur, p_prev, l_cur, l_prev):
        ro = jnp.maximum(2 * j - 4, 0)
        rs = jnp.minimum(2 * j, GRID_H - 2)
        for half, bt_ref in enumerate((bt0_ref, bt1_ref)):
            start_o = _window_start(ro + half)
            start_s = _window_start(rs + half)
            rows = slice(half * GRID_W, (half + 1) * GRID_W)
            for hp in range(N_HEADS // 2):
                cs = slice(hp * LANE, (hp + 1) * LANE)
                units = (half * N_HEADS + 2 * hp, half * N_HEADS + 2 * hp + 1)

                vw = v_ref[pl.ds(start_o, NLOC), cs]
                vc = vc_ref[:, cs]
                o = []
                for u in units:
                    acc = (jnp.dot(p_cur[u, :, 0:NLOC], vw, preferred_element_type=F32)
                           + jnp.dot(p_cur[u, :, NLOC:NKEY], vc, preferred_element_type=F32))
                    o.append(acc / l_cur[u])
                o_ref[rows, cs] = jnp.where(lo, o[0], o[1]).astype(BF16)

                for u in units:
                    x = s_prev[u]
                    e = jnp.exp(x - jnp.max(x, axis=-1, keepdims=True))
                    l_prev[u] = jnp.broadcast_to(jnp.sum(e, axis=-1, keepdims=True), (GRID_W, LANE))
                    p_prev[u] = e.astype(BF16)

                qs = q_ref[rows, cs] * jnp.asarray(HEAD_DIM ** -0.5, BF16)
                kw = k_ref[pl.ds(start_s, NLOC), cs]
                kc = kc_ref[:, cs]
                for a, u in enumerate(units):
                    qa = jnp.where(lo if a == 0 else hi, qs, jnp.zeros_like(qs))
                    s_cur[u, :, 0:NLOC] = _nt_dot(qa, kw) + bt_ref[0, 2 * hp + a].astype(F32)
                    s_cur[u, :, NLOC:NKEY] = _nt_dot(qa, kc)

    @pl.when(j % 2 == 0)
    def _():
        step(s0, s1, p0, p1, l0, l1)

    @pl.when(j % 2 == 1)
    def _():
        step(s1, s0, p1, p0, l1, l0)


def _bias_tables(rpb):
    col = np.arange(GRID_W)
    c0 = np.clip(col - WIN_COLS // 2, 0, GRID_W - WIN_COLS)
    valid = (col[None, :] >= c0[:, None]) & (col[None, :] < c0[:, None] + WIN_COLS)
    dc = np.clip(col[None, :] - col[:, None] + (WIN_COLS - 1), 0, 2 * WIN_COLS - 2)
    pick = (np.arange(2 * WIN_COLS - 1)[:, None, None] == dc[None]).astype(np.float32)
    t = jnp.einsum('lhrd,dqk->lhrqk', rpb, jnp.asarray(pick), precision=HIGHEST)
    t = jnp.where(valid, t, NEG)
    win = jnp.stack([t[:, :, d0:d0 + WIN_ROWS] for d0 in range(WIN_ROWS)], axis=1)
    win = win.transpose(0, 1, 2, 4, 3, 5)
    return win.reshape(DEPTH * WIN_ROWS, N_HEADS, GRID_W, NLOC).astype(BF16)


def _bias_index(r):
    r0 = jnp.clip(r - WIN_ROWS // 2, 0, GRID_H - WIN_ROWS)
    return r0 - r + (WIN_ROWS - 1)


def _neighbourhood_attention(p, bt, layer):
    blk = GRID_H // 2
    ctx_blk0 = LAT_ROWS // CTX
    score_row = lambda j, half: jnp.minimum(2 * j, GRID_H - 2) + half
    bias_spec = lambda half: pl.BlockSpec(
        (1, N_HEADS, GRID_W, NLOC), lambda b, j: (layer * WIN_ROWS + _bias_index(score_row(j, half)), 0, 0, 0))
    return pl.pallas_call(
        _natt_kernel,
        grid=(NB, NA_STEPS),
        in_specs=[pl.BlockSpec((2 * GRID_W, ATT_W), lambda b, j: (b * blk + jnp.minimum(j, blk - 1), 0)),
                  pl.BlockSpec((SEQ, ATT_W), lambda b, j: (b, 1)),
                  pl.BlockSpec((SEQ, ATT_W), lambda b, j: (b, 2)),
                  pl.BlockSpec((CTX, ATT_W), lambda b, j: (ctx_blk0 + b, 1)),
                  pl.BlockSpec((CTX, ATT_W), lambda b, j: (ctx_blk0 + b, 2)),
                  bias_spec(0), bias_spec(1)],
        out_specs=pl.BlockSpec((2 * GRID_W, ATT_W), lambda b, j: (b * blk + jnp.maximum(j - 2, 0), 0)),
        out_shape=jax.ShapeDtypeStruct((ROWS, ATT_W), BF16),
        scratch_shapes=[pltpu.VMEM((NA_UNITS, GRID_W, NKEY), F32)] * 2
                       + [pltpu.VMEM((NA_UNITS, GRID_W, NKEY), BF16)] * 2
                       + [pltpu.VMEM((NA_UNITS, GRID_W, LANE), F32)] * 2,
        compiler_params=_params("parallel", "arbitrary"),
        name="neighbourhood_attention",
    )(p, p, p, p, p, bt, bt)


def _catt_kernel(q_ref, k_ref, v_ref, dst_ref, o_ref):
    del dst_ref
    for hp in range(N_HEADS // 2):
        cs = slice(hp * LANE, (hp + 1) * LANE)
        o = _attend_pair(q_ref[:, cs], [k_ref[:, cs]], [v_ref[:, cs]], [None])
        o_ref[:, cs] = o.astype(BF16)


def _context_attention(p, o_att):
    ctx_blk0 = LAT_ROWS // CTX
    return pl.pallas_call(
        _catt_kernel,
        grid=(NB,),
        in_specs=[pl.BlockSpec((CTX, ATT_W), lambda b: (ctx_blk0 + b, 0)),
                  pl.BlockSpec((CTX, ATT_W), lambda b: (ctx_blk0 + b, 1)),
                  pl.BlockSpec((CTX, ATT_W), lambda b: (ctx_blk0 + b, 2)),
                  pl.BlockSpec(memory_space=pl.ANY)],
        out_specs=pl.BlockSpec((CTX, ATT_W), lambda b: (ctx_blk0 + b, 0)),
        out_shape=jax.ShapeDtypeStruct((ROWS, ATT_W), BF16),
        input_output_aliases={3: 0},
        compiler_params=_params("parallel"),
        name="context_attention",
    )(p, p, p, o_att)


def _filter_features(L):
    pos = np.arange(L, dtype=np.float64)
    t = pos / max(L - 1, 1)
    bands = np.linspace(1e-4, HY_BANDS - 1, HY_BANDS)
    ang = (2.0 * math.pi / L) * pos[:, None] * bands[None, :]
    z = np.concatenate([t[:, None], np.cos(ang), -np.sin(ang)], axis=-1)
    zp = np.zeros((L, LANE), np.float32)
    zp[:, :HY_EMB] = z
    return zp


def _filter_kernel(z_ref, w1_ref, b1_ref, w2_ref, b2_ref, w3_ref, fr_ref, rate_ref, o_ref, *, L, tl):
    dot = functools.partial(jnp.dot, precision=HIGHEST, preferred_element_type=F32)
    fr = fr_ref[0]
    a = jnp.sin(fr[0:1] * (dot(z_ref[...], w1_ref[0]) + b1_ref[0]))
    a = jnp.sin(fr[1:2] * (dot(a, w2_ref[0]) + b2_ref[0]))
    h = dot(a, w3_ref[0])
    pos = (pl.program_id(1) * tl + lax.broadcasted_iota(jnp.int32, (tl, 1), 0)).astype(F32)
    dist = jnp.abs(pos - float(L // 2)) * (2.0 / L)
    o_ref[0] = h * (jnp.exp(-dist * rate_ref[...]) + HY_DECAY_SHIFT)


def _implicit_filters(L, w1, b1, w2, b2, w3, freq):
    tl = min(L, 512)
    z = jnp.asarray(_filter_features(L))
    w1p = jnp.zeros((DEPTH, LANE, HY_FILT), F32).at[:, :HY_EMB].set(w1)
    rates = np.abs(np.linspace(HY_MIN_DECAY, HY_MAX_DECAY, HY_ORDER * HY_W)).astype(np.float32)
    cw = HY_ORDER * HY_W
    return pl.pallas_call(
        functools.partial(_filter_kernel, L=L, tl=tl),
        grid=(DEPTH, L // tl),
        in_specs=[pl.BlockSpec((tl, LANE), lambda l, i: (i, 0)),
                  pl.BlockSpec((1, LANE, HY_FILT), lambda l, i: (l, 0, 0)),
                  pl.BlockSpec((1, 1, HY_FILT), lambda l, i: (l, 0, 0)),
                  pl.BlockSpec((1, HY_FILT, HY_FILT), lambda l, i: (l, 0, 0)),
                  pl.BlockSpec((1, 1, HY_FILT), lambda l, i: (l, 0, 0)),
                  pl.BlockSpec((1, HY_FILT, cw), lambda l, i: (l, 0, 0)),
                  pl.BlockSpec((1, 2, HY_FILT), lambda l, i: (l, 0, 0)),
                  pl.BlockSpec((1, cw), lambda l, i: (0, 0))],
        out_specs=pl.BlockSpec((1, tl, cw), lambda l, i: (l, i, 0)),
        out_shape=jax.ShapeDtypeStruct((DEPTH, L, cw), F32),
        compiler_params=_params("parallel", "parallel"),
        name=f"implicit_filters_{L}",
    )(z, w1p, b1.reshape(DEPTH, 1, HY_FILT), w2, b2.reshape(DEPTH, 1, HY_FILT), w3, freq,
      jnp.asarray(rates).reshape(1, cw))


FFT_N = 2 * SEQ
FA = 64
FB = 128
FA_IN = SEQ // FB
PITCH = 136
HALO = 8
UNROLL_ROWS = 16
UNROLL_SLABS = 8


def _fft_tables():
    a = np.arange(FA_IN)
    r = np.arange(FA)
    b = np.arange(FB)
    q = np.arange(FB)
    ph = a[None, None, :] * r[None, :, None] / FA + b[:, None, None] * r[None, :, None] / FFT_N
    g1 = np.exp(-2j * np.pi * ph)
    g1 = np.concatenate([g1.real, g1.imag], axis=1)
    f = np.exp(-2j * np.pi * q[:, None] * b[None, :] / FB)
    f2 = np.concatenate([f.real, f.imag], axis=0)
    f2c = np.concatenate([f.real, -f.imag], axis=0)
    ao = np.arange(FA_IN) + (SEQ // 2) // FB
    ph2 = ao[None, :, None] * r[None, None, :] / FA + b[:, None, None] * r[None, None, :] / FFT_N
    g2 = np.exp(2j * np.pi * ph2) / FFT_N
    g2 = np.concatenate([g2.real, g2.imag], axis=1)
    return tuple(jnp.asarray(t, dtype=BF16) for t in (g1, f2, f2c, g2))


def _cplx(out, m, n):
    re = out[0:m, 0:n] - out[m:2 * m, n:2 * n]
    im = out[0:m, n:2 * n] + out[m:2 * m, 0:n]
    return re, im


def _short_conv_slabs(src_ref, rows, t_scr, cw_ref, cb_ref, apply_conv, emit):
    nslab = rows // FB
    if not apply_conv:
        for a in range(nslab):
            emit(a, src_ref[a * FB:(a + 1) * FB, :].astype(F32))
        return
    zero = jnp.zeros((HALO, LANE), F32)
    t_scr[0:HALO, :] = zero
    t_scr[HALO + rows:2 * HALO + rows, :] = zero
    for a in range(nslab):
        t_scr[HALO + a * FB:HALO + (a + 1) * FB, :] = src_ref[a * FB:(a + 1) * FB, :].astype(F32)
    w0, w1, w2, bias = cw_ref[0:1, :], cw_ref[1:2, :], cw_ref[2:3, :], cb_ref[...]
    for a in range(nslab):
        o = HALO + a * FB
        y = (w0 * t_scr[o - 1:o - 1 + FB, :] + w1 * t_scr[o:o + FB, :]
             + w2 * t_scr[o + 1:o + 1 + FB, :] + bias)
        emit(a, y)


def _fwd_stage1(b, s_refs, g1_ref, y_r, y_i):
    cols = [s[pl.ds(b, FA_IN, stride=PITCH), :].astype(BF16) for s in s_refs]
    rhs = cols[0] if len(cols) == 1 else jnp.concatenate(cols, axis=1)
    out = jnp.dot(g1_ref[b], rhs, preferred_element_type=F32)
    if len(cols) == 1:
        re, im = out[0:FA], out[FA:2 * FA]
    else:
        re, im = _cplx(out, FA, LANE)
    y_r[pl.ds(b, FA, stride=PITCH), :] = re
    y_i[pl.ds(b, FA, stride=PITCH), :] = im


def _fwd_stage2(r, y_r, y_i, f_ref):
    o = pl.multiple_of(r * PITCH, 8)
    rhs = jnp.concatenate([y_r[pl.ds(o, FB), :].astype(BF16), y_i[pl.ds(o, FB), :].astype(BF16)], axis=1)
    out = jnp.dot(f_ref[...], rhs, preferred_element_type=F32)
    return _cplx(out, FB, LANE)


def _spectrum_kernel(h_ref, g1_ref, f_ref, hr_ref, hi_ref, s_scr, y_r, y_i):
    for a in range(FA_IN):
        s_scr[a * PITCH:a * PITCH + FB, :] = h_ref[0, a * FB:(a + 1) * FB, :]

    def s1(b, c):
        _fwd_stage1(b, [s_scr], g1_ref, y_r, y_i)
        return c
    lax.fori_loop(0, FB, s1, 0, unroll=UNROLL_ROWS)

    def s2(r, c):
        xr, xi = _fwd_stage2(r, y_r, y_i, f_ref)
        o = pl.multiple_of(r * FB, FB)
        hr_ref[0, pl.ds(o, FB), :] = xr
        hi_ref[0, pl.ds(o, FB), :] = xi
        return c
    lax.fori_loop(0, FA, s2, 0, unroll=UNROLL_SLABS)


def _filter_spectra(h, tabs):
    g1, f2, _, _ = tabs
    cw = HY_ORDER * HY_W
    spec = pl.BlockSpec((1, FFT_N, LANE), lambda l, c: (l, 0, c))
    return pl.pallas_call(
        _spectrum_kernel,
        grid=(DEPTH, cw // LANE),
        in_specs=[pl.BlockSpec((1, SEQ, LANE), lambda l, c: (l, 0, c)),
                  pl.BlockSpec(g1.shape, lambda l, c: (0, 0, 0)),
                  pl.BlockSpec(f2.shape, lambda l, c: (0, 0))],
        out_specs=[spec, spec],
        out_shape=[jax.ShapeDtypeStruct((DEPTH, FFT_N, cw), F32)] * 2,
        scratch_shapes=[pltpu.VMEM((FA_IN * PITCH, LANE), F32),
                        pltpu.VMEM((FA * PITCH, LANE), F32),
                        pltpu.VMEM((FA * PITCH, LANE), F32)],
        compiler_params=_params("parallel", "parallel"),
        name="filter_spectra",
    )(h, g1, f2)


def _hyconv_kernel(u_ref, x_ref, cwu_ref, cbu_ref, cwx_ref, cbx_ref,
                   hr_ref, hi_ref, bias_ref, g1_ref, f_ref, fc_ref, g2_ref,
                   o_ref, t_scr, s_r, s_i, y_r, y_i, *, conv_input):
    halves = [(s_r, slice(0, SEQ)), (s_i, slice(SEQ, 2 * SEQ))]

    def fill(s):
        def emit(a, slab):
            s[a * PITCH:a * PITCH + FB, :] = slab
        return emit
    for s, rows in halves:
        _short_conv_slabs(u_ref.at[rows, :], SEQ, t_scr, cwu_ref, cbu_ref, conv_input, fill(s))

    def s1(b, c):
        _fwd_stage1(b, [s_r, s_i], g1_ref, y_r, y_i)
        return c
    lax.fori_loop(0, FB, s1, 0, unroll=UNROLL_ROWS)

    def s2(r, c):
        xr, xi = _fwd_stage2(r, y_r, y_i, f_ref)
        oh = pl.multiple_of(r * FB, FB)
        hr = hr_ref[0, pl.ds(oh, FB), :]
        hi = hi_ref[0, pl.ds(oh, FB), :]
        vr = xr * hr - xi * hi
        vi = xr * hi + xi * hr
        rhs = jnp.concatenate([vr.astype(BF16), vi.astype(BF16)], axis=1)
        wr, wi = _cplx(jnp.dot(fc_ref[...], rhs, preferred_element_type=F32), FB, LANE)
        o = pl.multiple_of(r * PITCH, 8)
        y_r[pl.ds(o, FB), :] = wr
        y_i[pl.ds(o, FB), :] = wi
        return c
    lax.fori_loop(0, FA, s2, 0, unroll=UNROLL_SLABS)

    bias = bias_ref[0]

    def s3(b, c):
        rhs = jnp.concatenate([y_r[pl.ds(b, FA, stride=PITCH), :].astype(BF16),
                               y_i[pl.ds(b, FA, stride=PITCH), :].astype(BF16)], axis=1)
        re, im = _cplx(jnp.dot(g2_ref[b], rhs, preferred_element_type=F32), FA_IN, LANE)
        idx = pl.ds(b, FA_IN, stride=PITCH)
        s_r[idx, :] = re + s_r[idx, :] * bias
        s_i[idx, :] = im + s_i[idx, :] * bias
        return c
    lax.fori_loop(0, FB, s3, 0, unroll=UNROLL_ROWS)

    for s, rows in halves:
        def emit(a, slab, s=s, out=o_ref.at[rows, :]):
            out[a * FB:(a + 1) * FB, :] = (slab * s[a * PITCH:a * PITCH + FB, :]).astype(out.dtype)
        _short_conv_slabs(x_ref.at[rows, :], SEQ, t_scr, cwx_ref, cbx_ref, True, emit)


def _hyena_conv(u, u_col, x, x_col, conv_w, conv_b, cu_col, cx_col, spec_re, spec_im, layer, order,
                hy_bias_l, tabs, conv_input):
    g1, f2, f2c, g2 = tabs
    nblk = HY_W // LANE
    tok = lambda col: pl.BlockSpec((2 * SEQ, LANE), lambda c, p: (p, col + c))
    chan = lambda rows, col: pl.BlockSpec((rows, LANE), lambda c, p: (0, col + c))
    hspec = pl.BlockSpec((1, FFT_N, LANE), lambda c, p: (layer, 0, order * nblk + c))
    const = lambda shape: pl.BlockSpec(shape, lambda c, p: (0,) * len(shape))
    return pl.pallas_call(
        functools.partial(_hyconv_kernel, conv_input=conv_input),
        grid=(nblk, NB // 2),
        in_specs=[tok(u_col), tok(x_col),
                  chan(HY_SHORT, cu_col), chan(1, cu_col), chan(HY_SHORT, cx_col), chan(1, cx_col),
                  hspec, hspec,
                  pl.BlockSpec((1, 1, LANE), lambda c, p: (order, 0, c)),
                  const(g1.shape), const(f2.shape), const(f2c.shape), const(g2.shape)],
        out_specs=pl.BlockSpec((2 * SEQ, LANE), lambda c, p: (p, c)),
        out_shape=jax.ShapeDtypeStruct((ROWS, HY_W), BF16),
        scratch_shapes=[pltpu.VMEM((SEQ + 2 * HALO, LANE), F32),
                        pltpu.VMEM((FA_IN * PITCH, LANE), F32),
                        pltpu.VMEM((FA_IN * PITCH, LANE), F32),
                        pltpu.VMEM((FA * PITCH, LANE), F32),
                        pltpu.VMEM((FA * PITCH, LANE), F32)],
        compiler_params=_params("arbitrary", "arbitrary"),
        name=f"hyena_conv{order}",
    )(u, x, conv_w, conv_b, conv_w, conv_b, spec_re, spec_im,
      hy_bias_l.reshape(HY_ORDER, 1, HY_W), g1, f2, f2c, g2)


CN = 2 * CTX


def _ctx_tables():
    k = np.arange(CN)
    n = np.arange(CTX)
    f = np.exp(-2j * np.pi * k[:, None] * n[None, :] / CN)
    fwd = np.concatenate([f.real, f.imag], axis=0)
    t = np.arange(CTX) + CTX // 2
    g = np.exp(2j * np.pi * t[:, None] * k[None, :] / CN) / CN
    inv = np.concatenate([g.real, g.imag], axis=0)
    return jnp.asarray(fwd, dtype=BF16), jnp.asarray(inv, dtype=BF16)


def _ctx_hyena_kernel(v_ref, x1_ref, x2_ref, cwv_ref, cbv_ref, cw1_ref, cb1_ref, cw2_ref, cb2_ref,
                      h0_ref, h1_ref, bias_ref, fwd_ref, inv_ref, dst_ref, o_ref, t_scr, s_scr, g_scr):
    del dst_ref
    def conv_into(src_ref, b, cw_ref, cb_ref, dst, col):
        view = src_ref.at[b * CTX:(b + 1) * CTX, :]

        def emit(a, slab):
            dst[a * FB:(a + 1) * FB, col * LANE:(col + 1) * LANE] = slab
        _short_conv_slabs(view, CTX, t_scr, cw_ref, cb_ref, True, emit)

    def spectrum(h_ref):
        out = jnp.dot(fwd_ref[...], h_ref[0].astype(BF16), preferred_element_type=F32)
        return out[0:CN], out[CN:2 * CN]

    def long_conv(z2, hr, hi, bias):
        x = jnp.dot(fwd_ref[...], z2.astype(BF16), preferred_element_type=F32)
        xr, xi = _cplx(x, CN, LANE)
        vr = xr * hr - xi * hi
        vi = xr * hi + xi * hr
        rhs = jnp.concatenate([vr.astype(BF16), vi.astype(BF16)], axis=1)
        y = jnp.dot(inv_ref[...], rhs, preferred_element_type=F32)
        yr, yi = _cplx(y, CTX, LANE)
        return jnp.concatenate([yr, yi], axis=1) + z2 * jnp.concatenate([bias, bias], axis=1)

    h0r, h0i = spectrum(h0_ref)
    h1r, h1i = spectrum(h1_ref)
    for p in range(NB // 2):
        for half in range(2):
            conv_into(v_ref, 2 * p + half, cwv_ref, cbv_ref, s_scr, half)
            conv_into(x1_ref, 2 * p + half, cw1_ref, cb1_ref, g_scr, half)
        z = g_scr[...] * long_conv(s_scr[...], h0r, h0i, bias_ref[0])
        for half in range(2):
            conv_into(x2_ref, 2 * p + half, cw2_ref, cb2_ref, g_scr, half)
        y = g_scr[...] * long_conv(z, h1r, h1i, bias_ref[1])
        for half in range(2):
            b = 2 * p + half
            o_ref[b * CTX:(b + 1) * CTX, :] = y[:, half * LANE:(half + 1) * LANE].astype(BF16)


def _ctx_hyena(p, conv_w, conv_b, filt_ctx, layer, hy_bias_l, ctabs, o_hy):
    fwd, inv = ctabs
    nblk = HY_W // LANE
    ctx_blk = LAT_ROWS // CTX_ROWS
    tok = lambda col: pl.BlockSpec((CTX_ROWS, LANE), lambda c: (ctx_blk, col + c))
    chan = lambda rows, col: pl.BlockSpec((rows, LANE), lambda c: (0, col + c))
    hspec = lambda order: pl.BlockSpec((1, CTX, LANE), lambda c: (layer, 0, order * nblk + c))
    const = lambda shape: pl.BlockSpec(shape, lambda c: (0,) * len(shape))
    return pl.pallas_call(
        _ctx_hyena_kernel,
        grid=(nblk,),
        in_specs=[tok(COL_HV), tok(COL_HX1), tok(COL_HX2),
                  chan(HY_SHORT, 0), chan(1, 0), chan(HY_SHORT, nblk), chan(1, nblk),
                  chan(HY_SHORT, 2 * nblk), chan(1, 2 * nblk),
                  hspec(0), hspec(1),
                  pl.BlockSpec((HY_ORDER, 1, LANE), lambda c: (0, 0, c)),
                  const(fwd.shape), const(inv.shape), pl.BlockSpec(memory_space=pl.ANY)],
        out_specs=pl.BlockSpec((CTX_ROWS, LANE), lambda c: (ctx_blk, c)),
        out_shape=jax.ShapeDtypeStruct((ROWS, HY_W), BF16),
        input_output_aliases={14: 0},
        scratch_shapes=[pltpu.VMEM((CTX + 2 * HALO, LANE), F32),
                        pltpu.VMEM((CTX, 2 * LANE), F32),
                        pltpu.VMEM((CTX, 2 * LANE), F32)],
        compiler_params=_params("parallel"),
        name="context_hyena",
    )(p, p, p, conv_w, conv_b, conv_w, conv_b, conv_w, conv_b, filt_ctx, filt_ctx,
      hy_bias_l.reshape(HY_ORDER, 1, HY_W), fwd, inv, o_hy)


def _merge_kernel(oa_ref, oh_ref, ga_ref, gb_ref, x_ref, mod_ref, g2_ref, wpa_ref, wpb_ref, wo_ref,
                  *rest, routed):
    if routed:
        wr_ref, xo_ref, h_ref, rw_ref, wpa_s, wpb_s, wo_s = rest
    else:
        xo_ref, h_ref, wpa_s, wpb_s, wo_s = rest

    @pl.when(pl.program_id(0) == 0)
    def _():
        wpa_s[...] = wpa_ref[0].astype(BF16)
        wpb_s[...] = wpb_ref[0].astype(BF16)
        wo_s[...] = wo_ref[0].astype(BF16)

    a = jnp.dot(oa_ref[...], wpa_s[...], preferred_element_type=F32)
    b = jnp.dot(oh_ref[...], wpb_s[...], preferred_element_type=F32)
    mix = (jax.nn.sigmoid(ga_ref[...].astype(F32)) * a + jax.nn.sigmoid(gb_ref[...].astype(F32)) * b)
    y = jnp.dot(mix.astype(BF16), wo_s[...], preferred_element_type=F32)
    m = mod_ref[0]
    x = x_ref[...] + m[:, 2 * D:3 * D] * y
    xo_ref[...] = x
    h = _norm_mod(x, g2_ref[...], m[:, 3 * D:4 * D], m[:, 4 * D:5 * D])
    h_ref[...] = h.astype(BF16)
    if routed:
        wr = wr_ref[...]
        hh, wh = h.astype(BF16), wr.astype(BF16)
        hl, wl = (h - hh.astype(F32)).astype(BF16), (wr - wh.astype(F32)).astype(BF16)
        dot = functools.partial(jnp.dot, preferred_element_type=F32)
        rw_ref[...] = dot(hh, wh) + (dot(hh, wl) + dot(hl, wh))


def _merge(o_att, o_hy, p, x, mod_l, g2, w_pa, w_pb, w_o, layer, router):
    routed = router is not None
    tile = lambda w: pl.BlockSpec((TM, w), lambda i: (i, 0))
    full = lambda a: pl.BlockSpec(a.shape, lambda i: (0,) * a.ndim)
    layered = lambda a: pl.BlockSpec((1,) + a.shape[1:], lambda i: (layer, 0, 0))
    in_specs = [tile(ATT_W), tile(HY_W),
                pl.BlockSpec((TM, D), lambda i: (i, COL_GA)),
                pl.BlockSpec((TM, D), lambda i: (i, COL_GB)),
                tile(D),
                pl.BlockSpec((1, 1, N_MOD * D), lambda i: (_mod_row(i), 0, 0)),
                pl.BlockSpec((1, D), lambda i: (0, 0)),
                layered(w_pa), layered(w_pb), layered(w_o)]
    args = [o_att, o_hy, p, p, x, mod_l, g2, w_pa, w_pb, w_o]
    out_specs = [tile(D), tile(D)]
    out_shape = [jax.ShapeDtypeStruct((ROWS, D), F32), jax.ShapeDtypeStruct((ROWS, D), BF16)]
    if routed:
        wr = jnp.zeros((D, LANE), F32).at[:, :N_EXPERTS].set(router)
        in_specs.append(full(wr))
        args.append(wr)
        out_specs.append(tile(LANE))
        out_shape.append(jax.ShapeDtypeStruct((ROWS, LANE), F32))
    return pl.pallas_call(
        functools.partial(_merge_kernel, routed=routed),
        grid=(ROWS // TM,),
        in_specs=in_specs, out_specs=out_specs, out_shape=out_shape,
        scratch_shapes=[pltpu.VMEM(w_pa.shape[1:], BF16), pltpu.VMEM(w_pb.shape[1:], BF16),
                        pltpu.VMEM(w_o.shape[1:], BF16)],
        compiler_params=_params("arbitrary"),
        name="merge",
    )(*args)


TF = D_FF // 2


NF = D_FF // TF


SWIGLU_SUB = 512


def _swiglu_step(h, w1_ref, w3_ref, w2_ref, acc):
    tf = w1_ref.shape[2]
    for lo in range(0, tf, SWIGLU_SUB):
        hi = min(lo + SWIGLU_SUB, tf)
        a = jnp.dot(h, w1_ref[0, :, lo:hi].astype(BF16), preferred_element_type=F32)
        b = jnp.dot(h, w3_ref[0, :, lo:hi].astype(BF16), preferred_element_type=F32)
        t = a * jax.nn.sigmoid(a) * b
        acc[...] += jnp.dot(t.astype(BF16), w2_ref[0, lo:hi, :].astype(BF16), preferred_element_type=F32)


def _residual_out(x, gate, y, fg_ref):
    x = x + gate * y
    if fg_ref is not None:
        x = x * lax.rsqrt(jnp.mean(x * x, axis=-1, keepdims=True) + EPS) * fg_ref[...]
    return x


def _ffn_kernel(h_ref, x_ref, mod_ref, w1_ref, w3_ref, w2_ref, *rest, final):
    rest = list(rest)
    fg_ref = rest.pop(0) if final else None
    o_ref, acc = rest
    f = pl.program_id(1)

    @pl.when(f == 0)
    def _():
        acc[...] = jnp.zeros_like(acc)

    _swiglu_step(h_ref[...], w1_ref, w3_ref, w2_ref, acc)

    @pl.when(f == NF - 1)
    def _():
        o_ref[...] = _residual_out(x_ref[...], mod_ref[0][:, 5 * D:6 * D], acc[...], fg_ref)


def _dense_mixer(h, x, mod_l, w1, w3, w2, e, final_g=None):
    final = final_g is not None
    rows = LAT_ROWS if final else ROWS
    tile = lambda w: pl.BlockSpec((TM, w), lambda i, f: (i, 0))
    in_specs = [tile(D), tile(D),
                pl.BlockSpec((1, 1, N_MOD * D), lambda i, f: (_mod_row(i), 0, 0)),
                pl.BlockSpec((1, D, TF), lambda i, f: (e, 0, f)),
                pl.BlockSpec((1, D, TF), lambda i, f: (e, 0, f)),
                pl.BlockSpec((1, TF, D), lambda i, f: (e, f, 0))]
    args = [h, x, mod_l, w1, w3, w2]
    if final:
        in_specs.append(pl.BlockSpec((1, D), lambda i, f: (0, 0)))
        args.append(final_g.reshape(1, D))
    return pl.pallas_call(
        functools.partial(_ffn_kernel, final=final),
        grid=(rows // TM, NF),
        in_specs=in_specs,
        out_specs=tile(D),
        out_shape=jax.ShapeDtypeStruct((rows, D), F32),
        scratch_shapes=[pltpu.VMEM((TM, D), F32)],
        compiler_params=_params("parallel", "arbitrary"),
        name="ffn",
    )(*args)


NT = ROWS // TM
MOE_CH = 64
MOE_CMAX = TM // MOE_CH
MOE_NCH = 2 * TM // MOE_CH + N_EXPERTS
MOE_SLOTS = MOE_NCH * MOE_CH
MOE_GC = 16
MOE_GROUPS = NT * MOE_NCH // MOE_GC + N_EXPERTS
MOE_SB = 512
MOE_TF, MOE_NF = TF, NF
PLAN_ROWS = 8


def _plan_kernel(lg_ref, col_ref, row_ref, nch_ref):
    lane = lax.broadcasted_iota(jnp.int32, (TM, LANE), 1)
    ninf = jnp.asarray(-jnp.inf, F32)
    lg = jnp.where(lane < N_EXPERTS, lg_ref[...], ninf)
    m1 = jnp.max(lg, axis=-1, keepdims=True)
    i1 = jnp.min(jnp.where(lg == m1, lane, LANE), axis=-1, keepdims=True)
    lg2 = jnp.where(lane == i1, ninf, lg)
    m2 = jnp.max(lg2, axis=-1, keepdims=True)
    i2 = jnp.min(jnp.where(lg2 == m2, lane, LANE), axis=-1, keepdims=True)
    e = jnp.exp(m2 - m1)
    den = 1.0 + e
    sel1 = lane == i1
    sel2 = lane == i2
    onehot = jnp.where(sel1, 1.0, jnp.where(sel2, 1.0, 0.0))
    tri = jnp.where(lax.broadcasted_iota(jnp.int32, (TM, TM), 1) < lax.broadcasted_iota(jnp.int32, (TM, TM), 0),
                    1.0, 0.0).astype(BF16)
    rank = jnp.dot(tri, onehot.astype(BF16), preferred_element_type=F32)
    cnt = jnp.sum(onehot, axis=0, keepdims=True)
    nch = jnp.floor((cnt + (MOE_CH - 1)) * (1.0 / MOE_CH))
    upper = jnp.where(lax.broadcasted_iota(jnp.int32, (LANE, LANE), 0) < lax.broadcasted_iota(jnp.int32, (LANE, LANE), 1),
                      1.0, 0.0).astype(BF16)
    cbase = jnp.dot(jnp.broadcast_to(nch, (8, LANE)).astype(BF16), upper,
                    preferred_element_type=F32)[0:1]
    slot = MOE_CH * cbase + rank
    slot0 = jnp.sum(jnp.where(sel1, slot, 0.0), axis=-1, keepdims=True)
    slot1 = jnp.sum(jnp.where(sel2, slot, 0.0), axis=-1, keepdims=True)
    col = jnp.where(lane == 0, slot0, jnp.where(lane == 1, slot1,
                    jnp.where(lane == 2, 1.0 / den, jnp.where(lane == 3, e / den, 0.0))))
    col_ref[...] = col
    row_ref[...] = jnp.transpose(col)[0:PLAN_ROWS]
    nch_ref[0] = nch


def _moe_plan(logits):
    return pl.pallas_call(
        _plan_kernel,
        grid=(NT,),
        in_specs=[pl.BlockSpec((TM, LANE), lambda i: (i, 0))],
        out_specs=[pl.BlockSpec((TM, LANE), lambda i: (i, 0)),
                   pl.BlockSpec((PLAN_ROWS, TM), lambda i: (0, i)),
                   pl.BlockSpec((1, 1, LANE), lambda i: (i, 0, 0))],
        out_shape=[jax.ShapeDtypeStruct((ROWS, LANE), F32),
                   jax.ShapeDtypeStruct((PLAN_ROWS, ROWS), F32),
                   jax.ShapeDtypeStruct((NT, 1, LANE), F32)],
        compiler_params=_params("parallel"),
        name="moe_plan",
    )(logits)


def _moe_tables(nch):
    n = nch[:, 0, :N_EXPERTS].astype(jnp.int32)
    cbase = jnp.cumsum(n, axis=1) - n
    used = jnp.sum(n, axis=1)
    c = jnp.arange(MOE_CMAX)
    valid = (c[None, None, :] < n.T[:, :, None]).reshape(N_EXPERTS, -1)
    cid = (jnp.arange(NT)[None, :, None] * MOE_NCH + cbase.T[:, :, None] + c[None, None, :]).reshape(N_EXPERTS, -1)
    ne = jnp.sum(n, axis=0)
    ge = (ne + MOE_GC - 1) // MOE_GC
    gend = jnp.cumsum(ge)
    nused = gend[-1]
    size = MOE_GROUPS * MOE_GC
    pos = (gend - ge)[:, None] * MOE_GC + jnp.cumsum(valid, axis=1) - 1
    dest = jnp.where(valid, pos, size).reshape(-1)
    tab = jnp.full((size,), -1, jnp.int32).at[dest].set(cid.reshape(-1).astype(jnp.int32), mode='drop')
    at = jnp.arange(size)
    tab = tab[jnp.maximum(lax.cummax(jnp.where(tab >= 0, at, -1)), 0)]
    gexp = jnp.minimum(jnp.sum(jnp.arange(MOE_GROUPS)[:, None] >= gend[None, :], axis=1), N_EXPERTS - 1)
    pool = jnp.zeros((NT * MOE_NCH,), jnp.int32).at[jnp.where(at < nused * MOE_GC, tab, NT * MOE_NCH)].set(
        at.astype(jnp.int32), mode='drop')
    return tab, gexp.astype(jnp.int32), nused.reshape(1).astype(jnp.int32), used.astype(jnp.int32), pool


def _dispatch_kernel(used_ref, h_ref, row_ref, xs_ref):
    used_rows = used_ref[pl.program_id(0)] * MOE_CH
    s0 = row_ref[0:1, :]
    s1 = row_ref[1:2, :]
    for blk in range(MOE_SLOTS // MOE_SB):
        rows = slice(blk * MOE_SB, (blk + 1) * MOE_SB)

        @pl.when(blk * MOE_SB < used_rows)
        def _():
            r = (lax.broadcasted_iota(jnp.int32, (MOE_SB, TM), 0) + blk * MOE_SB).astype(F32)
            s = jnp.where(r == s0, 1.0, jnp.where(r == s1, 1.0, 0.0)).astype(BF16)
            xs_ref[rows, :] = jnp.dot(s, h_ref[...], preferred_element_type=F32).astype(BF16)

        @pl.when(blk * MOE_SB >= used_rows)
        def _():
            xs_ref[rows, :] = jnp.zeros((MOE_SB, D), BF16)


def _moe_dispatch(h, plan_row, used):
    return pl.pallas_call(
        _dispatch_kernel,
        grid_spec=pltpu.PrefetchScalarGridSpec(
            num_scalar_prefetch=1, grid=(NT,),
            in_specs=[pl.BlockSpec((TM, D), lambda i, u: (i, 0)),
                      pl.BlockSpec((PLAN_ROWS, TM), lambda i, u: (0, i))],
            out_specs=pl.BlockSpec((MOE_SLOTS, D), lambda i, u: (i, 0))),
        out_shape=jax.ShapeDtypeStruct((NT * MOE_SLOTS, D), BF16),
        compiler_params=_params("parallel"),
        name="moe_dispatch",
    )(used, h, plan_row)


def _experts_kernel(tab_ref, gexp_ref, nused_ref, *refs):
    x_refs = refs[:MOE_GC]
    w1_ref, w3_ref, w2_ref, y_ref, x_scr, acc = refs[MOE_GC:]
    f = pl.program_id(1)

    @pl.when(pl.program_id(0) < nused_ref[0])
    def _():
        @pl.when(f == 0)
        def _():
            for j, r in enumerate(x_refs):
                x_scr[j * MOE_CH:(j + 1) * MOE_CH, :] = r[...]
            acc[...] = jnp.zeros_like(acc)

        _swiglu_step(x_scr[...], w1_ref, w3_ref, w2_ref, acc)

        @pl.when(f == MOE_NF - 1)
        def _():
            y_ref[...] = acc[...].astype(BF16)


def _moe_experts(xs, tab, gexp, nused, w1, w3, w2, e0):
    grp = lambda g, nu: jnp.minimum(g, nu[0] - 1)
    ftile = lambda g, f, nu: jnp.where(g < nu[0], f, MOE_NF - 1)
    chunk = lambda j: pl.BlockSpec((MOE_CH, D), lambda g, f, tab, ge, nu: (tab[grp(g, nu) * MOE_GC + j], 0))
    rows = MOE_GC * MOE_CH
    return pl.pallas_call(
        _experts_kernel,
        grid_spec=pltpu.PrefetchScalarGridSpec(
            num_scalar_prefetch=3, grid=(MOE_GROUPS, MOE_NF),
            in_specs=[chunk(j) for j in range(MOE_GC)] + [
                pl.BlockSpec((1, D, MOE_TF), lambda g, f, tab, ge, nu: (e0 + ge[grp(g, nu)], 0, ftile(g, f, nu))),
                pl.BlockSpec((1, D, MOE_TF), lambda g, f, tab, ge, nu: (e0 + ge[grp(g, nu)], 0, ftile(g, f, nu))),
                pl.BlockSpec((1, MOE_TF, D), lambda g, f, tab, ge, nu: (e0 + ge[grp(g, nu)], ftile(g, f, nu), 0))],
            out_specs=pl.BlockSpec((rows, D), lambda g, f, tab, ge, nu: (grp(g, nu), 0)),
            scratch_shapes=[pltpu.VMEM((rows, D), BF16), pltpu.VMEM((rows, D), F32)]),
        out_shape=jax.ShapeDtypeStruct((MOE_GROUPS * rows, D), BF16),
        compiler_params=_params("arbitrary", "arbitrary"),
        name="moe_experts",
    )(tab, gexp, nused, *([xs] * MOE_GC), w1, w3, w2)


def _combine_kernel(used_ref, pool_ref, *refs, final):
    refs = list(refs)
    y_refs = [refs.pop(0) for _ in range(MOE_NCH)]
    col_ref, x_ref, mod_ref = refs[:3]
    fg_ref = refs[3] if final else None
    o_ref, y_scr, acc = refs[-3:]
    used = used_ref[pl.program_id(0)]
    col = col_ref[...]
    s0, s1, g0, g1 = col[:, 0:1], col[:, 1:2], col[:, 2:3], col[:, 3:4]
    acc[...] = jnp.zeros_like(acc)
    per_blk = MOE_SB // MOE_CH
    for blk in range(MOE_SLOTS // MOE_SB):
        @pl.when(blk * per_blk < used)
        def _():
            for q in range(blk * per_blk, (blk + 1) * per_blk):
                y_scr[q * MOE_CH:(q + 1) * MOE_CH, :] = y_refs[q][...]
            c = (lax.broadcasted_iota(jnp.int32, (TM, MOE_SB), 1) + blk * MOE_SB).astype(F32)
            w = jnp.where(c == s0, g0, jnp.where(c == s1, g1, 0.0)).astype(BF16)
            acc[...] += jnp.dot(w, y_scr[blk * MOE_SB:(blk + 1) * MOE_SB, :], preferred_element_type=F32)
    o_ref[...] = _residual_out(x_ref[...], mod_ref[0][:, 5 * D:6 * D], acc[...], fg_ref)


def _moe_combine(y, pool, used, plan_col, x, mod_l, final_g=None):
    final = final_g is not None
    rows = LAT_ROWS if final else ROWS
    tile = lambda w: pl.BlockSpec((TM, w), lambda i, u, pc: (i, 0))
    chunk = lambda q: pl.BlockSpec((MOE_CH, D), lambda i, u, pc: (pc[i * MOE_NCH + q], 0))
    in_specs = [chunk(q) for q in range(MOE_NCH)] + [
        tile(LANE), tile(D), pl.BlockSpec((1, 1, N_MOD * D), lambda i, u, pc: (_mod_row(i), 0, 0))]
    args = [y] * MOE_NCH + [plan_col, x, mod_l]
    if final:
        in_specs.append(pl.BlockSpec((1, D), lambda i, u, pc: (0, 0)))
        args.append(final_g.reshape(1, D))
    return pl.pallas_call(
        functools.partial(_combine_kernel, final=final),
        grid_spec=pltpu.PrefetchScalarGridSpec(
            num_scalar_prefetch=2, grid=(rows // TM,),
            in_specs=in_specs, out_specs=tile(D),
            scratch_shapes=[pltpu.VMEM((MOE_SLOTS, D), BF16), pltpu.VMEM((TM, D), F32)]),
        out_shape=jax.ShapeDtypeStruct((rows, D), F32),
        compiler_params=_params("parallel"),
        name="moe_combine",
    )(used, pool, *args)


def _moe_mixer(h, logits, x, mod_l, w1, w3, w2, e0, final_g=None):
    plan_col, plan_row, nch = _moe_plan(logits)
    tab, gexp, nused, used, pool = _moe_tables(nch)
    xs = _moe_dispatch(h, plan_row, used)
    y = _moe_experts(xs, tab, gexp, nused, w1, w3, w2, e0)
    return _moe_combine(y, pool, used, plan_col, x, mod_l, final_g)


def kernel(x, c, ctx, c_ctx, w_mod, b_mod, norm1_g, norm2_g, w_in, rpb, hy_conv_w, hy_conv_b, hy_w1, hy_b1, hy_w2, hy_b2, hy_w3, hy_freq, hy_bias, w_pa, w_pb, w_o, ffn_w1, ffn_w3, ffn_w2, moe_router, moe_w1, moe_w3, moe_w2, final_g):
    xs = jnp.concatenate([x.reshape(LAT_ROWS, D), ctx.reshape(CTX_ROWS, D)], axis=0)
    mod = _modulation(c, c_ctx, w_mod, b_mod)
    tabs = _fft_tables()
    ctabs = _ctx_tables()
    filt_lat = _implicit_filters(SEQ, hy_w1, hy_b1, hy_w2, hy_b2, hy_w3, hy_freq)
    filt_ctx = _implicit_filters(CTX, hy_w1, hy_b1, hy_w2, hy_b2, hy_w3, hy_freq)
    spec_re, spec_im = _filter_spectra(filt_lat, tabs)
    nblk = HY_W // LANE

    moe_w = [w.reshape((-1,) + w.shape[2:]).astype(BF16) for w in (moe_w1, moe_w3, moe_w2)]
    ffn_w = [w.astype(BF16) for w in (ffn_w1, ffn_w3, ffn_w2)]
    w_in_b = w_in.astype(BF16)
    bias_tabs = _bias_tables(rpb)

    for l in range(DEPTH):
        last = l == DEPTH - 1
        mod_l = mod[l].reshape(MOD_ROWS, 1, N_MOD * D)
        cw, cb = hy_conv_w[l], hy_conv_b[l].reshape(1, 3 * HY_W)
        p = _in_proj(xs, mod_l, norm1_g[l].reshape(1, D), w_in_b, l)

        o_att = _context_attention(p, _neighbourhood_attention(p, bias_tabs, l))
        z = _hyena_conv(p, COL_HV, p, COL_HX1, cw, cb, 0, nblk, spec_re, spec_im, l, 0,
                        hy_bias[l], tabs, True)
        o_hy = _hyena_conv(z, 0, p, COL_HX2, cw, cb, 0, 2 * nblk, spec_re, spec_im, l, 1,
                           hy_bias[l], tabs, False)
        o_hy = _ctx_hyena(p, cw, cb, filt_ctx, l, hy_bias[l], ctabs, o_hy)

        i = l // 2
        router = moe_router[i] if l % 2 else None
        res = _merge(o_att, o_hy, p, xs, mod_l, norm2_g[l].reshape(1, D), w_pa, w_pb, w_o, l, router)
        fg = final_g if last else None
        if l % 2 == 0:
            xs = _dense_mixer(res[1], res[0], mod_l, *ffn_w, i, final_g=fg)
        else:
            xs = _moe_mixer(res[1], res[2], res[0], mod_l, *moe_w, i * N_EXPERTS, final_g=fg)
    return xs.reshape(NB, SEQ, D)
```

```python
import functools
import math

import numpy as np
import jax
import jax.numpy as jnp
from jax import lax
from jax.experimental import pallas as pl
from jax.experimental.pallas import tpu as pltpu

F32 = jnp.float32
BF16 = jnp.bfloat16
HIGHEST = lax.Precision.HIGHEST

D = 1024
NB = 4
SEQ = 4096
DEPTH = 4
CTX = 256
GRID_W = 64
GRID_H = SEQ // GRID_W
N_HEADS = 8
HEAD_DIM = 64
ATT_W = N_HEADS * HEAD_DIM
WIN_ROWS = 8
WIN_COLS = 16
HY_W = 512
HY_ORDER = 2
HY_SHORT = 3
HY_EMB = 33
HY_BANDS = (HY_EMB - 1) // 2
HY_FILT = 64
HY_DECAY_TARGET = 1e-2
HY_MAX_DECAY = math.log(HY_DECAY_TARGET) / 0.3
HY_MIN_DECAY = math.log(HY_DECAY_TARGET) / 1.5
HY_DECAY_SHIFT = 0.05
PROJ_W = 3 * ATT_W + 3 * HY_W + 2 * D
D_FF = 2816
N_EXPERTS = 8
N_MOD = 6
EPS = 1e-6

LAT_ROWS = NB * SEQ
CTX_ROWS = NB * CTX
ROWS = LAT_ROWS + CTX_ROWS
MOD_ROWS = 8
CTX_MOD_ROW = NB

LANE = 128
TM = 1024
VMEM_LIMIT = 56 * 1024 * 1024

COL_Q, COL_K, COL_V = 0, ATT_W // LANE, 2 * ATT_W // LANE
COL_HV = 3 * ATT_W // LANE
COL_HX1 = COL_HV + HY_W // LANE
COL_HX2 = COL_HX1 + HY_W // LANE
COL_GA = (3 * ATT_W + 3 * HY_W) // D
COL_GB = COL_GA + 1


def _mod_row(i):
    return jnp.where(i < LAT_ROWS // TM, (i * TM) // SEQ, CTX_MOD_ROW)


def _params(*sem):
    return pltpu.CompilerParams(dimension_semantics=sem, vmem_limit_bytes=VMEM_LIMIT)


def _mod_kernel(s_ref, w_ref, b_ref, o_ref):
    s = s_ref[...]
    s = s * jax.nn.sigmoid(s)
    o_ref[0] = jnp.dot(s, w_ref[0], precision=HIGHEST, preferred_element_type=F32) + b_ref[0]


def _modulation(c, c_ctx, w_mod, b_mod):
    s = jnp.zeros((MOD_ROWS, D), F32).at[:NB].set(c).at[CTX_MOD_ROW].set(c_ctx)
    tn = 1536
    return pl.pallas_call(
        _mod_kernel,
        grid=(DEPTH, N_MOD * D // tn),
        in_specs=[pl.BlockSpec((MOD_ROWS, D), lambda l, j: (0, 0)),
                  pl.BlockSpec((1, D, tn), lambda l, j: (l, 0, j)),
                  pl.BlockSpec((1, 1, tn), lambda l, j: (l, 0, j))],
        out_specs=pl.BlockSpec((1, MOD_ROWS, tn), lambda l, j: (l, 0, j)),
        out_shape=jax.ShapeDtypeStruct((DEPTH, MOD_ROWS, N_MOD * D), F32),
        compiler_params=_params("parallel", "parallel"),
        name="modulation",
    )(s, w_mod, b_mod.reshape(DEPTH, 1, N_MOD * D))


def _norm_mod(x, g, shift, scale):
    y = x * lax.rsqrt(jnp.mean(x * x, axis=-1, keepdims=True) + EPS) * g
    return y * (1.0 + scale) + shift


def _inproj_kernel(x_ref, mod_ref, g_ref, w_ref, o_ref, h_scr):
    @pl.when(pl.program_id(1) == 0)
    def _():
        m = mod_ref[0]
        h = _norm_mod(x_ref[...], g_ref[...], m[:, 0:D], m[:, D:2 * D])
        h_scr[...] = h.astype(BF16)

    o_ref[...] = jnp.dot(h_scr[...], w_ref[0].astype(BF16),
                         preferred_element_type=F32).astype(BF16)


def _in_proj(x, mod_l, g, w, layer):
    tn = 1024
    return pl.pallas_call(
        _inproj_kernel,
        grid=(ROWS // TM, PROJ_W // tn),
        in_specs=[pl.BlockSpec((TM, D), lambda i, j: (i, 0)),
                  pl.BlockSpec((1, 1, N_MOD * D), lambda i, j: (_mod_row(i), 0, 0)),
                  pl.BlockSpec((1, D), lambda i, j: (0, 0)),
                  pl.BlockSpec((1, D, tn), lambda i, j: (layer, 0, j))],
        out_specs=pl.BlockSpec((TM, tn), lambda i, j: (i, j)),
        out_shape=jax.ShapeDtypeStruct((ROWS, PROJ_W), BF16),
        scratch_shapes=[pltpu.VMEM((TM, D), BF16)],
        compiler_params=_params("parallel", "arbitrary"),
        name="in_proj",
    )(x, mod_l, g, w)


NEG = -1e30


def _nt_dot(a, b):
    return lax.dot_general(a, b, (((1,), (1,)), ((), ())), preferred_element_type=F32)


def _attend_pair(qp, keys, values, biases):
    lane = lax.broadcasted_iota(jnp.int32, qp.shape, 1)
    qs = qp * jnp.asarray(HEAD_DIM ** -0.5, BF16)
    outs = []
    for a in range(2):
        sel = (lane < HEAD_DIM) if a == 0 else (lane >= HEAD_DIM)
        qa = jnp.where(sel, qs, jnp.zeros_like(qs))
        ss = []
        for k, b in zip(keys, biases):
            s = _nt_dot(qa, k)
            if b is not None:
                s = s + b[a]
            ss.append(s)
        m = functools.reduce(jnp.maximum, [jnp.max(s, axis=-1, keepdims=True) for s in ss])
        ps = [jnp.exp(s - m) for s in ss]
        den = functools.reduce(jnp.add, [jnp.sum(p, axis=-1, keepdims=True) for p in ps])
        o = functools.reduce(jnp.add, [jnp.dot(p.astype(BF16), v, preferred_element_type=F32)
                                       for p, v in zip(ps, values)])
        outs.append(o / den)
    return jnp.where(lane < HEAD_DIM, outs[0], outs[1])


NLOC = WIN_ROWS * GRID_W
NKEY = NLOC + CTX
NA_PAIRS = GRID_H // 2
NA_STEPS = NA_PAIRS + 2
NA_UNITS = 2 * N_HEADS


def _window_start(r):
    r0 = jnp.clip(r - WIN_ROWS // 2, 0, GRID_H - WIN_ROWS)
    return pl.multiple_of(r0 * GRID_W, GRID_W)


def _natt_kernel(q_ref, k_ref, v_ref, kc_ref, vc_ref, bt0_ref, bt1_ref, o_ref, s0, s1, p0, p1, l0, l1):
    j = pl.program_id(1)
    lane = lax.broadcasted_iota(jnp.int32, (GRID_W, LANE), 1)
    lo = lane < HEAD_DIM
    hi = lane >= HEAD_DIM

    @pl.when(j == 0)
    def _():
        s1[...] = jnp.zeros_like(s1)
        p0[...] = jnp.zeros_like(p0)
        p1[...] = jnp.zeros_like(p1)
        l0[...] = jnp.ones_like(l0)
        l1[...] = jnp.ones_like(l1)

    def step(s_cur, s_prev, p_cur, p_prev, l_cur, l_prev):
        ro = jnp.maximum(2 * j - 4, 0)
        rs = jnp.minimum(2 * j, GRID_H - 2)
        for half, bt_ref in enumerate((bt0_ref, bt1_ref)):
            start_o = _window_start(ro + half)
            start_s = _window_start(rs + half)
            rows = slice(half * GRID_W, (half + 1) * GRID_W)
            for hp in range(N_HEADS // 2):
                cs = slice(hp * LANE, (hp + 1) * LANE)
                u0 = half * N_HEADS + 2 * hp
                units = (u0, u0 + 1)
                pair = slice(u0, u0 + 2)
                m2 = 2 * GRID_W

                p2 = p_cur[pair].reshape(m2, NKEY)
                acc = (jnp.dot(p2[:, 0:NLOC], v_ref[pl.ds(start_o, NLOC), cs], preferred_element_type=F32)
                       + jnp.dot(p2[:, NLOC:NKEY], vc_ref[:, cs], preferred_element_type=F32))
                o_ref[rows, cs] = jnp.where(lo, acc[0:GRID_W] / l_cur[u0],
                                            acc[GRID_W:m2] / l_cur[u0 + 1]).astype(BF16)

                for u in units:
                    x = s_prev[u]
                    e = jnp.exp(x - jnp.max(x, axis=-1, keepdims=True))
                    l_prev[u] = jnp.broadcast_to(jnp.sum(e, axis=-1, keepdims=True), (GRID_W, LANE))
                    p_prev[u] = e.astype(BF16)

                qs = q_ref[rows, cs] * jnp.asarray(HEAD_DIM ** -0.5, BF16)
                zero = jnp.zeros_like(qs)
                q2 = jnp.concatenate([jnp.where(lo, qs, zero), jnp.where(hi, qs, zero)], axis=0)
                bias = bt_ref[0, 2 * hp:2 * hp + 2].astype(F32)
                s_cur[pair, :, 0:NLOC] = _nt_dot(q2, k_ref[pl.ds(start_s, NLOC), cs]).reshape(2, GRID_W, NLOC) + bias
                s_cur[pair, :, NLOC:NKEY] = _nt_dot(q2, kc_ref[:, cs]).reshape(2, GRID_W, CTX)

    @pl.when(j % 2 == 0)
    def _():
        step(s0, s1, p0, p1, l0, l1)

    @pl.when(j % 2 == 1)
    def _():
        step(s1, s0, p1, p0, l1, l0)


def _bias_tables(rpb):
    col = np.arange(GRID_W)
    c0 = np.clip(col - WIN_COLS // 2, 0, GRID_W - WIN_COLS)
    valid = (col[None, :] >= c0[:, None]) & (col[None, :] < c0[:, None] + WIN_COLS)
    dc = np.clip(col[None, :] - col[:, None] + (WIN_COLS - 1), 0, 2 * WIN_COLS - 2)
    pick = (np.arange(2 * WIN_COLS - 1)[:, None, None] == dc[None]).astype(np.float32)
    t = jnp.einsum('lhrd,dqk->lhrqk', rpb, jnp.asarray(pick), precision=HIGHEST)
    t = jnp.where(valid, t, NEG)
    win = jnp.stack([t[:, :, d0:d0 + WIN_ROWS] for d0 in range(WIN_ROWS)], axis=1)
    win = win.transpose(0, 1, 2, 4, 3, 5)
    return win.reshape(DEPTH * WIN_ROWS, N_HEADS, GRID_W, NLOC).astype(BF16)


def _bias_index(r):
    r0 = jnp.clip(r - WIN_ROWS // 2, 0, GRID_H - WIN_ROWS)
    return r0 - r + (WIN_ROWS - 1)


def _neighbourhood_attention(p, bt, layer):
    blk = GRID_H // 2
    ctx_blk0 = LAT_ROWS // CTX
    score_row = lambda j, half: jnp.minimum(2 * j, GRID_H - 2) + half
    bias_spec = lambda half: pl.BlockSpec(
        (1, N_HEADS, GRID_W, NLOC), lambda b, j: (layer * WIN_ROWS + _bias_index(score_row(j, half)), 0, 0, 0))
    return pl.pallas_call(
        _natt_kernel,
        grid=(NB, NA_STEPS),
        in_specs=[pl.BlockSpec((2 * GRID_W, ATT_W), lambda b, j: (b * blk + jnp.minimum(j, blk - 1), 0)),
                  pl.BlockSpec((SEQ, ATT_W), lambda b, j: (b, 1)),
                  pl.BlockSpec((SEQ, ATT_W), lambda b, j: (b, 2)),
                  pl.BlockSpec((CTX, ATT_W), lambda b, j: (ctx_blk0 + b, 1)),
                  pl.BlockSpec((CTX, ATT_W), lambda b, j: (ctx_blk0 + b, 2)),
                  bias_spec(0), bias_spec(1)],
        out_specs=pl.BlockSpec((2 * GRID_W, ATT_W), lambda b, j: (b * blk + jnp.maximum(j - 2, 0), 0)),
        out_shape=jax.ShapeDtypeStruct((ROWS, ATT_W), BF16),
        scratch_shapes=[pltpu.VMEM((NA_UNITS, GRID_W, NKEY), F32)] * 2
                       + [pltpu.VMEM((NA_UNITS, GRID_W, NKEY), BF16)] * 2
                       + [pltpu.VMEM((NA_UNITS, GRID_W, LANE), F32)] * 2,
        compiler_params=_params("parallel", "arbitrary"),
        name="neighbourhood_attention",
    )(p, p, p, p, p, bt, bt)


def _catt_kernel(q_ref, k_ref, v_ref, dst_ref, o_ref):
    del dst_ref
    for hp in range(N_HEADS // 2):
        cs = slice(hp * LANE, (hp + 1) * LANE)
        o = _attend_pair(q_ref[:, cs], [k_ref[:, cs]], [v_ref[:, cs]], [None])
        o_ref[:, cs] = o.astype(BF16)


def _context_attention(p, o_att):
    ctx_blk0 = LAT_ROWS // CTX
    return pl.pallas_call(
        _catt_kernel,
        grid=(NB,),
        in_specs=[pl.BlockSpec((CTX, ATT_W), lambda b: (ctx_blk0 + b, 0)),
                  pl.BlockSpec((CTX, ATT_W), lambda b: (ctx_blk0 + b, 1)),
                  pl.BlockSpec((CTX, ATT_W), lambda b: (ctx_blk0 + b, 2)),
                  pl.BlockSpec(memory_space=pl.ANY)],
        out_specs=pl.BlockSpec((CTX, ATT_W), lambda b: (ctx_blk0 + b, 0)),
        out_shape=jax.ShapeDtypeStruct((ROWS, ATT_W), BF16),
        input_output_aliases={3: 0},
        compiler_params=_params("parallel"),
        name="context_attention",
    )(p, p, p, o_att)


def _filter_features(L):
    pos = np.arange(L, dtype=np.float64)
    t = pos / max(L - 1, 1)
    bands = np.linspace(1e-4, HY_BANDS - 1, HY_BANDS)
    ang = (2.0 * math.pi / L) * pos[:, None] * bands[None, :]
    z = np.concatenate([t[:, None], np.cos(ang), -np.sin(ang)], axis=-1)
    zp = np.zeros((L, LANE), np.float32)
    zp[:, :HY_EMB] = z
    return zp


def _filter_kernel(z_ref, w1_ref, b1_ref, w2_ref, b2_ref, w3_ref, fr_ref, rate_ref, o_ref, *, L, tl):
    dot = functools.partial(jnp.dot, precision=HIGHEST, preferred_element_type=F32)
    fr = fr_ref[0]
    a = jnp.sin(fr[0:1] * (dot(z_ref[...], w1_ref[0]) + b1_ref[0]))
    a = jnp.sin(fr[1:2] * (dot(a, w2_ref[0]) + b2_ref[0]))
    h = dot(a, w3_ref[0])
    pos = (pl.program_id(1) * tl + lax.broadcasted_iota(jnp.int32, (tl, 1), 0)).astype(F32)
    dist = jnp.abs(pos - float(L // 2)) * (2.0 / L)
    o_ref[0] = h * (jnp.exp(-dist * rate_ref[...]) + HY_DECAY_SHIFT)


def _implicit_filters(L, w1, b1, w2, b2, w3, freq):
    tl = min(L, 512)
    z = jnp.asarray(_filter_features(L))
    w1p = jnp.zeros((DEPTH, LANE, HY_FILT), F32).at[:, :HY_EMB].set(w1)
    rates = np.abs(np.linspace(HY_MIN_DECAY, HY_MAX_DECAY, HY_ORDER * HY_W)).astype(np.float32)
    cw = HY_ORDER * HY_W
    return pl.pallas_call(
        functools.partial(_filter_kernel, L=L, tl=tl),
        grid=(DEPTH, L // tl),
        in_specs=[pl.BlockSpec((tl, LANE), lambda l, i: (i, 0)),
                  pl.BlockSpec((1, LANE, HY_FILT), lambda l, i: (l, 0, 0)),
                  pl.BlockSpec((1, 1, HY_FILT), lambda l, i: (l, 0, 0)),
                  pl.BlockSpec((1, HY_FILT, HY_FILT), lambda l, i: (l, 0, 0)),
                  pl.BlockSpec((1, 1, HY_FILT), lambda l, i: (l, 0, 0)),
                  pl.BlockSpec((1, HY_FILT, cw), lambda l, i: (l, 0, 0)),
                  pl.BlockSpec((1, 2, HY_FILT), lambda l, i: (l, 0, 0)),
                  pl.BlockSpec((1, cw), lambda l, i: (0, 0))],
        out_specs=pl.BlockSpec((1, tl, cw), lambda l, i: (l, i, 0)),
        out_shape=jax.ShapeDtypeStruct((DEPTH, L, cw), F32),
        compiler_params=_params("parallel", "parallel"),
        name=f"implicit_filters_{L}",
    )(z, w1p, b1.reshape(DEPTH, 1, HY_FILT), w2, b2.reshape(DEPTH, 1, HY_FILT), w3, freq,
      jnp.asarray(rates).reshape(1, cw))


FFT_N = 2 * SEQ
FA = 64
FB = 128
FA_IN = SEQ // FB
PITCH = 136
HALO = 8
UNROLL_ROWS = 16
UNROLL_SLABS = 8


def _fft_tables():
    a = np.arange(FA_IN)
    r = np.arange(FA)
    b = np.arange(FB)
    q = np.arange(FB)
    ph = a[None, None, :] * r[None, :, None] / FA + b[:, None, None] * r[None, :, None] / FFT_N
    g1 = np.exp(-2j * np.pi * ph)
    g1 = np.concatenate([g1.real, g1.imag], axis=1)
    f = np.exp(-2j * np.pi * q[:, None] * b[None, :] / FB)
    f2 = np.concatenate([f.real, f.imag], axis=0)
    f2c = np.concatenate([f.real, -f.imag], axis=0)
    ao = np.arange(FA_IN) + (SEQ // 2) // FB
    ph2 = ao[None, :, None] * r[None, None, :] / FA + b[:, None, None] * r[None, None, :] / FFT_N
    g2 = np.exp(2j * np.pi * ph2) / FFT_N
    g2 = np.concatenate([g2.real, g2.imag], axis=1)
    return tuple(jnp.asarray(t, dtype=BF16) for t in (g1, f2, f2c, g2))


def _cplx(out, m, n):
    re = out[0:m, 0:n] - out[m:2 * m, n:2 * n]
    im = out[0:m, n:2 * n] + out[m:2 * m, 0:n]
    return re, im


def _short_conv_slabs(src_ref, rows, t_scr, cw_ref, cb_ref, apply_conv, emit):
    nslab = rows // FB
    if not apply_conv:
        for a in range(nslab):
            emit(a, src_ref[a * FB:(a + 1) * FB, :].astype(F32))
        return
    zero = jnp.zeros((HALO, LANE), F32)
    t_scr[0:HALO, :] = zero
    t_scr[HALO + rows:2 * HALO + rows, :] = zero
    for a in range(nslab):
        t_scr[HALO + a * FB:HALO + (a + 1) * FB, :] = src_ref[a * FB:(a + 1) * FB, :].astype(F32)
    w0, w1, w2, bias = cw_ref[0:1, :], cw_ref[1:2, :], cw_ref[2:3, :], cb_ref[...]
    for a in range(nslab):
        o = HALO + a * FB
        y = (w0 * t_scr[o - 1:o - 1 + FB, :] + w1 * t_scr[o:o + FB, :]
             + w2 * t_scr[o + 1:o + 1 + FB, :] + bias)
        emit(a, y)


def _fwd_stage1(b, s_refs, g1_ref, y_r, y_i):
    cols = [s[pl.ds(b, FA_IN, stride=PITCH), :].astype(BF16) for s in s_refs]
    rhs = cols[0] if len(cols) == 1 else jnp.concatenate(cols, axis=1)
    out = jnp.dot(g1_ref[b], rhs, preferred_element_type=F32)
    if len(cols) == 1:
        re, im = out[0:FA], out[FA:2 * FA]
    else:
        re, im = _cplx(out, FA, LANE)
    y_r[pl.ds(b, FA, stride=PITCH), :] = re
    y_i[pl.ds(b, FA, stride=PITCH), :] = im


def _fwd_stage2(r, y_r, y_i, f_ref):
    o = pl.multiple_of(r * PITCH, 8)
    rhs = jnp.concatenate([y_r[pl.ds(o, FB), :].astype(BF16), y_i[pl.ds(o, FB), :].astype(BF16)], axis=1)
    out = jnp.dot(f_ref[...], rhs, preferred_element_type=F32)
    return _cplx(out, FB, LANE)


def _spectrum_kernel(h_ref, g1_ref, f_ref, hr_ref, hi_ref, s_scr, y_r, y_i):
    for a in range(FA_IN):
        s_scr[a * PITCH:a * PITCH + FB, :] = h_ref[0, a * FB:(a + 1) * FB, :]

    def s1(b, c):
        _fwd_stage1(b, [s_scr], g1_ref, y_r, y_i)
        return c
    lax.fori_loop(0, FB, s1, 0, unroll=UNROLL_ROWS)

    def s2(r, c):
        xr, xi = _fwd_stage2(r, y_r, y_i, f_ref)
        o = pl.multiple_of(r * FB, FB)
        hr_ref[0, pl.ds(o, FB), :] = xr
        hi_ref[0, pl.ds(o, FB), :] = xi
        return c
    lax.fori_loop(0, FA, s2, 0, unroll=UNROLL_SLABS)


def _filter_spectra(h, tabs):
    g1, f2, _, _ = tabs
    cw = HY_ORDER * HY_W
    spec = pl.BlockSpec((1, FFT_N, LANE), lambda l, c: (l, 0, c))
    return pl.pallas_call(
        _spectrum_kernel,
        grid=(DEPTH, cw // LANE),
        in_specs=[pl.BlockSpec((1, SEQ, LANE), lambda l, c: (l, 0, c)),
                  pl.BlockSpec(g1.shape, lambda l, c: (0, 0, 0)),
                  pl.BlockSpec(f2.shape, lambda l, c: (0, 0))],
        out_specs=[spec, spec],
        out_shape=[jax.ShapeDtypeStruct((DEPTH, FFT_N, cw), F32)] * 2,
        scratch_shapes=[pltpu.VMEM((FA_IN * PITCH, LANE), F32),
                        pltpu.VMEM((FA * PITCH, LANE), F32),
                        pltpu.VMEM((FA * PITCH, LANE), F32)],
        compiler_params=_params("parallel", "parallel"),
        name="filter_spectra",
    )(h, g1, f2)


def _hyconv_kernel(u_ref, x_ref, cwu_ref, cbu_ref, cwx_ref, cbx_ref,
                   hr_ref, hi_ref, bias_ref, g1_ref, f_ref, fc_ref, g2_ref,
                   o_ref, t_scr, s_r, s_i, y_r, y_i, *, conv_input):
    halves = [(s_r, slice(0, SEQ)), (s_i, slice(SEQ, 2 * SEQ))]

    def fill(s):
        def emit(a, slab):
            s[a * PITCH:a * PITCH + FB, :] = slab
        return emit
    for s, rows in halves:
        _short_conv_slabs(u_ref.at[rows, :], SEQ, t_scr, cwu_ref, cbu_ref, conv_input, fill(s))

    def s1(b, c):
        _fwd_stage1(b, [s_r, s_i], g1_ref, y_r, y_i)
        return c
    lax.fori_loop(0, FB, s1, 0, unroll=UNROLL_ROWS)

    def s2(r, c):
        xr, xi = _fwd_stage2(r, y_r, y_i, f_ref)
        oh = pl.multiple_of(r * FB, FB)
        hr = hr_ref[0, pl.ds(oh, FB), :]
        hi = hi_ref[0, pl.ds(oh, FB), :]
        vr = xr * hr - xi * hi
        vi = xr * hi + xi * hr
        rhs = jnp.concatenate([vr.astype(BF16), vi.astype(BF16)], axis=1)
        wr, wi = _cplx(jnp.dot(fc_ref[...], rhs, preferred_element_type=F32), FB, LANE)
        o = pl.multiple_of(r * PITCH, 8)
        y_r[pl.ds(o, FB), :] = wr
        y_i[pl.ds(o, FB), :] = wi
        return c
    lax.fori_loop(0, FA, s2, 0, unroll=UNROLL_SLABS)

    bias = bias_ref[0]

    def s3(b, c):
        rhs = jnp.concatenate([y_r[pl.ds(b, FA, stride=PITCH), :].astype(BF16),
                               y_i[pl.ds(b, FA, stride=PITCH), :].astype(BF16)], axis=1)
        re, im = _cplx(jnp.dot(g2_ref[b], rhs, preferred_element_type=F32), FA_IN, LANE)
        idx = pl.ds(b, FA_IN, stride=PITCH)
        s_r[idx, :] = re + s_r[idx, :] * bias
        s_i[idx, :] = im + s_i[idx, :] * bias
        return c
    lax.fori_loop(0, FB, s3, 0, unroll=UNROLL_ROWS)

    for s, rows in halves:
        def emit(a, slab, s=s, out=o_ref.at[rows, :]):
            out[a * FB:(a + 1) * FB, :] = (slab * s[a * PITCH:a * PITCH + FB, :]).astype(out.dtype)
        _short_conv_slabs(x_ref.at[rows, :], SEQ, t_scr, cwx_ref, cbx_ref, True, emit)


def _hyena_conv(u, u_col, x, x_col, conv_w, conv_b, cu_col, cx_col, spec_re, spec_im, layer, order,
                hy_bias_l, tabs, conv_input):
    g1, f2, f2c, g2 = tabs
    nblk = HY_W // LANE
    tok = lambda col: pl.BlockSpec((2 * SEQ, LANE), lambda c, p: (p, col + c))
    chan = lambda rows, col: pl.BlockSpec((rows, LANE), lambda c, p: (0, col + c))
    hspec = pl.BlockSpec((1, FFT_N, LANE), lambda c, p: (layer, 0, order * nblk + c))
    const = lambda shape: pl.BlockSpec(shape, lambda c, p: (0,) * len(shape))
    return pl.pallas_call(
        functools.partial(_hyconv_kernel, conv_input=conv_input),
        grid=(nblk, NB // 2),
        in_specs=[tok(u_col), tok(x_col),
                  chan(HY_SHORT, cu_col), chan(1, cu_col), chan(HY_SHORT, cx_col), chan(1, cx_col),
                  hspec, hspec,
                  pl.BlockSpec((1, 1, LANE), lambda c, p: (order, 0, c)),
                  const(g1.shape), const(f2.shape), const(f2c.shape), const(g2.shape)],
        out_specs=pl.BlockSpec((2 * SEQ, LANE), lambda c, p: (p, c)),
        out_shape=jax.ShapeDtypeStruct((ROWS, HY_W), BF16),
        scratch_shapes=[pltpu.VMEM((SEQ + 2 * HALO, LANE), F32),
                        pltpu.VMEM((FA_IN * PITCH, LANE), F32),
                        pltpu.VMEM((FA_IN * PITCH, LANE), F32),
                        pltpu.VMEM((FA * PITCH, LANE), F32),
                        pltpu.VMEM((FA * PITCH, LANE), F32)],
        compiler_params=_params("arbitrary", "arbitrary"),
        name=f"hyena_conv{order}",
    )(u, x, conv_w, conv_b, conv_w, conv_b, spec_re, spec_im,
      hy_bias_l.reshape(HY_ORDER, 1, HY_W), g1, f2, f2c, g2)


CN = 2 * CTX


def _ctx_tables():
    k = np.arange(CN)
    n = np.arange(CTX)
    f = np.exp(-2j * np.pi * k[:, None] * n[None, :] / CN)
    fwd = np.concatenate([f.real, f.imag], axis=0)
    t = np.arange(CTX) + CTX // 2
    g = np.exp(2j * np.pi * t[:, None] * k[None, :] / CN) / CN
    inv = np.concatenate([g.real, g.imag], axis=0)
    return jnp.asarray(fwd, dtype=BF16), jnp.asarray(inv, dtype=BF16)


def _ctx_hyena_kernel(v_ref, x1_ref, x2_ref, cwv_ref, cbv_ref, cw1_ref, cb1_ref, cw2_ref, cb2_ref,
                      h0_ref, h1_ref, bias_ref, fwd_ref, inv_ref, dst_ref, o_ref, t_scr, s_scr, g_scr):
    del dst_ref
    def conv_into(src_ref, b, cw_ref, cb_ref, dst, col):
        view = src_ref.at[b * CTX:(b + 1) * CTX, :]

        def emit(a, slab):
            dst[a * FB:(a + 1) * FB, col * LANE:(col + 1) * LANE] = slab
        _short_conv_slabs(view, CTX, t_scr, cw_ref, cb_ref, True, emit)

    def spectrum(h_ref):
        out = jnp.dot(fwd_ref[...], h_ref[0].astype(BF16), preferred_element_type=F32)
        return out[0:CN], out[CN:2 * CN]

    def long_conv(z2, hr, hi, bias):
        x = jnp.dot(fwd_ref[...], z2.astype(BF16), preferred_element_type=F32)
        xr, xi = _cplx(x, CN, LANE)
        vr = xr * hr - xi * hi
        vi = xr * hi + xi * hr
        rhs = jnp.concatenate([vr.astype(BF16), vi.astype(BF16)], axis=1)
        y = jnp.dot(inv_ref[...], rhs, preferred_element_type=F32)
        yr, yi = _cplx(y, CTX, LANE)
        return jnp.concatenate([yr, yi], axis=1) + z2 * jnp.concatenate([bias, bias], axis=1)

    h0r, h0i = spectrum(h0_ref)
    h1r, h1i = spectrum(h1_ref)
    for p in range(NB // 2):
        for half in range(2):
            conv_into(v_ref, 2 * p + half, cwv_ref, cbv_ref, s_scr, half)
            conv_into(x1_ref, 2 * p + half, cw1_ref, cb1_ref, g_scr, half)
        z = g_scr[...] * long_conv(s_scr[...], h0r, h0i, bias_ref[0])
        for half in range(2):
            conv_into(x2_ref, 2 * p + half, cw2_ref, cb2_ref, g_scr, half)
        y = g_scr[...] * long_conv(z, h1r, h1i, bias_ref[1])
        for half in range(2):
            b = 2 * p + half
            o_ref[b * CTX:(b + 1) * CTX, :] = y[:, half * LANE:(half + 1) * LANE].astype(BF16)


def _ctx_hyena(p, conv_w, conv_b, filt_ctx, layer, hy_bias_l, ctabs, o_hy):
    fwd, inv = ctabs
    nblk = HY_W // LANE
    ctx_blk = LAT_ROWS // CTX_ROWS
    tok = lambda col: pl.BlockSpec((CTX_ROWS, LANE), lambda c: (ctx_blk, col + c))
    chan = lambda rows, col: pl.BlockSpec((rows, LANE), lambda c: (0, col + c))
    hspec = lambda order: pl.BlockSpec((1, CTX, LANE), lambda c: (layer, 0, order * nblk + c))
    const = lambda shape: pl.BlockSpec(shape, lambda c: (0,) * len(shape))
    return pl.pallas_call(
        _ctx_hyena_kernel,
        grid=(nblk,),
        in_specs=[tok(COL_HV), tok(COL_HX1), tok(COL_HX2),
                  chan(HY_SHORT, 0), chan(1, 0), chan(HY_SHORT, nblk), chan(1, nblk),
                  chan(HY_SHORT, 2 * nblk), chan(1, 2 * nblk),
                  hspec(0), hspec(1),
                  pl.BlockSpec((HY_ORDER, 1, LANE), lambda c: (0, 0, c)),
                  const(fwd.shape), const(inv.shape), pl.BlockSpec(memory_space=pl.ANY)],
        out_specs=pl.BlockSpec((CTX_ROWS, LANE), lambda c: (ctx_blk, c)),
        out_shape=jax.ShapeDtypeStruct((ROWS, HY_W), BF16),
        input_output_aliases={14: 0},
        scratch_shapes=[pltpu.VMEM((CTX + 2 * HALO, LANE), F32),
                        pltpu.VMEM((CTX, 2 * LANE), F32),
                        pltpu.VMEM((CTX, 2 * LANE), F32)],
        compiler_params=_params("parallel"),
        name="context_hyena",
    )(p, p, p, conv_w, conv_b, conv_w, conv_b, conv_w, conv_b, filt_ctx, filt_ctx,
      hy_bias_l.reshape(HY_ORDER, 1, HY_W), fwd, inv, o_hy)


def _merge_kernel(oa_ref, oh_ref, ga_ref, gb_ref, x_ref, mod_ref, g2_ref, wpa_ref, wpb_ref, wo_ref,
                  *rest, routed):
    if routed:
        wr_ref, xo_ref, h_ref, rw_ref, wpa_s, wpb_s, wo_s = rest
    else:
        xo_ref, h_ref, wpa_s, wpb_s, wo_s = rest

    @pl.when(pl.program_id(0) == 0)
    def _():
        wpa_s[...] = wpa_ref[0].astype(BF16)
        wpb_s[...] = wpb_ref[0].astype(BF16)
        wo_s[...] = wo_ref[0].astype(BF16)

    a = jnp.dot(oa_ref[...], wpa_s[...], preferred_element_type=F32)
    b = jnp.dot(oh_ref[...], wpb_s[...], preferred_element_type=F32)
    mix = (jax.nn.sigmoid(ga_ref[...].astype(F32)) * a + jax.nn.sigmoid(gb_ref[...].astype(F32)) * b)
    y = jnp.dot(mix.astype(BF16), wo_s[...], preferred_element_type=F32)
    m = mod_ref[0]
    x = x_ref[...] + m[:, 2 * D:3 * D] * y
    xo_ref[...] = x
    h = _norm_mod(x, g2_ref[...], m[:, 3 * D:4 * D], m[:, 4 * D:5 * D])
    h_ref[...] = h.astype(BF16)
    if routed:
        wr = wr_ref[...]
        hh, wh = h.astype(BF16), wr.astype(BF16)
        hl, wl = (h - hh.astype(F32)).astype(BF16), (wr - wh.astype(F32)).astype(BF16)
        dot = functools.partial(jnp.dot, preferred_element_type=F32)
        rw_ref[...] = dot(hh, wh) + (dot(hh, wl) + dot(hl, wh))


def _merge(o_att, o_hy, p, x, mod_l, g2, w_pa, w_pb, w_o, layer, router):
    routed = router is not None
    tile = lambda w: pl.BlockSpec((TM, w), lambda i: (i, 0))
    full = lambda a: pl.BlockSpec(a.shape, lambda i: (0,) * a.ndim)
    layered = lambda a: pl.BlockSpec((1,) + a.shape[1:], lambda i: (layer, 0, 0))
    in_specs = [tile(ATT_W), tile(HY_W),
                pl.BlockSpec((TM, D), lambda i: (i, COL_GA)),
                pl.BlockSpec((TM, D), lambda i: (i, COL_GB)),
                tile(D),
                pl.BlockSpec((1, 1, N_MOD * D), lambda i: (_mod_row(i), 0, 0)),
                pl.BlockSpec((1, D), lambda i: (0, 0)),
                layered(w_pa), layered(w_pb), layered(w_o)]
    args = [o_att, o_hy, p, p, x, mod_l, g2, w_pa, w_pb, w_o]
    out_specs = [tile(D), tile(D)]
    out_shape = [jax.ShapeDtypeStruct((ROWS, D), F32), jax.ShapeDtypeStruct((ROWS, D), BF16)]
    if routed:
        wr = jnp.zeros((D, LANE), F32).at[:, :N_EXPERTS].set(router)
        in_specs.append(full(wr))
        args.append(wr)
        out_specs.append(tile(LANE))
        out_shape.append(jax.ShapeDtypeStruct((ROWS, LANE), F32))
    return pl.pallas_call(
        functools.partial(_merge_kernel, routed=routed),
        grid=(ROWS // TM,),
        in_specs=in_specs, out_specs=out_specs, out_shape=out_shape,
        scratch_shapes=[pltpu.VMEM(w_pa.shape[1:], BF16), pltpu.VMEM(w_pb.shape[1:], BF16),
                        pltpu.VMEM(w_o.shape[1:], BF16)],
        compiler_params=_params("arbitrary"),
        name="merge",
    )(*args)


TF = D_FF // 2


NF = D_FF // TF


SWIGLU_SUB = 512


def _swiglu_step(h, w1_ref, w3_ref, w2_ref, acc):
    tf = w1_ref.shape[2]
    for lo in range(0, tf, SWIGLU_SUB):
        hi = min(lo + SWIGLU_SUB, tf)
        a = jnp.dot(h, w1_ref[0, :, lo:hi].astype(BF16), preferred_element_type=F32)
        b = jnp.dot(h, w3_ref[0, :, lo:hi].astype(BF16), preferred_element_type=F32)
        t = a * jax.nn.sigmoid(a) * b
        acc[...] += jnp.dot(t.astype(BF16), w2_ref[0, lo:hi, :].astype(BF16), preferred_element_type=F32)


def _residual_out(x, gate, y, fg_ref):
    x = x + gate * y
    if fg_ref is not None:
        x = x * lax.rsqrt(jnp.mean(x * x, axis=-1, keepdims=True) + EPS) * fg_ref[...]
    return x


def _ffn_kernel(h_ref, x_ref, mod_ref, w1_ref, w3_ref, w2_ref, *rest, final):
    rest = list(rest)
    fg_ref = rest.pop(0) if final else None
    o_ref, acc = rest
    f = pl.program_id(1)

    @pl.when(f == 0)
    def _():
        acc[...] = jnp.zeros_like(acc)

    _swiglu_step(h_ref[...], w1_ref, w3_ref, w2_ref, acc)

    @pl.when(f == NF - 1)
    def _():
        o_ref[...] = _residual_out(x_ref[...], mod_ref[0][:, 5 * D:6 * D], acc[...], fg_ref)


def _dense_mixer(h, x, mod_l, w1, w3, w2, e, final_g=None):
    final = final_g is not None
    rows = LAT_ROWS if final else ROWS
    tile = lambda w: pl.BlockSpec((TM, w), lambda i, f: (i, 0))
    in_specs = [tile(D), tile(D),
                pl.BlockSpec((1, 1, N_MOD * D), lambda i, f: (_mod_row(i), 0, 0)),
                pl.BlockSpec((1, D, TF), lambda i, f: (e, 0, f)),
                pl.BlockSpec((1, D, TF), lambda i, f: (e, 0, f)),
                pl.BlockSpec((1, TF, D), lambda i, f: (e, f, 0))]
    args = [h, x, mod_l, w1, w3, w2]
    if final:
        in_specs.append(pl.BlockSpec((1, D), lambda i, f: (0, 0)))
        args.append(final_g.reshape(1, D))
    return pl.pallas_call(
        functools.partial(_ffn_kernel, final=final),
        grid=(rows // TM, NF),
        in_specs=in_specs,
        out_specs=tile(D),
        out_shape=jax.ShapeDtypeStruct((rows, D), F32),
        scratch_shapes=[pltpu.VMEM((TM, D), F32)],
        compiler_params=_params("parallel", "arbitrary"),
        name="ffn",
    )(*args)


NT = ROWS // TM
MOE_CH = 64
MOE_CMAX = TM // MOE_CH
MOE_NCH = 2 * TM // MOE_CH + N_EXPERTS
MOE_SLOTS = MOE_NCH * MOE_CH
MOE_GC = 16
MOE_GROUPS = NT * MOE_NCH // MOE_GC + N_EXPERTS
MOE_SB = 512
MOE_TF, MOE_NF = TF, NF
PLAN_ROWS = 8


def _plan_kernel(lg_ref, col_ref, row_ref, nch_ref):
    lane = lax.broadcasted_iota(jnp.int32, (TM, LANE), 1)
    ninf = jnp.asarray(-jnp.inf, F32)
    lg = jnp.where(lane < N_EXPERTS, lg_ref[...], ninf)
    m1 = jnp.max(lg, axis=-1, keepdims=True)
    i1 = jnp.min(jnp.where(lg == m1, lane, LANE), axis=-1, keepdims=True)
    lg2 = jnp.where(lane == i1, ninf, lg)
    m2 = jnp.max(lg2, axis=-1, keepdims=True)
    i2 = jnp.min(jnp.where(lg2 == m2, lane, LANE), axis=-1, keepdims=True)
    e = jnp.exp(m2 - m1)
    den = 1.0 + e
    sel1 = lane == i1
    sel2 = lane == i2
    onehot = jnp.where(sel1, 1.0, jnp.where(sel2, 1.0, 0.0))
    tri = jnp.where(lax.broadcasted_iota(jnp.int32, (TM, TM), 1) < lax.broadcasted_iota(jnp.int32, (TM, TM), 0),
                    1.0, 0.0).astype(BF16)
    rank = jnp.dot(tri, onehot.astype(BF16), preferred_element_type=F32)
    cnt = jnp.sum(onehot, axis=0, keepdims=True)
    nch = jnp.floor((cnt + (MOE_CH - 1)) * (1.0 / MOE_CH))
    upper = jnp.where(lax.broadcasted_iota(jnp.int32, (LANE, LANE), 0) < lax.broadcasted_iota(jnp.int32, (LANE, LANE), 1),
                      1.0, 0.0).astype(BF16)
    cbase = jnp.dot(jnp.broadcast_to(nch, (8, LANE)).astype(BF16), upper,
                    preferred_element_type=F32)[0:1]
    slot = MOE_CH * cbase + rank
    slot0 = jnp.sum(jnp.where(sel1, slot, 0.0), axis=-1, keepdims=True)
    slot1 = jnp.sum(jnp.where(sel2, slot, 0.0), axis=-1, keepdims=True)
    col = jnp.where(lane == 0, slot0, jnp.where(lane == 1, slot1,
                    jnp.where(lane == 2, 1.0 / den, jnp.where(lane == 3, e / den, 0.0))))
    col_ref[...] = col
    row_ref[...] = jnp.transpose(col)[0:PLAN_ROWS]
    nch_ref[0] = nch


def _moe_plan(logits):
    return pl.pallas_call(
        _plan_kernel,
        grid=(NT,),
        in_specs=[pl.BlockSpec((TM, LANE), lambda i: (i, 0))],
        out_specs=[pl.BlockSpec((TM, LANE), lambda i: (i, 0)),
                   pl.BlockSpec((PLAN_ROWS, TM), lambda i: (0, i)),
                   pl.BlockSpec((1, 1, LANE), lambda i: (i, 0, 0))],
        out_shape=[jax.ShapeDtypeStruct((ROWS, LANE), F32),
                   jax.ShapeDtypeStruct((PLAN_ROWS, ROWS), F32),
                   jax.ShapeDtypeStruct((NT, 1, LANE), F32)],
        compiler_params=_params("parallel"),
        name="moe_plan",
    )(logits)


def _moe_tables(nch):
    n = nch[:, 0, :N_EXPERTS].astype(jnp.int32)
    cbase = jnp.cumsum(n, axis=1) - n
    used = jnp.sum(n, axis=1)
    c = jnp.arange(MOE_CMAX)
    valid = (c[None, None, :] < n.T[:, :, None]).reshape(N_EXPERTS, -1)
    cid = (jnp.arange(NT)[None, :, None] * MOE_NCH + cbase.T[:, :, None] + c[None, None, :]).reshape(N_EXPERTS, -1)
    ne = jnp.sum(n, axis=0)
    ge = (ne + MOE_GC - 1) // MOE_GC
    gend = jnp.cumsum(ge)
    nused = gend[-1]
    size = MOE_GROUPS * MOE_GC
    pos = (gend - ge)[:, None] * MOE_GC + jnp.cumsum(valid, axis=1) - 1
    dest = jnp.where(valid, pos, size).reshape(-1)
    tab = jnp.full((size,), -1, jnp.int32).at[dest].set(cid.reshape(-1).astype(jnp.int32), mode='drop')
    at = jnp.arange(size)
    tab = tab[jnp.maximum(lax.cummax(jnp.where(tab >= 0, at, -1)), 0)]
    gexp = jnp.minimum(jnp.sum(jnp.arange(MOE_GROUPS)[:, None] >= gend[None, :], axis=1), N_EXPERTS - 1)
    pool = jnp.zeros((NT * MOE_NCH,), jnp.int32).at[jnp.where(at < nused * MOE_GC, tab, NT * MOE_NCH)].set(
        at.astype(jnp.int32), mode='drop')
    return tab, gexp.astype(jnp.int32), nused.reshape(1).astype(jnp.int32), used.astype(jnp.int32), pool


def _dispatch_kernel(used_ref, h_ref, row_ref, xs_ref):
    used_rows = used_ref[pl.program_id(0)] * MOE_CH
    s0 = row_ref[0:1, :]
    s1 = row_ref[1:2, :]
    for blk in range(MOE_SLOTS // MOE_SB):
        rows = slice(blk * MOE_SB, (blk + 1) * MOE_SB)

        @pl.when(blk * MOE_SB < used_rows)
        def _():
            r = (lax.broadcasted_iota(jnp.int32, (MOE_SB, TM), 0) + blk * MOE_SB).astype(F32)
            s = jnp.where(r == s0, 1.0, jnp.where(r == s1, 1.0, 0.0)).astype(BF16)
            xs_ref[rows, :] = jnp.dot(s, h_ref[...], preferred_element_type=F32).astype(BF16)

        @pl.when(blk * MOE_SB >= used_rows)
        def _():
            xs_ref[rows, :] = jnp.zeros((MOE_SB, D), BF16)


def _moe_dispatch(h, plan_row, used):
    return pl.pallas_call(
        _dispatch_kernel,
        grid_spec=pltpu.PrefetchScalarGridSpec(
            num_scalar_prefetch=1, grid=(NT,),
            in_specs=[pl.BlockSpec((TM, D), lambda i, u: (i, 0)),
                      pl.BlockSpec((PLAN_ROWS, TM), lambda i, u: (0, i))],
            out_specs=pl.BlockSpec((MOE_SLOTS, D), lambda i, u: (i, 0))),
        out_shape=jax.ShapeDtypeStruct((NT * MOE_SLOTS, D), BF16),
        compiler_params=_params("parallel"),
        name="moe_dispatch",
    )(used, h, plan_row)


def _experts_kernel(tab_ref, gexp_ref, nused_ref, *refs):
    x_refs = refs[:MOE_GC]
    w1_ref, w3_ref, w2_ref, y_ref, x_scr, acc = refs[MOE_GC:]
    f = pl.program_id(1)

    @pl.when(pl.program_id(0) < nused_ref[0])
    def _():
        @pl.when(f == 0)
        def _():
            for j, r in enumerate(x_refs):
                x_scr[j * MOE_CH:(j + 1) * MOE_CH, :] = r[...]
            acc[...] = jnp.zeros_like(acc)

        _swiglu_step(x_scr[...], w1_ref, w3_ref, w2_ref, acc)

        @pl.when(f == MOE_NF - 1)
        def _():
            y_ref[...] = acc[...].astype(BF16)


def _moe_experts(xs, tab, gexp, nused, w1, w3, w2, e0):
    grp = lambda g, nu: jnp.minimum(g, nu[0] - 1)
    ftile = lambda g, f, nu: jnp.where(g < nu[0], f, MOE_NF - 1)
    chunk = lambda j: pl.BlockSpec((MOE_CH, D), lambda g, f, tab, ge, nu: (tab[grp(g, nu) * MOE_GC + j], 0))
    rows = MOE_GC * MOE_CH
    return pl.pallas_call(
        _experts_kernel,
        grid_spec=pltpu.PrefetchScalarGridSpec(
            num_scalar_prefetch=3, grid=(MOE_GROUPS, MOE_NF),
            in_specs=[chunk(j) for j in range(MOE_GC)] + [
                pl.BlockSpec((1, D, MOE_TF), lambda g, f, tab, ge, nu: (e0 + ge[grp(g, nu)], 0, ftile(g, f, nu))),
                pl.BlockSpec((1, D, MOE_TF), lambda g, f, tab, ge, nu: (e0 + ge[grp(g, nu)], 0, ftile(g, f, nu))),
                pl.BlockSpec((1, MOE_TF, D), lambda g, f, tab, ge, nu: (e0 + ge[grp(g, nu)], ftile(g, f, nu), 0))],
            out_specs=pl.BlockSpec((rows, D), lambda g, f, tab, ge, nu: (grp(g, nu), 0)),
            scratch_shapes=[pltpu.VMEM((rows, D), BF16), pltpu.VMEM((rows, D), F32)]),
        out_shape=jax.ShapeDtypeStruct((MOE_GROUPS * rows, D), BF16),
        compiler_params=_params("arbitrary", "arbitrary"),
        name="moe_experts",
    )(tab, gexp, nused, *([xs] * MOE_GC), w1, w3, w2)


def _combine_kernel(used_ref, pool_ref, *refs, final):
    refs = list(refs)
    y_refs = [refs.pop(0) for _ in range(MOE_NCH)]
    col_ref, x_ref, mod_ref = refs[:3]
    fg_ref = refs[3] if final else None
    o_ref, y_scr, acc = refs[-3:]
    used = used_ref[pl.program_id(0)]
    col = col_ref[...]
    s0, s1, g0, g1 = col[:, 0:1], col[:, 1:2], col[:, 2:3], col[:, 3:4]
    acc[...] = jnp.zeros_like(acc)
    per_blk = MOE_SB // MOE_CH
    for blk in range(MOE_SLOTS // MOE_SB):
        @pl.when(blk * per_blk < used)
        def _():
            for q in range(blk * per_blk, (blk + 1) * per_blk):
                y_scr[q * MOE_CH:(q + 1) * MOE_CH, :] = y_refs[q][...]
            c = (lax.broadcasted_iota(jnp.int32, (TM, MOE_SB), 1) + blk * MOE_SB).astype(F32)
            w = jnp.where(c == s0, g0, jnp.where(c == s1, g1, 0.0)).astype(BF16)
            acc[...] += jnp.dot(w, y_scr[blk * MOE_SB:(blk + 1) * MOE_SB, :], preferred_element_type=F32)
    o_ref[...] = _residual_out(x_ref[...], mod_ref[0][:, 5 * D:6 * D], acc[...], fg_ref)


def _moe_combine(y, pool, used, plan_col, x, mod_l, final_g=None):
    final = final_g is not None
    rows = LAT_ROWS if final else ROWS
    tile = lambda w: pl.BlockSpec((TM, w), lambda i, u, pc: (i, 0))
    chunk = lambda q: pl.BlockSpec((MOE_CH, D), lambda i, u, pc: (pc[i * MOE_NCH + q], 0))
    in_specs = [chunk(q) for q in range(MOE_NCH)] + [
        tile(LANE), tile(D), pl.BlockSpec((1, 1, N_MOD * D), lambda i, u, pc: (_mod_row(i), 0, 0))]
    args = [y] * MOE_NCH + [plan_col, x, mod_l]
    if final:
        in_specs.append(pl.BlockSpec((1, D), lambda i, u, pc: (0, 0)))
        args.append(final_g.reshape(1, D))
    return pl.pallas_call(
        functools.partial(_combine_kernel, final=final),
        grid_spec=pltpu.PrefetchScalarGridSpec(
            num_scalar_prefetch=2, grid=(rows // TM,),
            in_specs=in_specs, out_specs=tile(D),
            scratch_shapes=[pltpu.VMEM((MOE_SLOTS, D), BF16), pltpu.VMEM((TM, D), F32)]),
        out_shape=jax.ShapeDtypeStruct((rows, D), F32),
        compiler_params=_params("parallel"),
        name="moe_combine",
    )(used, pool, *args)


def _moe_mixer(h, logits, x, mod_l, w1, w3, w2, e0, final_g=None):
    plan_col, plan_row, nch = _moe_plan(logits)
    tab, gexp, nused, used, pool = _moe_tables(nch)
    xs = _moe_dispatch(h, plan_row, used)
    y = _moe_experts(xs, tab, gexp, nused, w1, w3, w2, e0)
    return _moe_combine(y, pool, used, plan_col, x, mod_l, final_g)


def kernel(x, c, ctx, c_ctx, w_mod, b_mod, norm1_g, norm2_g, w_in, rpb, hy_conv_w, hy_conv_b, hy_w1, hy_b1, hy_w2, hy_b2, hy_w3, hy_freq, hy_bias, w_pa, w_pb, w_o, ffn_w1, ffn_w3, ffn_w2, moe_router, moe_w1, moe_w3, moe_w2, final_g):
    xs = jnp.concatenate([x.reshape(LAT_ROWS, D), ctx.reshape(CTX_ROWS, D)], axis=0)
    mod = _modulation(c, c_ctx, w_mod, b_mod)
    tabs = _fft_tables()
    ctabs = _ctx_tables()
    filt_lat = _implicit_filters(SEQ, hy_w1, hy_b1, hy_w2, hy_b2, hy_w3, hy_freq)
    filt_ctx = _implicit_filters(CTX, hy_w1, hy_b1, hy_w2, hy_b2, hy_w3, hy_freq)
    spec_re, spec_im = _filter_spectra(filt_lat, tabs)
    nblk = HY_W // LANE

    moe_w = [w.reshape((-1,) + w.shape[2:]).astype(BF16) for w in (moe_w1, moe_w3, moe_w2)]
    ffn_w = [w.astype(BF16) for w in (ffn_w1, ffn_w3, ffn_w2)]
    w_in_b = w_in.astype(BF16)
    bias_tabs = _bias_tables(rpb)

    for l in range(DEPTH):
        last = l == DEPTH - 1
        mod_l = mod[l].reshape(MOD_ROWS, 1, N_MOD * D)
        cw, cb = hy_conv_w[l], hy_conv_b[l].reshape(1, 3 * HY_W)
        p = _in_proj(xs, mod_l, norm1_g[l].reshape(1, D), w_in_b, l)

        o_att = _context_attention(p, _neighbourhood_attention(p, bias_tabs, l))
        z = _hyena_conv(p, COL_HV, p, COL_HX1, cw, cb, 0, nblk, spec_re, spec_im, l, 0,
                        hy_bias[l], tabs, True)
        o_hy = _hyena_conv(z, 0, p, COL_HX2, cw, cb, 0, 2 * nblk, spec_re, spec_im, l, 1,
                           hy_bias[l], tabs, False)
        o_hy = _ctx_hyena(p, cw, cb, filt_ctx, l, hy_bias[l], ctabs, o_hy)

        i = l // 2
        router = moe_router[i] if l % 2 else None
        res = _merge(o_att, o_hy, p, xs, mod_l, norm2_g[l].reshape(1, D), w_pa, w_pb, w_o, l, router)
        fg = final_g if last else None
        if l % 2 == 0:
            xs = _dense_mixer(res[1], res[0], mod_l, *ffn_w, i, final_g=fg)
        else:
            xs = _moe_mixer(res[1], res[2], res[0], mod_l, *moe_w, i * N_EXPERTS, final_g=fg)
    return xs.reshape(NB, SEQ, D)
```

```python
import functools
import math

import numpy as np
import jax
import jax.numpy as jnp
from jax import lax
from jax.experimental import pallas as pl
from jax.experimental.pallas import tpu as pltpu

F32 = jnp.float32
BF16 = jnp.bfloat16
HIGHEST = lax.Precision.HIGHEST

D = 1024
NB = 4
SEQ = 4096
DEPTH = 4
CTX = 256
GRID_W = 64
GRID_H = SEQ // GRID_W
N_HEADS = 8
HEAD_DIM = 64
ATT_W = N_HEADS * HEAD_DIM
WIN_ROWS = 8
WIN_COLS = 16
HY_W = 512
HY_ORDER = 2
HY_SHORT = 3
HY_EMB = 33
HY_BANDS = (HY_EMB - 1) // 2
HY_FILT = 64
HY_DECAY_TARGET = 1e-2
HY_MAX_DECAY = math.log(HY_DECAY_TARGET) / 0.3
HY_MIN_DECAY = math.log(HY_DECAY_TARGET) / 1.5
HY_DECAY_SHIFT = 0.05
PROJ_W = 3 * ATT_W + 3 * HY_W + 2 * D
D_FF = 2816
N_EXPERTS = 8
N_MOD = 6
EPS = 1e-6

LAT_ROWS = NB * SEQ
CTX_ROWS = NB * CTX
ROWS = LAT_ROWS + CTX_ROWS
MOD_ROWS = 8
CTX_MOD_ROW = NB

LANE = 128
TM = 1024
VMEM_LIMIT = 56 * 1024 * 1024

COL_Q, COL_K, COL_V = 0, ATT_W // LANE, 2 * ATT_W // LANE
COL_HV = 3 * ATT_W // LANE
COL_HX1 = COL_HV + HY_W // LANE
COL_HX2 = COL_HX1 + HY_W // LANE
COL_GA = (3 * ATT_W + 3 * HY_W) // D
COL_GB = COL_GA + 1


def _mod_row(i):
    return jnp.where(i < LAT_ROWS // TM, (i * TM) // SEQ, CTX_MOD_ROW)


def _params(*sem):
    return pltpu.CompilerParams(dimension_semantics=sem, vmem_limit_bytes=VMEM_LIMIT)


def _mod_kernel(s_ref, w_ref, b_ref, o_ref):
    s = s_ref[...]
    s = s * jax.nn.sigmoid(s)
    o_ref[0] = jnp.dot(s, w_ref[0], precision=HIGHEST, preferred_element_type=F32) + b_ref[0]


def _modulation(c, c_ctx, w_mod, b_mod):
    s = jnp.zeros((MOD_ROWS, D), F32).at[:NB].set(c).at[CTX_MOD_ROW].set(c_ctx)
    tn = 1536
    return pl.pallas_call(
        _mod_kernel,
        grid=(DEPTH, N_MOD * D // tn),
        in_specs=[pl.BlockSpec((MOD_ROWS, D), lambda l, j: (0, 0)),
                  pl.BlockSpec((1, D, tn), lambda l, j: (l, 0, j)),
                  pl.BlockSpec((1, 1, tn), lambda l, j: (l, 0, j))],
        out_specs=pl.BlockSpec((1, MOD_ROWS, tn), lambda l, j: (l, 0, j)),
        out_shape=jax.ShapeDtypeStruct((DEPTH, MOD_ROWS, N_MOD * D), F32),
        compiler_params=_params("parallel", "parallel"),
        name="modulation",
    )(s, w_mod, b_mod.reshape(DEPTH, 1, N_MOD * D))


def _norm_mod(x, g, shift, scale):
    y = x * lax.rsqrt(jnp.mean(x * x, axis=-1, keepdims=True) + EPS) * g
    return y * (1.0 + scale) + shift


def _inproj_kernel(x_ref, mod_ref, g_ref, w_ref, o_ref, h_scr):
    @pl.when(pl.program_id(1) == 0)
    def _():
        m = mod_ref[0]
        h = _norm_mod(x_ref[...], g_ref[...], m[:, 0:D], m[:, D:2 * D])
        h_scr[...] = h.astype(BF16)

    o_ref[...] = jnp.dot(h_scr[...], w_ref[0].astype(BF16),
                         preferred_element_type=F32).astype(BF16)


def _in_proj(x, mod_l, g, w, layer):
    tn = 1024
    return pl.pallas_call(
        _inproj_kernel,
        grid=(ROWS // TM, PROJ_W // tn),
        in_specs=[pl.BlockSpec((TM, D), lambda i, j: (i, 0)),
                  pl.BlockSpec((1, 1, N_MOD * D), lambda i, j: (_mod_row(i), 0, 0)),
                  pl.BlockSpec((1, D), lambda i, j: (0, 0)),
                  pl.BlockSpec((1, D, tn), lambda i, j: (layer, 0, j))],
        out_specs=pl.BlockSpec((TM, tn), lambda i, j: (i, j)),
        out_shape=jax.ShapeDtypeStruct((ROWS, PROJ_W), BF16),
        scratch_shapes=[pltpu.VMEM((TM, D), BF16)],
        compiler_params=_params("parallel", "arbitrary"),
        name="in_proj",
    )(x, mod_l, g, w)


NEG = -1e30


def _nt_dot(a, b):
    return lax.dot_general(a, b, (((1,), (1,)), ((), ())), preferred_element_type=F32)


def _attend_pair(qp, keys, values, biases):
    lane = lax.broadcasted_iota(jnp.int32, qp.shape, 1)
    qs = qp * jnp.asarray(HEAD_DIM ** -0.5, BF16)
    outs = []
    for a in range(2):
        sel = (lane < HEAD_DIM) if a == 0 else (lane >= HEAD_DIM)
        qa = jnp.where(sel, qs, jnp.zeros_like(qs))
        ss = []
        for k, b in zip(keys, biases):
            s = _nt_dot(qa, k)
            if b is not None:
                s = s + b[a]
            ss.append(s)
        m = functools.reduce(jnp.maximum, [jnp.max(s, axis=-1, keepdims=True) for s in ss])
        ps = [jnp.exp(s - m) for s in ss]
        den = functools.reduce(jnp.add, [jnp.sum(p, axis=-1, keepdims=True) for p in ps])
        o = functools.reduce(jnp.add, [jnp.dot(p.astype(BF16), v, preferred_element_type=F32)
                                       for p, v in zip(ps, values)])
        outs.append(o / den)
    return jnp.where(lane < HEAD_DIM, outs[0], outs[1])


NLOC = WIN_ROWS * GRID_W
NKEY = NLOC + CTX
NA_PAIRS = GRID_H // 2
NA_STEPS = NA_PAIRS + 2
NA_UNITS = 2 * N_HEADS


def _window_start(r):
    r0 = jnp.clip(r - WIN_ROWS // 2, 0, GRID_H - WIN_ROWS)
    return pl.multiple_of(r0 * GRID_W, GRID_W)


def _natt_kernel(q_ref, k_ref, v_ref, kc_ref, vc_ref, bt0_ref, bt1_ref, o_ref, s0, s1, p0, p1, l0, l1):
    j = pl.program_id(1)
    lane = lax.broadcasted_iota(jnp.int32, (GRID_W, LANE), 1)
    lo = lane < HEAD_DIM
    hi = lane >= HEAD_DIM

    @pl.when(j == 0)
    def _():
        s1[...] = jnp.zeros_like(s1)
        p0[...] = jnp.zeros_like(p0)
        p1[...] = jnp.zeros_like(p1)
        l0[...] = jnp.ones_like(l0)
        l1[...] = jnp.ones_like(l1)

    def step(s_cur, s_prev, p_cur, p_prev, l_cur, l_prev):
        ro = jnp.maximum(2 * j - 4, 0)
        rs = jnp.minimum(2 * j, GRID_H - 2)
        for half, bt_ref in enumerate((bt0_ref, bt1_ref)):
            start_o = _window_start(ro + half)
            start_s = _window_start(rs + half)
            rows = slice(half * GRID_W, (half + 1) * GRID_W)
            for hp in range(N_HEADS // 2):
                cs = slice(hp * LANE, (hp + 1) * LANE)
                u0 = half * N_HEADS + 2 * hp
                units = (u0, u0 + 1)
                pair = slice(u0, u0 + 2)
                m2 = 2 * GRID_W

                p2 = p_cur[pair].reshape(m2, NKEY)
                acc = (jnp.dot(p2[:, 0:NLOC], v_ref[pl.ds(start_o, NLOC), cs], preferred_element_type=F32)
                       + jnp.dot(p2[:, NLOC:NKEY], vc_ref[:, cs], preferred_element_type=F32))
                o_ref[rows, cs] = jnp.where(lo, acc[0:GRID_W] / l_cur[u0],
                                            acc[GRID_W:m2] / l_cur[u0 + 1]).astype(BF16)

                for u in units:
                    x = s_prev[u]
                    e = jnp.exp(x - jnp.max(x, axis=-1, keepdims=True))
                    l_prev[u] = jnp.broadcast_to(jnp.sum(e, axis=-1, keepdims=True), (GRID_W, LANE))
                    p_prev[u] = e.astype(BF16)

                qs = q_ref[rows, cs] * jnp.asarray(HEAD_DIM ** -0.5, BF16)
                zero = jnp.zeros_like(qs)
                q2 = jnp.concatenate([jnp.where(lo, qs, zero), jnp.where(hi, qs, zero)], axis=0)
                bias = bt_ref[0, 2 * hp:2 * hp + 2].astype(F32)
                s_cur[pair, :, 0:NLOC] = _nt_dot(q2, k_ref[pl.ds(start_s, NLOC), cs]).reshape(2, GRID_W, NLOC) + bias
                s_cur[pair, :, NLOC:NKEY] = _nt_dot(q2, kc_ref[:, cs]).reshape(2, GRID_W, CTX)

    @pl.when(j % 2 == 0)
    def _():
        step(s0, s1, p0, p1, l0, l1)

    @pl.when(j % 2 == 1)
    def _():
        step(s1, s0, p1, p0, l1, l0)


def _bias_tables(rpb):
    col = np.arange(GRID_W)
    c0 = np.clip(col - WIN_COLS // 2, 0, GRID_W - WIN_COLS)
    valid = (col[None, :] >= c0[:, None]) & (col[None, :] < c0[:, None] + WIN_COLS)
    dc = np.clip(col[None, :] - col[:, None] + (WIN_COLS - 1), 0, 2 * WIN_COLS - 2)
    pick = (np.arange(2 * WIN_COLS - 1)[:, None, None] == dc[None]).astype(np.float32)
    t = jnp.einsum('lhrd,dqk->lhrqk', rpb, jnp.asarray(pick), precision=HIGHEST)
    t = jnp.where(valid, t, NEG)
    win = jnp.stack([t[:, :, d0:d0 + WIN_ROWS] for d0 in range(WIN_ROWS)], axis=1)
    win = win.transpose(0, 1, 2, 4, 3, 5)
    return win.reshape(DEPTH * WIN_ROWS, N_HEADS, GRID_W, NLOC).astype(BF16)


def _bias_index(r):
    r0 = jnp.clip(r - WIN_ROWS // 2, 0, GRID_H - WIN_ROWS)
    return r0 - r + (WIN_ROWS - 1)


def _neighbourhood_attention(p, bt, layer):
    blk = GRID_H // 2
    ctx_blk0 = LAT_ROWS // CTX
    score_row = lambda j, half: jnp.minimum(2 * j, GRID_H - 2) + half
    bias_spec = lambda half: pl.BlockSpec(
        (1, N_HEADS, GRID_W, NLOC), lambda b, j: (layer * WIN_ROWS + _bias_index(score_row(j, half)), 0, 0, 0))
    return pl.pallas_call(
        _natt_kernel,
        grid=(NB, NA_STEPS),
        in_specs=[pl.BlockSpec((2 * GRID_W, ATT_W), lambda b, j: (b * blk + jnp.minimum(j, blk - 1), 0)),
                  pl.BlockSpec((SEQ, ATT_W), lambda b, j: (b, 1)),
                  pl.BlockSpec((SEQ, ATT_W), lambda b, j: (b, 2)),
                  pl.BlockSpec((CTX, ATT_W), lambda b, j: (ctx_blk0 + b, 1)),
                  pl.BlockSpec((CTX, ATT_W), lambda b, j: (ctx_blk0 + b, 2)),
                  bias_spec(0), bias_spec(1)],
        out_specs=pl.BlockSpec((2 * GRID_W, ATT_W), lambda b, j: (b * blk + jnp.maximum(j - 2, 0), 0)),
        out_shape=jax.ShapeDtypeStruct((ROWS, ATT_W), BF16),
        scratch_shapes=[pltpu.VMEM((NA_UNITS, GRID_W, NKEY), F32)] * 2
                       + [pltpu.VMEM((NA_UNITS, GRID_W, NKEY), BF16)] * 2
                       + [pltpu.VMEM((NA_UNITS, GRID_W, LANE), F32)] * 2,
        compiler_params=_params("parallel", "arbitrary"),
        name="neighbourhood_attention",
    )(p, p, p, p, p, bt, bt)


def _catt_kernel(q_ref, k_ref, v_ref, dst_ref, o_ref):
    del dst_ref
    for hp in range(N_HEADS // 2):
        cs = slice(hp * LANE, (hp + 1) * LANE)
        o = _attend_pair(q_ref[:, cs], [k_ref[:, cs]], [v_ref[:, cs]], [None])
        o_ref[:, cs] = o.astype(BF16)


def _context_attention(p, o_att):
    ctx_blk0 = LAT_ROWS // CTX
    return pl.pallas_call(
        _catt_kernel,
        grid=(NB,),
        in_specs=[pl.BlockSpec((CTX, ATT_W), lambda b: (ctx_blk0 + b, 0)),
                  pl.BlockSpec((CTX, ATT_W), lambda b: (ctx_blk0 + b, 1)),
                  pl.BlockSpec((CTX, ATT_W), lambda b: (ctx_blk0 + b, 2)),
                  pl.BlockSpec(memory_space=pl.ANY)],
        out_specs=pl.BlockSpec((CTX, ATT_W), lambda b: (ctx_blk0 + b, 0)),
        out_shape=jax.ShapeDtypeStruct((ROWS, ATT_W), BF16),
        input_output_aliases={3: 0},
        compiler_params=_params("parallel"),
        name="context_attention",
    )(p, p, p, o_att)


def _filter_features(L):
    pos = np.arange(L, dtype=np.float64)
    t = pos / max(L - 1, 1)
    bands = np.linspace(1e-4, HY_BANDS - 1, HY_BANDS)
    ang = (2.0 * math.pi / L) * pos[:, None] * bands[None, :]
    z = np.concatenate([t[:, None], np.cos(ang), -np.sin(ang)], axis=-1)
    zp = np.zeros((L, LANE), np.float32)
    zp[:, :HY_EMB] = z
    return zp


def _filter_kernel(z_ref, w1_ref, b1_ref, w2_ref, b2_ref, w3_ref, fr_ref, rate_ref, o_ref, *, L, tl):
    dot = functools.partial(jnp.dot, precision=HIGHEST, preferred_element_type=F32)
    fr = fr_ref[0]
    a = jnp.sin(fr[0:1] * (dot(z_ref[...], w1_ref[0]) + b1_ref[0]))
    a = jnp.sin(fr[1:2] * (dot(a, w2_ref[0]) + b2_ref[0]))
    h = dot(a, w3_ref[0])
    pos = (pl.program_id(1) * tl + lax.broadcasted_iota(jnp.int32, (tl, 1), 0)).astype(F32)
    dist = jnp.abs(pos - float(L // 2)) * (2.0 / L)
    o_ref[0] = h * (jnp.exp(-dist * rate_ref[...]) + HY_DECAY_SHIFT)


def _implicit_filters(L, w1, b1, w2, b2, w3, freq):
    tl = min(L, 512)
    z = jnp.asarray(_filter_features(L))
    w1p = jnp.zeros((DEPTH, LANE, HY_FILT), F32).at[:, :HY_EMB].set(w1)
    rates = np.abs(np.linspace(HY_MIN_DECAY, HY_MAX_DECAY, HY_ORDER * HY_W)).astype(np.float32)
    cw = HY_ORDER * HY_W
    return pl.pallas_call(
        functools.partial(_filter_kernel, L=L, tl=tl),
        grid=(DEPTH, L // tl),
        in_specs=[pl.BlockSpec((tl, LANE), lambda l, i: (i, 0)),
                  pl.BlockSpec((1, LANE, HY_FILT), lambda l, i: (l, 0, 0)),
                  pl.BlockSpec((1, 1, HY_FILT), lambda l, i: (l, 0, 0)),
                  pl.BlockSpec((1, HY_FILT, HY_FILT), lambda l, i: (l, 0, 0)),
                  pl.BlockSpec((1, 1, HY_FILT), lambda l, i: (l, 0, 0)),
                  pl.BlockSpec((1, HY_FILT, cw), lambda l, i: (l, 0, 0)),
                  pl.BlockSpec((1, 2, HY_FILT), lambda l, i: (l, 0, 0)),
                  pl.BlockSpec((1, cw), lambda l, i: (0, 0))],
        out_specs=pl.BlockSpec((1, tl, cw), lambda l, i: (l, i, 0)),
        out_shape=jax.ShapeDtypeStruct((DEPTH, L, cw), F32),
        compiler_params=_params("parallel", "parallel"),
        name=f"implicit_filters_{L}",
    )(z, w1p, b1.reshape(DEPTH, 1, HY_FILT), w2, b2.reshape(DEPTH, 1, HY_FILT), w3, freq,
      jnp.asarray(rates).reshape(1, cw))


FFT_N = 2 * SEQ
FA = 64
FB = 128
FA_IN = SEQ // FB
PITCH = 136
HALO = 8
UNROLL_ROWS = 32
UNROLL_SLABS = 16


def _fft_tables():
    a = np.arange(FA_IN)
    r = np.arange(FA)
    b = np.arange(FB)
    q = np.arange(FB)
    ph = a[None, None, :] * r[None, :, None] / FA + b[:, None, None] * r[None, :, None] / FFT_N
    g1 = np.exp(-2j * np.pi * ph)
    g1 = np.concatenate([g1.real, g1.imag], axis=1)
    f = np.exp(-2j * np.pi * q[:, None] * b[None, :] / FB)
    f2 = np.concatenate([f.real, f.imag], axis=0)
    f2c = np.concatenate([f.real, -f.imag], axis=0)
    ao = np.arange(FA_IN) + (SEQ // 2) // FB
    ph2 = ao[None, :, None] * r[None, None, :] / FA + b[:, None, None] * r[None, None, :] / FFT_N
    g2 = np.exp(2j * np.pi * ph2) / FFT_N
    g2 = np.concatenate([g2.real, g2.imag], axis=1)
    return tuple(jnp.asarray(t, dtype=BF16) for t in (g1, f2, f2c, g2))


def _cplx(out, m, n):
    re = out[0:m, 0:n] - out[m:2 * m, n:2 * n]
    im = out[0:m, n:2 * n] + out[m:2 * m, 0:n]
    return re, im


def _short_conv_slabs(src_ref, rows, t_scr, cw_ref, cb_ref, apply_conv, emit):
    nslab = rows // FB
    if not apply_conv:
        for a in range(nslab):
            emit(a, src_ref[a * FB:(a + 1) * FB, :].astype(F32))
        return
    zero = jnp.zeros((HALO, LANE), F32)
    t_scr[0:HALO, :] = zero
    t_scr[HALO + rows:2 * HALO + rows, :] = zero
    for a in range(nslab):
        t_scr[HALO + a * FB:HALO + (a + 1) * FB, :] = src_ref[a * FB:(a + 1) * FB, :].astype(F32)
    w0, w1, w2, bias = cw_ref[0:1, :], cw_ref[1:2, :], cw_ref[2:3, :], cb_ref[...]
    for a in range(nslab):
        o = HALO + a * FB
        y = (w0 * t_scr[o - 1:o - 1 + FB, :] + w1 * t_scr[o:o + FB, :]
             + w2 * t_scr[o + 1:o + 1 + FB, :] + bias)
        emit(a, y)


def _fwd_stage1(b, s_refs, g1_ref, y_r, y_i):
    cols = [s[pl.ds(b, FA_IN, stride=PITCH), :].astype(BF16) for s in s_refs]
    rhs = cols[0] if len(cols) == 1 else jnp.concatenate(cols, axis=1)
    out = jnp.dot(g1_ref[b], rhs, preferred_element_type=F32)
    if len(cols) == 1:
        re, im = out[0:FA], out[FA:2 * FA]
    else:
        re, im = _cplx(out, FA, LANE)
    y_r[pl.ds(b, FA, stride=PITCH), :] = re
    y_i[pl.ds(b, FA, stride=PITCH), :] = im


def _fwd_stage2(r, y_r, y_i, f_ref):
    o = pl.multiple_of(r * PITCH, 8)
    rhs = jnp.concatenate([y_r[pl.ds(o, FB), :].astype(BF16), y_i[pl.ds(o, FB), :].astype(BF16)], axis=1)
    out = jnp.dot(f_ref[...], rhs, preferred_element_type=F32)
    return _cplx(out, FB, LANE)


def _spectrum_kernel(h_ref, g1_ref, f_ref, hr_ref, hi_ref, s_scr, y_r, y_i):
    for a in range(FA_IN):
        s_scr[a * PITCH:a * PITCH + FB, :] = h_ref[0, a * FB:(a + 1) * FB, :]

    def s1(b, c):
        _fwd_stage1(b, [s_scr], g1_ref, y_r, y_i)
        return c
    lax.fori_loop(0, FB, s1, 0, unroll=UNROLL_ROWS)

    def s2(r, c):
        xr, xi = _fwd_stage2(r, y_r, y_i, f_ref)
        o = pl.multiple_of(r * FB, FB)
        hr_ref[0, pl.ds(o, FB), :] = xr
        hi_ref[0, pl.ds(o, FB), :] = xi
        return c
    lax.fori_loop(0, FA, s2, 0, unroll=UNROLL_SLABS)


def _filter_spectra(h, tabs):
    g1, f2, _, _ = tabs
    cw = HY_ORDER * HY_W
    spec = pl.BlockSpec((1, FFT_N, LANE), lambda l, c: (l, 0, c))
    return pl.pallas_call(
        _spectrum_kernel,
        grid=(DEPTH, cw // LANE),
        in_specs=[pl.BlockSpec((1, SEQ, LANE), lambda l, c: (l, 0, c)),
                  pl.BlockSpec(g1.shape, lambda l, c: (0, 0, 0)),
                  pl.BlockSpec(f2.shape, lambda l, c: (0, 0))],
        out_specs=[spec, spec],
        out_shape=[jax.ShapeDtypeStruct((DEPTH, FFT_N, cw), F32)] * 2,
        scratch_shapes=[pltpu.VMEM((FA_IN * PITCH, LANE), F32),
                        pltpu.VMEM((FA * PITCH, LANE), F32),
                        pltpu.VMEM((FA * PITCH, LANE), F32)],
        compiler_params=_params("parallel", "parallel"),
        name="filter_spectra",
    )(h, g1, f2)


def _hyconv_kernel(u_ref, x_ref, cwu_ref, cbu_ref, cwx_ref, cbx_ref,
                   hr_ref, hi_ref, bias_ref, g1_ref, f_ref, fc_ref, g2_ref,
                   o_ref, t_scr, s_r, s_i, y_r, y_i, *, conv_input):
    halves = [(s_r, slice(0, SEQ)), (s_i, slice(SEQ, 2 * SEQ))]

    def fill(s):
        def emit(a, slab):
            s[a * PITCH:a * PITCH + FB, :] = slab
        return emit
    for s, rows in halves:
        _short_conv_slabs(u_ref.at[rows, :], SEQ, t_scr, cwu_ref, cbu_ref, conv_input, fill(s))

    def s1(b, c):
        _fwd_stage1(b, [s_r, s_i], g1_ref, y_r, y_i)
        return c
    lax.fori_loop(0, FB, s1, 0, unroll=UNROLL_ROWS)

    def s2(r, c):
        xr, xi = _fwd_stage2(r, y_r, y_i, f_ref)
        oh = pl.multiple_of(r * FB, FB)
        hr = hr_ref[0, pl.ds(oh, FB), :]
        hi = hi_ref[0, pl.ds(oh, FB), :]
        vr = xr * hr - xi * hi
        vi = xr * hi + xi * hr
        rhs = jnp.concatenate([vr.astype(BF16), vi.astype(BF16)], axis=1)
        wr, wi = _cplx(jnp.dot(fc_ref[...], rhs, preferred_element_type=F32), FB, LANE)
        o = pl.multiple_of(r * PITCH, 8)
        y_r[pl.ds(o, FB), :] = wr
        y_i[pl.ds(o, FB), :] = wi
        return c
    lax.fori_loop(0, FA, s2, 0, unroll=UNROLL_SLABS)

    bias = bias_ref[0]

    def s3(b, c):
        rhs = jnp.concatenate([y_r[pl.ds(b, FA, stride=PITCH), :].astype(BF16),
                               y_i[pl.ds(b, FA, stride=PITCH), :].astype(BF16)], axis=1)
        re, im = _cplx(jnp.dot(g2_ref[b], rhs, preferred_element_type=F32), FA_IN, LANE)
        idx = pl.ds(b, FA_IN, stride=PITCH)
        s_r[idx, :] = re + s_r[idx, :] * bias
        s_i[idx, :] = im + s_i[idx, :] * bias
        return c
    lax.fori_loop(0, FB, s3, 0, unroll=UNROLL_ROWS)

    for s, rows in halves:
        def emit(a, slab, s=s, out=o_ref.at[rows, :]):
            out[a * FB:(a + 1) * FB, :] = (slab * s[a * PITCH:a * PITCH + FB, :]).astype(out.dtype)
        _short_conv_slabs(x_ref.at[rows, :], SEQ, t_scr, cwx_ref, cbx_ref, True, emit)


def _hyena_conv(u, u_col, x, x_col, conv_w, conv_b, cu_col, cx_col, spec_re, spec_im, layer, order,
                hy_bias_l, tabs, conv_input):
    g1, f2, f2c, g2 = tabs
    nblk = HY_W // LANE
    tok = lambda col: pl.BlockSpec((2 * SEQ, LANE), lambda c, p: (p, col + c))
    chan = lambda rows, col: pl.BlockSpec((rows, LANE), lambda c, p: (0, col + c))
    hspec = pl.BlockSpec((1, FFT_N, LANE), lambda c, p: (layer, 0, order * nblk + c))
    const = lambda shape: pl.BlockSpec(shape, lambda c, p: (0,) * len(shape))
    return pl.pallas_call(
        functools.partial(_hyconv_kernel, conv_input=conv_input),
        grid=(nblk, NB // 2),
        in_specs=[tok(u_col), tok(x_col),
                  chan(HY_SHORT, cu_col), chan(1, cu_col), chan(HY_SHORT, cx_col), chan(1, cx_col),
                  hspec, hspec,
                  pl.BlockSpec((1, 1, LANE), lambda c, p: (order, 0, c)),
                  const(g1.shape), const(f2.shape), const(f2c.shape), const(g2.shape)],
        out_specs=pl.BlockSpec((2 * SEQ, LANE), lambda c, p: (p, c)),
        out_shape=jax.ShapeDtypeStruct((ROWS, HY_W), BF16),
        scratch_shapes=[pltpu.VMEM((SEQ + 2 * HALO, LANE), F32),
                        pltpu.VMEM((FA_IN * PITCH, LANE), F32),
                        pltpu.VMEM((FA_IN * PITCH, LANE), F32),
                        pltpu.VMEM((FA * PITCH, LANE), F32),
                        pltpu.VMEM((FA * PITCH, LANE), F32)],
        compiler_params=_params("arbitrary", "arbitrary"),
        name=f"hyena_conv{order}",
    )(u, x, conv_w, conv_b, conv_w, conv_b, spec_re, spec_im,
      hy_bias_l.reshape(HY_ORDER, 1, HY_W), g1, f2, f2c, g2)


CN = 2 * CTX


def _ctx_tables():
    k = np.arange(CN)
    n = np.arange(CTX)
    f = np.exp(-2j * np.pi * k[:, None] * n[None, :] / CN)
    fwd = np.concatenate([f.real, f.imag], axis=0)
    t = np.arange(CTX) + CTX // 2
    g = np.exp(2j * np.pi * t[:, None] * k[None, :] / CN) / CN
    inv = np.concatenate([g.real, g.imag], axis=0)
    return jnp.asarray(fwd, dtype=BF16), jnp.asarray(inv, dtype=BF16)


def _ctx_hyena_kernel(v_ref, x1_ref, x2_ref, cwv_ref, cbv_ref, cw1_ref, cb1_ref, cw2_ref, cb2_ref,
                      h0_ref, h1_ref, bias_ref, fwd_ref, inv_ref, dst_ref, o_ref, t_scr, s_scr, g_scr):
    del dst_ref
    def conv_into(src_ref, b, cw_ref, cb_ref, dst, col):
        view = src_ref.at[b * CTX:(b + 1) * CTX, :]

        def emit(a, slab):
            dst[a * FB:(a + 1) * FB, col * LANE:(col + 1) * LANE] = slab
        _short_conv_slabs(view, CTX, t_scr, cw_ref, cb_ref, True, emit)

    def spectrum(h_ref):
        out = jnp.dot(fwd_ref[...], h_ref[0].astype(BF16), preferred_element_type=F32)
        return out[0:CN], out[CN:2 * CN]

    def long_conv(z2, hr, hi, bias):
        x = jnp.dot(fwd_ref[...], z2.astype(BF16), preferred_element_type=F32)
        xr, xi = _cplx(x, CN, LANE)
        vr = xr * hr - xi * hi
        vi = xr * hi + xi * hr
        rhs = jnp.concatenate([vr.astype(BF16), vi.astype(BF16)], axis=1)
        y = jnp.dot(inv_ref[...], rhs, preferred_element_type=F32)
        yr, yi = _cplx(y, CTX, LANE)
        return jnp.concatenate([yr, yi], axis=1) + z2 * jnp.concatenate([bias, bias], axis=1)

    h0r, h0i = spectrum(h0_ref)
    h1r, h1i = spectrum(h1_ref)
    for p in range(NB // 2):
        for half in range(2):
            conv_into(v_ref, 2 * p + half, cwv_ref, cbv_ref, s_scr, half)
            conv_into(x1_ref, 2 * p + half, cw1_ref, cb1_ref, g_scr, half)
        z = g_scr[...] * long_conv(s_scr[...], h0r, h0i, bias_ref[0])
        for half in range(2):
            conv_into(x2_ref, 2 * p + half, cw2_ref, cb2_ref, g_scr, half)
        y = g_scr[...] * long_conv(z, h1r, h1i, bias_ref[1])
        for half in range(2):
            b = 2 * p + half
            o_ref[b * CTX:(b + 1) * CTX, :] = y[:, half * LANE:(half + 1) * LANE].astype(BF16)


def _ctx_hyena(p, conv_w, conv_b, filt_ctx, layer, hy_bias_l, ctabs, o_hy):
    fwd, inv = ctabs
    nblk = HY_W // LANE
    ctx_blk = LAT_ROWS // CTX_ROWS
    tok = lambda col: pl.BlockSpec((CTX_ROWS, LANE), lambda c: (ctx_blk, col + c))
    chan = lambda rows, col: pl.BlockSpec((rows, LANE), lambda c: (0, col + c))
    hspec = lambda order: pl.BlockSpec((1, CTX, LANE), lambda c: (layer, 0, order * nblk + c))
    const = lambda shape: pl.BlockSpec(shape, lambda c: (0,) * len(shape))
    return pl.pallas_call(
        _ctx_hyena_kernel,
        grid=(nblk,),
        in_specs=[tok(COL_HV), tok(COL_HX1), tok(COL_HX2),
                  chan(HY_SHORT, 0), chan(1, 0), chan(HY_SHORT, nblk), chan(1, nblk),
                  chan(HY_SHORT, 2 * nblk), chan(1, 2 * nblk),
                  hspec(0), hspec(1),
                  pl.BlockSpec((HY_ORDER, 1, LANE), lambda c: (0, 0, c)),
                  const(fwd.shape), const(inv.shape), pl.BlockSpec(memory_space=pl.ANY)],
        out_specs=pl.BlockSpec((CTX_ROWS, LANE), lambda c: (ctx_blk, c)),
        out_shape=jax.ShapeDtypeStruct((ROWS, HY_W), BF16),
        input_output_aliases={14: 0},
        scratch_shapes=[pltpu.VMEM((CTX + 2 * HALO, LANE), F32),
                        pltpu.VMEM((CTX, 2 * LANE), F32),
                        pltpu.VMEM((CTX, 2 * LANE), F32)],
        compiler_params=_params("parallel"),
        name="context_hyena",
    )(p, p, p, conv_w, conv_b, conv_w, conv_b, conv_w, conv_b, filt_ctx, filt_ctx,
      hy_bias_l.reshape(HY_ORDER, 1, HY_W), fwd, inv, o_hy)


def _merge_kernel(oa_ref, oh_ref, ga_ref, gb_ref, x_ref, mod_ref, g2_ref, wpa_ref, wpb_ref, wo_ref,
                  *rest, routed):
    if routed:
        wr_ref, xo_ref, h_ref, rw_ref, wpa_s, wpb_s, wo_s = rest
    else:
        xo_ref, h_ref, wpa_s, wpb_s, wo_s = rest

    @pl.when(pl.program_id(0) == 0)
    def _():
        wpa_s[...] = wpa_ref[0].astype(BF16)
        wpb_s[...] = wpb_ref[0].astype(BF16)
        wo_s[...] = wo_ref[0].astype(BF16)

    a = jnp.dot(oa_ref[...], wpa_s[...], preferred_element_type=F32)
    b = jnp.dot(oh_ref[...], wpb_s[...], preferred_element_type=F32)
    mix = (jax.nn.sigmoid(ga_ref[...].astype(F32)) * a + jax.nn.sigmoid(gb_ref[...].astype(F32)) * b)
    y = jnp.dot(mix.astype(BF16), wo_s[...], preferred_element_type=F32)
    m = mod_ref[0]
    x = x_ref[...] + m[:, 2 * D:3 * D] * y
    xo_ref[...] = x
    h = _norm_mod(x, g2_ref[...], m[:, 3 * D:4 * D], m[:, 4 * D:5 * D])
    h_ref[...] = h.astype(BF16)
    if routed:
        wr = wr_ref[...]
        hh, wh = h.astype(BF16), wr.astype(BF16)
        hl, wl = (h - hh.astype(F32)).astype(BF16), (wr - wh.astype(F32)).astype(BF16)
        dot = functools.partial(jnp.dot, preferred_element_type=F32)
        rw_ref[...] = dot(hh, wh) + (dot(hh, wl) + dot(hl, wh))


def _merge(o_att, o_hy, p, x, mod_l, g2, w_pa, w_pb, w_o, layer, router):
    routed = router is not None
    tile = lambda w: pl.BlockSpec((TM, w), lambda i: (i, 0))
    full = lambda a: pl.BlockSpec(a.shape, lambda i: (0,) * a.ndim)
    layered = lambda a: pl.BlockSpec((1,) + a.shape[1:], lambda i: (layer, 0, 0))
    in_specs = [tile(ATT_W), tile(HY_W),
                pl.BlockSpec((TM, D), lambda i: (i, COL_GA)),
                pl.BlockSpec((TM, D), lambda i: (i, COL_GB)),
                tile(D),
                pl.BlockSpec((1, 1, N_MOD * D), lambda i: (_mod_row(i), 0, 0)),
                pl.BlockSpec((1, D), lambda i: (0, 0)),
                layered(w_pa), layered(w_pb), layered(w_o)]
    args = [o_att, o_hy, p, p, x, mod_l, g2, w_pa, w_pb, w_o]
    out_specs = [tile(D), tile(D)]
    out_shape = [jax.ShapeDtypeStruct((ROWS, D), F32), jax.ShapeDtypeStruct((ROWS, D), BF16)]
    if routed:
        wr = jnp.zeros((D, LANE), F32).at[:, :N_EXPERTS].set(router)
        in_specs.append(full(wr))
        args.append(wr)
        out_specs.append(tile(LANE))
        out_shape.append(jax.ShapeDtypeStruct((ROWS, LANE), F32))
    return pl.pallas_call(
        functools.partial(_merge_kernel, routed=routed),
        grid=(ROWS // TM,),
        in_specs=in_specs, out_specs=out_specs, out_shape=out_shape,
        scratch_shapes=[pltpu.VMEM(w_pa.shape[1:], BF16), pltpu.VMEM(w_pb.shape[1:], BF16),
                        pltpu.VMEM(w_o.shape[1:], BF16)],
        compiler_params=_params("arbitrary"),
        name="merge",
    )(*args)


TF = D_FF // 2


NF = D_FF // TF


SWIGLU_SUB = 768


def _swiglu_step(h, w1_ref, w3_ref, w2_ref, acc):
    tf = w1_ref.shape[2]
    for lo in range(0, tf, SWIGLU_SUB):
        hi = min(lo + SWIGLU_SUB, tf)
        a = jnp.dot(h, w1_ref[0, :, lo:hi].astype(BF16), preferred_element_type=F32)
        b = jnp.dot(h, w3_ref[0, :, lo:hi].astype(BF16), preferred_element_type=F32)
        t = a * jax.nn.sigmoid(a) * b
        acc[...] += jnp.dot(t.astype(BF16), w2_ref[0, lo:hi, :].astype(BF16), preferred_element_type=F32)


def _residual_out(x, gate, y, fg_ref):
    x = x + gate * y
    if fg_ref is not None:
        x = x * lax.rsqrt(jnp.mean(x * x, axis=-1, keepdims=True) + EPS) * fg_ref[...]
    return x


def _ffn_kernel(h_ref, x_ref, mod_ref, w1_ref, w3_ref, w2_ref, *rest, final):
    rest = list(rest)
    fg_ref = rest.pop(0) if final else None
    o_ref, acc = rest
    f = pl.program_id(1)

    @pl.when(f == 0)
    def _():
        acc[...] = jnp.zeros_like(acc)

    _swiglu_step(h_ref[...], w1_ref, w3_ref, w2_ref, acc)

    @pl.when(f == NF - 1)
    def _():
        o_ref[...] = _residual_out(x_ref[...], mod_ref[0][:, 5 * D:6 * D], acc[...], fg_ref)


def _dense_mixer(h, x, mod_l, w1, w3, w2, e, final_g=None):
    final = final_g is not None
    rows = LAT_ROWS if final else ROWS
    tile = lambda w: pl.BlockSpec((TM, w), lambda i, f: (i, 0))
    in_specs = [tile(D), tile(D),
                pl.BlockSpec((1, 1, N_MOD * D), lambda i, f: (_mod_row(i), 0, 0)),
                pl.BlockSpec((1, D, TF), lambda i, f: (e, 0, f)),
                pl.BlockSpec((1, D, TF), lambda i, f: (e, 0, f)),
                pl.BlockSpec((1, TF, D), lambda i, f: (e, f, 0))]
    args = [h, x, mod_l, w1, w3, w2]
    if final:
        in_specs.append(pl.BlockSpec((1, D), lambda i, f: (0, 0)))
        args.append(final_g.reshape(1, D))
    return pl.pallas_call(
        functools.partial(_ffn_kernel, final=final),
        grid=(rows // TM, NF),
        in_specs=in_specs,
        out_specs=tile(D),
        out_shape=jax.ShapeDtypeStruct((rows, D), F32),
        scratch_shapes=[pltpu.VMEM((TM, D), F32)],
        compiler_params=_params("parallel", "arbitrary"),
        name="ffn",
    )(*args)


NT = ROWS // TM
MOE_CH = 64
MOE_CMAX = TM // MOE_CH
MOE_NCH = 2 * TM // MOE_CH + N_EXPERTS
MOE_SLOTS = MOE_NCH * MOE_CH
MOE_GC = 16
MOE_GROUPS = NT * MOE_NCH // MOE_GC + N_EXPERTS
MOE_SB = 512
MOE_TF, MOE_NF = TF, NF
PLAN_ROWS = 8


def _plan_kernel(lg_ref, col_ref, row_ref, nch_ref):
    lane = lax.broadcasted_iota(jnp.int32, (TM, LANE), 1)
    ninf = jnp.asarray(-jnp.inf, F32)
    lg = jnp.where(lane < N_EXPERTS, lg_ref[...], ninf)
    m1 = jnp.max(lg, axis=-1, keepdims=True)
    i1 = jnp.min(jnp.where(lg == m1, lane, LANE), axis=-1, keepdims=True)
    lg2 = jnp.where(lane == i1, ninf, lg)
    m2 = jnp.max(lg2, axis=-1, keepdims=True)
    i2 = jnp.min(jnp.where(lg2 == m2, lane, LANE), axis=-1, keepdims=True)
    e = jnp.exp(m2 - m1)
    den = 1.0 + e
    sel1 = lane == i1
    sel2 = lane == i2
    onehot = jnp.where(sel1, 1.0, jnp.where(sel2, 1.0, 0.0))
    tri = jnp.where(lax.broadcasted_iota(jnp.int32, (TM, TM), 1) < lax.broadcasted_iota(jnp.int32, (TM, TM), 0),
                    1.0, 0.0).astype(BF16)
    rank = jnp.dot(tri, onehot.astype(BF16), preferred_element_type=F32)
    cnt = jnp.sum(onehot, axis=0, keepdims=True)
    nch = jnp.floor((cnt + (MOE_CH - 1)) * (1.0 / MOE_CH))
    upper = jnp.where(lax.broadcasted_iota(jnp.int32, (LANE, LANE), 0) < lax.broadcasted_iota(jnp.int32, (LANE, LANE), 1),
                      1.0, 0.0).astype(BF16)
    cbase = jnp.dot(jnp.broadcast_to(nch, (8, LANE)).astype(BF16), upper,
                    preferred_element_type=F32)[0:1]
    slot = MOE_CH * cbase + rank
    slot0 = jnp.sum(jnp.where(sel1, slot, 0.0), axis=-1, keepdims=True)
    slot1 = jnp.sum(jnp.where(sel2, slot, 0.0), axis=-1, keepdims=True)
    col = jnp.where(lane == 0, slot0, jnp.where(lane == 1, slot1,
                    jnp.where(lane == 2, 1.0 / den, jnp.where(lane == 3, e / den, 0.0))))
    col_ref[...] = col
    row_ref[...] = jnp.transpose(col)[0:PLAN_ROWS]
    nch_ref[0] = nch


def _moe_plan(logits):
    return pl.pallas_call(
        _plan_kernel,
        grid=(NT,),
        in_specs=[pl.BlockSpec((TM, LANE), lambda i: (i, 0))],
        out_specs=[pl.BlockSpec((TM, LANE), lambda i: (i, 0)),
                   pl.BlockSpec((PLAN_ROWS, TM), lambda i: (0, i)),
                   pl.BlockSpec((1, 1, LANE), lambda i: (i, 0, 0))],
        out_shape=[jax.ShapeDtypeStruct((ROWS, LANE), F32),
                   jax.ShapeDtypeStruct((PLAN_ROWS, ROWS), F32),
                   jax.ShapeDtypeStruct((NT, 1, LANE), F32)],
        compiler_params=_params("parallel"),
        name="moe_plan",
    )(logits)


def _moe_tables(nch):
    n = nch[:, 0, :N_EXPERTS].astype(jnp.int32)
    cbase = jnp.cumsum(n, axis=1) - n
    used = jnp.sum(n, axis=1)
    c = jnp.arange(MOE_CMAX)
    valid = (c[None, None, :] < n.T[:, :, None]).reshape(N_EXPERTS, -1)
    cid = (jnp.arange(NT)[None, :, None] * MOE_NCH + cbase.T[:, :, None] + c[None, None, :]).reshape(N_EXPERTS, -1)
    ne = jnp.sum(n, axis=0)
    ge = (ne + MOE_GC - 1) // MOE_GC
    gend = jnp.cumsum(ge)
    nused = gend[-1]
    size = MOE_GROUPS * MOE_GC
    pos = (gend - ge)[:, None] * MOE_GC + jnp.cumsum(valid, axis=1) - 1
    dest = jnp.where(valid, pos, size).reshape(-1)
    tab = jnp.full((size,), -1, jnp.int32).at[dest].set(cid.reshape(-1).astype(jnp.int32), mode='drop')
    at = jnp.arange(size)
    tab = tab[jnp.maximum(lax.cummax(jnp.where(tab >= 0, at, -1)), 0)]
    gexp = jnp.minimum(jnp.sum(jnp.arange(MOE_GROUPS)[:, None] >= gend[None, :], axis=1), N_EXPERTS - 1)
    pool = jnp.zeros((NT * MOE_NCH,), jnp.int32).at[jnp.where(at < nused * MOE_GC, tab, NT * MOE_NCH)].set(
        at.astype(jnp.int32), mode='drop')
    return tab, gexp.astype(jnp.int32), nused.reshape(1).astype(jnp.int32), used.astype(jnp.int32), pool


def _dispatch_kernel(used_ref, h_ref, row_ref, xs_ref):
    used_rows = used_ref[pl.program_id(0)] * MOE_CH
    s0 = row_ref[0:1, :]
    s1 = row_ref[1:2, :]
    for blk in range(MOE_SLOTS // MOE_SB):
        rows = slice(blk * MOE_SB, (blk + 1) * MOE_SB)

        @pl.when(blk * MOE_SB < used_rows)
        def _():
            r = (lax.broadcasted_iota(jnp.int32, (MOE_SB, TM), 0) + blk * MOE_SB).astype(F32)
            s = jnp.where(r == s0, 1.0, jnp.where(r == s1, 1.0, 0.0)).astype(BF16)
            xs_ref[rows, :] = jnp.dot(s, h_ref[...], preferred_element_type=F32).astype(BF16)

        @pl.when(blk * MOE_SB >= used_rows)
        def _():
            xs_ref[rows, :] = jnp.zeros((MOE_SB, D), BF16)


def _moe_dispatch(h, plan_row, used):
    return pl.pallas_call(
        _dispatch_kernel,
        grid_spec=pltpu.PrefetchScalarGridSpec(
            num_scalar_prefetch=1, grid=(NT,),
            in_specs=[pl.BlockSpec((TM, D), lambda i, u: (i, 0)),
                      pl.BlockSpec((PLAN_ROWS, TM), lambda i, u: (0, i))],
            out_specs=pl.BlockSpec((MOE_SLOTS, D), lambda i, u: (i, 0))),
        out_shape=jax.ShapeDtypeStruct((NT * MOE_SLOTS, D), BF16),
        compiler_params=_params("parallel"),
        name="moe_dispatch",
    )(used, h, plan_row)


def _experts_kernel(tab_ref, gexp_ref, nused_ref, *refs):
    x_refs = refs[:MOE_GC]
    w1_ref, w3_ref, w2_ref, y_ref, x_scr, acc = refs[MOE_GC:]
    f = pl.program_id(1)

    @pl.when(pl.program_id(0) < nused_ref[0])
    def _():
        @pl.when(f == 0)
        def _():
            for j, r in enumerate(x_refs):
                x_scr[j * MOE_CH:(j + 1) * MOE_CH, :] = r[...]
            acc[...] = jnp.zeros_like(acc)

        _swiglu_step(x_scr[...], w1_ref, w3_ref, w2_ref, acc)

        @pl.when(f == MOE_NF - 1)
        def _():
            y_ref[...] = acc[...].astype(BF16)


def _moe_experts(xs, tab, gexp, nused, w1, w3, w2, e0):
    grp = lambda g, nu: jnp.minimum(g, nu[0] - 1)
    ftile = lambda g, f, nu: jnp.where(g < nu[0], f, MOE_NF - 1)
    chunk = lambda j: pl.BlockSpec((MOE_CH, D), lambda g, f, tab, ge, nu: (tab[grp(g, nu) * MOE_GC + j], 0))
    rows = MOE_GC * MOE_CH
    return pl.pallas_call(
        _experts_kernel,
        grid_spec=pltpu.PrefetchScalarGridSpec(
            num_scalar_prefetch=3, grid=(MOE_GROUPS, MOE_NF),
            in_specs=[chunk(j) for j in range(MOE_GC)] + [
                pl.BlockSpec((1, D, MOE_TF), lambda g, f, tab, ge, nu: (e0 + ge[grp(g, nu)], 0, ftile(g, f, nu))),
                pl.BlockSpec((1, D, MOE_TF), lambda g, f, tab, ge, nu: (e0 + ge[grp(g, nu)], 0, ftile(g, f, nu))),
                pl.BlockSpec((1, MOE_TF, D), lambda g, f, tab, ge, nu: (e0 + ge[grp(g, nu)], ftile(g, f, nu), 0))],
            out_specs=pl.BlockSpec((rows, D), lambda g, f, tab, ge, nu: (grp(g, nu), 0)),
            scratch_shapes=[pltpu.VMEM((rows, D), BF16), pltpu.VMEM((rows, D), F32)]),
        out_shape=jax.ShapeDtypeStruct((MOE_GROUPS * rows, D), BF16),
        compiler_params=_params("arbitrary", "arbitrary"),
        name="moe_experts",
    )(tab, gexp, nused, *([xs] * MOE_GC), w1, w3, w2)


def _combine_kernel(used_ref, pool_ref, *refs, final):
    refs = list(refs)
    y_refs = [refs.pop(0) for _ in range(MOE_NCH)]
    col_ref, x_ref, mod_ref = refs[:3]
    fg_ref = refs[3] if final else None
    o_ref, y_scr, acc = refs[-3:]
    used = used_ref[pl.program_id(0)]
    col = col_ref[...]
    s0, s1, g0, g1 = col[:, 0:1], col[:, 1:2], col[:, 2:3], col[:, 3:4]
    acc[...] = jnp.zeros_like(acc)
    per_blk = MOE_SB // MOE_CH
    for blk in range(MOE_SLOTS // MOE_SB):
        @pl.when(blk * per_blk < used)
        def _():
            for q in range(blk * per_blk, (blk + 1) * per_blk):
                y_scr[q * MOE_CH:(q + 1) * MOE_CH, :] = y_refs[q][...]
            c = (lax.broadcasted_iota(jnp.int32, (TM, MOE_SB), 1) + blk * MOE_SB).astype(F32)
            w = jnp.where(c == s0, g0, jnp.where(c == s1, g1, 0.0)).astype(BF16)
            acc[...] += jnp.dot(w, y_scr[blk * MOE_SB:(blk + 1) * MOE_SB, :], preferred_element_type=F32)
    o_ref[...] = _residual_out(x_ref[...], mod_ref[0][:, 5 * D:6 * D], acc[...], fg_ref)


def _moe_combine(y, pool, used, plan_col, x, mod_l, final_g=None):
    final = final_g is not None
    rows = LAT_ROWS if final else ROWS
    tile = lambda w: pl.BlockSpec((TM, w), lambda i, u, pc: (i, 0))
    chunk = lambda q: pl.BlockSpec((MOE_CH, D), lambda i, u, pc: (pc[i * MOE_NCH + q], 0))
    in_specs = [chunk(q) for q in range(MOE_NCH)] + [
        tile(LANE), tile(D), pl.BlockSpec((1, 1, N_MOD * D), lambda i, u, pc: (_mod_row(i), 0, 0))]
    args = [y] * MOE_NCH + [plan_col, x, mod_l]
    if final:
        in_specs.append(pl.BlockSpec((1, D), lambda i, u, pc: (0, 0)))
        args.append(final_g.reshape(1, D))
    return pl.pallas_call(
        functools.partial(_combine_kernel, final=final),
        grid_spec=pltpu.PrefetchScalarGridSpec(
            num_scalar_prefetch=2, grid=(rows // TM,),
            in_specs=in_specs, out_specs=tile(D),
            scratch_shapes=[pltpu.VMEM((MOE_SLOTS, D), BF16), pltpu.VMEM((TM, D), F32)]),
        out_shape=jax.ShapeDtypeStruct((rows, D), F32),
        compiler_params=_params("parallel"),
        name="moe_combine",
    )(used, pool, *args)


def _moe_mixer(h, logits, x, mod_l, w1, w3, w2, e0, final_g=None):
    plan_col, plan_row, nch = _moe_plan(logits)
    tab, gexp, nused, used, pool = _moe_tables(nch)
    xs = _moe_dispatch(h, plan_row, used)
    y = _moe_experts(xs, tab, gexp, nused, w1, w3, w2, e0)
    return _moe_combine(y, pool, used, plan_col, x, mod_l, final_g)


def kernel(x, c, ctx, c_ctx, w_mod, b_mod, norm1_g, norm2_g, w_in, rpb, hy_conv_w, hy_conv_b, hy_w1, hy_b1, hy_w2, hy_b2, hy_w3, hy_freq, hy_bias, w_pa, w_pb, w_o, ffn_w1, ffn_w3, ffn_w2, moe_router, moe_w1, moe_w3, moe_w2, final_g):
    xs = jnp.concatenate([x.reshape(LAT_ROWS, D), ctx.reshape(CTX_ROWS, D)], axis=0)
    mod = _modulation(c, c_ctx, w_mod, b_mod)
    tabs = _fft_tables()
    ctabs = _ctx_tables()
    filt_lat = _implicit_filters(SEQ, hy_w1, hy_b1, hy_w2, hy_b2, hy_w3, hy_freq)
    filt_ctx = _implicit_filters(CTX, hy_w1, hy_b1, hy_w2, hy_b2, hy_w3, hy_freq)
    spec_re, spec_im = _filter_spectra(filt_lat, tabs)
    nblk = HY_W // LANE

    moe_w = [w.reshape((-1,) + w.shape[2:]).astype(BF16) for w in (moe_w1, moe_w3, moe_w2)]
    ffn_w = [w.astype(BF16) for w in (ffn_w1, ffn_w3, ffn_w2)]
    w_in_b = w_in.astype(BF16)
    bias_tabs = _bias_tables(rpb)

    for l in range(DEPTH):
        last = l == DEPTH - 1
        mod_l = mod[l].reshape(MOD_ROWS, 1, N_MOD * D)
        cw, cb = hy_conv_w[l], hy_conv_b[l].reshape(1, 3 * HY_W)
        p = _in_proj(xs, mod_l, norm1_g[l].reshape(1, D), w_in_b, l)

        o_att = _context_attention(p, _neighbourhood_attention(p, bias_tabs, l))
        z = _hyena_conv(p, COL_HV, p, COL_HX1, cw, cb, 0, nblk, spec_re, spec_im, l, 0,
                        hy_bias[l], tabs, True)
        o_hy = _hyena_conv(z, 0, p, COL_HX2, cw, cb, 0, 2 * nblk, spec_re, spec_im, l, 1,
                           hy_bias[l], tabs, False)
        o_hy = _ctx_hyena(p, cw, cb, filt_ctx, l, hy_bias[l], ctabs, o_hy)

        i = l // 2
        router = moe_router[i] if l % 2 else None
        res = _merge(o_att, o_hy, p, xs, mod_l, norm2_g[l].reshape(1, D), w_pa, w_pb, w_o, l, router)
        fg = final_g if last else None
        if l % 2 == 0:
            xs = _dense_mixer(res[1], res[0], mod_l, *ffn_w, i, final_g=fg)
        else:
            xs = _moe_mixer(res[1], res[2], res[0], mod_l, *moe_w, i * N_EXPERTS, final_g=fg)
    return xs.reshape(NB, SEQ, D)
```

```python
import functools
import math

import numpy as np
import jax
import jax.numpy as jnp
from jax import lax
from jax.experimental import pallas as pl
from jax.experimental.pallas import tpu as pltpu

F32 = jnp.float32
BF16 = jnp.bfloat16
HIGHEST = lax.Precision.HIGHEST

D = 1024
NB = 4
SEQ = 4096
DEPTH = 4
CTX = 256
GRID_W = 64
GRID_H = SEQ // GRID_W
N_HEADS = 8
HEAD_DIM = 64
ATT_W = N_HEADS * HEAD_DIM
WIN_ROWS = 8
WIN_COLS = 16
HY_W = 512
HY_ORDER = 2
HY_SHORT = 3
HY_EMB = 33
HY_BANDS = (HY_EMB - 1) // 2
HY_FILT = 64
HY_DECAY_TARGET = 1e-2
HY_MAX_DECAY = math.log(HY_DECAY_TARGET) / 0.3
HY_MIN_DECAY = math.log(HY_DECAY_TARGET) / 1.5
HY_DECAY_SHIFT = 0.05
PROJ_W = 3 * ATT_W + 3 * HY_W + 2 * D
D_FF = 2816
N_EXPERTS = 8
N_MOD = 6
EPS = 1e-6

LAT_ROWS = NB * SEQ
CTX_ROWS = NB * CTX
ROWS = LAT_ROWS + CTX_ROWS
MOD_ROWS = 8
CTX_MOD_ROW = NB

LANE = 128
TM = 1024
VMEM_LIMIT = 56 * 1024 * 1024

COL_Q, COL_K, COL_V = 0, ATT_W // LANE, 2 * ATT_W // LANE
COL_HV = 3 * ATT_W // LANE
COL_HX1 = COL_HV + HY_W // LANE
COL_HX2 = COL_HX1 + HY_W // LANE
COL_GA = (3 * ATT_W + 3 * HY_W) // D
COL_GB = COL_GA + 1


def _mod_row(i):
    return jnp.where(i < LAT_ROWS // TM, (i * TM) // SEQ, CTX_MOD_ROW)


def _params(*sem):
    return pltpu.CompilerParams(dimension_semantics=sem, vmem_limit_bytes=VMEM_LIMIT)


def _mod_kernel(s_ref, w_ref, b_ref, o_ref):
    s = s_ref[...]
    s = s * jax.nn.sigmoid(s)
    o_ref[0] = jnp.dot(s, w_ref[0], precision=HIGHEST, preferred_element_type=F32) + b_ref[0]


def _modulation(c, c_ctx, w_mod, b_mod):
    s = jnp.zeros((MOD_ROWS, D), F32).at[:NB].set(c).at[CTX_MOD_ROW].set(c_ctx)
    tn = 1536
    return pl.pallas_call(
        _mod_kernel,
        grid=(DEPTH, N_MOD * D // tn),
        in_specs=[pl.BlockSpec((MOD_ROWS, D), lambda l, j: (0, 0)),
                  pl.BlockSpec((1, D, tn), lambda l, j: (l, 0, j)),
                  pl.BlockSpec((1, 1, tn), lambda l, j: (l, 0, j))],
        out_specs=pl.BlockSpec((1, MOD_ROWS, tn), lambda l, j: (l, 0, j)),
        out_shape=jax.ShapeDtypeStruct((DEPTH, MOD_ROWS, N_MOD * D), F32),
        compiler_params=_params("parallel", "parallel"),
        name="modulation",
    )(s, w_mod, b_mod.reshape(DEPTH, 1, N_MOD * D))


def _dot_hilo(a, b):
    ah, bh = a.astype(BF16), b.astype(BF16)
    al, bl = (a - ah.astype(F32)).astype(BF16), (b - bh.astype(F32)).astype(BF16)
    dot = functools.partial(jnp.dot, preferred_element_type=F32)
    return dot(ah, bh) + (dot(ah, bl) + dot(al, bh))


def _norm_mod(x, g, shift, scale):
    y = x * lax.rsqrt(jnp.mean(x * x, axis=-1, keepdims=True) + EPS) * g
    return y * (1.0 + scale) + shift


def _inproj_kernel(x_ref, mod_ref, g_ref, w_ref, o_ref, h_scr):
    @pl.when(pl.program_id(1) == 0)
    def _():
        m = mod_ref[0]
        h = _norm_mod(x_ref[...], g_ref[...], m[:, 0:D], m[:, D:2 * D])
        h_scr[...] = h.astype(BF16)

    o_ref[...] = jnp.dot(h_scr[...], w_ref[0].astype(BF16),
                         preferred_element_type=F32).astype(BF16)


def _in_proj(x, mod_l, g, w, layer):
    tn = 1024
    return pl.pallas_call(
        _inproj_kernel,
        grid=(ROWS // TM, PROJ_W // tn),
        in_specs=[pl.BlockSpec((TM, D), lambda i, j: (i, 0)),
                  pl.BlockSpec((1, 1, N_MOD * D), lambda i, j: (_mod_row(i), 0, 0)),
                  pl.BlockSpec((1, D), lambda i, j: (0, 0)),
                  pl.BlockSpec((1, D, tn), lambda i, j: (layer, 0, j))],
        out_specs=pl.BlockSpec((TM, tn), lambda i, j: (i, j)),
        out_shape=jax.ShapeDtypeStruct((ROWS, PROJ_W), BF16),
        scratch_shapes=[pltpu.VMEM((TM, D), BF16)],
        compiler_params=_params("parallel", "arbitrary"),
        name="in_proj",
    )(x, mod_l, g, w)


NEG = -1e30


def _nt_dot(a, b):
    return lax.dot_general(a, b, (((1,), (1,)), ((), ())), preferred_element_type=F32)


def _attend_pair(qp, keys, values, biases):
    lane = lax.broadcasted_iota(jnp.int32, qp.shape, 1)
    qs = qp * jnp.asarray(HEAD_DIM ** -0.5, BF16)
    outs = []
    for a in range(2):
        sel = (lane < HEAD_DIM) if a == 0 else (lane >= HEAD_DIM)
        qa = jnp.where(sel, qs, jnp.zeros_like(qs))
        ss = []
        for k, b in zip(keys, biases):
            s = _nt_dot(qa, k)
            if b is not None:
                s = s + b[a]
            ss.append(s)
        m = functools.reduce(jnp.maximum, [jnp.max(s, axis=-1, keepdims=True) for s in ss])
        ps = [jnp.exp(s - m) for s in ss]
        den = functools.reduce(jnp.add, [jnp.sum(p, axis=-1, keepdims=True) for p in ps])
        o = functools.reduce(jnp.add, [jnp.dot(p.astype(BF16), v, preferred_element_type=F32)
                                       for p, v in zip(ps, values)])
        outs.append(o / den)
    return jnp.where(lane < HEAD_DIM, outs[0], outs[1])


NLOC = WIN_ROWS * GRID_W
NKEY = NLOC + CTX
NA_PAIRS = GRID_H // 2
NA_STEPS = NA_PAIRS + 2
NA_UNITS = 2 * N_HEADS


def _window_start(r):
    r0 = jnp.clip(r - WIN_ROWS // 2, 0, GRID_H - WIN_ROWS)
    return pl.multiple_of(r0 * GRID_W, GRID_W)


def _natt_kernel(q_ref, k_ref, v_ref, kc_ref, vc_ref, bt0_ref, bt1_ref, o_ref, s0, s1, p0, p1, l0, l1):
    j = pl.program_id(1)
    lane = lax.broadcasted_iota(jnp.int32, (GRID_W, LANE), 1)
    lo = lane < HEAD_DIM
    hi = lane >= HEAD_DIM

    @pl.when(j == 0)
    def _():
        s1[...] = jnp.zeros_like(s1)
        p0[...] = jnp.zeros_like(p0)
        p1[...] = jnp.zeros_like(p1)
        l0[...] = jnp.ones_like(l0)
        l1[...] = jnp.ones_like(l1)

    def step(s_cur, s_prev, p_cur, p_prev, l_cur, l_prev):
        ro = jnp.maximum(2 * j - 4, 0)
        rs = jnp.minimum(2 * j, GRID_H - 2)
        for half, bt_ref in enumerate((bt0_ref, bt1_ref)):
            start_o = _window_start(ro + half)
            start_s = _window_start(rs + half)
            rows = slice(half * GRID_W, (half + 1) * GRID_W)
            for hp in range(N_HEADS // 2):
                cs = slice(hp * LANE, (hp + 1) * LANE)
                u0 = half * N_HEADS + 2 * hp
                units = (u0, u0 + 1)
                pair = slice(u0, u0 + 2)
                m2 = 2 * GRID_W

                p2 = p_cur[pair].reshape(m2, NKEY)
                acc = (jnp.dot(p2[:, 0:NLOC], v_ref[pl.ds(start_o, NLOC), cs], preferred_element_type=F32)
                       + jnp.dot(p2[:, NLOC:NKEY], vc_ref[:, cs], preferred_element_type=F32))
                o_ref[rows, cs] = jnp.where(lo, acc[0:GRID_W] / l_cur[u0],
                                            acc[GRID_W:m2] / l_cur[u0 + 1]).astype(BF16)

                for u in units:
                    x = s_prev[u]
                    e = jnp.exp(x - jnp.max(x, axis=-1, keepdims=True))
                    l_prev[u] = jnp.broadcast_to(jnp.sum(e, axis=-1, keepdims=True), (GRID_W, LANE))
                    p_prev[u] = e.astype(BF16)

                qs = q_ref[rows, cs] * jnp.asarray(HEAD_DIM ** -0.5, BF16)
                zero = jnp.zeros_like(qs)
                q2 = jnp.concatenate([jnp.where(lo, qs, zero), jnp.where(hi, qs, zero)], axis=0)
                bias = bt_ref[0, 2 * hp:2 * hp + 2].astype(F32)
                s_cur[pair, :, 0:NLOC] = _nt_dot(q2, k_ref[pl.ds(start_s, NLOC), cs]).reshape(2, GRID_W, NLOC) + bias
                s_cur[pair, :, NLOC:NKEY] = _nt_dot(q2, kc_ref[:, cs]).reshape(2, GRID_W, CTX)

    @pl.when(j % 2 == 0)
    def _():
        step(s0, s1, p0, p1, l0, l1)

    @pl.when(j % 2 == 1)
    def _():
        step(s1, s0, p1, p0, l1, l0)


def _bias_tables(rpb):
    col = np.arange(GRID_W)
    c0 = np.clip(col - WIN_COLS // 2, 0, GRID_W - WIN_COLS)
    valid = (col[None, :] >= c0[:, None]) & (col[None, :] < c0[:, None] + WIN_COLS)
    dc = np.clip(col[None, :] - col[:, None] + (WIN_COLS - 1), 0, 2 * WIN_COLS - 2)
    pick = (np.arange(2 * WIN_COLS - 1)[:, None, None] == dc[None]).astype(np.float32)
    t = jnp.einsum('lhrd,dqk->lhrqk', rpb, jnp.asarray(pick), precision=HIGHEST)
    t = jnp.where(valid, t, NEG)
    win = jnp.stack([t[:, :, d0:d0 + WIN_ROWS] for d0 in range(WIN_ROWS)], axis=1)
    win = win.transpose(0, 1, 2, 4, 3, 5)
    return win.reshape(DEPTH * WIN_ROWS, N_HEADS, GRID_W, NLOC).astype(BF16)


def _bias_index(r):
    r0 = jnp.clip(r - WIN_ROWS // 2, 0, GRID_H - WIN_ROWS)
    return r0 - r + (WIN_ROWS - 1)


def _neighbourhood_attention(p, bt, layer):
    blk = GRID_H // 2
    ctx_blk0 = LAT_ROWS // CTX
    score_row = lambda j, half: jnp.minimum(2 * j, GRID_H - 2) + half
    bias_spec = lambda half: pl.BlockSpec(
        (1, N_HEADS, GRID_W, NLOC), lambda b, j: (layer * WIN_ROWS + _bias_index(score_row(j, half)), 0, 0, 0))
    return pl.pallas_call(
        _natt_kernel,
        grid=(NB, NA_STEPS),
        in_specs=[pl.BlockSpec((2 * GRID_W, ATT_W), lambda b, j: (b * blk + jnp.minimum(j, blk - 1), 0)),
                  pl.BlockSpec((SEQ, ATT_W), lambda b, j: (b, 1)),
                  pl.BlockSpec((SEQ, ATT_W), lambda b, j: (b, 2)),
                  pl.BlockSpec((CTX, ATT_W), lambda b, j: (ctx_blk0 + b, 1)),
                  pl.BlockSpec((CTX, ATT_W), lambda b, j: (ctx_blk0 + b, 2)),
                  bias_spec(0), bias_spec(1)],
        out_specs=pl.BlockSpec((2 * GRID_W, ATT_W), lambda b, j: (b * blk + jnp.maximum(j - 2, 0), 0)),
        out_shape=jax.ShapeDtypeStruct((ROWS, ATT_W), BF16),
        scratch_shapes=[pltpu.VMEM((NA_UNITS, GRID_W, NKEY), F32)] * 2
                       + [pltpu.VMEM((NA_UNITS, GRID_W, NKEY), BF16)] * 2
                       + [pltpu.VMEM((NA_UNITS, GRID_W, LANE), F32)] * 2,
        compiler_params=_params("parallel", "arbitrary"),
        name="neighbourhood_attention",
    )(p, p, p, p, p, bt, bt)


def _catt_kernel(q_ref, k_ref, v_ref, dst_ref, o_ref):
    del dst_ref
    for hp in range(N_HEADS // 2):
        cs = slice(hp * LANE, (hp + 1) * LANE)
        o = _attend_pair(q_ref[:, cs], [k_ref[:, cs]], [v_ref[:, cs]], [None])
        o_ref[:, cs] = o.astype(BF16)


def _context_attention(p, o_att):
    ctx_blk0 = LAT_ROWS // CTX
    return pl.pallas_call(
        _catt_kernel,
        grid=(NB,),
        in_specs=[pl.BlockSpec((CTX, ATT_W), lambda b: (ctx_blk0 + b, 0)),
                  pl.BlockSpec((CTX, ATT_W), lambda b: (ctx_blk0 + b, 1)),
                  pl.BlockSpec((CTX, ATT_W), lambda b: (ctx_blk0 + b, 2)),
                  pl.BlockSpec(memory_space=pl.ANY)],
        out_specs=pl.BlockSpec((CTX, ATT_W), lambda b: (ctx_blk0 + b, 0)),
        out_shape=jax.ShapeDtypeStruct((ROWS, ATT_W), BF16),
        input_output_aliases={3: 0},
        compiler_params=_params("parallel"),
        name="context_attention",
    )(p, p, p, o_att)


def _filter_features(L):
    pos = np.arange(L, dtype=np.float64)
    t = pos / max(L - 1, 1)
    bands = np.linspace(1e-4, HY_BANDS - 1, HY_BANDS)
    ang = (2.0 * math.pi / L) * pos[:, None] * bands[None, :]
    z = np.concatenate([t[:, None], np.cos(ang), -np.sin(ang)], axis=-1)
    zp = np.zeros((L, LANE), np.float32)
    zp[:, :HY_EMB] = z
    return zp


def _filter_kernel(z_ref, w1_ref, b1_ref, w2_ref, b2_ref, w3_ref, fr_ref, rate_ref, o_ref, *, L, tl):
    dot = functools.partial(jnp.dot, precision=HIGHEST, preferred_element_type=F32)
    fr = fr_ref[0]
    a = jnp.sin(fr[0:1] * (dot(z_ref[...], w1_ref[0]) + b1_ref[0]))
    a = jnp.sin(fr[1:2] * (dot(a, w2_ref[0]) + b2_ref[0]))
    h = _dot_hilo(a, w3_ref[0])
    pos = (pl.program_id(1) * tl + lax.broadcasted_iota(jnp.int32, (tl, 1), 0)).astype(F32)
    dist = jnp.abs(pos - float(L // 2)) * (2.0 / L)
    o_ref[0] = h * (jnp.exp(-dist * rate_ref[...]) + HY_DECAY_SHIFT)


def _implicit_filters(L, w1, b1, w2, b2, w3, freq):
    tl = min(L, 512)
    z = jnp.asarray(_filter_features(L))
    w1p = jnp.zeros((DEPTH, LANE, HY_FILT), F32).at[:, :HY_EMB].set(w1)
    rates = np.abs(np.linspace(HY_MIN_DECAY, HY_MAX_DECAY, HY_ORDER * HY_W)).astype(np.float32)
    cw = HY_ORDER * HY_W
    return pl.pallas_call(
        functools.partial(_filter_kernel, L=L, tl=tl),
        grid=(DEPTH, L // tl),
        in_specs=[pl.BlockSpec((tl, LANE), lambda l, i: (i, 0)),
                  pl.BlockSpec((1, LANE, HY_FILT), lambda l, i: (l, 0, 0)),
                  pl.BlockSpec((1, 1, HY_FILT), lambda l, i: (l, 0, 0)),
                  pl.BlockSpec((1, HY_FILT, HY_FILT), lambda l, i: (l, 0, 0)),
                  pl.BlockSpec((1, 1, HY_FILT), lambda l, i: (l, 0, 0)),
                  pl.BlockSpec((1, HY_FILT, cw), lambda l, i: (l, 0, 0)),
                  pl.BlockSpec((1, 2, HY_FILT), lambda l, i: (l, 0, 0)),
                  pl.BlockSpec((1, cw), lambda l, i: (0, 0))],
        out_specs=pl.BlockSpec((1, tl, cw), lambda l, i: (l, i, 0)),
        out_shape=jax.ShapeDtypeStruct((DEPTH, L, cw), F32),
        compiler_params=_params("parallel", "parallel"),
        name=f"implicit_filters_{L}",
    )(z, w1p, b1.reshape(DEPTH, 1, HY_FILT), w2, b2.reshape(DEPTH, 1, HY_FILT), w3, freq,
      jnp.asarray(rates).reshape(1, cw))


FFT_N = 2 * SEQ
FA = 64
FB = 128
FA_IN = SEQ // FB
PITCH = 136
HALO = 8
UNROLL_ROWS = 64
UNROLL_SLABS = 32


def _fft_tables():
    a = np.arange(FA_IN)
    r = np.arange(FA)
    b = np.arange(FB)
    q = np.arange(FB)
    ph = a[None, None, :] * r[None, :, None] / FA + b[:, None, None] * r[None, :, None] / FFT_N
    g1 = np.exp(-2j * np.pi * ph)
    g1 = np.concatenate([g1.real, g1.imag], axis=1)
    f = np.exp(-2j * np.pi * q[:, None] * b[None, :] / FB)
    f2 = np.concatenate([f.real, f.imag], axis=0)
    f2c = np.concatenate([f.real, -f.imag], axis=0)
    ao = np.arange(FA_IN) + (SEQ // 2) // FB
    ph2 = ao[None, :, None] * r[None, None, :] / FA + b[:, None, None] * r[None, None, :] / FFT_N
    g2 = np.exp(2j * np.pi * ph2) / FFT_N
    g2 = np.concatenate([g2.real, g2.imag], axis=1)
    return tuple(jnp.asarray(t, dtype=BF16) for t in (g1, f2, f2c, g2))


def _cplx(out, m, n):
    re = out[0:m, 0:n] - out[m:2 * m, n:2 * n]
    im = out[0:m, n:2 * n] + out[m:2 * m, 0:n]
    return re, im


def _short_conv_slabs(src_ref, rows, t_scr, cw_ref, cb_ref, apply_conv, emit):
    nslab = rows // FB
    if not apply_conv:
        for a in range(nslab):
            emit(a, src_ref[a * FB:(a + 1) * FB, :].astype(F32))
        return
    zero = jnp.zeros((HALO, LANE), F32)
    t_scr[0:HALO, :] = zero
    t_scr[HALO + rows:2 * HALO + rows, :] = zero
    for a in range(nslab):
        t_scr[HALO + a * FB:HALO + (a + 1) * FB, :] = src_ref[a * FB:(a + 1) * FB, :].astype(F32)
    w0, w1, w2, bias = cw_ref[0:1, :], cw_ref[1:2, :], cw_ref[2:3, :], cb_ref[...]
    for a in range(nslab):
        o = HALO + a * FB
        y = (w0 * t_scr[o - 1:o - 1 + FB, :] + w1 * t_scr[o:o + FB, :]
             + w2 * t_scr[o + 1:o + 1 + FB, :] + bias)
        emit(a, y)


def _fwd_stage1(b, s_refs, g1_ref, y_r, y_i):
    cols = [s[pl.ds(b, FA_IN, stride=PITCH), :].astype(BF16) for s in s_refs]
    rhs = cols[0] if len(cols) == 1 else jnp.concatenate(cols, axis=1)
    out = jnp.dot(g1_ref[b], rhs, preferred_element_type=F32)
    if len(cols) == 1:
        re, im = out[0:FA], out[FA:2 * FA]
    else:
        re, im = _cplx(out, FA, LANE)
    y_r[pl.ds(b, FA, stride=PITCH), :] = re
    y_i[pl.ds(b, FA, stride=PITCH), :] = im


def _fwd_stage2(r, y_r, y_i, f_ref):
    o = pl.multiple_of(r * PITCH, 8)
    rhs = jnp.concatenate([y_r[pl.ds(o, FB), :].astype(BF16), y_i[pl.ds(o, FB), :].astype(BF16)], axis=1)
    out = jnp.dot(f_ref[...], rhs, preferred_element_type=F32)
    return _cplx(out, FB, LANE)


def _spectrum_kernel(h_ref, g1_ref, f_ref, hr_ref, hi_ref, s_scr, y_r, y_i):
    for a in range(FA_IN):
        s_scr[a * PITCH:a * PITCH + FB, :] = h_ref[0, a * FB:(a + 1) * FB, :]

    def s1(b, c):
        _fwd_stage1(b, [s_scr], g1_ref, y_r, y_i)
        return c
    lax.fori_loop(0, FB, s1, 0, unroll=UNROLL_ROWS)

    def s2(r, c):
        xr, xi = _fwd_stage2(r, y_r, y_i, f_ref)
        o = pl.multiple_of(r * FB, FB)
        hr_ref[0, pl.ds(o, FB), :] = xr
        hi_ref[0, pl.ds(o, FB), :] = xi
        return c
    lax.fori_loop(0, FA, s2, 0, unroll=UNROLL_SLABS)


def _filter_spectra(h, tabs):
    g1, f2, _, _ = tabs
    cw = HY_ORDER * HY_W
    spec = pl.BlockSpec((1, FFT_N, LANE), lambda l, c: (l, 0, c))
    return pl.pallas_call(
        _spectrum_kernel,
        grid=(DEPTH, cw // LANE),
        in_specs=[pl.BlockSpec((1, SEQ, LANE), lambda l, c: (l, 0, c)),
                  pl.BlockSpec(g1.shape, lambda l, c: (0, 0, 0)),
                  pl.BlockSpec(f2.shape, lambda l, c: (0, 0))],
        out_specs=[spec, spec],
        out_shape=[jax.ShapeDtypeStruct((DEPTH, FFT_N, cw), F32)] * 2,
        scratch_shapes=[pltpu.VMEM((FA_IN * PITCH, LANE), F32),
                        pltpu.VMEM((FA * PITCH, LANE), F32),
                        pltpu.VMEM((FA * PITCH, LANE), F32)],
        compiler_params=_params("parallel", "parallel"),
        name="filter_spectra",
    )(h, g1, f2)


def _hyconv_kernel(u_ref, x_ref, cwu_ref, cbu_ref, cwx_ref, cbx_ref,
                   hr_ref, hi_ref, bias_ref, g1_ref, f_ref, fc_ref, g2_ref,
                   o_ref, t_scr, s_r, s_i, y_r, y_i, *, conv_input):
    halves = [(s_r, slice(0, SEQ)), (s_i, slice(SEQ, 2 * SEQ))]

    def fill(s):
        def emit(a, slab):
            s[a * PITCH:a * PITCH + FB, :] = slab
        return emit
    for s, rows in halves:
        _short_conv_slabs(u_ref.at[rows, :], SEQ, t_scr, cwu_ref, cbu_ref, conv_input, fill(s))

    def s1(b, c):
        _fwd_stage1(b, [s_r, s_i], g1_ref, y_r, y_i)
        return c
    lax.fori_loop(0, FB, s1, 0, unroll=UNROLL_ROWS)

    def s2(r, c):
        xr, xi = _fwd_stage2(r, y_r, y_i, f_ref)
        oh = pl.multiple_of(r * FB, FB)
        hr = hr_ref[0, pl.ds(oh, FB), :]
        hi = hi_ref[0, pl.ds(oh, FB), :]
        vr = xr * hr - xi * hi
        vi = xr * hi + xi * hr
        rhs = jnp.concatenate([vr.astype(BF16), vi.astype(BF16)], axis=1)
        wr, wi = _cplx(jnp.dot(fc_ref[...], rhs, preferred_element_type=F32), FB, LANE)
        o = pl.multiple_of(r * PITCH, 8)
        y_r[pl.ds(o, FB), :] = wr
        y_i[pl.ds(o, FB), :] = wi
        return c
    lax.fori_loop(0, FA, s2, 0, unroll=UNROLL_SLABS)

    bias = bias_ref[0]

    def s3(b, c):
        rhs = jnp.concatenate([y_r[pl.ds(b, FA, stride=PITCH), :].astype(BF16),
                               y_i[pl.ds(b, FA, stride=PITCH), :].astype(BF16)], axis=1)
        re, im = _cplx(jnp.dot(g2_ref[b], rhs, preferred_element_type=F32), FA_IN, LANE)
        idx = pl.ds(b, FA_IN, stride=PITCH)
        s_r[idx, :] = re + s_r[idx, :] * bias
        s_i[idx, :] = im + s_i[idx, :] * bias
        return c
    lax.fori_loop(0, FB, s3, 0, unroll=UNROLL_ROWS)

    for s, rows in halves:
        def emit(a, slab, s=s, out=o_ref.at[rows, :]):
            out[a * FB:(a + 1) * FB, :] = (slab * s[a * PITCH:a * PITCH + FB, :]).astype(out.dtype)
        _short_conv_slabs(x_ref.at[rows, :], SEQ, t_scr, cwx_ref, cbx_ref, True, emit)


def _hyena_conv(u, u_col, x, x_col, conv_w, conv_b, cu_col, cx_col, spec_re, spec_im, layer, order,
                hy_bias_l, tabs, conv_input):
    g1, f2, f2c, g2 = tabs
    nblk = HY_W // LANE
    tok = lambda col: pl.BlockSpec((2 * SEQ, LANE), lambda c, p: (p, col + c))
    chan = lambda rows, col: pl.BlockSpec((rows, LANE), lambda c, p: (0, col + c))
    hspec = pl.BlockSpec((1, FFT_N, LANE), lambda c, p: (layer, 0, order * nblk + c))
    const = lambda shape: pl.BlockSpec(shape, lambda c, p: (0,) * len(shape))
    return pl.pallas_call(
        functools.partial(_hyconv_kernel, conv_input=conv_input),
        grid=(nblk, NB // 2),
        in_specs=[tok(u_col), tok(x_col),
                  chan(HY_SHORT, cu_col), chan(1, cu_col), chan(HY_SHORT, cx_col), chan(1, cx_col),
                  hspec, hspec,
                  pl.BlockSpec((1, 1, LANE), lambda c, p: (order, 0, c)),
                  const(g1.shape), const(f2.shape), const(f2c.shape), const(g2.shape)],
        out_specs=pl.BlockSpec((2 * SEQ, LANE), lambda c, p: (p, c)),
        out_shape=jax.ShapeDtypeStruct((ROWS, HY_W), BF16),
        scratch_shapes=[pltpu.VMEM((SEQ + 2 * HALO, LANE), F32),
                        pltpu.VMEM((FA_IN * PITCH, LANE), F32),
                        pltpu.VMEM((FA_IN * PITCH, LANE), F32),
                        pltpu.VMEM((FA * PITCH, LANE), F32),
                        pltpu.VMEM((FA * PITCH, LANE), F32)],
        compiler_params=_params("arbitrary", "arbitrary"),
        name=f"hyena_conv{order}",
    )(u, x, conv_w, conv_b, conv_w, conv_b, spec_re, spec_im,
      hy_bias_l.reshape(HY_ORDER, 1, HY_W), g1, f2, f2c, g2)


CN = 2 * CTX


def _ctx_tables():
    k = np.arange(CN)
    n = np.arange(CTX)
    f = np.exp(-2j * np.pi * k[:, None] * n[None, :] / CN)
    fwd = np.concatenate([f.real, f.imag], axis=0)
    t = np.arange(CTX) + CTX // 2
    g = np.exp(2j * np.pi * t[:, None] * k[None, :] / CN) / CN
    inv = np.concatenate([g.real, g.imag], axis=0)
    return jnp.asarray(fwd, dtype=BF16), jnp.asarray(inv, dtype=BF16)


def _ctx_hyena_kernel(v_ref, x1_ref, x2_ref, cwv_ref, cbv_ref, cw1_ref, cb1_ref, cw2_ref, cb2_ref,
                      h0_ref, h1_ref, bias_ref, fwd_ref, inv_ref, dst_ref, o_ref, t_scr, s_scr, g_scr):
    del dst_ref
    def conv_into(src_ref, b, cw_ref, cb_ref, dst, col):
        view = src_ref.at[b * CTX:(b + 1) * CTX, :]

        def emit(a, slab):
            dst[a * FB:(a + 1) * FB, col * LANE:(col + 1) * LANE] = slab
        _short_conv_slabs(view, CTX, t_scr, cw_ref, cb_ref, True, emit)

    def spectrum(h_ref):
        out = jnp.dot(fwd_ref[...], h_ref[0].astype(BF16), preferred_element_type=F32)
        return out[0:CN], out[CN:2 * CN]

    def long_conv(z2, hr, hi, bias):
        x = jnp.dot(fwd_ref[...], z2.astype(BF16), preferred_element_type=F32)
        xr, xi = _cplx(x, CN, LANE)
        vr = xr * hr - xi * hi
        vi = xr * hi + xi * hr
        rhs = jnp.concatenate([vr.astype(BF16), vi.astype(BF16)], axis=1)
        y = jnp.dot(inv_ref[...], rhs, preferred_element_type=F32)
        yr, yi = _cplx(y, CTX, LANE)
        return jnp.concatenate([yr, yi], axis=1) + z2 * jnp.concatenate([bias, bias], axis=1)

    h0r, h0i = spectrum(h0_ref)
    h1r, h1i = spectrum(h1_ref)
    for p in range(NB // 2):
        for half in range(2):
            conv_into(v_ref, 2 * p + half, cwv_ref, cbv_ref, s_scr, half)
            conv_into(x1_ref, 2 * p + half, cw1_ref, cb1_ref, g_scr, half)
        z = g_scr[...] * long_conv(s_scr[...], h0r, h0i, bias_ref[0])
        for half in range(2):
            conv_into(x2_ref, 2 * p + half, cw2_ref, cb2_ref, g_scr, half)
        y = g_scr[...] * long_conv(z, h1r, h1i, bias_ref[1])
        for half in range(2):
            b = 2 * p + half
            o_ref[b * CTX:(b + 1) * CTX, :] = y[:, half * LANE:(half + 1) * LANE].astype(BF16)


def _ctx_hyena(p, conv_w, conv_b, filt_ctx, layer, hy_bias_l, ctabs, o_hy):
    fwd, inv = ctabs
    nblk = HY_W // LANE
    ctx_blk = LAT_ROWS // CTX_ROWS
    tok = lambda col: pl.BlockSpec((CTX_ROWS, LANE), lambda c: (ctx_blk, col + c))
    chan = lambda rows, col: pl.BlockSpec((rows, LANE), lambda c: (0, col + c))
    hspec = lambda order: pl.BlockSpec((1, CTX, LANE), lambda c: (layer, 0, order * nblk + c))
    const = lambda shape: pl.BlockSpec(shape, lambda c: (0,) * len(shape))
    return pl.pallas_call(
        _ctx_hyena_kernel,
        grid=(nblk,),
        in_specs=[tok(COL_HV), tok(COL_HX1), tok(COL_HX2),
                  chan(HY_SHORT, 0), chan(1, 0), chan(HY_SHORT, nblk), chan(1, nblk),
                  chan(HY_SHORT, 2 * nblk), chan(1, 2 * nblk),
                  hspec(0), hspec(1),
                  pl.BlockSpec((HY_ORDER, 1, LANE), lambda c: (0, 0, c)),
                  const(fwd.shape), const(inv.shape), pl.BlockSpec(memory_space=pl.ANY)],
        out_specs=pl.BlockSpec((CTX_ROWS, LANE), lambda c: (ctx_blk, c)),
        out_shape=jax.ShapeDtypeStruct((ROWS, HY_W), BF16),
        input_output_aliases={14: 0},
        scratch_shapes=[pltpu.VMEM((CTX + 2 * HALO, LANE), F32),
                        pltpu.VMEM((CTX, 2 * LANE), F32),
                        pltpu.VMEM((CTX, 2 * LANE), F32)],
        compiler_params=_params("parallel"),
        name="context_hyena",
    )(p, p, p, conv_w, conv_b, conv_w, conv_b, conv_w, conv_b, filt_ctx, filt_ctx,
      hy_bias_l.reshape(HY_ORDER, 1, HY_W), fwd, inv, o_hy)


def _merge_kernel(oa_ref, oh_ref, ga_ref, gb_ref, x_ref, mod_ref, g2_ref, wpa_ref, wpb_ref, wo_ref,
                  *rest, routed):
    if routed:
        wr_ref, xo_ref, h_ref, rw_ref, wpa_s, wpb_s, wo_s = rest
    else:
        xo_ref, h_ref, wpa_s, wpb_s, wo_s = rest

    @pl.when(pl.program_id(0) == 0)
    def _():
        wpa_s[...] = wpa_ref[0].astype(BF16)
        wpb_s[...] = wpb_ref[0].astype(BF16)
        wo_s[...] = wo_ref[0].astype(BF16)

    a = jnp.dot(oa_ref[...], wpa_s[...], preferred_element_type=F32)
    b = jnp.dot(oh_ref[...], wpb_s[...], preferred_element_type=F32)
    mix = (jax.nn.sigmoid(ga_ref[...].astype(F32)) * a + jax.nn.sigmoid(gb_ref[...].astype(F32)) * b)
    y = jnp.dot(mix.astype(BF16), wo_s[...], preferred_element_type=F32)
    m = mod_ref[0]
    x = x_ref[...] + m[:, 2 * D:3 * D] * y
    xo_ref[...] = x
    h = _norm_mod(x, g2_ref[...], m[:, 3 * D:4 * D], m[:, 4 * D:5 * D])
    h_ref[...] = h.astype(BF16)
    if routed:
        rw_ref[...] = _dot_hilo(h, wr_ref[...])


def _merge(o_att, o_hy, p, x, mod_l, g2, w_pa, w_pb, w_o, layer, router):
    routed = router is not None
    tile = lambda w: pl.BlockSpec((TM, w), lambda i: (i, 0))
    full = lambda a: pl.BlockSpec(a.shape, lambda i: (0,) * a.ndim)
    layered = lambda a: pl.BlockSpec((1,) + a.shape[1:], lambda i: (layer, 0, 0))
    in_specs = [tile(ATT_W), tile(HY_W),
                pl.BlockSpec((TM, D), lambda i: (i, COL_GA)),
                pl.BlockSpec((TM, D), lambda i: (i, COL_GB)),
                tile(D),
                pl.BlockSpec((1, 1, N_MOD * D), lambda i: (_mod_row(i), 0, 0)),
                pl.BlockSpec((1, D), lambda i: (0, 0)),
                layered(w_pa), layered(w_pb), layered(w_o)]
    args = [o_att, o_hy, p, p, x, mod_l, g2, w_pa, w_pb, w_o]
    out_specs = [tile(D), tile(D)]
    out_shape = [jax.ShapeDtypeStruct((ROWS, D), F32), jax.ShapeDtypeStruct((ROWS, D), BF16)]
    if routed:
        wr = jnp.zeros((D, LANE), F32).at[:, :N_EXPERTS].set(router)
        in_specs.append(full(wr))
        args.append(wr)
        out_specs.append(tile(LANE))
        out_shape.append(jax.ShapeDtypeStruct((ROWS, LANE), F32))
    return pl.pallas_call(
        functools.partial(_merge_kernel, routed=routed),
        grid=(ROWS // TM,),
        in_specs=in_specs, out_specs=out_specs, out_shape=out_shape,
        scratch_shapes=[pltpu.VMEM(w_pa.shape[1:], BF16), pltpu.VMEM(w_pb.shape[1:], BF16),
                        pltpu.VMEM(w_o.shape[1:], BF16)],
        compiler_params=_params("arbitrary"),
        name="merge",
    )(*args)


TF = D_FF // 2


NF = D_FF // TF


SWIGLU_SUB = 768


def _swiglu_step(h, w1_ref, w3_ref, w2_ref, acc):
    tf = w1_ref.shape[2]
    for lo in range(0, tf, SWIGLU_SUB):
        hi = min(lo + SWIGLU_SUB, tf)
        a = jnp.dot(h, w1_ref[0, :, lo:hi].astype(BF16), preferred_element_type=F32)
        b = jnp.dot(h, w3_ref[0, :, lo:hi].astype(BF16), preferred_element_type=F32)
        t = a * jax.nn.sigmoid(a) * b
        acc[...] += jnp.dot(t.astype(BF16), w2_ref[0, lo:hi, :].astype(BF16), preferred_element_type=F32)


def _residual_out(x, gate, y, fg_ref):
    x = x + gate * y
    if fg_ref is not None:
        x = x * lax.rsqrt(jnp.mean(x * x, axis=-1, keepdims=True) + EPS) * fg_ref[...]
    return x


def _ffn_kernel(h_ref, x_ref, mod_ref, w1_ref, w3_ref, w2_ref, *rest, final):
    rest = list(rest)
    fg_ref = rest.pop(0) if final else None
    o_ref, acc = rest
    f = pl.program_id(1)

    @pl.when(f == 0)
    def _():
        acc[...] = jnp.zeros_like(acc)

    _swiglu_step(h_ref[...], w1_ref, w3_ref, w2_ref, acc)

    @pl.when(f == NF - 1)
    def _():
        o_ref[...] = _residual_out(x_ref[...], mod_ref[0][:, 5 * D:6 * D], acc[...], fg_ref)


def _dense_mixer(h, x, mod_l, w1, w3, w2, e, final_g=None):
    final = final_g is not None
    rows = LAT_ROWS if final else ROWS
    tile = lambda w: pl.BlockSpec((TM, w), lambda i, f: (i, 0))
    in_specs = [tile(D), tile(D),
                pl.BlockSpec((1, 1, N_MOD * D), lambda i, f: (_mod_row(i), 0, 0)),
                pl.BlockSpec((1, D, TF), lambda i, f: (e, 0, f)),
                pl.BlockSpec((1, D, TF), lambda i, f: (e, 0, f)),
                pl.BlockSpec((1, TF, D), lambda i, f: (e, f, 0))]
    args = [h, x, mod_l, w1, w3, w2]
    if final:
        in_specs.append(pl.BlockSpec((1, D), lambda i, f: (0, 0)))
        args.append(final_g.reshape(1, D))
    return pl.pallas_call(
        functools.partial(_ffn_kernel, final=final),
        grid=(rows // TM, NF),
        in_specs=in_specs,
        out_specs=tile(D),
        out_shape=jax.ShapeDtypeStruct((rows, D), F32),
        scratch_shapes=[pltpu.VMEM((TM, D), F32)],
        compiler_params=_params("parallel", "arbitrary"),
        name="ffn",
    )(*args)


NT = ROWS // TM
MOE_CH = 64
MOE_CMAX = TM // MOE_CH
MOE_NCH = 2 * TM // MOE_CH + N_EXPERTS
MOE_SLOTS = MOE_NCH * MOE_CH
MOE_GC = 16
MOE_GROUPS = NT * MOE_NCH // MOE_GC + N_EXPERTS
MOE_SB = 512
MOE_TF, MOE_NF = TF, NF
PLAN_ROWS = 8


def _plan_kernel(lg_ref, col_ref, row_ref, nch_ref):
    lane = lax.broadcasted_iota(jnp.int32, (TM, LANE), 1)
    ninf = jnp.asarray(-jnp.inf, F32)
    lg = jnp.where(lane < N_EXPERTS, lg_ref[...], ninf)
    m1 = jnp.max(lg, axis=-1, keepdims=True)
    i1 = jnp.min(jnp.where(lg == m1, lane, LANE), axis=-1, keepdims=True)
    lg2 = jnp.where(lane == i1, ninf, lg)
    m2 = jnp.max(lg2, axis=-1, keepdims=True)
    i2 = jnp.min(jnp.where(lg2 == m2, lane, LANE), axis=-1, keepdims=True)
    e = jnp.exp(m2 - m1)
    den = 1.0 + e
    sel1 = lane == i1
    sel2 = lane == i2
    onehot = jnp.where(sel1, 1.0, jnp.where(sel2, 1.0, 0.0))
    tri = jnp.where(lax.broadcasted_iota(jnp.int32, (TM, TM), 1) < lax.broadcasted_iota(jnp.int32, (TM, TM), 0),
                    1.0, 0.0).astype(BF16)
    rank = jnp.dot(tri, onehot.astype(BF16), preferred_element_type=F32)
    cnt = jnp.sum(onehot, axis=0, keepdims=True)
    nch = jnp.floor((cnt + (MOE_CH - 1)) * (1.0 / MOE_CH))
    upper = jnp.where(lax.broadcasted_iota(jnp.int32, (LANE, LANE), 0) < lax.broadcasted_iota(jnp.int32, (LANE, LANE), 1),
                      1.0, 0.0).astype(BF16)
    cbase = jnp.dot(jnp.broadcast_to(nch, (8, LANE)).astype(BF16), upper,
                    preferred_element_type=F32)[0:1]
    slot = MOE_CH * cbase + rank
    slot0 = jnp.sum(jnp.where(sel1, slot, 0.0), axis=-1, keepdims=True)
    slot1 = jnp.sum(jnp.where(sel2, slot, 0.0), axis=-1, keepdims=True)
    col = jnp.where(lane == 0, slot0, jnp.where(lane == 1, slot1,
                    jnp.where(lane == 2, 1.0 / den, jnp.where(lane == 3, e / den, 0.0))))
    col_ref[...] = col
    row_ref[...] = jnp.transpose(col)[0:PLAN_ROWS]
    nch_ref[0] = nch


def _moe_plan(logits):
    return pl.pallas_call(
        _plan_kernel,
        grid=(NT,),
        in_specs=[pl.BlockSpec((TM, LANE), lambda i: (i, 0))],
        out_specs=[pl.BlockSpec((TM, LANE), lambda i: (i, 0)),
                   pl.BlockSpec((PLAN_ROWS, TM), lambda i: (0, i)),
                   pl.BlockSpec((1, 1, LANE), lambda i: (i, 0, 0))],
        out_shape=[jax.ShapeDtypeStruct((ROWS, LANE), F32),
                   jax.ShapeDtypeStruct((PLAN_ROWS, ROWS), F32),
                   jax.ShapeDtypeStruct((NT, 1, LANE), F32)],
        compiler_params=_params("parallel"),
        name="moe_plan",
    )(logits)


def _moe_tables(nch):
    n = nch[:, 0, :N_EXPERTS].astype(jnp.int32)
    lend = jnp.cumsum(n, axis=1)
    used = lend[:, -1]
    tend = jnp.cumsum(n, axis=0)
    ne = tend[-1]
    ge = (ne + MOE_GC - 1) // MOE_GC
    gend = jnp.cumsum(ge)
    gstart = gend - ge
    nused = gend[-1]
    last = N_EXPERTS - 1
    gexp = jnp.minimum(jnp.sum(jnp.arange(MOE_GROUPS)[:, None] >= gend[None, :], axis=1), last)
    t = jnp.arange(MOE_GROUPS * MOE_GC)
    te = gexp[t // MOE_GC]
    k = jnp.maximum(jnp.minimum(t - gstart[te] * MOE_GC, ne[te] - 1), 0)
    ti = jnp.minimum(jnp.sum(tend.T[te] <= k[:, None], axis=1), NT - 1)
    tab = ti * MOE_NCH + (lend - n)[ti, te] + k - (tend - n)[ti, te]
    c = jnp.arange(NT * MOE_NCH)
    ci, q = c // MOE_NCH, c % MOE_NCH
    qe = jnp.minimum(jnp.sum(lend[ci] <= q[:, None], axis=1), last)
    pos = gstart[qe] * MOE_GC + (tend - n)[ci, qe] + q - (lend - n)[ci, qe]
    pool = jnp.where(q < used[ci], pos, 0)
    i32 = lambda a: a.astype(jnp.int32)
    return i32(tab), i32(gexp), i32(nused.reshape(1)), i32(used), i32(pool)


def _dispatch_kernel(used_ref, h_ref, row_ref, xs_ref):
    used_rows = used_ref[pl.program_id(0)] * MOE_CH
    s0 = row_ref[0:1, :]
    s1 = row_ref[1:2, :]
    for blk in range(MOE_SLOTS // MOE_SB):
        rows = slice(blk * MOE_SB, (blk + 1) * MOE_SB)

        @pl.when(blk * MOE_SB < used_rows)
        def _():
            r = (lax.broadcasted_iota(jnp.int32, (MOE_SB, TM), 0) + blk * MOE_SB).astype(F32)
            s = jnp.where(r == s0, 1.0, jnp.where(r == s1, 1.0, 0.0)).astype(BF16)
            xs_ref[rows, :] = jnp.dot(s, h_ref[...], preferred_element_type=F32).astype(BF16)

        @pl.when(blk * MOE_SB >= used_rows)
        def _():
            xs_ref[rows, :] = jnp.zeros((MOE_SB, D), BF16)


def _moe_dispatch(h, plan_row, used):
    return pl.pallas_call(
        _dispatch_kernel,
        grid_spec=pltpu.PrefetchScalarGridSpec(
            num_scalar_prefetch=1, grid=(NT,),
            in_specs=[pl.BlockSpec((TM, D), lambda i, u: (i, 0)),
                      pl.BlockSpec((PLAN_ROWS, TM), lambda i, u: (0, i))],
            out_specs=pl.BlockSpec((MOE_SLOTS, D), lambda i, u: (i, 0))),
        out_shape=jax.ShapeDtypeStruct((NT * MOE_SLOTS, D), BF16),
        compiler_params=_params("parallel"),
        name="moe_dispatch",
    )(used, h, plan_row)


def _experts_kernel(tab_ref, gexp_ref, nused_ref, *refs):
    x_refs = refs[:MOE_GC]
    w1_ref, w3_ref, w2_ref, y_ref, x_scr, acc = refs[MOE_GC:]
    f = pl.program_id(1)

    @pl.when(pl.program_id(0) < nused_ref[0])
    def _():
        @pl.when(f == 0)
        def _():
            for j, r in enumerate(x_refs):
                x_scr[j * MOE_CH:(j + 1) * MOE_CH, :] = r[...]
            acc[...] = jnp.zeros_like(acc)

        _swiglu_step(x_scr[...], w1_ref, w3_ref, w2_ref, acc)

        @pl.when(f == MOE_NF - 1)
        def _():
            y_ref[...] = acc[...].astype(BF16)


def _moe_experts(xs, tab, gexp, nused, w1, w3, w2, e0):
    grp = lambda g, nu: jnp.minimum(g, nu[0] - 1)
    ftile = lambda g, f, nu: jnp.where(g < nu[0], f, MOE_NF - 1)
    chunk = lambda j: pl.BlockSpec((MOE_CH, D), lambda g, f, tab, ge, nu: (tab[grp(g, nu) * MOE_GC + j], 0))
    rows = MOE_GC * MOE_CH
    return pl.pallas_call(
        _experts_kernel,
        grid_spec=pltpu.PrefetchScalarGridSpec(
            num_scalar_prefetch=3, grid=(MOE_GROUPS, MOE_NF),
            in_specs=[chunk(j) for j in range(MOE_GC)] + [
                pl.BlockSpec((1, D, MOE_TF), lambda g, f, tab, ge, nu: (e0 + ge[grp(g, nu)], 0, ftile(g, f, nu))),
                pl.BlockSpec((1, D, MOE_TF), lambda g, f, tab, ge, nu: (e0 + ge[grp(g, nu)], 0, ftile(g, f, nu))),
                pl.BlockSpec((1, MOE_TF, D), lambda g, f, tab, ge, nu: (e0 + ge[grp(g, nu)], ftile(g, f, nu), 0))],
            out_specs=pl.BlockSpec((rows, D), lambda g, f, tab, ge, nu: (grp(g, nu), 0)),
            scratch_shapes=[pltpu.VMEM((rows, D), BF16), pltpu.VMEM((rows, D), F32)]),
        out_shape=jax.ShapeDtypeStruct((MOE_GROUPS * rows, D), BF16),
        compiler_params=_params("arbitrary", "arbitrary"),
        name="moe_experts",
    )(tab, gexp, nused, *([xs] * MOE_GC), w1, w3, w2)


def _combine_kernel(used_ref, pool_ref, *refs, final):
    refs = list(refs)
    y_refs = [refs.pop(0) for _ in range(MOE_NCH)]
    col_ref, x_ref, mod_ref = refs[:3]
    fg_ref = refs[3] if final else None
    o_ref, y_scr, acc = refs[-3:]
    used = used_ref[pl.program_id(0)]
    col = col_ref[...]
    s0, s1, g0, g1 = col[:, 0:1], col[:, 1:2], col[:, 2:3], col[:, 3:4]
    acc[...] = jnp.zeros_like(acc)
    per_blk = MOE_SB // MOE_CH
    for blk in range(MOE_SLOTS // MOE_SB):
        @pl.when(blk * per_blk < used)
        def _():
            for q in range(blk * per_blk, (blk + 1) * per_blk):
                y_scr[q * MOE_CH:(q + 1) * MOE_CH, :] = y_refs[q][...]
            c = (lax.broadcasted_iota(jnp.int32, (TM, MOE_SB), 1) + blk * MOE_SB).astype(F32)
            w = jnp.where(c == s0, g0, jnp.where(c == s1, g1, 0.0)).astype(BF16)
            acc[...] += jnp.dot(w, y_scr[blk * MOE_SB:(blk + 1) * MOE_SB, :], preferred_element_type=F32)
    o_ref[...] = _residual_out(x_ref[...], mod_ref[0][:, 5 * D:6 * D], acc[...], fg_ref)


def _moe_combine(y, pool, used, plan_col, x, mod_l, final_g=None):
    final = final_g is not None
    rows = LAT_ROWS if final else ROWS
    tile = lambda w: pl.BlockSpec((TM, w), lambda i, u, pc: (i, 0))
    chunk = lambda q: pl.BlockSpec((MOE_CH, D), lambda i, u, pc: (pc[i * MOE_NCH + q], 0))
    in_specs = [chunk(q) for q in range(MOE_NCH)] + [
        tile(LANE), tile(D), pl.BlockSpec((1, 1, N_MOD * D), lambda i, u, pc: (_mod_row(i), 0, 0))]
    args = [y] * MOE_NCH + [plan_col, x, mod_l]
    if final:
        in_specs.append(pl.BlockSpec((1, D), lambda i, u, pc: (0, 0)))
        args.append(final_g.reshape(1, D))
    return pl.pallas_call(
        functools.partial(_combine_kernel, final=final),
        grid_spec=pltpu.PrefetchScalarGridSpec(
            num_scalar_prefetch=2, grid=(rows // TM,),
            in_specs=in_specs, out_specs=tile(D),
            scratch_shapes=[pltpu.VMEM((MOE_SLOTS, D), BF16), pltpu.VMEM((TM, D), F32)]),
        out_shape=jax.ShapeDtypeStruct((rows, D), F32),
        compiler_params=_params("parallel"),
        name="moe_combine",
    )(used, pool, *args)


def _moe_mixer(h, logits, x, mod_l, w1, w3, w2, e0, final_g=None):
    plan_col, plan_row, nch = _moe_plan(logits)
    tab, gexp, nused, used, pool = _moe_tables(nch)
    xs = _moe_dispatch(h, plan_row, used)
    y = _moe_experts(xs, tab, gexp, nused, w1, w3, w2, e0)
    return _moe_combine(y, pool, used, plan_col, x, mod_l, final_g)


def kernel(x, c, ctx, c_ctx, w_mod, b_mod, norm1_g, norm2_g, w_in, rpb, hy_conv_w, hy_conv_b, hy_w1, hy_b1, hy_w2, hy_b2, hy_w3, hy_freq, hy_bias, w_pa, w_pb, w_o, ffn_w1, ffn_w3, ffn_w2, moe_router, moe_w1, moe_w3, moe_w2, final_g):
    xs = jnp.concatenate([x.reshape(LAT_ROWS, D), ctx.reshape(CTX_ROWS, D)], axis=0)
    mod = _modulation(c, c_ctx, w_mod, b_mod)
    tabs = _fft_tables()
    ctabs = _ctx_tables()
    filt_lat = _implicit_filters(SEQ, hy_w1, hy_b1, hy_w2, hy_b2, hy_w3, hy_freq)
    filt_ctx = _implicit_filters(CTX, hy_w1, hy_b1, hy_w2, hy_b2, hy_w3, hy_freq)
    spec_re, spec_im = _filter_spectra(filt_lat, tabs)
    nblk = HY_W // LANE

    moe_w = [w.reshape((-1,) + w.shape[2:]).astype(BF16) for w in (moe_w1, moe_w3, moe_w2)]
    ffn_w = [w.astype(BF16) for w in (ffn_w1, ffn_w3, ffn_w2)]
    w_in_b = w_in.astype(BF16)
    bias_tabs = _bias_tables(rpb)

    for l in range(DEPTH):
        last = l == DEPTH - 1
        mod_l = mod[l].reshape(MOD_ROWS, 1, N_MOD * D)
        cw, cb = hy_conv_w[l], hy_conv_b[l].reshape(1, 3 * HY_W)
        p = _in_proj(xs, mod_l, norm1_g[l].reshape(1, D), w_in_b, l)

        o_att = _context_attention(p, _neighbourhood_attention(p, bias_tabs, l))
        z = _hyena_conv(p, COL_HV, p, COL_HX1, cw, cb, 0, nblk, spec_re, spec_im, l, 0,
                        hy_bias[l], tabs, True)
        o_hy = _hyena_conv(z, 0, p, COL_HX2, cw, cb, 0, 2 * nblk, spec_re, spec_im, l, 1,
                           hy_bias[l], tabs, False)
        o_hy = _ctx_hyena(p, cw, cb, filt_ctx, l, hy_bias[l], ctabs, o_hy)

        i = l // 2
        router = moe_router[i] if l % 2 else None
        res = _merge(o_att, o_hy, p, xs, mod_l, norm2_g[l].reshape(1, D), w_pa, w_pb, w_o, l, router)
        fg = final_g if last else None
        if l % 2 == 0:
            xs = _dense_mixer(res[1], res[0], mod_l, *ffn_w, i, final_g=fg)
        else:
            xs = _moe_mixer(res[1], res[2], res[0], mod_l, *moe_w, i * N_EXPERTS, final_g=fg)
    return xs.reshape(NB, SEQ, D)
```

```python
import functools
import math

import numpy as np
import jax
import jax.numpy as jnp
from jax import lax
from jax.experimental import pallas as pl
from jax.experimental.pallas import tpu as pltpu

F32 = jnp.float32
BF16 = jnp.bfloat16
HIGHEST = lax.Precision.HIGHEST

D = 1024
NB = 4
SEQ = 4096
DEPTH = 4
CTX = 256
GRID_W = 64
GRID_H = SEQ // GRID_W
N_HEADS = 8
HEAD_DIM = 64
ATT_W = N_HEADS * HEAD_DIM
WIN_ROWS = 8
WIN_COLS = 16
HY_W = 512
HY_ORDER = 2
HY_SHORT = 3
HY_EMB = 33
HY_BANDS = (HY_EMB - 1) // 2
HY_FILT = 64
HY_DECAY_TARGET = 1e-2
HY_MAX_DECAY = math.log(HY_DECAY_TARGET) / 0.3
HY_MIN_DECAY = math.log(HY_DECAY_TARGET) / 1.5
HY_DECAY_SHIFT = 0.05
PROJ_W = 3 * ATT_W + 3 * HY_W + 2 * D
D_FF = 2816
N_EXPERTS = 8
N_MOD = 6
EPS = 1e-6

LAT_ROWS = NB * SEQ
CTX_ROWS = NB * CTX
ROWS = LAT_ROWS + CTX_ROWS
MOD_ROWS = 8
CTX_MOD_ROW = NB

LANE = 128
TM = 1024
VMEM_LIMIT = 56 * 1024 * 1024

COL_Q, COL_K, COL_V = 0, ATT_W // LANE, 2 * ATT_W // LANE
COL_HV = 3 * ATT_W // LANE
COL_HX1 = COL_HV + HY_W // LANE
COL_HX2 = COL_HX1 + HY_W // LANE
COL_GA = (3 * ATT_W + 3 * HY_W) // D
COL_GB = COL_GA + 1


def _mod_row(i):
    return jnp.where(i < LAT_ROWS // TM, (i * TM) // SEQ, CTX_MOD_ROW)


def _params(*sem):
    return pltpu.CompilerParams(dimension_semantics=sem, vmem_limit_bytes=VMEM_LIMIT)


def _mod_kernel(s_ref, w_ref, b_ref, o_ref):
    s = s_ref[...]
    s = s * jax.nn.sigmoid(s)
    o_ref[0] = jnp.dot(s, w_ref[0], precision=HIGHEST, preferred_element_type=F32) + b_ref[0]


def _modulation(c, c_ctx, w_mod, b_mod):
    s = jnp.zeros((MOD_ROWS, D), F32).at[:NB].set(c).at[CTX_MOD_ROW].set(c_ctx)
    tn = 1536
    return pl.pallas_call(
        _mod_kernel,
        grid=(DEPTH, N_MOD * D // tn),
        in_specs=[pl.BlockSpec((MOD_ROWS, D), lambda l, j: (0, 0)),
                  pl.BlockSpec((1, D, tn), lambda l, j: (l, 0, j)),
                  pl.BlockSpec((1, 1, tn), lambda l, j: (l, 0, j))],
        out_specs=pl.BlockSpec((1, MOD_ROWS, tn), lambda l, j: (l, 0, j)),
        out_shape=jax.ShapeDtypeStruct((DEPTH, MOD_ROWS, N_MOD * D), F32),
        compiler_params=_params("parallel", "parallel"),
        name="modulation",
    )(s, w_mod, b_mod.reshape(DEPTH, 1, N_MOD * D))


def _dot_hilo(a, b):
    ah, bh = a.astype(BF16), b.astype(BF16)
    al, bl = (a - ah.astype(F32)).astype(BF16), (b - bh.astype(F32)).astype(BF16)
    dot = functools.partial(jnp.dot, preferred_element_type=F32)
    return dot(ah, bh) + (dot(ah, bl) + dot(al, bh))


def _norm_mod(x, g, shift, scale):
    y = x * lax.rsqrt(jnp.mean(x * x, axis=-1, keepdims=True) + EPS) * g
    return y * (1.0 + scale) + shift


def _inproj_kernel(x_ref, mod_ref, g_ref, w_ref, o_ref, h_scr):
    @pl.when(pl.program_id(1) == 0)
    def _():
        m = mod_ref[0]
        h = _norm_mod(x_ref[...], g_ref[...], m[:, 0:D], m[:, D:2 * D])
        h_scr[...] = h.astype(BF16)

    o_ref[...] = jnp.dot(h_scr[...], w_ref[0].astype(BF16),
                         preferred_element_type=F32).astype(BF16)


def _in_proj(x, mod_l, g, w, layer):
    tn = 1024
    return pl.pallas_call(
        _inproj_kernel,
        grid=(ROWS // TM, PROJ_W // tn),
        in_specs=[pl.BlockSpec((TM, D), lambda i, j: (i, 0)),
                  pl.BlockSpec((1, 1, N_MOD * D), lambda i, j: (_mod_row(i), 0, 0)),
                  pl.BlockSpec((1, D), lambda i, j: (0, 0)),
                  pl.BlockSpec((1, D, tn), lambda i, j: (layer, 0, j))],
        out_specs=pl.BlockSpec((TM, tn), lambda i, j: (i, j)),
        out_shape=jax.ShapeDtypeStruct((ROWS, PROJ_W), BF16),
        scratch_shapes=[pltpu.VMEM((TM, D), BF16)],
        compiler_params=_params("parallel", "arbitrary"),
        name="in_proj",
    )(x, mod_l, g, w)


NEG = -1e30


def _nt_dot(a, b):
    return lax.dot_general(a, b, (((1,), (1,)), ((), ())), preferred_element_type=F32)


def _attend_pair(qp, keys, values, biases):
    lane = lax.broadcasted_iota(jnp.int32, qp.shape, 1)
    qs = qp * jnp.asarray(HEAD_DIM ** -0.5, BF16)
    outs = []
    for a in range(2):
        sel = (lane < HEAD_DIM) if a == 0 else (lane >= HEAD_DIM)
        qa = jnp.where(sel, qs, jnp.zeros_like(qs))
        ss = []
        for k, b in zip(keys, biases):
            s = _nt_dot(qa, k)
            if b is not None:
                s = s + b[a]
            ss.append(s)
        m = functools.reduce(jnp.maximum, [jnp.max(s, axis=-1, keepdims=True) for s in ss])
        ps = [jnp.exp(s - m) for s in ss]
        den = functools.reduce(jnp.add, [jnp.sum(p, axis=-1, keepdims=True) for p in ps])
        o = functools.reduce(jnp.add, [jnp.dot(p.astype(BF16), v, preferred_element_type=F32)
                                       for p, v in zip(ps, values)])
        outs.append(o / den)
    return jnp.where(lane < HEAD_DIM, outs[0], outs[1])


NLOC = WIN_ROWS * GRID_W
NKEY = NLOC + CTX
NA_PAIRS = GRID_H // 2
NA_STEPS = NA_PAIRS + 2
NA_UNITS = 2 * N_HEADS


def _window_start(r):
    r0 = jnp.clip(r - WIN_ROWS // 2, 0, GRID_H - WIN_ROWS)
    return pl.multiple_of(r0 * GRID_W, GRID_W)


def _natt_kernel(q_ref, k_ref, v_ref, kc_ref, vc_ref, bt0_ref, bt1_ref, o_ref, s0, s1, p0, p1, l0, l1):
    j = pl.program_id(1)
    lane = lax.broadcasted_iota(jnp.int32, (GRID_W, LANE), 1)
    lo = lane < HEAD_DIM
    hi = lane >= HEAD_DIM

    @pl.when(j == 0)
    def _():
        s1[...] = jnp.zeros_like(s1)
        p0[...] = jnp.zeros_like(p0)
        p1[...] = jnp.zeros_like(p1)
        l0[...] = jnp.ones_like(l0)
        l1[...] = jnp.ones_like(l1)

    def step(s_cur, s_prev, p_cur, p_prev, l_cur, l_prev):
        ro = jnp.maximum(2 * j - 4, 0)
        rs = jnp.minimum(2 * j, GRID_H - 2)
        for half, bt_ref in enumerate((bt0_ref, bt1_ref)):
            start_o = _window_start(ro + half)
            start_s = _window_start(rs + half)
            rows = slice(half * GRID_W, (half + 1) * GRID_W)
            for hp in range(N_HEADS // 2):
                cs = slice(hp * LANE, (hp + 1) * LANE)
                u0 = half * N_HEADS + 2 * hp
                units = (u0, u0 + 1)
                pair = slice(u0, u0 + 2)
                m2 = 2 * GRID_W

                p2 = p_cur[pair].reshape(m2, NKEY)
                acc = (jnp.dot(p2[:, 0:NLOC], v_ref[pl.ds(start_o, NLOC), cs], preferred_element_type=F32)
                       + jnp.dot(p2[:, NLOC:NKEY], vc_ref[:, cs], preferred_element_type=F32))
                o_ref[rows, cs] = jnp.where(lo, acc[0:GRID_W] / l_cur[u0],
                                            acc[GRID_W:m2] / l_cur[u0 + 1]).astype(BF16)

                for u in units:
                    x = s_prev[u]
                    e = jnp.exp(x - jnp.max(x, axis=-1, keepdims=True))
                    l_prev[u] = jnp.broadcast_to(jnp.sum(e, axis=-1, keepdims=True), (GRID_W, LANE))
                    p_prev[u] = e.astype(BF16)

                qs = q_ref[rows, cs] * jnp.asarray(HEAD_DIM ** -0.5, BF16)
                zero = jnp.zeros_like(qs)
                q2 = jnp.concatenate([jnp.where(lo, qs, zero), jnp.where(hi, qs, zero)], axis=0)
                bias = bt_ref[0, 2 * hp:2 * hp + 2].astype(F32)
                s_cur[pair, :, 0:NLOC] = _nt_dot(q2, k_ref[pl.ds(start_s, NLOC), cs]).reshape(2, GRID_W, NLOC) + bias
                s_cur[pair, :, NLOC:NKEY] = _nt_dot(q2, kc_ref[:, cs]).reshape(2, GRID_W, CTX)

    @pl.when(j % 2 == 0)
    def _():
        step(s0, s1, p0, p1, l0, l1)

    @pl.when(j % 2 == 1)
    def _():
        step(s1, s0, p1, p0, l1, l0)


def _bias_tables(rpb):
    col = np.arange(GRID_W)
    c0 = np.clip(col - WIN_COLS // 2, 0, GRID_W - WIN_COLS)
    valid = (col[None, :] >= c0[:, None]) & (col[None, :] < c0[:, None] + WIN_COLS)
    dc = np.clip(col[None, :] - col[:, None] + (WIN_COLS - 1), 0, 2 * WIN_COLS - 2)
    pick = (np.arange(2 * WIN_COLS - 1)[:, None, None] == dc[None]).astype(np.float32)
    t = jnp.einsum('lhrd,dqk->lhrqk', rpb, jnp.asarray(pick), precision=HIGHEST)
    t = jnp.where(valid, t, NEG)
    win = jnp.stack([t[:, :, d0:d0 + WIN_ROWS] for d0 in range(WIN_ROWS)], axis=1)
    win = win.transpose(0, 1, 2, 4, 3, 5)
    return win.reshape(DEPTH * WIN_ROWS, N_HEADS, GRID_W, NLOC).astype(BF16)


def _bias_index(r):
    r0 = jnp.clip(r - WIN_ROWS // 2, 0, GRID_H - WIN_ROWS)
    return r0 - r + (WIN_ROWS - 1)


def _neighbourhood_attention(p, bt, layer):
    blk = GRID_H // 2
    ctx_blk0 = LAT_ROWS // CTX
    score_row = lambda j, half: jnp.minimum(2 * j, GRID_H - 2) + half
    bias_spec = lambda half: pl.BlockSpec(
        (1, N_HEADS, GRID_W, NLOC), lambda b, j: (layer * WIN_ROWS + _bias_index(score_row(j, half)), 0, 0, 0))
    return pl.pallas_call(
        _natt_kernel,
        grid=(NB, NA_STEPS),
        in_specs=[pl.BlockSpec((2 * GRID_W, ATT_W), lambda b, j: (b * blk + jnp.minimum(j, blk - 1), 0)),
                  pl.BlockSpec((SEQ, ATT_W), lambda b, j: (b, 1)),
                  pl.BlockSpec((SEQ, ATT_W), lambda b, j: (b, 2)),
                  pl.BlockSpec((CTX, ATT_W), lambda b, j: (ctx_blk0 + b, 1)),
                  pl.BlockSpec((CTX, ATT_W), lambda b, j: (ctx_blk0 + b, 2)),
                  bias_spec(0), bias_spec(1)],
        out_specs=pl.BlockSpec((2 * GRID_W, ATT_W), lambda b, j: (b * blk + jnp.maximum(j - 2, 0), 0)),
        out_shape=jax.ShapeDtypeStruct((ROWS, ATT_W), BF16),
        scratch_shapes=[pltpu.VMEM((NA_UNITS, GRID_W, NKEY), F32)] * 2
                       + [pltpu.VMEM((NA_UNITS, GRID_W, NKEY), BF16)] * 2
                       + [pltpu.VMEM((NA_UNITS, GRID_W, LANE), F32)] * 2,
        compiler_params=_params("parallel", "arbitrary"),
        name="neighbourhood_attention",
    )(p, p, p, p, p, bt, bt)


def _catt_kernel(q_ref, k_ref, v_ref, dst_ref, o_ref):
    del dst_ref
    for hp in range(N_HEADS // 2):
        cs = slice(hp * LANE, (hp + 1) * LANE)
        o = _attend_pair(q_ref[:, cs], [k_ref[:, cs]], [v_ref[:, cs]], [None])
        o_ref[:, cs] = o.astype(BF16)


def _context_attention(p, o_att):
    ctx_blk0 = LAT_ROWS // CTX
    return pl.pallas_call(
        _catt_kernel,
        grid=(NB,),
        in_specs=[pl.BlockSpec((CTX, ATT_W), lambda b: (ctx_blk0 + b, 0)),
                  pl.BlockSpec((CTX, ATT_W), lambda b: (ctx_blk0 + b, 1)),
                  pl.BlockSpec((CTX, ATT_W), lambda b: (ctx_blk0 + b, 2)),
                  pl.BlockSpec(memory_space=pl.ANY)],
        out_specs=pl.BlockSpec((CTX, ATT_W), lambda b: (ctx_blk0 + b, 0)),
        out_shape=jax.ShapeDtypeStruct((ROWS, ATT_W), BF16),
        input_output_aliases={3: 0},
        compiler_params=_params("parallel"),
        name="context_attention",
    )(p, p, p, o_att)


def _filter_features(L):
    pos = np.arange(L, dtype=np.float64)
    t = pos / max(L - 1, 1)
    bands = np.linspace(1e-4, HY_BANDS - 1, HY_BANDS)
    ang = (2.0 * math.pi / L) * pos[:, None] * bands[None, :]
    z = np.concatenate([t[:, None], np.cos(ang), -np.sin(ang)], axis=-1)
    zp = np.zeros((L, LANE), np.float32)
    zp[:, :HY_EMB] = z
    return zp


def _filter_kernel(z_ref, w1_ref, b1_ref, w2_ref, b2_ref, w3_ref, fr_ref, rate_ref, o_ref, *, L, tl):
    dot = functools.partial(jnp.dot, precision=HIGHEST, preferred_element_type=F32)
    fr = fr_ref[0]
    a = jnp.sin(fr[0:1] * (dot(z_ref[...], w1_ref[0]) + b1_ref[0]))
    a = jnp.sin(fr[1:2] * (dot(a, w2_ref[0]) + b2_ref[0]))
    h = _dot_hilo(a, w3_ref[0])
    pos = (pl.program_id(1) * tl + lax.broadcasted_iota(jnp.int32, (tl, 1), 0)).astype(F32)
    dist = jnp.abs(pos - float(L // 2)) * (2.0 / L)
    o_ref[0] = h * (jnp.exp(-dist * rate_ref[...]) + HY_DECAY_SHIFT)


def _implicit_filters(L, w1, b1, w2, b2, w3, freq):
    tl = min(L, 512)
    z = jnp.asarray(_filter_features(L))
    w1p = jnp.zeros((DEPTH, LANE, HY_FILT), F32).at[:, :HY_EMB].set(w1)
    rates = np.abs(np.linspace(HY_MIN_DECAY, HY_MAX_DECAY, HY_ORDER * HY_W)).astype(np.float32)
    cw = HY_ORDER * HY_W
    return pl.pallas_call(
        functools.partial(_filter_kernel, L=L, tl=tl),
        grid=(DEPTH, L // tl),
        in_specs=[pl.BlockSpec((tl, LANE), lambda l, i: (i, 0)),
                  pl.BlockSpec((1, LANE, HY_FILT), lambda l, i: (l, 0, 0)),
                  pl.BlockSpec((1, 1, HY_FILT), lambda l, i: (l, 0, 0)),
                  pl.BlockSpec((1, HY_FILT, HY_FILT), lambda l, i: (l, 0, 0)),
                  pl.BlockSpec((1, 1, HY_FILT), lambda l, i: (l, 0, 0)),
                  pl.BlockSpec((1, HY_FILT, cw), lambda l, i: (l, 0, 0)),
                  pl.BlockSpec((1, 2, HY_FILT), lambda l, i: (l, 0, 0)),
                  pl.BlockSpec((1, cw), lambda l, i: (0, 0))],
        out_specs=pl.BlockSpec((1, tl, cw), lambda l, i: (l, i, 0)),
        out_shape=jax.ShapeDtypeStruct((DEPTH, L, cw), F32),
        compiler_params=_params("parallel", "parallel"),
        name=f"implicit_filters_{L}",
    )(z, w1p, b1.reshape(DEPTH, 1, HY_FILT), w2, b2.reshape(DEPTH, 1, HY_FILT), w3, freq,
      jnp.asarray(rates).reshape(1, cw))


FFT_N = 2 * SEQ
FA = 64
FB = 128
FA_IN = SEQ // FB
PITCH = 136
HALO = 8
UNROLL_ROWS = 64
UNROLL_SLABS = 32


def _fft_tables():
    a = np.arange(FA_IN)
    r = np.arange(FA)
    b = np.arange(FB)
    q = np.arange(FB)
    ph = a[None, None, :] * r[None, :, None] / FA + b[:, None, None] * r[None, :, None] / FFT_N
    g1 = np.exp(-2j * np.pi * ph)
    g1 = np.concatenate([g1.real, g1.imag], axis=1)
    f = np.exp(-2j * np.pi * q[:, None] * b[None, :] / FB)
    f2 = np.concatenate([f.real, f.imag], axis=0)
    f2c = np.concatenate([f.real, -f.imag], axis=0)
    ao = np.arange(FA_IN) + (SEQ // 2) // FB
    ph2 = ao[None, :, None] * r[None, None, :] / FA + b[:, None, None] * r[None, None, :] / FFT_N
    g2 = np.exp(2j * np.pi * ph2) / FFT_N
    g2 = np.concatenate([g2.real, g2.imag], axis=1)
    return tuple(jnp.asarray(t, dtype=BF16) for t in (g1, f2, f2c, g2))


def _cplx(out, m, n):
    re = out[0:m, 0:n] - out[m:2 * m, n:2 * n]
    im = out[0:m, n:2 * n] + out[m:2 * m, 0:n]
    return re, im


def _short_conv_slabs(src_ref, rows, t_scr, cw_ref, cb_ref, apply_conv, emit):
    nslab = rows // FB
    if not apply_conv:
        for a in range(nslab):
            emit(a, src_ref[a * FB:(a + 1) * FB, :].astype(F32))
        return
    zero = jnp.zeros((HALO, LANE), F32)
    t_scr[0:HALO, :] = zero
    t_scr[HALO + rows:2 * HALO + rows, :] = zero
    for a in range(nslab):
        t_scr[HALO + a * FB:HALO + (a + 1) * FB, :] = src_ref[a * FB:(a + 1) * FB, :].astype(F32)
    w0, w1, w2, bias = cw_ref[0:1, :], cw_ref[1:2, :], cw_ref[2:3, :], cb_ref[...]
    for a in range(nslab):
        o = HALO + a * FB
        y = (w0 * t_scr[o - 1:o - 1 + FB, :] + w1 * t_scr[o:o + FB, :]
             + w2 * t_scr[o + 1:o + 1 + FB, :] + bias)
        emit(a, y)


def _fwd_stage1(b, s_refs, g1_ref, y_r, y_i):
    cols = [s[pl.ds(b, FA_IN, stride=PITCH), :].astype(BF16) for s in s_refs]
    rhs = cols[0] if len(cols) == 1 else jnp.concatenate(cols, axis=1)
    out = jnp.dot(g1_ref[b], rhs, preferred_element_type=F32)
    if len(cols) == 1:
        re, im = out[0:FA], out[FA:2 * FA]
    else:
        re, im = _cplx(out, FA, LANE)
    y_r[pl.ds(b, FA, stride=PITCH), :] = re
    y_i[pl.ds(b, FA, stride=PITCH), :] = im


def _fwd_stage2(r, y_r, y_i, f_ref):
    o = pl.multiple_of(r * PITCH, 8)
    rhs = jnp.concatenate([y_r[pl.ds(o, FB), :].astype(BF16), y_i[pl.ds(o, FB), :].astype(BF16)], axis=1)
    out = jnp.dot(f_ref[...], rhs, preferred_element_type=F32)
    return _cplx(out, FB, LANE)


def _spectrum_kernel(h_ref, g1_ref, f_ref, hr_ref, hi_ref, s_scr, y_r, y_i):
    for a in range(FA_IN):
        s_scr[a * PITCH:a * PITCH + FB, :] = h_ref[0, a * FB:(a + 1) * FB, :]

    def s1(b, c):
        _fwd_stage1(b, [s_scr], g1_ref, y_r, y_i)
        return c
    lax.fori_loop(0, FB, s1, 0, unroll=UNROLL_ROWS)

    def s2(r, c):
        xr, xi = _fwd_stage2(r, y_r, y_i, f_ref)
        o = pl.multiple_of(r * FB, FB)
        hr_ref[0, pl.ds(o, FB), :] = xr
        hi_ref[0, pl.ds(o, FB), :] = xi
        return c
    lax.fori_loop(0, FA, s2, 0, unroll=UNROLL_SLABS)


def _filter_spectra(h, tabs):
    g1, f2, _, _ = tabs
    cw = HY_ORDER * HY_W
    spec = pl.BlockSpec((1, FFT_N, LANE), lambda l, c: (l, 0, c))
    return pl.pallas_call(
        _spectrum_kernel,
        grid=(DEPTH, cw // LANE),
        in_specs=[pl.BlockSpec((1, SEQ, LANE), lambda l, c: (l, 0, c)),
                  pl.BlockSpec(g1.shape, lambda l, c: (0, 0, 0)),
                  pl.BlockSpec(f2.shape, lambda l, c: (0, 0))],
        out_specs=[spec, spec],
        out_shape=[jax.ShapeDtypeStruct((DEPTH, FFT_N, cw), F32)] * 2,
        scratch_shapes=[pltpu.VMEM((FA_IN * PITCH, LANE), F32),
                        pltpu.VMEM((FA * PITCH, LANE), F32),
                        pltpu.VMEM((FA * PITCH, LANE), F32)],
        compiler_params=_params("parallel", "parallel"),
        name="filter_spectra",
    )(h, g1, f2)


def _hyconv_kernel(u_ref, x_ref, cwu_ref, cbu_ref, cwx_ref, cbx_ref,
                   hr_ref, hi_ref, bias_ref, g1_ref, f_ref, fc_ref, g2_ref,
                   o_ref, t_scr, s_r, s_i, y_r, y_i, *, conv_input):
    halves = [(s_r, slice(0, SEQ)), (s_i, slice(SEQ, 2 * SEQ))]

    def fill(s):
        def emit(a, slab):
            s[a * PITCH:a * PITCH + FB, :] = slab
        return emit
    for s, rows in halves:
        _short_conv_slabs(u_ref.at[rows, :], SEQ, t_scr, cwu_ref, cbu_ref, conv_input, fill(s))

    def s1(b, c):
        _fwd_stage1(b, [s_r, s_i], g1_ref, y_r, y_i)
        return c
    lax.fori_loop(0, FB, s1, 0, unroll=UNROLL_ROWS)

    def s2(r, c):
        xr, xi = _fwd_stage2(r, y_r, y_i, f_ref)
        oh = pl.multiple_of(r * FB, FB)
        hr = hr_ref[0, pl.ds(oh, FB), :]
        hi = hi_ref[0, pl.ds(oh, FB), :]
        vr = xr * hr - xi * hi
        vi = xr * hi + xi * hr
        rhs = jnp.concatenate([vr.astype(BF16), vi.astype(BF16)], axis=1)
        wr, wi = _cplx(jnp.dot(fc_ref[...], rhs, preferred_element_type=F32), FB, LANE)
        o = pl.multiple_of(r * PITCH, 8)
        y_r[pl.ds(o, FB), :] = wr
        y_i[pl.ds(o, FB), :] = wi
        return c
    lax.fori_loop(0, FA, s2, 0, unroll=UNROLL_SLABS)

    bias = bias_ref[0]

    def s3(b, c):
        rhs = jnp.concatenate([y_r[pl.ds(b, FA, stride=PITCH), :].astype(BF16),
                               y_i[pl.ds(b, FA, stride=PITCH), :].astype(BF16)], axis=1)
        re, im = _cplx(jnp.dot(g2_ref[b], rhs, preferred_element_type=F32), FA_IN, LANE)
        idx = pl.ds(b, FA_IN, stride=PITCH)
        s_r[idx, :] = re + s_r[idx, :] * bias
        s_i[idx, :] = im + s_i[idx, :] * bias
        return c
    lax.fori_loop(0, FB, s3, 0, unroll=UNROLL_ROWS)

    for s, rows in halves:
        def emit(a, slab, s=s, out=o_ref.at[rows, :]):
            out[a * FB:(a + 1) * FB, :] = (slab * s[a * PITCH:a * PITCH + FB, :]).astype(out.dtype)
        _short_conv_slabs(x_ref.at[rows, :], SEQ, t_scr, cwx_ref, cbx_ref, True, emit)


def _hyena_conv(u, u_col, x, x_col, conv_w, conv_b, cu_col, cx_col, spec_re, spec_im, layer, order,
                hy_bias_l, tabs, conv_input):
    g1, f2, f2c, g2 = tabs
    nblk = HY_W // LANE
    tok = lambda col: pl.BlockSpec((2 * SEQ, LANE), lambda c, p: (p, col + c))
    chan = lambda rows, col: pl.BlockSpec((rows, LANE), lambda c, p: (0, col + c))
    hspec = pl.BlockSpec((1, FFT_N, LANE), lambda c, p: (layer, 0, order * nblk + c))
    const = lambda shape: pl.BlockSpec(shape, lambda c, p: (0,) * len(shape))
    return pl.pallas_call(
        functools.partial(_hyconv_kernel, conv_input=conv_input),
        grid=(nblk, NB // 2),
        in_specs=[tok(u_col), tok(x_col),
                  chan(HY_SHORT, cu_col), chan(1, cu_col), chan(HY_SHORT, cx_col), chan(1, cx_col),
                  hspec, hspec,
                  pl.BlockSpec((1, 1, LANE), lambda c, p: (order, 0, c)),
                  const(g1.shape), const(f2.shape), const(f2c.shape), const(g2.shape)],
        out_specs=pl.BlockSpec((2 * SEQ, LANE), lambda c, p: (p, c)),
        out_shape=jax.ShapeDtypeStruct((ROWS, HY_W), BF16),
        scratch_shapes=[pltpu.VMEM((SEQ + 2 * HALO, LANE), F32),
                        pltpu.VMEM((FA_IN * PITCH, LANE), F32),
                        pltpu.VMEM((FA_IN * PITCH, LANE), F32),
                        pltpu.VMEM((FA * PITCH, LANE), F32),
                        pltpu.VMEM((FA * PITCH, LANE), F32)],
        compiler_params=_params("arbitrary", "arbitrary"),
        name=f"hyena_conv{order}",
    )(u, x, conv_w, conv_b, conv_w, conv_b, spec_re, spec_im,
      hy_bias_l.reshape(HY_ORDER, 1, HY_W), g1, f2, f2c, g2)


CN = 2 * CTX


def _ctx_tables():
    k = np.arange(CN)
    n = np.arange(CTX)
    f = np.exp(-2j * np.pi * k[:, None] * n[None, :] / CN)
    fwd = np.concatenate([f.real, f.imag], axis=0)
    t = np.arange(CTX) + CTX // 2
    g = np.exp(2j * np.pi * t[:, None] * k[None, :] / CN) / CN
    inv = np.concatenate([g.real, g.imag], axis=0)
    return jnp.asarray(fwd, dtype=BF16), jnp.asarray(inv, dtype=BF16)


def _ctx_hyena_kernel(v_ref, x1_ref, x2_ref, cwv_ref, cbv_ref, cw1_ref, cb1_ref, cw2_ref, cb2_ref,
                      h0_ref, h1_ref, bias_ref, fwd_ref, inv_ref, dst_ref, o_ref, t_scr, s_scr, g_scr):
    del dst_ref
    def conv_into(src_ref, b, cw_ref, cb_ref, dst, col):
        view = src_ref.at[b * CTX:(b + 1) * CTX, :]

        def emit(a, slab):
            dst[a * FB:(a + 1) * FB, col * LANE:(col + 1) * LANE] = slab
        _short_conv_slabs(view, CTX, t_scr, cw_ref, cb_ref, True, emit)

    def spectrum(h_ref):
        out = jnp.dot(fwd_ref[...], h_ref[0].astype(BF16), preferred_element_type=F32)
        return out[0:CN], out[CN:2 * CN]

    def long_conv(z2, hr, hi, bias):
        x = jnp.dot(fwd_ref[...], z2.astype(BF16), preferred_element_type=F32)
        xr, xi = _cplx(x, CN, LANE)
        vr = xr * hr - xi * hi
        vi = xr * hi + xi * hr
        rhs = jnp.concatenate([vr.astype(BF16), vi.astype(BF16)], axis=1)
        y = jnp.dot(inv_ref[...], rhs, preferred_element_type=F32)
        yr, yi = _cplx(y, CTX, LANE)
        return jnp.concatenate([yr, yi], axis=1) + z2 * jnp.concatenate([bias, bias], axis=1)

    h0r, h0i = spectrum(h0_ref)
    h1r, h1i = spectrum(h1_ref)
    for p in range(NB // 2):
        for half in range(2):
            conv_into(v_ref, 2 * p + half, cwv_ref, cbv_ref, s_scr, half)
            conv_into(x1_ref, 2 * p + half, cw1_ref, cb1_ref, g_scr, half)
        z = g_scr[...] * long_conv(s_scr[...], h0r, h0i, bias_ref[0])
        for half in range(2):
            conv_into(x2_ref, 2 * p + half, cw2_ref, cb2_ref, g_scr, half)
        y = g_scr[...] * long_conv(z, h1r, h1i, bias_ref[1])
        for half in range(2):
            b = 2 * p + half
            o_ref[b * CTX:(b + 1) * CTX, :] = y[:, half * LANE:(half + 1) * LANE].astype(BF16)


def _ctx_hyena(p, conv_w, conv_b, filt_ctx, layer, hy_bias_l, ctabs, o_hy):
    fwd, inv = ctabs
    nblk = HY_W // LANE
    ctx_blk = LAT_ROWS // CTX_ROWS
    tok = lambda col: pl.BlockSpec((CTX_ROWS, LANE), lambda c: (ctx_blk, col + c))
    chan = lambda rows, col: pl.BlockSpec((rows, LANE), lambda c: (0, col + c))
    hspec = lambda order: pl.BlockSpec((1, CTX, LANE), lambda c: (layer, 0, order * nblk + c))
    const = lambda shape: pl.BlockSpec(shape, lambda c: (0,) * len(shape))
    return pl.pallas_call(
        _ctx_hyena_kernel,
        grid=(nblk,),
        in_specs=[tok(COL_HV), tok(COL_HX1), tok(COL_HX2),
                  chan(HY_SHORT, 0), chan(1, 0), chan(HY_SHORT, nblk), chan(1, nblk),
                  chan(HY_SHORT, 2 * nblk), chan(1, 2 * nblk),
                  hspec(0), hspec(1),
                  pl.BlockSpec((HY_ORDER, 1, LANE), lambda c: (0, 0, c)),
                  const(fwd.shape), const(inv.shape), pl.BlockSpec(memory_space=pl.ANY)],
        out_specs=pl.BlockSpec((CTX_ROWS, LANE), lambda c: (ctx_blk, c)),
        out_shape=jax.ShapeDtypeStruct((ROWS, HY_W), BF16),
        input_output_aliases={14: 0},
        scratch_shapes=[pltpu.VMEM((CTX + 2 * HALO, LANE), F32),
                        pltpu.VMEM((CTX, 2 * LANE), F32),
                        pltpu.VMEM((CTX, 2 * LANE), F32)],
        compiler_params=_params("parallel"),
        name="context_hyena",
    )(p, p, p, conv_w, conv_b, conv_w, conv_b, conv_w, conv_b, filt_ctx, filt_ctx,
      hy_bias_l.reshape(HY_ORDER, 1, HY_W), fwd, inv, o_hy)


def _merge_kernel(oa_ref, oh_ref, ga_ref, gb_ref, x_ref, mod_ref, g2_ref, wpa_ref, wpb_ref, wo_ref,
                  *rest, routed):
    if routed:
        wr_ref, xo_ref, h_ref, rw_ref, wpa_s, wpb_s, wo_s = rest
    else:
        xo_ref, h_ref, wpa_s, wpb_s, wo_s = rest

    @pl.when(pl.program_id(0) == 0)
    def _():
        wpa_s[...] = wpa_ref[0].astype(BF16)
        wpb_s[...] = wpb_ref[0].astype(BF16)
        wo_s[...] = wo_ref[0].astype(BF16)

    a = jnp.dot(oa_ref[...], wpa_s[...], preferred_element_type=F32)
    b = jnp.dot(oh_ref[...], wpb_s[...], preferred_element_type=F32)
    mix = (jax.nn.sigmoid(ga_ref[...].astype(F32)) * a + jax.nn.sigmoid(gb_ref[...].astype(F32)) * b)
    y = jnp.dot(mix.astype(BF16), wo_s[...], preferred_element_type=F32)
    m = mod_ref[0]
    x = x_ref[...] + m[:, 2 * D:3 * D] * y
    xo_ref[...] = x
    h = _norm_mod(x, g2_ref[...], m[:, 3 * D:4 * D], m[:, 4 * D:5 * D])
    h_ref[...] = h.astype(BF16)
    if routed:
        rw_ref[...] = _dot_hilo(h, wr_ref[...])


def _merge(o_att, o_hy, p, x, mod_l, g2, w_pa, w_pb, w_o, layer, router):
    routed = router is not None
    tile = lambda w: pl.BlockSpec((TM, w), lambda i: (i, 0))
    full = lambda a: pl.BlockSpec(a.shape, lambda i: (0,) * a.ndim)
    layered = lambda a: pl.BlockSpec((1,) + a.shape[1:], lambda i: (layer, 0, 0))
    in_specs = [tile(ATT_W), tile(HY_W),
                pl.BlockSpec((TM, D), lambda i: (i, COL_GA)),
                pl.BlockSpec((TM, D), lambda i: (i, COL_GB)),
                tile(D),
                pl.BlockSpec((1, 1, N_MOD * D), lambda i: (_mod_row(i), 0, 0)),
                pl.BlockSpec((1, D), lambda i: (0, 0)),
                layered(w_pa), layered(w_pb), layered(w_o)]
    args = [o_att, o_hy, p, p, x, mod_l, g2, w_pa, w_pb, w_o]
    out_specs = [tile(D), tile(D)]
    out_shape = [jax.ShapeDtypeStruct((ROWS, D), F32), jax.ShapeDtypeStruct((ROWS, D), BF16)]
    if routed:
        wr = jnp.zeros((D, LANE), F32).at[:, :N_EXPERTS].set(router)
        in_specs.append(full(wr))
        args.append(wr)
        out_specs.append(tile(LANE))
        out_shape.append(jax.ShapeDtypeStruct((ROWS, LANE), F32))
    return pl.pallas_call(
        functools.partial(_merge_kernel, routed=routed),
        grid=(ROWS // TM,),
        in_specs=in_specs, out_specs=out_specs, out_shape=out_shape,
        scratch_shapes=[pltpu.VMEM(w_pa.shape[1:], BF16), pltpu.VMEM(w_pb.shape[1:], BF16),
                        pltpu.VMEM(w_o.shape[1:], BF16)],
        compiler_params=_params("arbitrary"),
        name="merge",
    )(*args)


TF = D_FF // 2


NF = D_FF // TF


SWIGLU_SUB = 768


def _swiglu_step(h, w1_ref, w3_ref, w2_ref, acc):
    tf = w1_ref.shape[2]
    for lo in range(0, tf, SWIGLU_SUB):
        hi = min(lo + SWIGLU_SUB, tf)
        a = jnp.dot(h, w1_ref[0, :, lo:hi].astype(BF16), preferred_element_type=F32)
        b = jnp.dot(h, w3_ref[0, :, lo:hi].astype(BF16), preferred_element_type=F32)
        t = a * jax.nn.sigmoid(a) * b
        acc[...] += jnp.dot(t.astype(BF16), w2_ref[0, lo:hi, :].astype(BF16), preferred_element_type=F32)


def _residual_out(x, gate, y, fg_ref):
    x = x + gate * y
    if fg_ref is not None:
        x = x * lax.rsqrt(jnp.mean(x * x, axis=-1, keepdims=True) + EPS) * fg_ref[...]
    return x


def _ffn_kernel(h_ref, x_ref, mod_ref, w1_ref, w3_ref, w2_ref, *rest, final):
    rest = list(rest)
    fg_ref = rest.pop(0) if final else None
    o_ref, acc = rest
    f = pl.program_id(1)

    @pl.when(f == 0)
    def _():
        acc[...] = jnp.zeros_like(acc)

    _swiglu_step(h_ref[...], w1_ref, w3_ref, w2_ref, acc)

    @pl.when(f == NF - 1)
    def _():
        o_ref[...] = _residual_out(x_ref[...], mod_ref[0][:, 5 * D:6 * D], acc[...], fg_ref)


def _dense_mixer(h, x, mod_l, w1, w3, w2, e, final_g=None):
    final = final_g is not None
    rows = LAT_ROWS if final else ROWS
    tile = lambda w: pl.BlockSpec((TM, w), lambda i, f: (i, 0))
    in_specs = [tile(D), tile(D),
                pl.BlockSpec((1, 1, N_MOD * D), lambda i, f: (_mod_row(i), 0, 0)),
                pl.BlockSpec((1, D, TF), lambda i, f: (e, 0, f)),
                pl.BlockSpec((1, D, TF), lambda i, f: (e, 0, f)),
                pl.BlockSpec((1, TF, D), lambda i, f: (e, f, 0))]
    args = [h, x, mod_l, w1, w3, w2]
    if final:
        in_specs.append(pl.BlockSpec((1, D), lambda i, f: (0, 0)))
        args.append(final_g.reshape(1, D))
    return pl.pallas_call(
        functools.partial(_ffn_kernel, final=final),
        grid=(rows // TM, NF),
        in_specs=in_specs,
        out_specs=tile(D),
        out_shape=jax.ShapeDtypeStruct((rows, D), F32),
        scratch_shapes=[pltpu.VMEM((TM, D), F32)],
        compiler_params=_params("parallel", "arbitrary"),
        name="ffn",
    )(*args)


NT = ROWS // TM
MOE_CH = 64
MOE_CMAX = TM // MOE_CH
MOE_NCH = 2 * TM // MOE_CH + N_EXPERTS
MOE_SLOTS = MOE_NCH * MOE_CH
MOE_GC = 16
MOE_GROUPS = NT * MOE_NCH // MOE_GC + N_EXPERTS
MOE_SB = 512
MOE_TF, MOE_NF = TF, NF
PLAN_ROWS = 8


def _plan_kernel(lg_ref, col_ref, row_ref, nch_ref):
    lane = lax.broadcasted_iota(jnp.int32, (TM, LANE), 1)
    ninf = jnp.asarray(-jnp.inf, F32)
    lg = jnp.where(lane < N_EXPERTS, lg_ref[...], ninf)
    m1 = jnp.max(lg, axis=-1, keepdims=True)
    i1 = jnp.min(jnp.where(lg == m1, lane, LANE), axis=-1, keepdims=True)
    lg2 = jnp.where(lane == i1, ninf, lg)
    m2 = jnp.max(lg2, axis=-1, keepdims=True)
    i2 = jnp.min(jnp.where(lg2 == m2, lane, LANE), axis=-1, keepdims=True)
    e = jnp.exp(m2 - m1)
    den = 1.0 + e
    sel1 = lane == i1
    sel2 = lane == i2
    onehot = jnp.where(sel1, 1.0, jnp.where(sel2, 1.0, 0.0))
    tri = jnp.where(lax.broadcasted_iota(jnp.int32, (TM, TM), 1) < lax.broadcasted_iota(jnp.int32, (TM, TM), 0),
                    1.0, 0.0).astype(BF16)
    rank = jnp.dot(tri, onehot.astype(BF16), preferred_element_type=F32)
    cnt = jnp.sum(onehot, axis=0, keepdims=True)
    nch = jnp.floor((cnt + (MOE_CH - 1)) * (1.0 / MOE_CH))
    upper = jnp.where(lax.broadcasted_iota(jnp.int32, (LANE, LANE), 0) < lax.broadcasted_iota(jnp.int32, (LANE, LANE), 1),
                      1.0, 0.0).astype(BF16)
    cbase = jnp.dot(jnp.broadcast_to(nch, (8, LANE)).astype(BF16), upper,
                    preferred_element_type=F32)[0:1]
    slot = MOE_CH * cbase + rank
    slot0 = jnp.sum(jnp.where(sel1, slot, 0.0), axis=-1, keepdims=True)
    slot1 = jnp.sum(jnp.where(sel2, slot, 0.0), axis=-1, keepdims=True)
    col = jnp.where(lane == 0, slot0, jnp.where(lane == 1, slot1,
                    jnp.where(lane == 2, 1.0 / den, jnp.where(lane == 3, e / den, 0.0))))
    col_ref[...] = col
    row_ref[...] = jnp.transpose(col)[0:PLAN_ROWS]
    nch_ref[0] = nch


def _moe_plan(logits):
    return pl.pallas_call(
        _plan_kernel,
        grid=(NT,),
        in_specs=[pl.BlockSpec((TM, LANE), lambda i: (i, 0))],
        out_specs=[pl.BlockSpec((TM, LANE), lambda i: (i, 0)),
                   pl.BlockSpec((PLAN_ROWS, TM), lambda i: (0, i)),
                   pl.BlockSpec((1, 1, LANE), lambda i: (i, 0, 0))],
        out_shape=[jax.ShapeDtypeStruct((ROWS, LANE), F32),
                   jax.ShapeDtypeStruct((PLAN_ROWS, ROWS), F32),
                   jax.ShapeDtypeStruct((NT, 1, LANE), F32)],
        compiler_params=_params("parallel"),
        name="moe_plan",
    )(logits)


def _moe_tables(nch):
    n = nch[:, 0, :N_EXPERTS].astype(jnp.int32)
    cbase = jnp.cumsum(n, axis=1) - n
    used = jnp.sum(n, axis=1)
    c = jnp.arange(MOE_CMAX)
    valid = (c[None, None, :] < n.T[:, :, None]).reshape(N_EXPERTS, -1)
    cid = (jnp.arange(NT)[None, :, None] * MOE_NCH + cbase.T[:, :, None] + c[None, None, :]).reshape(N_EXPERTS, -1)
    ne = jnp.sum(n, axis=0)
    ge = (ne + MOE_GC - 1) // MOE_GC
    gend = jnp.cumsum(ge)
    nused = gend[-1]
    size = MOE_GROUPS * MOE_GC
    pos = (gend - ge)[:, None] * MOE_GC + jnp.cumsum(valid, axis=1) - 1
    dest = jnp.where(valid, pos, size).reshape(-1)
    tab = jnp.full((size,), -1, jnp.int32).at[dest].set(cid.reshape(-1).astype(jnp.int32), mode='drop')
    at = jnp.arange(size)
    tab = tab[jnp.maximum(lax.cummax(jnp.where(tab >= 0, at, -1)), 0)]
    gexp = jnp.minimum(jnp.sum(jnp.arange(MOE_GROUPS)[:, None] >= gend[None, :], axis=1), N_EXPERTS - 1)
    pool = jnp.zeros((NT * MOE_NCH,), jnp.int32).at[jnp.where(at < nused * MOE_GC, tab, NT * MOE_NCH)].set(
        at.astype(jnp.int32), mode='drop')
    return tab, gexp.astype(jnp.int32), nused.reshape(1).astype(jnp.int32), used.astype(jnp.int32), pool


def _dispatch_kernel(used_ref, h_ref, row_ref, xs_ref):
    used_rows = used_ref[pl.program_id(0)] * MOE_CH
    s0 = row_ref[0:1, :]
    s1 = row_ref[1:2, :]
    for blk in range(MOE_SLOTS // MOE_SB):
        rows = slice(blk * MOE_SB, (blk + 1) * MOE_SB)

        @pl.when(blk * MOE_SB < used_rows)
        def _():
            r = (lax.broadcasted_iota(jnp.int32, (MOE_SB, TM), 0) + blk * MOE_SB).astype(F32)
            s = jnp.where(r == s0, 1.0, jnp.where(r == s1, 1.0, 0.0)).astype(BF16)
            xs_ref[rows, :] = jnp.dot(s, h_ref[...], preferred_element_type=F32).astype(BF16)

        @pl.when(blk * MOE_SB >= used_rows)
        def _():
            xs_ref[rows, :] = jnp.zeros((MOE_SB, D), BF16)


def _moe_dispatch(h, plan_row, used):
    return pl.pallas_call(
        _dispatch_kernel,
        grid_spec=pltpu.PrefetchScalarGridSpec(
            num_scalar_prefetch=1, grid=(NT,),
            in_specs=[pl.BlockSpec((TM, D), lambda i, u: (i, 0)),
                      pl.BlockSpec((PLAN_ROWS, TM), lambda i, u: (0, i))],
            out_specs=pl.BlockSpec((MOE_SLOTS, D), lambda i, u: (i, 0))),
        out_shape=jax.ShapeDtypeStruct((NT * MOE_SLOTS, D), BF16),
        compiler_params=_params("parallel"),
        name="moe_dispatch",
    )(used, h, plan_row)


def _experts_kernel(tab_ref, gexp_ref, nused_ref, *refs):
    x_refs = refs[:MOE_GC]
    w1_ref, w3_ref, w2_ref, y_ref, x_scr, acc = refs[MOE_GC:]
    f = pl.program_id(1)

    @pl.when(pl.program_id(0) < nused_ref[0])
    def _():
        @pl.when(f == 0)
        def _():
            for j, r in enumerate(x_refs):
                x_scr[j * MOE_CH:(j + 1) * MOE_CH, :] = r[...]
            acc[...] = jnp.zeros_like(acc)

        _swiglu_step(x_scr[...], w1_ref, w3_ref, w2_ref, acc)

        @pl.when(f == MOE_NF - 1)
        def _():
            y_ref[...] = acc[...].astype(BF16)


def _moe_experts(xs, tab, gexp, nused, w1, w3, w2, e0):
    grp = lambda g, nu: jnp.minimum(g, nu[0] - 1)
    ftile = lambda g, f, nu: jnp.where(g < nu[0], f, MOE_NF - 1)
    chunk = lambda j: pl.BlockSpec((MOE_CH, D), lambda g, f, tab, ge, nu: (tab[grp(g, nu) * MOE_GC + j], 0))
    rows = MOE_GC * MOE_CH
    return pl.pallas_call(
        _experts_kernel,
        grid_spec=pltpu.PrefetchScalarGridSpec(
            num_scalar_prefetch=3, grid=(MOE_GROUPS, MOE_NF),
            in_specs=[chunk(j) for j in range(MOE_GC)] + [
                pl.BlockSpec((1, D, MOE_TF), lambda g, f, tab, ge, nu: (e0 + ge[grp(g, nu)], 0, ftile(g, f, nu))),
                pl.BlockSpec((1, D, MOE_TF), lambda g, f, tab, ge, nu: (e0 + ge[grp(g, nu)], 0, ftile(g, f, nu))),
                pl.BlockSpec((1, MOE_TF, D), lambda g, f, tab, ge, nu: (e0 + ge[grp(g, nu)], ftile(g, f, nu), 0))],
            out_specs=pl.BlockSpec((rows, D), lambda g, f, tab, ge, nu: (grp(g, nu), 0)),
            scratch_shapes=[pltpu.VMEM((rows, D), BF16), pltpu.VMEM((rows, D), F32)]),
        out_shape=jax.ShapeDtypeStruct((MOE_GROUPS * rows, D), BF16),
        compiler_params=_params("arbitrary", "arbitrary"),
        name="moe_experts",
    )(tab, gexp, nused, *([xs] * MOE_GC), w1, w3, w2)


def _combine_kernel(used_ref, pool_ref, *refs, final):
    refs = list(refs)
    y_refs = [refs.pop(0) for _ in range(MOE_NCH)]
    col_ref, x_ref, mod_ref = refs[:3]
    fg_ref = refs[3] if final else None
    o_ref, y_scr, acc = refs[-3:]
    used = used_ref[pl.program_id(0)]
    col = col_ref[...]
    s0, s1, g0, g1 = col[:, 0:1], col[:, 1:2], col[:, 2:3], col[:, 3:4]
    acc[...] = jnp.zeros_like(acc)
    per_blk = MOE_SB // MOE_CH
    for blk in range(MOE_SLOTS // MOE_SB):
        @pl.when(blk * per_blk < used)
        def _():
            for q in range(blk * per_blk, (blk + 1) * per_blk):
                y_scr[q * MOE_CH:(q + 1) * MOE_CH, :] = y_refs[q][...]
            c = (lax.broadcasted_iota(jnp.int32, (TM, MOE_SB), 1) + blk * MOE_SB).astype(F32)
            w = jnp.where(c == s0, g0, jnp.where(c == s1, g1, 0.0)).astype(BF16)
            acc[...] += jnp.dot(w, y_scr[blk * MOE_SB:(blk + 1) * MOE_SB, :], preferred_element_type=F32)
    o_ref[...] = _residual_out(x_ref[...], mod_ref[0][:, 5 * D:6 * D], acc[...], fg_ref)


def _moe_combine(y, pool, used, plan_col, x, mod_l, final_g=None):
    final = final_g is not None
    rows = LAT_ROWS if final else ROWS
    tile = lambda w: pl.BlockSpec((TM, w), lambda i, u, pc: (i, 0))
    chunk = lambda q: pl.BlockSpec((MOE_CH, D), lambda i, u, pc: (pc[i * MOE_NCH + q], 0))
    in_specs = [chunk(q) for q in range(MOE_NCH)] + [
        tile(LANE), tile(D), pl.BlockSpec((1, 1, N_MOD * D), lambda i, u, pc: (_mod_row(i), 0, 0))]
    args = [y] * MOE_NCH + [plan_col, x, mod_l]
    if final:
        in_specs.append(pl.BlockSpec((1, D), lambda i, u, pc: (0, 0)))
        args.append(final_g.reshape(1, D))
    return pl.pallas_call(
        functools.partial(_combine_kernel, final=final),
        grid_spec=pltpu.PrefetchScalarGridSpec(
            num_scalar_prefetch=2, grid=(rows // TM,),
            in_specs=in_specs, out_specs=tile(D),
            scratch_shapes=[pltpu.VMEM((MOE_SLOTS, D), BF16), pltpu.VMEM((TM, D), F32)]),
        out_shape=jax.ShapeDtypeStruct((rows, D), F32),
        compiler_params=_params("parallel"),
        name="moe_combine",
    )(used, pool, *args)


def _moe_mixer(h, logits, x, mod_l, w1, w3, w2, e0, final_g=None):
    plan_col, plan_row, nch = _moe_plan(logits)
    tab, gexp, nused, used, pool = _moe_tables(nch)
    xs = _moe_dispatch(h, plan_row, used)
    y = _moe_experts(xs, tab, gexp, nused, w1, w3, w2, e0)
    return _moe_combine(y, pool, used, plan_col, x, mod_l, final_g)


def kernel(x, c, ctx, c_ctx, w_mod, b_mod, norm1_g, norm2_g, w_in, rpb, hy_conv_w, hy_conv_b, hy_w1, hy_b1, hy_w2, hy_b2, hy_w3, hy_freq, hy_bias, w_pa, w_pb, w_o, ffn_w1, ffn_w3, ffn_w2, moe_router, moe_w1, moe_w3, moe_w2, final_g):
    xs = jnp.concatenate([x.reshape(LAT_ROWS, D), ctx.reshape(CTX_ROWS, D)], axis=0)
    mod = _modulation(c, c_ctx, w_mod, b_mod)
    tabs = _fft_tables()
    ctabs = _ctx_tables()
    filt_lat = _implicit_filters(SEQ, hy_w1, hy_b1, hy_w2, hy_b2, hy_w3, hy_freq)
    filt_ctx = _implicit_filters(CTX, hy_w1, hy_b1, hy_w2, hy_b2, hy_w3, hy_freq)
    spec_re, spec_im = _filter_spectra(filt_lat, tabs)
    nblk = HY_W // LANE

    moe_w = [w.reshape((-1,) + w.shape[2:]).astype(BF16) for w in (moe_w1, moe_w3, moe_w2)]
    ffn_w = [w.astype(BF16) for w in (ffn_w1, ffn_w3, ffn_w2)]
    w_in_b = w_in.astype(BF16)
    bias_tabs = _bias_tables(rpb)

    for l in range(DEPTH):
        last = l == DEPTH - 1
        mod_l = mod[l].reshape(MOD_ROWS, 1, N_MOD * D)
        cw, cb = hy_conv_w[l], hy_conv_b[l].reshape(1, 3 * HY_W)
        p = _in_proj(xs, mod_l, norm1_g[l].reshape(1, D), w_in_b, l)

        o_att = _context_attention(p, _neighbourhood_attention(p, bias_tabs, l))
        z = _hyena_conv(p, COL_HV, p, COL_HX1, cw, cb, 0, nblk, spec_re, spec_im, l, 0,
                        hy_bias[l], tabs, True)
        o_hy = _hyena_conv(z, 0, p, COL_HX2, cw, cb, 0, 2 * nblk, spec_re, spec_im, l, 1,
                           hy_bias[l], tabs, False)
        o_hy = _ctx_hyena(p, cw, cb, filt_ctx, l, hy_bias[l], ctabs, o_hy)

        i = l // 2
        router = moe_router[i] if l % 2 else None
        res = _merge(o_att, o_hy, p, xs, mod_l, norm2_g[l].reshape(1, D), w_pa, w_pb, w_o, l, router)
        fg = final_g if last else None
        if l % 2 == 0:
            xs = _dense_mixer(res[1], res[0], mod_l, *ffn_w, i, final_g=fg)
        else:
            xs = _moe_mixer(res[1], res[2], res[0], mod_l, *moe_w, i * N_EXPERTS, final_g=fg)
    return xs.reshape(NB, SEQ, D)
```

```python
import functools
import math

import numpy as np
import jax
import jax.numpy as jnp
from jax import lax
from jax.experimental import pallas as pl
from jax.experimental.pallas import tpu as pltpu

F32 = jnp.float32
BF16 = jnp.bfloat16
HIGHEST = lax.Precision.HIGHEST

D = 1024
NB = 4
SEQ = 4096
DEPTH = 4
CTX = 256
GRID_W = 64
GRID_H = SEQ // GRID_W
N_HEADS = 8
HEAD_DIM = 64
ATT_W = N_HEADS * HEAD_DIM
WIN_ROWS = 8
WIN_COLS = 16
HY_W = 512
HY_ORDER = 2
HY_SHORT = 3
HY_EMB = 33
HY_BANDS = (HY_EMB - 1) // 2
HY_FILT = 64
HY_DECAY_TARGET = 1e-2
HY_MAX_DECAY = math.log(HY_DECAY_TARGET) / 0.3
HY_MIN_DECAY = math.log(HY_DECAY_TARGET) / 1.5
HY_DECAY_SHIFT = 0.05
PROJ_W = 3 * ATT_W + 3 * HY_W + 2 * D
D_FF = 2816
N_EXPERTS = 8
N_MOD = 6
EPS = 1e-6

LAT_ROWS = NB * SEQ
CTX_ROWS = NB * CTX
ROWS = LAT_ROWS + CTX_ROWS
MOD_ROWS = 8
CTX_MOD_ROW = NB

LANE = 128
TM = 1024
VMEM_LIMIT = 56 * 1024 * 1024

COL_Q, COL_K, COL_V = 0, ATT_W // LANE, 2 * ATT_W // LANE
COL_HV = 3 * ATT_W // LANE
COL_HX1 = COL_HV + HY_W // LANE
COL_HX2 = COL_HX1 + HY_W // LANE
COL_GA = (3 * ATT_W + 3 * HY_W) // D
COL_GB = COL_GA + 1


def _mod_row(i):
    return jnp.where(i < LAT_ROWS // TM, (i * TM) // SEQ, CTX_MOD_ROW)


def _params(*sem):
    return pltpu.CompilerParams(dimension_semantics=sem, vmem_limit_bytes=VMEM_LIMIT)


def _mod_kernel(s_ref, w_ref, b_ref, o_ref):
    s = s_ref[...]
    s = s * jax.nn.sigmoid(s)
    o_ref[0] = _dot_hilo(s, w_ref[0]) + b_ref[0]


def _modulation(c, c_ctx, w_mod, b_mod):
    s = jnp.zeros((MOD_ROWS, D), F32).at[:NB].set(c).at[CTX_MOD_ROW].set(c_ctx)
    tn = 1536
    return pl.pallas_call(
        _mod_kernel,
        grid=(DEPTH, N_MOD * D // tn),
        in_specs=[pl.BlockSpec((MOD_ROWS, D), lambda l, j: (0, 0)),
                  pl.BlockSpec((1, D, tn), lambda l, j: (l, 0, j)),
                  pl.BlockSpec((1, 1, tn), lambda l, j: (l, 0, j))],
        out_specs=pl.BlockSpec((1, MOD_ROWS, tn), lambda l, j: (l, 0, j)),
        out_shape=jax.ShapeDtypeStruct((DEPTH, MOD_ROWS, N_MOD * D), F32),
        compiler_params=_params("parallel", "parallel"),
        name="modulation",
    )(s, w_mod, b_mod.reshape(DEPTH, 1, N_MOD * D))


def _dot_hilo(a, b):
    ah, bh = a.astype(BF16), b.astype(BF16)
    al, bl = (a - ah.astype(F32)).astype(BF16), (b - bh.astype(F32)).astype(BF16)
    dot = functools.partial(jnp.dot, preferred_element_type=F32)
    return dot(ah, bh) + (dot(ah, bl) + dot(al, bh))


def _norm_mod(x, g, shift, scale):
    y = x * lax.rsqrt(jnp.mean(x * x, axis=-1, keepdims=True) + EPS) * g
    return y * (1.0 + scale) + shift


def _inproj_kernel(x_ref, mod_ref, g_ref, w_ref, o_ref, h_scr):
    @pl.when(pl.program_id(1) == 0)
    def _():
        m = mod_ref[0]
        h = _norm_mod(x_ref[...], g_ref[...], m[:, 0:D], m[:, D:2 * D])
        h_scr[...] = h.astype(BF16)

    o_ref[...] = jnp.dot(h_scr[...], w_ref[0].astype(BF16),
                         preferred_element_type=F32).astype(BF16)


def _in_proj(x, mod_l, g, w, layer):
    tn = 1024
    return pl.pallas_call(
        _inproj_kernel,
        grid=(ROWS // TM, PROJ_W // tn),
        in_specs=[pl.BlockSpec((TM, D), lambda i, j: (i, 0)),
                  pl.BlockSpec((1, 1, N_MOD * D), lambda i, j: (_mod_row(i), 0, 0)),
                  pl.BlockSpec((1, D), lambda i, j: (0, 0)),
                  pl.BlockSpec((1, D, tn), lambda i, j: (layer, 0, j))],
        out_specs=pl.BlockSpec((TM, tn), lambda i, j: (i, j)),
        out_shape=jax.ShapeDtypeStruct((ROWS, PROJ_W), BF16),
        scratch_shapes=[pltpu.VMEM((TM, D), BF16)],
        compiler_params=_params("parallel", "arbitrary"),
        name="in_proj",
    )(x, mod_l, g, w)


NEG = -1e30


def _nt_dot(a, b):
    return lax.dot_general(a, b, (((1,), (1,)), ((), ())), preferred_element_type=F32)


def _attend_pair(qp, keys, values, biases):
    lane = lax.broadcasted_iota(jnp.int32, qp.shape, 1)
    qs = qp * jnp.asarray(HEAD_DIM ** -0.5, BF16)
    outs = []
    for a in range(2):
        sel = (lane < HEAD_DIM) if a == 0 else (lane >= HEAD_DIM)
        qa = jnp.where(sel, qs, jnp.zeros_like(qs))
        ss = []
        for k, b in zip(keys, biases):
            s = _nt_dot(qa, k)
            if b is not None:
                s = s + b[a]
            ss.append(s)
        m = functools.reduce(jnp.maximum, [jnp.max(s, axis=-1, keepdims=True) for s in ss])
        ps = [jnp.exp(s - m) for s in ss]
        den = functools.reduce(jnp.add, [jnp.sum(p, axis=-1, keepdims=True) for p in ps])
        o = functools.reduce(jnp.add, [jnp.dot(p.astype(BF16), v, preferred_element_type=F32)
                                       for p, v in zip(ps, values)])
        outs.append(o / den)
    return jnp.where(lane < HEAD_DIM, outs[0], outs[1])


NLOC = WIN_ROWS * GRID_W
NKEY = NLOC + CTX
NA_PAIRS = GRID_H // 2
NA_STEPS = NA_PAIRS + 2
NA_UNITS = 2 * N_HEADS


def _window_start(r):
    r0 = jnp.clip(r - WIN_ROWS // 2, 0, GRID_H - WIN_ROWS)
    return pl.multiple_of(r0 * GRID_W, GRID_W)


def _natt_kernel(q_ref, k_ref, v_ref, kc_ref, vc_ref, bt0_ref, bt1_ref, o_ref, s0, s1, p0, p1, l0, l1):
    j = pl.program_id(1)
    lane = lax.broadcasted_iota(jnp.int32, (GRID_W, LANE), 1)
    lo = lane < HEAD_DIM
    hi = lane >= HEAD_DIM

    @pl.when(j == 0)
    def _():
        s1[...] = jnp.zeros_like(s1)
        p0[...] = jnp.zeros_like(p0)
        p1[...] = jnp.zeros_like(p1)
        l0[...] = jnp.ones_like(l0)
        l1[...] = jnp.ones_like(l1)

    def step(s_cur, s_prev, p_cur, p_prev, l_cur, l_prev):
        ro = jnp.maximum(2 * j - 4, 0)
        rs = jnp.minimum(2 * j, GRID_H - 2)
        for half, bt_ref in enumerate((bt0_ref, bt1_ref)):
            start_o = _window_start(ro + half)
            start_s = _window_start(rs + half)
            rows = slice(half * GRID_W, (half + 1) * GRID_W)
            for hp in range(N_HEADS // 2):
                cs = slice(hp * LANE, (hp + 1) * LANE)
                u0 = half * N_HEADS + 2 * hp
                units = (u0, u0 + 1)
                pair = slice(u0, u0 + 2)
                m2 = 2 * GRID_W

                p2 = p_cur[pair].reshape(m2, NKEY)
                acc = (jnp.dot(p2[:, 0:NLOC], v_ref[pl.ds(start_o, NLOC), cs], preferred_element_type=F32)
                       + jnp.dot(p2[:, NLOC:NKEY], vc_ref[:, cs], preferred_element_type=F32))
                o_ref[rows, cs] = jnp.where(lo, acc[0:GRID_W] / l_cur[u0],
                                            acc[GRID_W:m2] / l_cur[u0 + 1]).astype(BF16)

                for u in units:
                    x = s_prev[u]
                    e = jnp.exp(x - jnp.max(x, axis=-1, keepdims=True))
                    l_prev[u] = jnp.broadcast_to(jnp.sum(e, axis=-1, keepdims=True), (GRID_W, LANE))
                    p_prev[u] = e.astype(BF16)

                qs = q_ref[rows, cs] * jnp.asarray(HEAD_DIM ** -0.5, BF16)
                zero = jnp.zeros_like(qs)
                q2 = jnp.concatenate([jnp.where(lo, qs, zero), jnp.where(hi, qs, zero)], axis=0)
                bias = bt_ref[0, 2 * hp:2 * hp + 2].astype(F32)
                s_cur[pair, :, 0:NLOC] = _nt_dot(q2, k_ref[pl.ds(start_s, NLOC), cs]).reshape(2, GRID_W, NLOC) + bias
                s_cur[pair, :, NLOC:NKEY] = _nt_dot(q2, kc_ref[:, cs]).reshape(2, GRID_W, CTX)

    @pl.when(j % 2 == 0)
    def _():
        step(s0, s1, p0, p1, l0, l1)

    @pl.when(j % 2 == 1)
    def _():
        step(s1, s0, p1, p0, l1, l0)


def _bias_tables(rpb):
    col = np.arange(GRID_W)
    c0 = np.clip(col - WIN_COLS // 2, 0, GRID_W - WIN_COLS)
    valid = (col[None, :] >= c0[:, None]) & (col[None, :] < c0[:, None] + WIN_COLS)
    dc = np.clip(col[None, :] - col[:, None] + (WIN_COLS - 1), 0, 2 * WIN_COLS - 2)
    pick = (np.arange(2 * WIN_COLS - 1)[:, None, None] == dc[None]).astype(np.float32)
    t = jnp.einsum('lhrd,dqk->lhrqk', rpb, jnp.asarray(pick), precision=HIGHEST)
    t = jnp.where(valid, t, NEG)
    win = jnp.stack([t[:, :, d0:d0 + WIN_ROWS] for d0 in range(WIN_ROWS)], axis=1)
    win = win.transpose(0, 1, 2, 4, 3, 5)
    return win.reshape(DEPTH * WIN_ROWS, N_HEADS, GRID_W, NLOC).astype(BF16)


def _bias_index(r):
    r0 = jnp.clip(r - WIN_ROWS // 2, 0, GRID_H - WIN_ROWS)
    return r0 - r + (WIN_ROWS - 1)


def _neighbourhood_attention(p, bt, layer):
    blk = GRID_H // 2
    ctx_blk0 = LAT_ROWS // CTX
    score_row = lambda j, half: jnp.minimum(2 * j, GRID_H - 2) + half
    bias_spec = lambda half: pl.BlockSpec(
        (1, N_HEADS, GRID_W, NLOC), lambda b, j: (layer * WIN_ROWS + _bias_index(score_row(j, half)), 0, 0, 0))
    return pl.pallas_call(
        _natt_kernel,
        grid=(NB, NA_STEPS),
        in_specs=[pl.BlockSpec((2 * GRID_W, ATT_W), lambda b, j: (b * blk + jnp.minimum(j, blk - 1), 0)),
                  pl.BlockSpec((SEQ, ATT_W), lambda b, j: (b, 1)),
                  pl.BlockSpec((SEQ, ATT_W), lambda b, j: (b, 2)),
                  pl.BlockSpec((CTX, ATT_W), lambda b, j: (ctx_blk0 + b, 1)),
                  pl.BlockSpec((CTX, ATT_W), lambda b, j: (ctx_blk0 + b, 2)),
                  bias_spec(0), bias_spec(1)],
        out_specs=pl.BlockSpec((2 * GRID_W, ATT_W), lambda b, j: (b * blk + jnp.maximum(j - 2, 0), 0)),
        out_shape=jax.ShapeDtypeStruct((ROWS, ATT_W), BF16),
        scratch_shapes=[pltpu.VMEM((NA_UNITS, GRID_W, NKEY), F32)] * 2
                       + [pltpu.VMEM((NA_UNITS, GRID_W, NKEY), BF16)] * 2
                       + [pltpu.VMEM((NA_UNITS, GRID_W, LANE), F32)] * 2,
        compiler_params=_params("parallel", "arbitrary"),
        name="neighbourhood_attention",
    )(p, p, p, p, p, bt, bt)


def _catt_kernel(q_ref, k_ref, v_ref, dst_ref, o_ref):
    del dst_ref
    for hp in range(N_HEADS // 2):
        cs = slice(hp * LANE, (hp + 1) * LANE)
        o = _attend_pair(q_ref[:, cs], [k_ref[:, cs]], [v_ref[:, cs]], [None])
        o_ref[:, cs] = o.astype(BF16)


def _context_attention(p, o_att):
    ctx_blk0 = LAT_ROWS // CTX
    return pl.pallas_call(
        _catt_kernel,
        grid=(NB,),
        in_specs=[pl.BlockSpec((CTX, ATT_W), lambda b: (ctx_blk0 + b, 0)),
                  pl.BlockSpec((CTX, ATT_W), lambda b: (ctx_blk0 + b, 1)),
                  pl.BlockSpec((CTX, ATT_W), lambda b: (ctx_blk0 + b, 2)),
                  pl.BlockSpec(memory_space=pl.ANY)],
        out_specs=pl.BlockSpec((CTX, ATT_W), lambda b: (ctx_blk0 + b, 0)),
        out_shape=jax.ShapeDtypeStruct((ROWS, ATT_W), BF16),
        input_output_aliases={3: 0},
        compiler_params=_params("parallel"),
        name="context_attention",
    )(p, p, p, o_att)


def _filter_features(L):
    pos = np.arange(L, dtype=np.float64)
    t = pos / max(L - 1, 1)
    bands = np.linspace(1e-4, HY_BANDS - 1, HY_BANDS)
    ang = (2.0 * math.pi / L) * pos[:, None] * bands[None, :]
    z = np.concatenate([t[:, None], np.cos(ang), -np.sin(ang)], axis=-1)
    zp = np.zeros((L, LANE), np.float32)
    zp[:, :HY_EMB] = z
    return zp


def _filter_kernel(z_ref, w1_ref, b1_ref, w2_ref, b2_ref, w3_ref, fr_ref, rate_ref, o_ref, *, L, tl):
    fr = fr_ref[0]
    a = jnp.sin(fr[0:1] * (_dot_hilo(z_ref[...], w1_ref[0]) + b1_ref[0]))
    a = jnp.sin(fr[1:2] * (_dot_hilo(a, w2_ref[0]) + b2_ref[0]))
    h = _dot_hilo(a, w3_ref[0])
    pos = (pl.program_id(1) * tl + lax.broadcasted_iota(jnp.int32, (tl, 1), 0)).astype(F32)
    dist = jnp.abs(pos - float(L // 2)) * (2.0 / L)
    o_ref[0] = h * (jnp.exp(-dist * rate_ref[...]) + HY_DECAY_SHIFT)


def _implicit_filters(L, w1, b1, w2, b2, w3, freq):
    tl = min(L, 512)
    z = jnp.asarray(_filter_features(L))
    w1p = jnp.zeros((DEPTH, LANE, HY_FILT), F32).at[:, :HY_EMB].set(w1)
    rates = np.abs(np.linspace(HY_MIN_DECAY, HY_MAX_DECAY, HY_ORDER * HY_W)).astype(np.float32)
    cw = HY_ORDER * HY_W
    return pl.pallas_call(
        functools.partial(_filter_kernel, L=L, tl=tl),
        grid=(DEPTH, L // tl),
        in_specs=[pl.BlockSpec((tl, LANE), lambda l, i: (i, 0)),
                  pl.BlockSpec((1, LANE, HY_FILT), lambda l, i: (l, 0, 0)),
                  pl.BlockSpec((1, 1, HY_FILT), lambda l, i: (l, 0, 0)),
                  pl.BlockSpec((1, HY_FILT, HY_FILT), lambda l, i: (l, 0, 0)),
                  pl.BlockSpec((1, 1, HY_FILT), lambda l, i: (l, 0, 0)),
                  pl.BlockSpec((1, HY_FILT, cw), lambda l, i: (l, 0, 0)),
                  pl.BlockSpec((1, 2, HY_FILT), lambda l, i: (l, 0, 0)),
                  pl.BlockSpec((1, cw), lambda l, i: (0, 0))],
        out_specs=pl.BlockSpec((1, tl, cw), lambda l, i: (l, i, 0)),
        out_shape=jax.ShapeDtypeStruct((DEPTH, L, cw), F32),
        compiler_params=_params("parallel", "parallel"),
        name=f"implicit_filters_{L}",
    )(z, w1p, b1.reshape(DEPTH, 1, HY_FILT), w2, b2.reshape(DEPTH, 1, HY_FILT), w3, freq,
      jnp.asarray(rates).reshape(1, cw))


FFT_N = 2 * SEQ
FA = 64
FB = 128
FA_IN = SEQ // FB
PITCH = 136
HALO = 8
UNROLL_ROWS = 64
UNROLL_SLABS = 32


def _fft_tables():
    a = np.arange(FA_IN)
    r = np.arange(FA)
    b = np.arange(FB)
    q = np.arange(FB)
    ph = a[None, None, :] * r[None, :, None] / FA + b[:, None, None] * r[None, :, None] / FFT_N
    g1 = np.exp(-2j * np.pi * ph)
    g1 = np.concatenate([g1.real, g1.imag], axis=1)
    f = np.exp(-2j * np.pi * q[:, None] * b[None, :] / FB)
    f2 = np.concatenate([f.real, f.imag], axis=0)
    f2c = np.concatenate([f.real, -f.imag], axis=0)
    ao = np.arange(FA_IN) + (SEQ // 2) // FB
    ph2 = ao[None, :, None] * r[None, None, :] / FA + b[:, None, None] * r[None, None, :] / FFT_N
    g2 = np.exp(2j * np.pi * ph2) / FFT_N
    g2 = np.concatenate([g2.real, g2.imag], axis=1)
    return tuple(jnp.asarray(t, dtype=BF16) for t in (g1, f2, f2c, g2))


def _cplx(out, m, n):
    re = out[0:m, 0:n] - out[m:2 * m, n:2 * n]
    im = out[0:m, n:2 * n] + out[m:2 * m, 0:n]
    return re, im


def _short_conv_slabs(src_ref, rows, t_scr, cw_ref, cb_ref, apply_conv, emit):
    nslab = rows // FB
    if not apply_conv:
        for a in range(nslab):
            emit(a, src_ref[a * FB:(a + 1) * FB, :].astype(F32))
        return
    zero = jnp.zeros((HALO, LANE), F32)
    t_scr[0:HALO, :] = zero
    t_scr[HALO + rows:2 * HALO + rows, :] = zero
    for a in range(nslab):
        t_scr[HALO + a * FB:HALO + (a + 1) * FB, :] = src_ref[a * FB:(a + 1) * FB, :].astype(F32)
    w0, w1, w2, bias = cw_ref[0:1, :], cw_ref[1:2, :], cw_ref[2:3, :], cb_ref[...]
    for a in range(nslab):
        o = HALO + a * FB
        y = (w0 * t_scr[o - 1:o - 1 + FB, :] + w1 * t_scr[o:o + FB, :]
             + w2 * t_scr[o + 1:o + 1 + FB, :] + bias)
        emit(a, y)


def _fwd_stage1(b, s_refs, g1_ref, y_r, y_i):
    cols = [s[pl.ds(b, FA_IN, stride=PITCH), :].astype(BF16) for s in s_refs]
    rhs = cols[0] if len(cols) == 1 else jnp.concatenate(cols, axis=1)
    out = jnp.dot(g1_ref[b], rhs, preferred_element_type=F32)
    if len(cols) == 1:
        re, im = out[0:FA], out[FA:2 * FA]
    else:
        re, im = _cplx(out, FA, LANE)
    y_r[pl.ds(b, FA, stride=PITCH), :] = re
    y_i[pl.ds(b, FA, stride=PITCH), :] = im


def _fwd_stage2(r, y_r, y_i, f_ref):
    o = pl.multiple_of(r * PITCH, 8)
    rhs = jnp.concatenate([y_r[pl.ds(o, FB), :].astype(BF16), y_i[pl.ds(o, FB), :].astype(BF16)], axis=1)
    out = jnp.dot(f_ref[...], rhs, preferred_element_type=F32)
    return _cplx(out, FB, LANE)


def _spectrum_kernel(h_ref, g1_ref, f_ref, hr_ref, hi_ref, s_scr, y_r, y_i):
    for a in range(FA_IN):
        s_scr[a * PITCH:a * PITCH + FB, :] = h_ref[0, a * FB:(a + 1) * FB, :]

    def s1(b, c):
        _fwd_stage1(b, [s_scr], g1_ref, y_r, y_i)
        return c
    lax.fori_loop(0, FB, s1, 0, unroll=UNROLL_ROWS)

    def s2(r, c):
        xr, xi = _fwd_stage2(r, y_r, y_i, f_ref)
        o = pl.multiple_of(r * FB, FB)
        hr_ref[0, pl.ds(o, FB), :] = xr
        hi_ref[0, pl.ds(o, FB), :] = xi
        return c
    lax.fori_loop(0, FA, s2, 0, unroll=UNROLL_SLABS)


def _filter_spectra(h, tabs):
    g1, f2, _, _ = tabs
    cw = HY_ORDER * HY_W
    spec = pl.BlockSpec((1, FFT_N, LANE), lambda l, c: (l, 0, c))
    return pl.pallas_call(
        _spectrum_kernel,
        grid=(DEPTH, cw // LANE),
        in_specs=[pl.BlockSpec((1, SEQ, LANE), lambda l, c: (l, 0, c)),
                  pl.BlockSpec(g1.shape, lambda l, c: (0, 0, 0)),
                  pl.BlockSpec(f2.shape, lambda l, c: (0, 0))],
        out_specs=[spec, spec],
        out_shape=[jax.ShapeDtypeStruct((DEPTH, FFT_N, cw), F32)] * 2,
        scratch_shapes=[pltpu.VMEM((FA_IN * PITCH, LANE), F32),
                        pltpu.VMEM((FA * PITCH, LANE), F32),
                        pltpu.VMEM((FA * PITCH, LANE), F32)],
        compiler_params=_params("parallel", "parallel"),
        name="filter_spectra",
    )(h, g1, f2)


def _hyconv_kernel(u_ref, x_ref, cwu_ref, cbu_ref, cwx_ref, cbx_ref,
                   hr_ref, hi_ref, bias_ref, g1_ref, f_ref, fc_ref, g2_ref,
                   o_ref, t_scr, s_r, s_i, y_r, y_i, *, conv_input):
    halves = [(s_r, slice(0, SEQ)), (s_i, slice(SEQ, 2 * SEQ))]

    def fill(s):
        def emit(a, slab):
            s[a * PITCH:a * PITCH + FB, :] = slab
        return emit
    for s, rows in halves:
        _short_conv_slabs(u_ref.at[rows, :], SEQ, t_scr, cwu_ref, cbu_ref, conv_input, fill(s))

    def s1(b, c):
        _fwd_stage1(b, [s_r, s_i], g1_ref, y_r, y_i)
        return c
    lax.fori_loop(0, FB, s1, 0, unroll=UNROLL_ROWS)

    def s2(r, c):
        xr, xi = _fwd_stage2(r, y_r, y_i, f_ref)
        oh = pl.multiple_of(r * FB, FB)
        hr = hr_ref[0, pl.ds(oh, FB), :]
        hi = hi_ref[0, pl.ds(oh, FB), :]
        vr = xr * hr - xi * hi
        vi = xr * hi + xi * hr
        rhs = jnp.concatenate([vr.astype(BF16), vi.astype(BF16)], axis=1)
        wr, wi = _cplx(jnp.dot(fc_ref[...], rhs, preferred_element_type=F32), FB, LANE)
        o = pl.multiple_of(r * PITCH, 8)
        y_r[pl.ds(o, FB), :] = wr
        y_i[pl.ds(o, FB), :] = wi
        return c
    lax.fori_loop(0, FA, s2, 0, unroll=UNROLL_SLABS)

    bias = bias_ref[0]

    def s3(b, c):
        rhs = jnp.concatenate([y_r[pl.ds(b, FA, stride=PITCH), :].astype(BF16),
                               y_i[pl.ds(b, FA, stride=PITCH), :].astype(BF16)], axis=1)
        re, im = _cplx(jnp.dot(g2_ref[b], rhs, preferred_element_type=F32), FA_IN, LANE)
        idx = pl.ds(b, FA_IN, stride=PITCH)
        s_r[idx, :] = re + s_r[idx, :] * bias
        s_i[idx, :] = im + s_i[idx, :] * bias
        return c
    lax.fori_loop(0, FB, s3, 0, unroll=UNROLL_ROWS)

    for s, rows in halves:
        def emit(a, slab, s=s, out=o_ref.at[rows, :]):
            out[a * FB:(a + 1) * FB, :] = (slab * s[a * PITCH:a * PITCH + FB, :]).astype(out.dtype)
        _short_conv_slabs(x_ref.at[rows, :], SEQ, t_scr, cwx_ref, cbx_ref, True, emit)


def _hyena_conv(u, u_col, x, x_col, conv_w, conv_b, cu_col, cx_col, spec_re, spec_im, layer, order,
                hy_bias_l, tabs, conv_input):
    g1, f2, f2c, g2 = tabs
    nblk = HY_W // LANE
    tok = lambda col: pl.BlockSpec((2 * SEQ, LANE), lambda c, p: (p, col + c))
    chan = lambda rows, col: pl.BlockSpec((rows, LANE), lambda c, p: (0, col + c))
    hspec = pl.BlockSpec((1, FFT_N, LANE), lambda c, p: (layer, 0, order * nblk + c))
    const = lambda shape: pl.BlockSpec(shape, lambda c, p: (0,) * len(shape))
    return pl.pallas_call(
        functools.partial(_hyconv_kernel, conv_input=conv_input),
        grid=(nblk, NB // 2),
        in_specs=[tok(u_col), tok(x_col),
                  chan(HY_SHORT, cu_col), chan(1, cu_col), chan(HY_SHORT, cx_col), chan(1, cx_col),
                  hspec, hspec,
                  pl.BlockSpec((1, 1, LANE), lambda c, p: (order, 0, c)),
                  const(g1.shape), const(f2.shape), const(f2c.shape), const(g2.shape)],
        out_specs=pl.BlockSpec((2 * SEQ, LANE), lambda c, p: (p, c)),
        out_shape=jax.ShapeDtypeStruct((ROWS, HY_W), BF16),
        scratch_shapes=[pltpu.VMEM((SEQ + 2 * HALO, LANE), F32),
                        pltpu.VMEM((FA_IN * PITCH, LANE), F32),
                        pltpu.VMEM((FA_IN * PITCH, LANE), F32),
                        pltpu.VMEM((FA * PITCH, LANE), F32),
                        pltpu.VMEM((FA * PITCH, LANE), F32)],
        compiler_params=_params("arbitrary", "arbitrary"),
        name=f"hyena_conv{order}",
    )(u, x, conv_w, conv_b, conv_w, conv_b, spec_re, spec_im,
      hy_bias_l.reshape(HY_ORDER, 1, HY_W), g1, f2, f2c, g2)


CN = 2 * CTX


def _ctx_tables():
    k = np.arange(CN)
    n = np.arange(CTX)
    f = np.exp(-2j * np.pi * k[:, None] * n[None, :] / CN)
    fwd = np.concatenate([f.real, f.imag], axis=0)
    t = np.arange(CTX) + CTX // 2
    g = np.exp(2j * np.pi * t[:, None] * k[None, :] / CN) / CN
    inv = np.concatenate([g.real, g.imag], axis=0)
    return jnp.asarray(fwd, dtype=BF16), jnp.asarray(inv, dtype=BF16)


def _ctx_hyena_kernel(v_ref, x1_ref, x2_ref, cwv_ref, cbv_ref, cw1_ref, cb1_ref, cw2_ref, cb2_ref,
                      h0_ref, h1_ref, bias_ref, fwd_ref, inv_ref, dst_ref, o_ref, t_scr, s_scr, g_scr):
    del dst_ref
    def conv_into(src_ref, b, cw_ref, cb_ref, dst, col):
        view = src_ref.at[b * CTX:(b + 1) * CTX, :]

        def emit(a, slab):
            dst[a * FB:(a + 1) * FB, col * LANE:(col + 1) * LANE] = slab
        _short_conv_slabs(view, CTX, t_scr, cw_ref, cb_ref, True, emit)

    def spectrum(h_ref):
        out = jnp.dot(fwd_ref[...], h_ref[0].astype(BF16), preferred_element_type=F32)
        return out[0:CN], out[CN:2 * CN]

    def long_conv(z2, hr, hi, bias):
        x = jnp.dot(fwd_ref[...], z2.astype(BF16), preferred_element_type=F32)
        xr, xi = _cplx(x, CN, LANE)
        vr = xr * hr - xi * hi
        vi = xr * hi + xi * hr
        rhs = jnp.concatenate([vr.astype(BF16), vi.astype(BF16)], axis=1)
        y = jnp.dot(inv_ref[...], rhs, preferred_element_type=F32)
        yr, yi = _cplx(y, CTX, LANE)
        return jnp.concatenate([yr, yi], axis=1) + z2 * jnp.concatenate([bias, bias], axis=1)

    h0r, h0i = spectrum(h0_ref)
    h1r, h1i = spectrum(h1_ref)
    for p in range(NB // 2):
        for half in range(2):
            conv_into(v_ref, 2 * p + half, cwv_ref, cbv_ref, s_scr, half)
            conv_into(x1_ref, 2 * p + half, cw1_ref, cb1_ref, g_scr, half)
        z = g_scr[...] * long_conv(s_scr[...], h0r, h0i, bias_ref[0])
        for half in range(2):
            conv_into(x2_ref, 2 * p + half, cw2_ref, cb2_ref, g_scr, half)
        y = g_scr[...] * long_conv(z, h1r, h1i, bias_ref[1])
        for half in range(2):
            b = 2 * p + half
            o_ref[b * CTX:(b + 1) * CTX, :] = y[:, half * LANE:(half + 1) * LANE].astype(BF16)


def _ctx_hyena(p, conv_w, conv_b, filt_ctx, layer, hy_bias_l, ctabs, o_hy):
    fwd, inv = ctabs
    nblk = HY_W // LANE
    ctx_blk = LAT_ROWS // CTX_ROWS
    tok = lambda col: pl.BlockSpec((CTX_ROWS, LANE), lambda c: (ctx_blk, col + c))
    chan = lambda rows, col: pl.BlockSpec((rows, LANE), lambda c: (0, col + c))
    hspec = lambda order: pl.BlockSpec((1, CTX, LANE), lambda c: (layer, 0, order * nblk + c))
    const = lambda shape: pl.BlockSpec(shape, lambda c: (0,) * len(shape))
    return pl.pallas_call(
        _ctx_hyena_kernel,
        grid=(nblk,),
        in_specs=[tok(COL_HV), tok(COL_HX1), tok(COL_HX2),
                  chan(HY_SHORT, 0), chan(1, 0), chan(HY_SHORT, nblk), chan(1, nblk),
                  chan(HY_SHORT, 2 * nblk), chan(1, 2 * nblk),
                  hspec(0), hspec(1),
                  pl.BlockSpec((HY_ORDER, 1, LANE), lambda c: (0, 0, c)),
                  const(fwd.shape), const(inv.shape), pl.BlockSpec(memory_space=pl.ANY)],
        out_specs=pl.BlockSpec((CTX_ROWS, LANE), lambda c: (ctx_blk, c)),
        out_shape=jax.ShapeDtypeStruct((ROWS, HY_W), BF16),
        input_output_aliases={14: 0},
        scratch_shapes=[pltpu.VMEM((CTX + 2 * HALO, LANE), F32),
                        pltpu.VMEM((CTX, 2 * LANE), F32),
                        pltpu.VMEM((CTX, 2 * LANE), F32)],
        compiler_params=_params("parallel"),
        name="context_hyena",
    )(p, p, p, conv_w, conv_b, conv_w, conv_b, conv_w, conv_b, filt_ctx, filt_ctx,
      hy_bias_l.reshape(HY_ORDER, 1, HY_W), fwd, inv, o_hy)


def _merge_kernel(oa_ref, oh_ref, ga_ref, gb_ref, x_ref, mod_ref, g2_ref, wpa_ref, wpb_ref, wo_ref,
                  *rest, routed):
    if routed:
        wr_ref, xo_ref, h_ref, rw_ref, wpa_s, wpb_s, wo_s = rest
    else:
        xo_ref, h_ref, wpa_s, wpb_s, wo_s = rest

    @pl.when(pl.program_id(0) == 0)
    def _():
        wpa_s[...] = wpa_ref[0].astype(BF16)
        wpb_s[...] = wpb_ref[0].astype(BF16)
        wo_s[...] = wo_ref[0].astype(BF16)

    a = jnp.dot(oa_ref[...], wpa_s[...], preferred_element_type=F32)
    b = jnp.dot(oh_ref[...], wpb_s[...], preferred_element_type=F32)
    mix = (jax.nn.sigmoid(ga_ref[...].astype(F32)) * a + jax.nn.sigmoid(gb_ref[...].astype(F32)) * b)
    y = jnp.dot(mix.astype(BF16), wo_s[...], preferred_element_type=F32)
    m = mod_ref[0]
    x = x_ref[...] + m[:, 2 * D:3 * D] * y
    xo_ref[...] = x
    h = _norm_mod(x, g2_ref[...], m[:, 3 * D:4 * D], m[:, 4 * D:5 * D])
    h_ref[...] = h.astype(BF16)
    if routed:
        rw_ref[...] = _dot_hilo(h, wr_ref[...])


def _merge(o_att, o_hy, p, x, mod_l, g2, w_pa, w_pb, w_o, layer, router):
    routed = router is not None
    tile = lambda w: pl.BlockSpec((TM, w), lambda i: (i, 0))
    full = lambda a: pl.BlockSpec(a.shape, lambda i: (0,) * a.ndim)
    layered = lambda a: pl.BlockSpec((1,) + a.shape[1:], lambda i: (layer, 0, 0))
    in_specs = [tile(ATT_W), tile(HY_W),
                pl.BlockSpec((TM, D), lambda i: (i, COL_GA)),
                pl.BlockSpec((TM, D), lambda i: (i, COL_GB)),
                tile(D),
                pl.BlockSpec((1, 1, N_MOD * D), lambda i: (_mod_row(i), 0, 0)),
                pl.BlockSpec((1, D), lambda i: (0, 0)),
                layered(w_pa), layered(w_pb), layered(w_o)]
    args = [o_att, o_hy, p, p, x, mod_l, g2, w_pa, w_pb, w_o]
    out_specs = [tile(D), tile(D)]
    out_shape = [jax.ShapeDtypeStruct((ROWS, D), F32), jax.ShapeDtypeStruct((ROWS, D), BF16)]
    if routed:
        wr = jnp.zeros((D, LANE), F32).at[:, :N_EXPERTS].set(router)
        in_specs.append(full(wr))
        args.append(wr)
        out_specs.append(tile(LANE))
        out_shape.append(jax.ShapeDtypeStruct((ROWS, LANE), F32))
    return pl.pallas_call(
        functools.partial(_merge_kernel, routed=routed),
        grid=(ROWS // TM,),
        in_specs=in_specs, out_specs=out_specs, out_shape=out_shape,
        scratch_shapes=[pltpu.VMEM(w_pa.shape[1:], BF16), pltpu.VMEM(w_pb.shape[1:], BF16),
                        pltpu.VMEM(w_o.shape[1:], BF16)],
        compiler_params=_params("arbitrary"),
        name="merge",
    )(*args)


TF = D_FF // 2


NF = D_FF // TF


SWIGLU_SUB = 768


def _swiglu_step(h, w1_ref, w3_ref, w2_ref, acc):
    tf = w1_ref.shape[2]
    for lo in range(0, tf, SWIGLU_SUB):
        hi = min(lo + SWIGLU_SUB, tf)
        a = jnp.dot(h, w1_ref[0, :, lo:hi].astype(BF16), preferred_element_type=F32)
        b = jnp.dot(h, w3_ref[0, :, lo:hi].astype(BF16), preferred_element_type=F32)
        t = a * jax.nn.sigmoid(a) * b
        acc[...] += jnp.dot(t.astype(BF16), w2_ref[0, lo:hi, :].astype(BF16), preferred_element_type=F32)


def _residual_out(x, gate, y, fg_ref):
    x = x + gate * y
    if fg_ref is not None:
        x = x * lax.rsqrt(jnp.mean(x * x, axis=-1, keepdims=True) + EPS) * fg_ref[...]
    return x


def _ffn_kernel(h_ref, x_ref, mod_ref, w1_ref, w3_ref, w2_ref, *rest, final):
    rest = list(rest)
    fg_ref = rest.pop(0) if final else None
    o_ref, acc = rest
    f = pl.program_id(1)

    @pl.when(f == 0)
    def _():
        acc[...] = jnp.zeros_like(acc)

    _swiglu_step(h_ref[...], w1_ref, w3_ref, w2_ref, acc)

    @pl.when(f == NF - 1)
    def _():
        o_ref[...] = _residual_out(x_ref[...], mod_ref[0][:, 5 * D:6 * D], acc[...], fg_ref)


def _dense_mixer(h, x, mod_l, w1, w3, w2, e, final_g=None):
    final = final_g is not None
    rows = LAT_ROWS if final else ROWS
    tile = lambda w: pl.BlockSpec((TM, w), lambda i, f: (i, 0))
    in_specs = [tile(D), tile(D),
                pl.BlockSpec((1, 1, N_MOD * D), lambda i, f: (_mod_row(i), 0, 0)),
                pl.BlockSpec((1, D, TF), lambda i, f: (e, 0, f)),
                pl.BlockSpec((1, D, TF), lambda i, f: (e, 0, f)),
                pl.BlockSpec((1, TF, D), lambda i, f: (e, f, 0))]
    args = [h, x, mod_l, w1, w3, w2]
    if final:
        in_specs.append(pl.BlockSpec((1, D), lambda i, f: (0, 0)))
        args.append(final_g.reshape(1, D))
    return pl.pallas_call(
        functools.partial(_ffn_kernel, final=final),
        grid=(rows // TM, NF),
        in_specs=in_specs,
        out_specs=tile(D),
        out_shape=jax.ShapeDtypeStruct((rows, D), F32),
        scratch_shapes=[pltpu.VMEM((TM, D), F32)],
        compiler_params=_params("parallel", "arbitrary"),
        name="ffn",
    )(*args)


NT = ROWS // TM
MOE_CH = 64
MOE_CMAX = TM // MOE_CH
MOE_NCH = 2 * TM // MOE_CH + N_EXPERTS
MOE_SLOTS = MOE_NCH * MOE_CH
MOE_GC = 16
MOE_GROUPS = NT * MOE_NCH // MOE_GC + N_EXPERTS
MOE_SB = 512
MOE_TF, MOE_NF = TF, NF
PLAN_ROWS = 8


def _plan_kernel(lg_ref, col_ref, row_ref, nch_ref):
    lane = lax.broadcasted_iota(jnp.int32, (TM, LANE), 1)
    ninf = jnp.asarray(-jnp.inf, F32)
    lg = jnp.where(lane < N_EXPERTS, lg_ref[...], ninf)
    m1 = jnp.max(lg, axis=-1, keepdims=True)
    i1 = jnp.min(jnp.where(lg == m1, lane, LANE), axis=-1, keepdims=True)
    lg2 = jnp.where(lane == i1, ninf, lg)
    m2 = jnp.max(lg2, axis=-1, keepdims=True)
    i2 = jnp.min(jnp.where(lg2 == m2, lane, LANE), axis=-1, keepdims=True)
    e = jnp.exp(m2 - m1)
    den = 1.0 + e
    sel1 = lane == i1
    sel2 = lane == i2
    onehot = jnp.where(sel1, 1.0, jnp.where(sel2, 1.0, 0.0))
    tri = jnp.where(lax.broadcasted_iota(jnp.int32, (TM, TM), 1) < lax.broadcasted_iota(jnp.int32, (TM, TM), 0),
                    1.0, 0.0).astype(BF16)
    rank = jnp.dot(tri, onehot.astype(BF16), preferred_element_type=F32)
    cnt = jnp.sum(onehot, axis=0, keepdims=True)
    nch = jnp.floor((cnt + (MOE_CH - 1)) * (1.0 / MOE_CH))
    upper = jnp.where(lax.broadcasted_iota(jnp.int32, (LANE, LANE), 0) < lax.broadcasted_iota(jnp.int32, (LANE, LANE), 1),
                      1.0, 0.0).astype(BF16)
    cbase = jnp.dot(jnp.broadcast_to(nch, (8, LANE)).astype(BF16), upper,
                    preferred_element_type=F32)[0:1]
    slot = MOE_CH * cbase + rank
    slot0 = jnp.sum(jnp.where(sel1, slot, 0.0), axis=-1, keepdims=True)
    slot1 = jnp.sum(jnp.where(sel2, slot, 0.0), axis=-1, keepdims=True)
    col = jnp.where(lane == 0, slot0, jnp.where(lane == 1, slot1,
                    jnp.where(lane == 2, 1.0 / den, jnp.where(lane == 3, e / den, 0.0))))
    col_ref[...] = col
    row_ref[...] = jnp.transpose(col)[0:PLAN_ROWS]
    nch_ref[0] = nch


def _moe_plan(logits):
    return pl.pallas_call(
        _plan_kernel,
        grid=(NT,),
        in_specs=[pl.BlockSpec((TM, LANE), lambda i: (i, 0))],
        out_specs=[pl.BlockSpec((TM, LANE), lambda i: (i, 0)),
                   pl.BlockSpec((PLAN_ROWS, TM), lambda i: (0, i)),
                   pl.BlockSpec((1, 1, LANE), lambda i: (i, 0, 0))],
        out_shape=[jax.ShapeDtypeStruct((ROWS, LANE), F32),
                   jax.ShapeDtypeStruct((PLAN_ROWS, ROWS), F32),
                   jax.ShapeDtypeStruct((NT, 1, LANE), F32)],
        compiler_params=_params("parallel"),
        name="moe_plan",
    )(logits)


def _moe_tables(nch):
    n = nch[:, 0, :N_EXPERTS].astype(jnp.int32)
    cbase = jnp.cumsum(n, axis=1) - n
    used = jnp.sum(n, axis=1)
    c = jnp.arange(MOE_CMAX)
    valid = (c[None, None, :] < n.T[:, :, None]).reshape(N_EXPERTS, -1)
    cid = (jnp.arange(NT)[None, :, None] * MOE_NCH + cbase.T[:, :, None] + c[None, None, :]).reshape(N_EXPERTS, -1)
    ne = jnp.sum(n, axis=0)
    ge = (ne + MOE_GC - 1) // MOE_GC
    gend = jnp.cumsum(ge)
    nused = gend[-1]
    size = MOE_GROUPS * MOE_GC
    pos = (gend - ge)[:, None] * MOE_GC + jnp.cumsum(valid, axis=1) - 1
    dest = jnp.where(valid, pos, size).reshape(-1)
    tab = jnp.full((size,), -1, jnp.int32).at[dest].set(cid.reshape(-1).astype(jnp.int32), mode='drop')
    at = jnp.arange(size)
    tab = tab[jnp.maximum(lax.cummax(jnp.where(tab >= 0, at, -1)), 0)]
    gexp = jnp.minimum(jnp.sum(jnp.arange(MOE_GROUPS)[:, None] >= gend[None, :], axis=1), N_EXPERTS - 1)
    pool = jnp.zeros((NT * MOE_NCH,), jnp.int32).at[jnp.where(at < nused * MOE_GC, tab, NT * MOE_NCH)].set(
        at.astype(jnp.int32), mode='drop')
    return tab, gexp.astype(jnp.int32), nused.reshape(1).astype(jnp.int32), used.astype(jnp.int32), pool


def _dispatch_kernel(used_ref, h_ref, row_ref, xs_ref):
    used_rows = used_ref[pl.program_id(0)] * MOE_CH
    s0 = row_ref[0:1, :]
    s1 = row_ref[1:2, :]
    for blk in range(MOE_SLOTS // MOE_SB):
        rows = slice(blk * MOE_SB, (blk + 1) * MOE_SB)

        @pl.when(blk * MOE_SB < used_rows)
        def _():
            r = (lax.broadcasted_iota(jnp.int32, (MOE_SB, TM), 0) + blk * MOE_SB).astype(F32)
            s = jnp.where(r == s0, 1.0, jnp.where(r == s1, 1.0, 0.0)).astype(BF16)
            xs_ref[rows, :] = jnp.dot(s, h_ref[...], preferred_element_type=F32).astype(BF16)

        @pl.when(blk * MOE_SB >= used_rows)
        def _():
            xs_ref[rows, :] = jnp.zeros((MOE_SB, D), BF16)


def _moe_dispatch(h, plan_row, used):
    return pl.pallas_call(
        _dispatch_kernel,
        grid_spec=pltpu.PrefetchScalarGridSpec(
            num_scalar_prefetch=1, grid=(NT,),
            in_specs=[pl.BlockSpec((TM, D), lambda i, u: (i, 0)),
                      pl.BlockSpec((PLAN_ROWS, TM), lambda i, u: (0, i))],
            out_specs=pl.BlockSpec((MOE_SLOTS, D), lambda i, u: (i, 0))),
        out_shape=jax.ShapeDtypeStruct((NT * MOE_SLOTS, D), BF16),
        compiler_params=_params("parallel"),
        name="moe_dispatch",
    )(used, h, plan_row)


def _experts_kernel(tab_ref, gexp_ref, nused_ref, *refs):
    x_refs = refs[:MOE_GC]
    w1_ref, w3_ref, w2_ref, y_ref, x_scr, acc = refs[MOE_GC:]
    f = pl.program_id(1)

    @pl.when(pl.program_id(0) < nused_ref[0])
    def _():
        @pl.when(f == 0)
        def _():
            for j, r in enumerate(x_refs):
                x_scr[j * MOE_CH:(j + 1) * MOE_CH, :] = r[...]
            acc[...] = jnp.zeros_like(acc)

        _swiglu_step(x_scr[...], w1_ref, w3_ref, w2_ref, acc)

        @pl.when(f == MOE_NF - 1)
        def _():
            y_ref[...] = acc[...].astype(BF16)


def _moe_experts(xs, tab, gexp, nused, w1, w3, w2, e0):
    grp = lambda g, nu: jnp.minimum(g, nu[0] - 1)
    ftile = lambda g, f, nu: jnp.where(g < nu[0], f, MOE_NF - 1)
    chunk = lambda j: pl.BlockSpec((MOE_CH, D), lambda g, f, tab, ge, nu: (tab[grp(g, nu) * MOE_GC + j], 0))
    rows = MOE_GC * MOE_CH
    return pl.pallas_call(
        _experts_kernel,
        grid_spec=pltpu.PrefetchScalarGridSpec(
            num_scalar_prefetch=3, grid=(MOE_GROUPS, MOE_NF),
            in_specs=[chunk(j) for j in range(MOE_GC)] + [
                pl.BlockSpec((1, D, MOE_TF), lambda g, f, tab, ge, nu: (e0 + ge[grp(g, nu)], 0, ftile(g, f, nu))),
                pl.BlockSpec((1, D, MOE_TF), lambda g, f, tab, ge, nu: (e0 + ge[grp(g, nu)], 0, ftile(g, f, nu))),
                pl.BlockSpec((1, MOE_TF, D), lambda g, f, tab, ge, nu: (e0 + ge[grp(g, nu)], ftile(g, f, nu), 0))],
            out_specs=pl.BlockSpec((rows, D), lambda g, f, tab, ge, nu: (grp(g, nu), 0)),
            scratch_shapes=[pltpu.VMEM((rows, D), BF16), pltpu.VMEM((rows, D), F32)]),
        out_shape=jax.ShapeDtypeStruct((MOE_GROUPS * rows, D), BF16),
        compiler_params=_params("arbitrary", "arbitrary"),
        name="moe_experts",
    )(tab, gexp, nused, *([xs] * MOE_GC), w1, w3, w2)


def _combine_kernel(used_ref, pool_ref, *refs, final):
    refs = list(refs)
    y_refs = [refs.pop(0) for _ in range(MOE_NCH)]
    col_ref, x_ref, mod_ref = refs[:3]
    fg_ref = refs[3] if final else None
    o_ref, y_scr, acc = refs[-3:]
    used = used_ref[pl.program_id(0)]
    col = col_ref[...]
    s0, s1, g0, g1 = col[:, 0:1], col[:, 1:2], col[:, 2:3], col[:, 3:4]
    acc[...] = jnp.zeros_like(acc)
    per_blk = MOE_SB // MOE_CH
    for blk in range(MOE_SLOTS // MOE_SB):
        @pl.when(blk * per_blk < used)
        def _():
            for q in range(blk * per_blk, (blk + 1) * per_blk):
                y_scr[q * MOE_CH:(q + 1) * MOE_CH, :] = y_refs[q][...]
            c = (lax.broadcasted_iota(jnp.int32, (TM, MOE_SB), 1) + blk * MOE_SB).astype(F32)
            w = jnp.where(c == s0, g0, jnp.where(c == s1, g1, 0.0)).astype(BF16)
            acc[...] += jnp.dot(w, y_scr[blk * MOE_SB:(blk + 1) * MOE_SB, :], preferred_element_type=F32)
    o_ref[...] = _residual_out(x_ref[...], mod_ref[0][:, 5 * D:6 * D], acc[...], fg_ref)


def _moe_combine(y, pool, used, plan_col, x, mod_l, final_g=None):
    final = final_g is not None
    rows = LAT_ROWS if final else ROWS
    tile = lambda w: pl.BlockSpec((TM, w), lambda i, u, pc: (i, 0))
    chunk = lambda q: pl.BlockSpec((MOE_CH, D), lambda i, u, pc: (pc[i * MOE_NCH + q], 0))
    in_specs = [chunk(q) for q in range(MOE_NCH)] + [
        tile(LANE), tile(D), pl.BlockSpec((1, 1, N_MOD * D), lambda i, u, pc: (_mod_row(i), 0, 0))]
    args = [y] * MOE_NCH + [plan_col, x, mod_l]
    if final:
        in_specs.append(pl.BlockSpec((1, D), lambda i, u, pc: (0, 0)))
        args.append(final_g.reshape(1, D))
    return pl.pallas_call(
        functools.partial(_combine_kernel, final=final),
        grid_spec=pltpu.PrefetchScalarGridSpec(
            num_scalar_prefetch=2, grid=(rows // TM,),
            in_specs=in_specs, out_specs=tile(D),
            scratch_shapes=[pltpu.VMEM((MOE_SLOTS, D), BF16), pltpu.VMEM((TM, D), F32)]),
        out_shape=jax.ShapeDtypeStruct((rows, D), F32),
        compiler_params=_params("parallel"),
        name="moe_combine",
    )(used, pool, *args)


def _moe_mixer(h, logits, x, mod_l, w1, w3, w2, e0, final_g=None):
    plan_col, plan_row, nch = _moe_plan(logits)
    tab, gexp, nused, used, pool = _moe_tables(nch)
    xs = _moe_dispatch(h, plan_row, used)
    y = _moe_experts(xs, tab, gexp, nused, w1, w3, w2, e0)
    return _moe_combine(y, pool, used, plan_col, x, mod_l, final_g)


def kernel(x, c, ctx, c_ctx, w_mod, b_mod, norm1_g, norm2_g, w_in, rpb, hy_conv_w, hy_conv_b, hy_w1, hy_b1, hy_w2, hy_b2, hy_w3, hy_freq, hy_bias, w_pa, w_pb, w_o, ffn_w1, ffn_w3, ffn_w2, moe_router, moe_w1, moe_w3, moe_w2, final_g):
    xs = jnp.concatenate([x.reshape(LAT_ROWS, D), ctx.reshape(CTX_ROWS, D)], axis=0)
    mod = _modulation(c, c_ctx, w_mod, b_mod)
    tabs = _fft_tables()
    ctabs = _ctx_tables()
    filt_lat = _implicit_filters(SEQ, hy_w1, hy_b1, hy_w2, hy_b2, hy_w3, hy_freq)
    filt_ctx = _implicit_filters(CTX, hy_w1, hy_b1, hy_w2, hy_b2, hy_w3, hy_freq)
    spec_re, spec_im = _filter_spectra(filt_lat, tabs)
    nblk = HY_W // LANE

    moe_w = [w.reshape((-1,) + w.shape[2:]).astype(BF16) for w in (moe_w1, moe_w3, moe_w2)]
    ffn_w = [w.astype(BF16) for w in (ffn_w1, ffn_w3, ffn_w2)]
    w_in_b = w_in.astype(BF16)
    bias_tabs = _bias_tables(rpb)

    for l in range(DEPTH):
        last = l == DEPTH - 1
        mod_l = mod[l].reshape(MOD_ROWS, 1, N_MOD * D)
        cw, cb = hy_conv_w[l], hy_conv_b[l].reshape(1, 3 * HY_W)
        p = _in_proj(xs, mod_l, norm1_g[l].reshape(1, D), w_in_b, l)

        o_att = _context_attention(p, _neighbourhood_attention(p, bias_tabs, l))
        z = _hyena_conv(p, COL_HV, p, COL_HX1, cw, cb, 0, nblk, spec_re, spec_im, l, 0,
                        hy_bias[l], tabs, True)
        o_hy = _hyena_conv(z, 0, p, COL_HX2, cw, cb, 0, 2 * nblk, spec_re, spec_im, l, 1,
                           hy_bias[l], tabs, False)
        o_hy = _ctx_hyena(p, cw, cb, filt_ctx, l, hy_bias[l], ctabs, o_hy)

        i = l // 2
        router = moe_router[i] if l % 2 else None
        res = _merge(o_att, o_hy, p, xs, mod_l, norm2_g[l].reshape(1, D), w_pa, w_pb, w_o, l, router)
        fg = final_g if last else None
        if l % 2 == 0:
            xs = _dense_mixer(res[1], res[0], mod_l, *ffn_w, i, final_g=fg)
        else:
            xs = _moe_mixer(res[1], res[2], res[0], mod_l, *moe_w, i * N_EXPERTS, final_g=fg)
    return xs.reshape(NB, SEQ, D)
```

```python
import functools
import math

import numpy as np
import jax
import jax.numpy as jnp
from jax import lax
from jax.experimental import pallas as pl
from jax.experimental.pallas import tpu as pltpu

F32 = jnp.float32
BF16 = jnp.bfloat16
HIGHEST = lax.Precision.HIGHEST

D = 1024
NB = 4
SEQ = 4096
DEPTH = 4
CTX = 256
GRID_W = 64
GRID_H = SEQ // GRID_W
N_HEADS = 8
HEAD_DIM = 64
ATT_W = N_HEADS * HEAD_DIM
WIN_ROWS = 8
WIN_COLS = 16
HY_W = 512
HY_ORDER = 2
HY_SHORT = 3
HY_EMB = 33
HY_BANDS = (HY_EMB - 1) // 2
HY_FILT = 64
HY_DECAY_TARGET = 1e-2
HY_MAX_DECAY = math.log(HY_DECAY_TARGET) / 0.3
HY_MIN_DECAY = math.log(HY_DECAY_TARGET) / 1.5
HY_DECAY_SHIFT = 0.05
PROJ_W = 3 * ATT_W + 3 * HY_W + 2 * D
D_FF = 2816
N_EXPERTS = 8
N_MOD = 6
EPS = 1e-6

LAT_ROWS = NB * SEQ
CTX_ROWS = NB * CTX
ROWS = LAT_ROWS + CTX_ROWS
MOD_ROWS = 8
CTX_MOD_ROW = NB

LANE = 128
TM = 1024
VMEM_LIMIT = 56 * 1024 * 1024

COL_Q, COL_K, COL_V = 0, ATT_W // LANE, 2 * ATT_W // LANE
COL_HV = 3 * ATT_W // LANE
COL_HX1 = COL_HV + HY_W // LANE
COL_HX2 = COL_HX1 + HY_W // LANE
COL_GA = (3 * ATT_W + 3 * HY_W) // D
COL_GB = COL_GA + 1


def _mod_row(i):
    return jnp.where(i < LAT_ROWS // TM, (i * TM) // SEQ, CTX_MOD_ROW)


def _params(*sem):
    return pltpu.CompilerParams(dimension_semantics=sem, vmem_limit_bytes=VMEM_LIMIT)


def _mod_kernel(s_ref, w_ref, b_ref, o_ref):
    s = s_ref[...]
    s = s * jax.nn.sigmoid(s)
    o_ref[0] = _dot_hilo(s, w_ref[0]) + b_ref[0]


def _modulation(c, c_ctx, w_mod, b_mod):
    s = jnp.zeros((MOD_ROWS, D), F32).at[:NB].set(c).at[CTX_MOD_ROW].set(c_ctx)
    tn = 1536
    return pl.pallas_call(
        _mod_kernel,
        grid=(DEPTH, N_MOD * D // tn),
        in_specs=[pl.BlockSpec((MOD_ROWS, D), lambda l, j: (0, 0)),
                  pl.BlockSpec((1, D, tn), lambda l, j: (l, 0, j)),
                  pl.BlockSpec((1, 1, tn), lambda l, j: (l, 0, j))],
        out_specs=pl.BlockSpec((1, MOD_ROWS, tn), lambda l, j: (l, 0, j)),
        out_shape=jax.ShapeDtypeStruct((DEPTH, MOD_ROWS, N_MOD * D), F32),
        compiler_params=_params("parallel", "parallel"),
        name="modulation",
    )(s, w_mod, b_mod.reshape(DEPTH, 1, N_MOD * D))


def _dot_hilo(a, b):
    ah, bh = a.astype(BF16), b.astype(BF16)
    al, bl = (a - ah.astype(F32)).astype(BF16), (b - bh.astype(F32)).astype(BF16)
    dot = functools.partial(jnp.dot, preferred_element_type=F32)
    return dot(ah, bh) + (dot(ah, bl) + dot(al, bh))


def _norm_mod(x, g, shift, scale):
    y = x * lax.rsqrt(jnp.mean(x * x, axis=-1, keepdims=True) + EPS) * g
    return y * (1.0 + scale) + shift


INPROJ_TN = 1024


def _inproj_kernel(x_ref, mod_ref, g_ref, w_ref, o_ref):
    m = mod_ref[0]
    h = _norm_mod(x_ref[...], g_ref[...], m[:, 0:D], m[:, D:2 * D]).astype(BF16)
    for j in range(PROJ_W // INPROJ_TN):
        cols = slice(j * INPROJ_TN, (j + 1) * INPROJ_TN)
        o_ref[:, cols] = jnp.dot(h, w_ref[0, :, cols].astype(BF16), preferred_element_type=F32).astype(BF16)


def _in_proj(x, mod_l, g, w, layer):
    return pl.pallas_call(
        _inproj_kernel,
        grid=(ROWS // TM,),
        in_specs=[pl.BlockSpec((TM, D), lambda i: (i, 0)),
                  pl.BlockSpec((1, 1, N_MOD * D), lambda i: (_mod_row(i), 0, 0)),
                  pl.BlockSpec((1, D), lambda i: (0, 0)),
                  pl.BlockSpec((1, D, PROJ_W), lambda i: (layer, 0, 0), pipeline_mode=pl.Buffered(1))],
        out_specs=pl.BlockSpec((TM, PROJ_W), lambda i: (i, 0)),
        out_shape=jax.ShapeDtypeStruct((ROWS, PROJ_W), BF16),
        compiler_params=_params("parallel"),
        name="in_proj",
    )(x, mod_l, g, w)


NEG = -1e30


def _nt_dot(a, b):
    return lax.dot_general(a, b, (((1,), (1,)), ((), ())), preferred_element_type=F32)


def _attend_pair(qp, keys, values, biases):
    lane = lax.broadcasted_iota(jnp.int32, qp.shape, 1)
    qs = qp * jnp.asarray(HEAD_DIM ** -0.5, BF16)
    outs = []
    for a in range(2):
        sel = (lane < HEAD_DIM) if a == 0 else (lane >= HEAD_DIM)
        qa = jnp.where(sel, qs, jnp.zeros_like(qs))
        ss = []
        for k, b in zip(keys, biases):
            s = _nt_dot(qa, k)
            if b is not None:
                s = s + b[a]
            ss.append(s)
        m = functools.reduce(jnp.maximum, [jnp.max(s, axis=-1, keepdims=True) for s in ss])
        ps = [jnp.exp(s - m) for s in ss]
        den = functools.reduce(jnp.add, [jnp.sum(p, axis=-1, keepdims=True) for p in ps])
        o = functools.reduce(jnp.add, [jnp.dot(p.astype(BF16), v, preferred_element_type=F32)
                                       for p, v in zip(ps, values)])
        outs.append(o / den)
    return jnp.where(lane < HEAD_DIM, outs[0], outs[1])


NLOC = WIN_ROWS * GRID_W
NKEY = NLOC + CTX
NA_PAIRS = GRID_H // 2
NA_STEPS = NA_PAIRS + 2
NA_UNITS = 2 * N_HEADS


def _window_start(r):
    r0 = jnp.clip(r - WIN_ROWS // 2, 0, GRID_H - WIN_ROWS)
    return pl.multiple_of(r0 * GRID_W, GRID_W)


def _natt_kernel(q_ref, k_ref, v_ref, kc_ref, vc_ref, bt0_ref, bt1_ref, o_ref, s0, s1, p0, p1, l0, l1):
    j = pl.program_id(1)
    lane = lax.broadcasted_iota(jnp.int32, (GRID_W, LANE), 1)
    lo = lane < HEAD_DIM
    hi = lane >= HEAD_DIM

    @pl.when(j == 0)
    def _():
        s1[...] = jnp.zeros_like(s1)
        p0[...] = jnp.zeros_like(p0)
        p1[...] = jnp.zeros_like(p1)
        l0[...] = jnp.ones_like(l0)
        l1[...] = jnp.ones_like(l1)

    def step(s_cur, s_prev, p_cur, p_prev, l_cur, l_prev):
        ro = jnp.maximum(2 * j - 4, 0)
        rs = jnp.minimum(2 * j, GRID_H - 2)
        for half, bt_ref in enumerate((bt0_ref, bt1_ref)):
            start_o = _window_start(ro + half)
            start_s = _window_start(rs + half)
            rows = slice(half * GRID_W, (half + 1) * GRID_W)
            for hp in range(N_HEADS // 2):
                cs = slice(hp * LANE, (hp + 1) * LANE)
                u0 = half * N_HEADS + 2 * hp
                units = (u0, u0 + 1)
                pair = slice(u0, u0 + 2)
                m2 = 2 * GRID_W

                p2 = p_cur[pair].reshape(m2, NKEY)
                acc = (jnp.dot(p2[:, 0:NLOC], v_ref[pl.ds(start_o, NLOC), cs], preferred_element_type=F32)
                       + jnp.dot(p2[:, NLOC:NKEY], vc_ref[:, cs], preferred_element_type=F32))
                o_ref[rows, cs] = jnp.where(lo, acc[0:GRID_W] / l_cur[u0],
                                            acc[GRID_W:m2] / l_cur[u0 + 1]).astype(BF16)

                for u in units:
                    x = s_prev[u]
                    e = jnp.exp(x - jnp.max(x, axis=-1, keepdims=True))
                    l_prev[u] = jnp.broadcast_to(jnp.sum(e, axis=-1, keepdims=True), (GRID_W, LANE))
                    p_prev[u] = e.astype(BF16)

                qs = q_ref[rows, cs] * jnp.asarray(HEAD_DIM ** -0.5, BF16)
                zero = jnp.zeros_like(qs)
                q2 = jnp.concatenate([jnp.where(lo, qs, zero), jnp.where(hi, qs, zero)], axis=0)
                bias = bt_ref[0, 2 * hp:2 * hp + 2].astype(F32)
                s_cur[pair, :, 0:NLOC] = _nt_dot(q2, k_ref[pl.ds(start_s, NLOC), cs]).reshape(2, GRID_W, NLOC) + bias
                s_cur[pair, :, NLOC:NKEY] = _nt_dot(q2, kc_ref[:, cs]).reshape(2, GRID_W, CTX)

    @pl.when(j % 2 == 0)
    def _():
        step(s0, s1, p0, p1, l0, l1)

    @pl.when(j % 2 == 1)
    def _():
        step(s1, s0, p1, p0, l1, l0)


def _bias_tables(rpb):
    col = np.arange(GRID_W)
    c0 = np.clip(col - WIN_COLS // 2, 0, GRID_W - WIN_COLS)
    valid = (col[None, :] >= c0[:, None]) & (col[None, :] < c0[:, None] + WIN_COLS)
    dc = np.clip(col[None, :] - col[:, None] + (WIN_COLS - 1), 0, 2 * WIN_COLS - 2)
    pick = (np.arange(2 * WIN_COLS - 1)[:, None, None] == dc[None]).astype(np.float32)
    t = jnp.einsum('lhrd,dqk->lhrqk', rpb, jnp.asarray(pick), precision=HIGHEST)
    t = jnp.where(valid, t, NEG)
    win = jnp.stack([t[:, :, d0:d0 + WIN_ROWS] for d0 in range(WIN_ROWS)], axis=1)
    win = win.transpose(0, 1, 2, 4, 3, 5)
    return win.reshape(DEPTH * WIN_ROWS, N_HEADS, GRID_W, NLOC).astype(BF16)


def _bias_index(r):
    r0 = jnp.clip(r - WIN_ROWS // 2, 0, GRID_H - WIN_ROWS)
    return r0 - r + (WIN_ROWS - 1)


def _neighbourhood_attention(p, bt, layer):
    blk = GRID_H // 2
    ctx_blk0 = LAT_ROWS // CTX
    score_row = lambda j, half: jnp.minimum(2 * j, GRID_H - 2) + half
    bias_spec = lambda half: pl.BlockSpec(
        (1, N_HEADS, GRID_W, NLOC), lambda b, j: (layer * WIN_ROWS + _bias_index(score_row(j, half)), 0, 0, 0))
    return pl.pallas_call(
        _natt_kernel,
        grid=(NB, NA_STEPS),
        in_specs=[pl.BlockSpec((2 * GRID_W, ATT_W), lambda b, j: (b * blk + jnp.minimum(j, blk - 1), 0)),
                  pl.BlockSpec((SEQ, ATT_W), lambda b, j: (b, 1)),
                  pl.BlockSpec((SEQ, ATT_W), lambda b, j: (b, 2)),
                  pl.BlockSpec((CTX, ATT_W), lambda b, j: (ctx_blk0 + b, 1)),
                  pl.BlockSpec((CTX, ATT_W), lambda b, j: (ctx_blk0 + b, 2)),
                  bias_spec(0), bias_spec(1)],
        out_specs=pl.BlockSpec((2 * GRID_W, ATT_W), lambda b, j: (b * blk + jnp.maximum(j - 2, 0), 0)),
        out_shape=jax.ShapeDtypeStruct((ROWS, ATT_W), BF16),
        scratch_shapes=[pltpu.VMEM((NA_UNITS, GRID_W, NKEY), F32)] * 2
                       + [pltpu.VMEM((NA_UNITS, GRID_W, NKEY), BF16)] * 2
                       + [pltpu.VMEM((NA_UNITS, GRID_W, LANE), F32)] * 2,
        compiler_params=_params("parallel", "arbitrary"),
        name="neighbourhood_attention",
    )(p, p, p, p, p, bt, bt)


def _catt_kernel(q_ref, k_ref, v_ref, dst_ref, o_ref):
    del dst_ref
    for hp in range(N_HEADS // 2):
        cs = slice(hp * LANE, (hp + 1) * LANE)
        o = _attend_pair(q_ref[:, cs], [k_ref[:, cs]], [v_ref[:, cs]], [None])
        o_ref[:, cs] = o.astype(BF16)


def _context_attention(p, o_att):
    ctx_blk0 = LAT_ROWS // CTX
    return pl.pallas_call(
        _catt_kernel,
        grid=(NB,),
        in_specs=[pl.BlockSpec((CTX, ATT_W), lambda b: (ctx_blk0 + b, 0)),
                  pl.BlockSpec((CTX, ATT_W), lambda b: (ctx_blk0 + b, 1)),
                  pl.BlockSpec((CTX, ATT_W), lambda b: (ctx_blk0 + b, 2)),
                  pl.BlockSpec(memory_space=pl.ANY)],
        out_specs=pl.BlockSpec((CTX, ATT_W), lambda b: (ctx_blk0 + b, 0)),
        out_shape=jax.ShapeDtypeStruct((ROWS, ATT_W), BF16),
        input_output_aliases={3: 0},
        compiler_params=_params("parallel"),
        name="context_attention",
    )(p, p, p, o_att)


def _filter_features(L):
    pos = np.arange(L, dtype=np.float64)
    t = pos / max(L - 1, 1)
    bands = np.linspace(1e-4, HY_BANDS - 1, HY_BANDS)
    ang = (2.0 * math.pi / L) * pos[:, None] * bands[None, :]
    z = np.concatenate([t[:, None], np.cos(ang), -np.sin(ang)], axis=-1)
    zp = np.zeros((L, LANE), np.float32)
    zp[:, :HY_EMB] = z
    return zp


def _filter_kernel(z_ref, w1_ref, b1_ref, w2_ref, b2_ref, w3_ref, fr_ref, rate_ref, o_ref, *, L, tl):
    fr = fr_ref[0]
    a = jnp.sin(fr[0:1] * (_dot_hilo(z_ref[...], w1_ref[0]) + b1_ref[0]))
    a = jnp.sin(fr[1:2] * (_dot_hilo(a, w2_ref[0]) + b2_ref[0]))
    h = _dot_hilo(a, w3_ref[0])
    pos = (pl.program_id(1) * tl + lax.broadcasted_iota(jnp.int32, (tl, 1), 0)).astype(F32)
    dist = jnp.abs(pos - float(L // 2)) * (2.0 / L)
    o_ref[0] = h * (jnp.exp(-dist * rate_ref[...]) + HY_DECAY_SHIFT)


def _implicit_filters(L, w1, b1, w2, b2, w3, freq):
    tl = min(L, 512)
    z = jnp.asarray(_filter_features(L))
    w1p = jnp.zeros((DEPTH, LANE, HY_FILT), F32).at[:, :HY_EMB].set(w1)
    rates = np.abs(np.linspace(HY_MIN_DECAY, HY_MAX_DECAY, HY_ORDER * HY_W)).astype(np.float32)
    cw = HY_ORDER * HY_W
    return pl.pallas_call(
        functools.partial(_filter_kernel, L=L, tl=tl),
        grid=(DEPTH, L // tl),
        in_specs=[pl.BlockSpec((tl, LANE), lambda l, i: (i, 0)),
                  pl.BlockSpec((1, LANE, HY_FILT), lambda l, i: (l, 0, 0)),
                  pl.BlockSpec((1, 1, HY_FILT), lambda l, i: (l, 0, 0)),
                  pl.BlockSpec((1, HY_FILT, HY_FILT), lambda l, i: (l, 0, 0)),
                  pl.BlockSpec((1, 1, HY_FILT), lambda l, i: (l, 0, 0)),
                  pl.BlockSpec((1, HY_FILT, cw), lambda l, i: (l, 0, 0)),
                  pl.BlockSpec((1, 2, HY_FILT), lambda l, i: (l, 0, 0)),
                  pl.BlockSpec((1, cw), lambda l, i: (0, 0))],
        out_specs=pl.BlockSpec((1, tl, cw), lambda l, i: (l, i, 0)),
        out_shape=jax.ShapeDtypeStruct((DEPTH, L, cw), F32),
        compiler_params=_params("parallel", "parallel"),
        name=f"implicit_filters_{L}",
    )(z, w1p, b1.reshape(DEPTH, 1, HY_FILT), w2, b2.reshape(DEPTH, 1, HY_FILT), w3, freq,
      jnp.asarray(rates).reshape(1, cw))


FFT_N = 2 * SEQ
FA = 64
FB = 128
FA_IN = SEQ // FB
PITCH = 136
HALO = 8
UNROLL_ROWS = 64
UNROLL_SLABS = 32


def _fft_tables():
    a = np.arange(FA_IN)
    r = np.arange(FA)
    b = np.arange(FB)
    q = np.arange(FB)
    ph = a[None, None, :] * r[None, :, None] / FA + b[:, None, None] * r[None, :, None] / FFT_N
    g1 = np.exp(-2j * np.pi * ph)
    g1 = np.concatenate([g1.real, g1.imag], axis=1)
    f = np.exp(-2j * np.pi * q[:, None] * b[None, :] / FB)
    f2 = np.concatenate([f.real, f.imag], axis=0)
    f2c = np.concatenate([f.real, -f.imag], axis=0)
    ao = np.arange(FA_IN) + (SEQ // 2) // FB
    ph2 = ao[None, :, None] * r[None, None, :] / FA + b[:, None, None] * r[None, None, :] / FFT_N
    g2 = np.exp(2j * np.pi * ph2) / FFT_N
    g2 = np.concatenate([g2.real, g2.imag], axis=1)
    return tuple(jnp.asarray(t, dtype=BF16) for t in (g1, f2, f2c, g2))


def _cplx(out, m, n):
    re = out[0:m, 0:n] - out[m:2 * m, n:2 * n]
    im = out[0:m, n:2 * n] + out[m:2 * m, 0:n]
    return re, im


def _short_conv_slabs(src_ref, rows, t_scr, cw_ref, cb_ref, apply_conv, emit):
    nslab = rows // FB
    if not apply_conv:
        for a in range(nslab):
            emit(a, src_ref[a * FB:(a + 1) * FB, :].astype(F32))
        return
    zero = jnp.zeros((HALO, LANE), F32)
    t_scr[0:HALO, :] = zero
    t_scr[HALO + rows:2 * HALO + rows, :] = zero
    for a in range(nslab):
        t_scr[HALO + a * FB:HALO + (a + 1) * FB, :] = src_ref[a * FB:(a + 1) * FB, :].astype(F32)
    w0, w1, w2, bias = cw_ref[0:1, :], cw_ref[1:2, :], cw_ref[2:3, :], cb_ref[...]
    for a in range(nslab):
        o = HALO + a * FB
        y = (w0 * t_scr[o - 1:o - 1 + FB, :] + w1 * t_scr[o:o + FB, :]
             + w2 * t_scr[o + 1:o + 1 + FB, :] + bias)
        emit(a, y)


def _fwd_stage1(b, s_refs, g1_ref, y_r, y_i):
    cols = [s[pl.ds(b, FA_IN, stride=PITCH), :].astype(BF16) for s in s_refs]
    rhs = cols[0] if len(cols) == 1 else jnp.concatenate(cols, axis=1)
    out = jnp.dot(g1_ref[b], rhs, preferred_element_type=F32)
    if len(cols) == 1:
        re, im = out[0:FA], out[FA:2 * FA]
    else:
        re, im = _cplx(out, FA, LANE)
    y_r[pl.ds(b, FA, stride=PITCH), :] = re
    y_i[pl.ds(b, FA, stride=PITCH), :] = im


def _fwd_stage2(r, y_r, y_i, f_ref):
    o = pl.multiple_of(r * PITCH, 8)
    rhs = jnp.concatenate([y_r[pl.ds(o, FB), :].astype(BF16), y_i[pl.ds(o, FB), :].astype(BF16)], axis=1)
    out = jnp.dot(f_ref[...], rhs, preferred_element_type=F32)
    return _cplx(out, FB, LANE)


def _spectrum_kernel(h_ref, g1_ref, f_ref, hr_ref, hi_ref, s_scr, y_r, y_i):
    for a in range(FA_IN):
        s_scr[a * PITCH:a * PITCH + FB, :] = h_ref[0, a * FB:(a + 1) * FB, :]

    def s1(b, c):
        _fwd_stage1(b, [s_scr], g1_ref, y_r, y_i)
        return c
    lax.fori_loop(0, FB, s1, 0, unroll=UNROLL_ROWS)

    def s2(r, c):
        xr, xi = _fwd_stage2(r, y_r, y_i, f_ref)
        o = pl.multiple_of(r * FB, FB)
        hr_ref[0, pl.ds(o, FB), :] = xr
        hi_ref[0, pl.ds(o, FB), :] = xi
        return c
    lax.fori_loop(0, FA, s2, 0, unroll=UNROLL_SLABS)


def _filter_spectra(h, tabs):
    g1, f2, _, _ = tabs
    cw = HY_ORDER * HY_W
    spec = pl.BlockSpec((1, FFT_N, LANE), lambda l, c: (l, 0, c))
    return pl.pallas_call(
        _spectrum_kernel,
        grid=(DEPTH, cw // LANE),
        in_specs=[pl.BlockSpec((1, SEQ, LANE), lambda l, c: (l, 0, c)),
                  pl.BlockSpec(g1.shape, lambda l, c: (0, 0, 0)),
                  pl.BlockSpec(f2.shape, lambda l, c: (0, 0))],
        out_specs=[spec, spec],
        out_shape=[jax.ShapeDtypeStruct((DEPTH, FFT_N, cw), F32)] * 2,
        scratch_shapes=[pltpu.VMEM((FA_IN * PITCH, LANE), F32),
                        pltpu.VMEM((FA * PITCH, LANE), F32),
                        pltpu.VMEM((FA * PITCH, LANE), F32)],
        compiler_params=_params("parallel", "parallel"),
        name="filter_spectra",
    )(h, g1, f2)


def _hyconv_kernel(u_ref, x_ref, cwu_ref, cbu_ref, cwx_ref, cbx_ref,
                   hr_ref, hi_ref, bias_ref, g1_ref, f_ref, fc_ref, g2_ref,
                   o_ref, t_scr, s_r, s_i, y_r, y_i, *, conv_input):
    halves = [(s_r, slice(0, SEQ)), (s_i, slice(SEQ, 2 * SEQ))]

    def fill(s):
        def emit(a, slab):
            s[a * PITCH:a * PITCH + FB, :] = slab
        return emit
    for s, rows in halves:
        _short_conv_slabs(u_ref.at[rows, :], SEQ, t_scr, cwu_ref, cbu_ref, conv_input, fill(s))

    def s1(b, c):
        _fwd_stage1(b, [s_r, s_i], g1_ref, y_r, y_i)
        return c
    lax.fori_loop(0, FB, s1, 0, unroll=UNROLL_ROWS)

    def s2(r, c):
        xr, xi = _fwd_stage2(r, y_r, y_i, f_ref)
        oh = pl.multiple_of(r * FB, FB)
        hr = hr_ref[0, pl.ds(oh, FB), :]
        hi = hi_ref[0, pl.ds(oh, FB), :]
        vr = xr * hr - xi * hi
        vi = xr * hi + xi * hr
        rhs = jnp.concatenate([vr.astype(BF16), vi.astype(BF16)], axis=1)
        wr, wi = _cplx(jnp.dot(fc_ref[...], rhs, preferred_element_type=F32), FB, LANE)
        o = pl.multiple_of(r * PITCH, 8)
        y_r[pl.ds(o, FB), :] = wr
        y_i[pl.ds(o, FB), :] = wi
        return c
    lax.fori_loop(0, FA, s2, 0, unroll=UNROLL_SLABS)

    bias = bias_ref[0]

    def s3(b, c):
        rhs = jnp.concatenate([y_r[pl.ds(b, FA, stride=PITCH), :].astype(BF16),
                               y_i[pl.ds(b, FA, stride=PITCH), :].astype(BF16)], axis=1)
        re, im = _cplx(jnp.dot(g2_ref[b], rhs, preferred_element_type=F32), FA_IN, LANE)
        idx = pl.ds(b, FA_IN, stride=PITCH)
        s_r[idx, :] = re + s_r[idx, :] * bias
        s_i[idx, :] = im + s_i[idx, :] * bias
        return c
    lax.fori_loop(0, FB, s3, 0, unroll=UNROLL_ROWS)

    for s, rows in halves:
        def emit(a, slab, s=s, out=o_ref.at[rows, :]):
            out[a * FB:(a + 1) * FB, :] = (slab * s[a * PITCH:a * PITCH + FB, :]).astype(out.dtype)
        _short_conv_slabs(x_ref.at[rows, :], SEQ, t_scr, cwx_ref, cbx_ref, True, emit)


def _hyena_conv(u, u_col, x, x_col, conv_w, conv_b, cu_col, cx_col, spec_re, spec_im, layer, order,
                hy_bias_l, tabs, conv_input):
    g1, f2, f2c, g2 = tabs
    nblk = HY_W // LANE
    tok = lambda col: pl.BlockSpec((2 * SEQ, LANE), lambda c, p: (p, col + c))
    chan = lambda rows, col: pl.BlockSpec((rows, LANE), lambda c, p: (0, col + c))
    hspec = pl.BlockSpec((1, FFT_N, LANE), lambda c, p: (layer, 0, order * nblk + c))
    const = lambda shape: pl.BlockSpec(shape, lambda c, p: (0,) * len(shape))
    return pl.pallas_call(
        functools.partial(_hyconv_kernel, conv_input=conv_input),
        grid=(nblk, NB // 2),
        in_specs=[tok(u_col), tok(x_col),
                  chan(HY_SHORT, cu_col), chan(1, cu_col), chan(HY_SHORT, cx_col), chan(1, cx_col),
                  hspec, hspec,
                  pl.BlockSpec((1, 1, LANE), lambda c, p: (order, 0, c)),
                  const(g1.shape), const(f2.shape), const(f2c.shape), const(g2.shape)],
        out_specs=pl.BlockSpec((2 * SEQ, LANE), lambda c, p: (p, c)),
        out_shape=jax.ShapeDtypeStruct((ROWS, HY_W), BF16),
        scratch_shapes=[pltpu.VMEM((SEQ + 2 * HALO, LANE), F32),
                        pltpu.VMEM((FA_IN * PITCH, LANE), F32),
                        pltpu.VMEM((FA_IN * PITCH, LANE), F32),
                        pltpu.VMEM((FA * PITCH, LANE), F32),
                        pltpu.VMEM((FA * PITCH, LANE), F32)],
        compiler_params=_params("arbitrary", "arbitrary"),
        name=f"hyena_conv{order}",
    )(u, x, conv_w, conv_b, conv_w, conv_b, spec_re, spec_im,
      hy_bias_l.reshape(HY_ORDER, 1, HY_W), g1, f2, f2c, g2)


CN = 2 * CTX


def _ctx_tables():
    k = np.arange(CN)
    n = np.arange(CTX)
    f = np.exp(-2j * np.pi * k[:, None] * n[None, :] / CN)
    fwd = np.concatenate([f.real, f.imag], axis=0)
    t = np.arange(CTX) + CTX // 2
    g = np.exp(2j * np.pi * t[:, None] * k[None, :] / CN) / CN
    inv = np.concatenate([g.real, g.imag], axis=0)
    return jnp.asarray(fwd, dtype=BF16), jnp.asarray(inv, dtype=BF16)


def _ctx_hyena_kernel(v_ref, x1_ref, x2_ref, cwv_ref, cbv_ref, cw1_ref, cb1_ref, cw2_ref, cb2_ref,
                      h0_ref, h1_ref, bias_ref, fwd_ref, inv_ref, dst_ref, o_ref, t_scr, s_scr, g_scr):
    del dst_ref
    def conv_into(src_ref, b, cw_ref, cb_ref, dst, col):
        view = src_ref.at[b * CTX:(b + 1) * CTX, :]

        def emit(a, slab):
            dst[a * FB:(a + 1) * FB, col * LANE:(col + 1) * LANE] = slab
        _short_conv_slabs(view, CTX, t_scr, cw_ref, cb_ref, True, emit)

    def spectrum(h_ref):
        out = jnp.dot(fwd_ref[...], h_ref[0].astype(BF16), preferred_element_type=F32)
        return out[0:CN], out[CN:2 * CN]

    def long_conv(z2, hr, hi, bias):
        x = jnp.dot(fwd_ref[...], z2.astype(BF16), preferred_element_type=F32)
        xr, xi = _cplx(x, CN, LANE)
        vr = xr * hr - xi * hi
        vi = xr * hi + xi * hr
        rhs = jnp.concatenate([vr.astype(BF16), vi.astype(BF16)], axis=1)
        y = jnp.dot(inv_ref[...], rhs, preferred_element_type=F32)
        yr, yi = _cplx(y, CTX, LANE)
        return jnp.concatenate([yr, yi], axis=1) + z2 * jnp.concatenate([bias, bias], axis=1)

    h0r, h0i = spectrum(h0_ref)
    h1r, h1i = spectrum(h1_ref)
    for p in range(NB // 2):
        for half in range(2):
            conv_into(v_ref, 2 * p + half, cwv_ref, cbv_ref, s_scr, half)
            conv_into(x1_ref, 2 * p + half, cw1_ref, cb1_ref, g_scr, half)
        z = g_scr[...] * long_conv(s_scr[...], h0r, h0i, bias_ref[0])
        for half in range(2):
            conv_into(x2_ref, 2 * p + half, cw2_ref, cb2_ref, g_scr, half)
        y = g_scr[...] * long_conv(z, h1r, h1i, bias_ref[1])
        for half in range(2):
            b = 2 * p + half
            o_ref[b * CTX:(b + 1) * CTX, :] = y[:, half * LANE:(half + 1) * LANE].astype(BF16)


def _ctx_hyena(p, conv_w, conv_b, filt_ctx, layer, hy_bias_l, ctabs, o_hy):
    fwd, inv = ctabs
    nblk = HY_W // LANE
    ctx_blk = LAT_ROWS // CTX_ROWS
    tok = lambda col: pl.BlockSpec((CTX_ROWS, LANE), lambda c: (ctx_blk, col + c))
    chan = lambda rows, col: pl.BlockSpec((rows, LANE), lambda c: (0, col + c))
    hspec = lambda order: pl.BlockSpec((1, CTX, LANE), lambda c: (layer, 0, order * nblk + c))
    const = lambda shape: pl.BlockSpec(shape, lambda c: (0,) * len(shape))
    return pl.pallas_call(
        _ctx_hyena_kernel,
        grid=(nblk,),
        in_specs=[tok(COL_HV), tok(COL_HX1), tok(COL_HX2),
                  chan(HY_SHORT, 0), chan(1, 0), chan(HY_SHORT, nblk), chan(1, nblk),
                  chan(HY_SHORT, 2 * nblk), chan(1, 2 * nblk),
                  hspec(0), hspec(1),
                  pl.BlockSpec((HY_ORDER, 1, LANE), lambda c: (0, 0, c)),
                  const(fwd.shape), const(inv.shape), pl.BlockSpec(memory_space=pl.ANY)],
        out_specs=pl.BlockSpec((CTX_ROWS, LANE), lambda c: (ctx_blk, c)),
        out_shape=jax.ShapeDtypeStruct((ROWS, HY_W), BF16),
        input_output_aliases={14: 0},
        scratch_shapes=[pltpu.VMEM((CTX + 2 * HALO, LANE), F32),
                        pltpu.VMEM((CTX, 2 * LANE), F32),
                        pltpu.VMEM((CTX, 2 * LANE), F32)],
        compiler_params=_params("parallel"),
        name="context_hyena",
    )(p, p, p, conv_w, conv_b, conv_w, conv_b, conv_w, conv_b, filt_ctx, filt_ctx,
      hy_bias_l.reshape(HY_ORDER, 1, HY_W), fwd, inv, o_hy)


def _merge_kernel(oa_ref, oh_ref, ga_ref, gb_ref, x_ref, mod_ref, g2_ref, wpa_ref, wpb_ref, wo_ref,
                  *rest, routed):
    if routed:
        wr_ref, xo_ref, h_ref, rw_ref, wpa_s, wpb_s, wo_s = rest
    else:
        xo_ref, h_ref, wpa_s, wpb_s, wo_s = rest

    @pl.when(pl.program_id(0) == 0)
    def _():
        wpa_s[...] = wpa_ref[0].astype(BF16)
        wpb_s[...] = wpb_ref[0].astype(BF16)
        wo_s[...] = wo_ref[0].astype(BF16)

    a = jnp.dot(oa_ref[...], wpa_s[...], preferred_element_type=F32)
    b = jnp.dot(oh_ref[...], wpb_s[...], preferred_element_type=F32)
    mix = (jax.nn.sigmoid(ga_ref[...].astype(F32)) * a + jax.nn.sigmoid(gb_ref[...].astype(F32)) * b)
    y = jnp.dot(mix.astype(BF16), wo_s[...], preferred_element_type=F32)
    m = mod_ref[0]
    x = x_ref[...] + m[:, 2 * D:3 * D] * y
    xo_ref[...] = x
    h = _norm_mod(x, g2_ref[...], m[:, 3 * D:4 * D], m[:, 4 * D:5 * D])
    h_ref[...] = h.astype(BF16)
    if routed:
        rw_ref[...] = _dot_hilo(h, wr_ref[...])


def _merge(o_att, o_hy, p, x, mod_l, g2, w_pa, w_pb, w_o, layer, router):
    routed = router is not None
    tile = lambda w: pl.BlockSpec((TM, w), lambda i: (i, 0))
    full = lambda a: pl.BlockSpec(a.shape, lambda i: (0,) * a.ndim)
    layered = lambda a: pl.BlockSpec((1,) + a.shape[1:], lambda i: (layer, 0, 0))
    in_specs = [tile(ATT_W), tile(HY_W),
                pl.BlockSpec((TM, D), lambda i: (i, COL_GA)),
                pl.BlockSpec((TM, D), lambda i: (i, COL_GB)),
                tile(D),
                pl.BlockSpec((1, 1, N_MOD * D), lambda i: (_mod_row(i), 0, 0)),
                pl.BlockSpec((1, D), lambda i: (0, 0)),
                layered(w_pa), layered(w_pb), layered(w_o)]
    args = [o_att, o_hy, p, p, x, mod_l, g2, w_pa, w_pb, w_o]
    out_specs = [tile(D), tile(D)]
    out_shape = [jax.ShapeDtypeStruct((ROWS, D), F32), jax.ShapeDtypeStruct((ROWS, D), BF16)]
    if routed:
        wr = jnp.zeros((D, LANE), F32).at[:, :N_EXPERTS].set(router)
        in_specs.append(full(wr))
        args.append(wr)
        out_specs.append(tile(LANE))
        out_shape.append(jax.ShapeDtypeStruct((ROWS, LANE), F32))
    return pl.pallas_call(
        functools.partial(_merge_kernel, routed=routed),
        grid=(ROWS // TM,),
        in_specs=in_specs, out_specs=out_specs, out_shape=out_shape,
        scratch_shapes=[pltpu.VMEM(w_pa.shape[1:], BF16), pltpu.VMEM(w_pb.shape[1:], BF16),
                        pltpu.VMEM(w_o.shape[1:], BF16)],
        compiler_params=_params("arbitrary"),
        name="merge",
    )(*args)


TF = D_FF // 2


NF = D_FF // TF


SWIGLU_SUB = 768


def _swiglu_step(h, w1_ref, w3_ref, w2_ref, acc):
    tf = w1_ref.shape[2]
    for lo in range(0, tf, SWIGLU_SUB):
        hi = min(lo + SWIGLU_SUB, tf)
        a = jnp.dot(h, w1_ref[0, :, lo:hi].astype(BF16), preferred_element_type=F32)
        b = jnp.dot(h, w3_ref[0, :, lo:hi].astype(BF16), preferred_element_type=F32)
        t = a * jax.nn.sigmoid(a) * b
        acc[...] += jnp.dot(t.astype(BF16), w2_ref[0, lo:hi, :].astype(BF16), preferred_element_type=F32)


def _residual_out(x, gate, y, fg_ref):
    x = x + gate * y
    if fg_ref is not None:
        x = x * lax.rsqrt(jnp.mean(x * x, axis=-1, keepdims=True) + EPS) * fg_ref[...]
    return x


def _ffn_kernel(h_ref, x_ref, mod_ref, w1_ref, w3_ref, w2_ref, *rest, final):
    rest = list(rest)
    fg_ref = rest.pop(0) if final else None
    o_ref, acc = rest
    f = pl.program_id(1)

    @pl.when(f == 0)
    def _():
        acc[...] = jnp.zeros_like(acc)

    _swiglu_step(h_ref[...], w1_ref, w3_ref, w2_ref, acc)

    @pl.when(f == NF - 1)
    def _():
        o_ref[...] = _residual_out(x_ref[...], mod_ref[0][:, 5 * D:6 * D], acc[...], fg_ref)


def _dense_mixer(h, x, mod_l, w1, w3, w2, e, final_g=None):
    final = final_g is not None
    rows = LAT_ROWS if final else ROWS
    tile = lambda w: pl.BlockSpec((TM, w), lambda i, f: (i, 0))
    in_specs = [tile(D), tile(D),
                pl.BlockSpec((1, 1, N_MOD * D), lambda i, f: (_mod_row(i), 0, 0)),
                pl.BlockSpec((1, D, TF), lambda i, f: (e, 0, f)),
                pl.BlockSpec((1, D, TF), lambda i, f: (e, 0, f)),
                pl.BlockSpec((1, TF, D), lambda i, f: (e, f, 0))]
    args = [h, x, mod_l, w1, w3, w2]
    if final:
        in_specs.append(pl.BlockSpec((1, D), lambda i, f: (0, 0)))
        args.append(final_g.reshape(1, D))
    return pl.pallas_call(
        functools.partial(_ffn_kernel, final=final),
        grid=(rows // TM, NF),
        in_specs=in_specs,
        out_specs=tile(D),
        out_shape=jax.ShapeDtypeStruct((rows, D), F32),
        scratch_shapes=[pltpu.VMEM((TM, D), F32)],
        compiler_params=_params("parallel", "arbitrary"),
        name="ffn",
    )(*args)


NT = ROWS // TM
MOE_CH = 64
MOE_CMAX = TM // MOE_CH
MOE_NCH = 2 * TM // MOE_CH + N_EXPERTS
MOE_SLOTS = MOE_NCH * MOE_CH
MOE_GC = 16
MOE_GROUPS = NT * MOE_NCH // MOE_GC + N_EXPERTS
MOE_SB = 512
MOE_TF, MOE_NF = TF, NF
PLAN_ROWS = 8


def _plan_kernel(lg_ref, col_ref, row_ref, nch_ref):
    lane = lax.broadcasted_iota(jnp.int32, (TM, LANE), 1)
    ninf = jnp.asarray(-jnp.inf, F32)
    lg = jnp.where(lane < N_EXPERTS, lg_ref[...], ninf)
    m1 = jnp.max(lg, axis=-1, keepdims=True)
    i1 = jnp.min(jnp.where(lg == m1, lane, LANE), axis=-1, keepdims=True)
    lg2 = jnp.where(lane == i1, ninf, lg)
    m2 = jnp.max(lg2, axis=-1, keepdims=True)
    i2 = jnp.min(jnp.where(lg2 == m2, lane, LANE), axis=-1, keepdims=True)
    e = jnp.exp(m2 - m1)
    den = 1.0 + e
    sel1 = lane == i1
    sel2 = lane == i2
    onehot = jnp.where(sel1, 1.0, jnp.where(sel2, 1.0, 0.0))
    tri = jnp.where(lax.broadcasted_iota(jnp.int32, (TM, TM), 1) < lax.broadcasted_iota(jnp.int32, (TM, TM), 0),
                    1.0, 0.0).astype(BF16)
    rank = jnp.dot(tri, onehot.astype(BF16), preferred_element_type=F32)
    cnt = jnp.sum(onehot, axis=0, keepdims=True)
    nch = jnp.floor((cnt + (MOE_CH - 1)) * (1.0 / MOE_CH))
    upper = jnp.where(lax.broadcasted_iota(jnp.int32, (LANE, LANE), 0) < lax.broadcasted_iota(jnp.int32, (LANE, LANE), 1),
                      1.0, 0.0).astype(BF16)
    cbase = jnp.dot(jnp.broadcast_to(nch, (8, LANE)).astype(BF16), upper,
                    preferred_element_type=F32)[0:1]
    slot = MOE_CH * cbase + rank
    slot0 = jnp.sum(jnp.where(sel1, slot, 0.0), axis=-1, keepdims=True)
    slot1 = jnp.sum(jnp.where(sel2, slot, 0.0), axis=-1, keepdims=True)
    col = jnp.where(lane == 0, slot0, jnp.where(lane == 1, slot1,
                    jnp.where(lane == 2, 1.0 / den, jnp.where(lane == 3, e / den, 0.0))))
    col_ref[...] = col
    row_ref[...] = jnp.transpose(col)[0:PLAN_ROWS]
    nch_ref[0] = nch


def _moe_plan(logits):
    return pl.pallas_call(
        _plan_kernel,
        grid=(NT,),
        in_specs=[pl.BlockSpec((TM, LANE), lambda i: (i, 0))],
        out_specs=[pl.BlockSpec((TM, LANE), lambda i: (i, 0)),
                   pl.BlockSpec((PLAN_ROWS, TM), lambda i: (0, i)),
                   pl.BlockSpec((1, 1, LANE), lambda i: (i, 0, 0))],
        out_shape=[jax.ShapeDtypeStruct((ROWS, LANE), F32),
                   jax.ShapeDtypeStruct((PLAN_ROWS, ROWS), F32),
                   jax.ShapeDtypeStruct((NT, 1, LANE), F32)],
        compiler_params=_params("parallel"),
        name="moe_plan",
    )(logits)


def _moe_tables(nch):
    n = nch[:, 0, :N_EXPERTS].astype(jnp.int32)
    cbase = jnp.cumsum(n, axis=1) - n
    used = jnp.sum(n, axis=1)
    c = jnp.arange(MOE_CMAX)
    valid = (c[None, None, :] < n.T[:, :, None]).reshape(N_EXPERTS, -1)
    cid = (jnp.arange(NT)[None, :, None] * MOE_NCH + cbase.T[:, :, None] + c[None, None, :]).reshape(N_EXPERTS, -1)
    ne = jnp.sum(n, axis=0)
    ge = (ne + MOE_GC - 1) // MOE_GC
    gend = jnp.cumsum(ge)
    nused = gend[-1]
    size = MOE_GROUPS * MOE_GC
    pos = (gend - ge)[:, None] * MOE_GC + jnp.cumsum(valid, axis=1) - 1
    dest = jnp.where(valid, pos, size).reshape(-1)
    tab = jnp.full((size,), -1, jnp.int32).at[dest].set(cid.reshape(-1).astype(jnp.int32), mode='drop')
    at = jnp.arange(size)
    tab = tab[jnp.maximum(lax.cummax(jnp.where(tab >= 0, at, -1)), 0)]
    gexp = jnp.minimum(jnp.sum(jnp.arange(MOE_GROUPS)[:, None] >= gend[None, :], axis=1), N_EXPERTS - 1)
    pool = jnp.zeros((NT * MOE_NCH,), jnp.int32).at[jnp.where(at < nused * MOE_GC, tab, NT * MOE_NCH)].set(
        at.astype(jnp.int32), mode='drop')
    return tab, gexp.astype(jnp.int32), nused.reshape(1).astype(jnp.int32), used.astype(jnp.int32), pool


def _dispatch_kernel(used_ref, h_ref, row_ref, xs_ref):
    used_rows = used_ref[pl.program_id(0)] * MOE_CH
    s0 = row_ref[0:1, :]
    s1 = row_ref[1:2, :]
    for blk in range(MOE_SLOTS // MOE_SB):
        rows = slice(blk * MOE_SB, (blk + 1) * MOE_SB)

        @pl.when(blk * MOE_SB < used_rows)
        def _():
            r = (lax.broadcasted_iota(jnp.int32, (MOE_SB, TM), 0) + blk * MOE_SB).astype(F32)
            s = jnp.where(r == s0, 1.0, jnp.where(r == s1, 1.0, 0.0)).astype(BF16)
            xs_ref[rows, :] = jnp.dot(s, h_ref[...], preferred_element_type=F32).astype(BF16)

        @pl.when(blk * MOE_SB >= used_rows)
        def _():
            xs_ref[rows, :] = jnp.zeros((MOE_SB, D), BF16)


def _moe_dispatch(h, plan_row, used):
    return pl.pallas_call(
        _dispatch_kernel,
        grid_spec=pltpu.PrefetchScalarGridSpec(
            num_scalar_prefetch=1, grid=(NT,),
            in_specs=[pl.BlockSpec((TM, D), lambda i, u: (i, 0)),
                      pl.BlockSpec((PLAN_ROWS, TM), lambda i, u: (0, i))],
            out_specs=pl.BlockSpec((MOE_SLOTS, D), lambda i, u: (i, 0))),
        out_shape=jax.ShapeDtypeStruct((NT * MOE_SLOTS, D), BF16),
        compiler_params=_params("parallel"),
        name="moe_dispatch",
    )(used, h, plan_row)


def _experts_kernel(tab_ref, gexp_ref, nused_ref, *refs):
    x_refs = refs[:MOE_GC]
    w1_ref, w3_ref, w2_ref, y_ref, x_scr, acc = refs[MOE_GC:]
    f = pl.program_id(1)

    @pl.when(pl.program_id(0) < nused_ref[0])
    def _():
        @pl.when(f == 0)
        def _():
            for j, r in enumerate(x_refs):
                x_scr[j * MOE_CH:(j + 1) * MOE_CH, :] = r[...]
            acc[...] = jnp.zeros_like(acc)

        _swiglu_step(x_scr[...], w1_ref, w3_ref, w2_ref, acc)

        @pl.when(f == MOE_NF - 1)
        def _():
            y_ref[...] = acc[...].astype(BF16)


def _moe_experts(xs, tab, gexp, nused, w1, w3, w2, e0):
    grp = lambda g, nu: jnp.minimum(g, nu[0] - 1)
    ftile = lambda g, f, nu: jnp.where(g < nu[0], f, MOE_NF - 1)
    chunk = lambda j: pl.BlockSpec((MOE_CH, D), lambda g, f, tab, ge, nu: (tab[grp(g, nu) * MOE_GC + j], 0))
    rows = MOE_GC * MOE_CH
    return pl.pallas_call(
        _experts_kernel,
        grid_spec=pltpu.PrefetchScalarGridSpec(
            num_scalar_prefetch=3, grid=(MOE_GROUPS, MOE_NF),
            in_specs=[chunk(j) for j in range(MOE_GC)] + [
                pl.BlockSpec((1, D, MOE_TF), lambda g, f, tab, ge, nu: (e0 + ge[grp(g, nu)], 0, ftile(g, f, nu))),
                pl.BlockSpec((1, D, MOE_TF), lambda g, f, tab, ge, nu: (e0 + ge[grp(g, nu)], 0, ftile(g, f, nu))),
                pl.BlockSpec((1, MOE_TF, D), lambda g, f, tab, ge, nu: (e0 + ge[grp(g, nu)], ftile(g, f, nu), 0))],
            out_specs=pl.BlockSpec((rows, D), lambda g, f, tab, ge, nu: (grp(g, nu), 0)),
            scratch_shapes=[pltpu.VMEM((rows, D), BF16), pltpu.VMEM((rows, D), F32)]),
        out_shape=jax.ShapeDtypeStruct((MOE_GROUPS * rows, D), BF16),
        compiler_params=_params("arbitrary", "arbitrary"),
        name="moe_experts",
    )(tab, gexp, nused, *([xs] * MOE_GC), w1, w3, w2)


def _combine_kernel(used_ref, pool_ref, *refs, final):
    refs = list(refs)
    y_refs = [refs.pop(0) for _ in range(MOE_NCH)]
    col_ref, x_ref, mod_ref = refs[:3]
    fg_ref = refs[3] if final else None
    o_ref, y_scr, acc = refs[-3:]
    used = used_ref[pl.program_id(0)]
    col = col_ref[...]
    s0, s1, g0, g1 = col[:, 0:1], col[:, 1:2], col[:, 2:3], col[:, 3:4]
    acc[...] = jnp.zeros_like(acc)
    per_blk = MOE_SB // MOE_CH
    for blk in range(MOE_SLOTS // MOE_SB):
        @pl.when(blk * per_blk < used)
        def _():
            for q in range(blk * per_blk, (blk + 1) * per_blk):
                y_scr[q * MOE_CH:(q + 1) * MOE_CH, :] = y_refs[q][...]
            c = (lax.broadcasted_iota(jnp.int32, (TM, MOE_SB), 1) + blk * MOE_SB).astype(F32)
            w = jnp.where(c == s0, g0, jnp.where(c == s1, g1, 0.0)).astype(BF16)
            acc[...] += jnp.dot(w, y_scr[blk * MOE_SB:(blk + 1) * MOE_SB, :], preferred_element_type=F32)
    o_ref[...] = _residual_out(x_ref[...], mod_ref[0][:, 5 * D:6 * D], acc[...], fg_ref)


def _moe_combine(y, pool, used, plan_col, x, mod_l, final_g=None):
    final = final_g is not None
    rows = LAT_ROWS if final else ROWS
    tile = lambda w: pl.BlockSpec((TM, w), lambda i, u, pc: (i, 0))
    chunk = lambda q: pl.BlockSpec((MOE_CH, D), lambda i, u, pc: (pc[i * MOE_NCH + q], 0))
    in_specs = [chunk(q) for q in range(MOE_NCH)] + [
        tile(LANE), tile(D), pl.BlockSpec((1, 1, N_MOD * D), lambda i, u, pc: (_mod_row(i), 0, 0))]
    args = [y] * MOE_NCH + [plan_col, x, mod_l]
    if final:
        in_specs.append(pl.BlockSpec((1, D), lambda i, u, pc: (0, 0)))
        args.append(final_g.reshape(1, D))
    return pl.pallas_call(
        functools.partial(_combine_kernel, final=final),
        grid_spec=pltpu.PrefetchScalarGridSpec(
            num_scalar_prefetch=2, grid=(rows // TM,),
            in_specs=in_specs, out_specs=tile(D),
            scratch_shapes=[pltpu.VMEM((MOE_SLOTS, D), BF16), pltpu.VMEM((TM, D), F32)]),
        out_shape=jax.ShapeDtypeStruct((rows, D), F32),
        compiler_params=_params("parallel"),
        name="moe_combine",
    )(used, pool, *args)


def _moe_mixer(h, logits, x, mod_l, w1, w3, w2, e0, final_g=None):
    plan_col, plan_row, nch = _moe_plan(logits)
    tab, gexp, nused, used, pool = _moe_tables(nch)
    xs = _moe_dispatch(h, plan_row, used)
    y = _moe_experts(xs, tab, gexp, nused, w1, w3, w2, e0)
    return _moe_combine(y, pool, used, plan_col, x, mod_l, final_g)


def kernel(x, c, ctx, c_ctx, w_mod, b_mod, norm1_g, norm2_g, w_in, rpb, hy_conv_w, hy_conv_b, hy_w1, hy_b1, hy_w2, hy_b2, hy_w3, hy_freq, hy_bias, w_pa, w_pb, w_o, ffn_w1, ffn_w3, ffn_w2, moe_router, moe_w1, moe_w3, moe_w2, final_g):
    xs = jnp.concatenate([x.reshape(LAT_ROWS, D), ctx.reshape(CTX_ROWS, D)], axis=0)
    mod = _modulation(c, c_ctx, w_mod, b_mod)
    tabs = _fft_tables()
    ctabs = _ctx_tables()
    filt_lat = _implicit_filters(SEQ, hy_w1, hy_b1, hy_w2, hy_b2, hy_w3, hy_freq)
    filt_ctx = _implicit_filters(CTX, hy_w1, hy_b1, hy_w2, hy_b2, hy_w3, hy_freq)
    spec_re, spec_im = _filter_spectra(filt_lat, tabs)
    nblk = HY_W // LANE

    moe_w = [w.reshape((-1,) + w.shape[2:]).astype(BF16) for w in (moe_w1, moe_w3, moe_w2)]
    ffn_w = [w.astype(BF16) for w in (ffn_w1, ffn_w3, ffn_w2)]
    w_in_b = w_in.astype(BF16)
    bias_tabs = _bias_tables(rpb)

    for l in range(DEPTH):
        last = l == DEPTH - 1
        mod_l = mod[l].reshape(MOD_ROWS, 1, N_MOD * D)
        cw, cb = hy_conv_w[l], hy_conv_b[l].reshape(1, 3 * HY_W)
        p = _in_proj(xs, mod_l, norm1_g[l].reshape(1, D), w_in_b, l)

        o_att = _context_attention(p, _neighbourhood_attention(p, bias_tabs, l))
        z = _hyena_conv(p, COL_HV, p, COL_HX1, cw, cb, 0, nblk, spec_re, spec_im, l, 0,
                        hy_bias[l], tabs, True)
        o_hy = _hyena_conv(z, 0, p, COL_HX2, cw, cb, 0, 2 * nblk, spec_re, spec_im, l, 1,
                           hy_bias[l], tabs, False)
        o_hy = _ctx_hyena(p, cw, cb, filt_ctx, l, hy_bias[l], ctabs, o_hy)

        i = l // 2
        router = moe_router[i] if l % 2 else None
        res = _merge(o_att, o_hy, p, xs, mod_l, norm2_g[l].reshape(1, D), w_pa, w_pb, w_o, l, router)
        fg = final_g if last else None
        if l % 2 == 0:
            xs = _dense_mixer(res[1], res[0], mod_l, *ffn_w, i, final_g=fg)
        else:
            xs = _moe_mixer(res[1], res[2], res[0], mod_l, *moe_w, i * N_EXPERTS, final_g=fg)
    return xs.reshape(NB, SEQ, D)
```
